```python
import jax, jax.numpy as jnp
from jax import lax
import numpy as np


D_MODEL = 1024
BATCH = 8
SEQ = 2048
DEPTH = 1
DEC_BATCH = 128
DEC_SEQ = 4
PAST_LEN = 16384
PAGE_SIZE = 128

D_MIX = D_MODEL
D_A = D_MIX // 2
HEAD_A = 64
H_A = D_A // HEAD_A
D_R = D_MIX - D_A
H_R = 4
HEAD_R = D_R // H_R
LORA_W = 64
LORA_A = 64
LORA_G = 128
D_FF = 2816
RET_CHUNK = 128
ROPE_BASE = 10000.0
EPS = 1e-6
GN_EPS_A = 64e-5
GN_EPS_R = 1e-5
N_SHIFT = 3 * D_A + LORA_W + LORA_A + LORA_G
N_COLS = N_SHIFT + 4 * D_R

kernel_name = 'hymba_rwkv7_retnet_macaron_step'


def rms_norm(x, g):
    xf = x.astype(jnp.float32)
    y = xf * lax.rsqrt(jnp.mean(xf * xf, axis=-1, keepdims=True) + EPS)
    return (y * g.astype(jnp.float32)).astype(x.dtype)


def swiglu(h, wg, wu, wd):
    return (jax.nn.silu(h @ wg) * (h @ wu)) @ wd


def head_norm(y, eps):
    mu = jnp.mean(y, axis=-1, keepdims=True)
    var = jnp.mean(jnp.square(y - mu), axis=-1, keepdims=True)
    yn = (y - mu) * lax.rsqrt(var + eps)
    return yn.reshape(y.shape[0], y.shape[1], -1)


def rope(x, pos):
    half = x.shape[-1] // 2
    inv = ROPE_BASE ** (-jnp.arange(half, dtype=jnp.float32) / half)
    ang = pos.astype(jnp.float32)[:, None] * inv[None, :]
    cos = jnp.cos(ang)[None, :, None, :]
    sin = jnp.sin(ang)[None, :, None, :]
    x1, x2 = x[..., :half], x[..., half:]
    return jnp.concatenate([x1 * cos - x2 * sin, x1 * sin + x2 * cos], axis=-1)


def rwkv7_group(mixed, s0, w0, w2, a0, a2, g2, k_k, k_a, r_k, lnx_w, lnx_b):
    B, T, _ = mixed.shape
    f = mixed.astype(jnp.float32)
    o1, o2, o3 = D_A, 2 * D_A, 3 * D_A
    o4, o5 = o3 + LORA_W, o3 + LORA_W + LORA_A
    r, k, v = f[..., :o1], f[..., o1:o2], f[..., o2:o3]
    wd, ad, gd = f[..., o3:o4], f[..., o4:o5], f[..., o5:]
    w = -jax.nn.softplus(-(w0 + jnp.tanh(wd) @ w2)) - 0.5
    decay = jnp.exp(-jnp.exp(w))
    a = jax.nn.sigmoid(a0 + ad @ a2)
    g = jax.nn.sigmoid(gd) @ g2
    hd = lambda t: t.reshape(B, T, H_A, HEAD_A)
    kk = hd(k * k_k)
    kk = kk / jnp.maximum(jnp.sqrt(jnp.sum(kk * kk, axis=-1, keepdims=True)), 1e-12)
    k = k * (1.0 + (a - 1.0) * k_a)
    rh, kh, vh = hd(r), hd(k), hd(v)
    a_vec = -kk
    b_vec = kk * hd(a)
    xs = tuple(t.transpose(1, 0, 2, 3) for t in (rh, hd(decay), kh, vh, a_vec, b_vec))

    def step(S, inp):
        r_t, w_t, k_t, v_t, a_t, b_t = inp
        sa = jnp.einsum('bhij,bhj->bhi', S, a_t)
        S = S * w_t[:, :, None, :] + sa[..., None] * b_t[:, :, None, :] + v_t[..., None] * k_t[:, :, None, :]
        y = jnp.einsum('bhij,bhj->bhi', S, r_t)
        return S, y

    S, ys = lax.scan(step, s0.astype(jnp.float32), xs)
    ys = ys.transpose(1, 0, 2, 3)
    y = head_norm(ys, GN_EPS_A) * lnx_w + lnx_b
    bonus = (jnp.sum(rh * kh * r_k, axis=-1, keepdims=True) * vh).reshape(B, T, D_A)
    y = (y + bonus) * g
    return y.astype(mixed.dtype), S


def retention_chunked(q, k, v, s0):
    B, T, H, D = q.shape
    C = min(RET_CHUNK, T)
    n = T // C
    lg = jnp.log1p(-jnp.exp2(-5.0 - jnp.arange(H, dtype=jnp.float32)))
    idx = jnp.arange(C, dtype=jnp.float32)
    diff = idx[:, None] - idx[None, :]
    dmask = jnp.where(diff >= 0, jnp.exp(lg[:, None, None] * jnp.maximum(diff, 0.0)), 0.0)
    q_dec = jnp.exp(lg[:, None] * (idx + 1.0))
    k_dec = jnp.exp(lg[:, None] * (C - 1.0 - idx))
    c_dec = jnp.exp(lg * C)

    def to_chunks(t):
        return t.reshape(B, n, C, H, D).transpose(1, 0, 3, 2, 4)

    def step(S, inp):
        qc, kc, vc = inp
        inner = jnp.einsum('bhid,bhjd->bhij', qc, kc) * dmask
        y = jnp.einsum('bhij,bhje->bhie', inner, vc) + jnp.einsum('bhid,bhde->bhie', qc * q_dec[..., None], S)
        S = S * c_dec[:, None, None] + jnp.einsum('bhjd,bhje->bhde', kc * k_dec[..., None], vc)
        return S, y

    S, ys = lax.scan(step, s0.astype(jnp.float32), (to_chunks(q), to_chunks(k), to_chunks(v)))
    y = ys.transpose(1, 0, 3, 2, 4).reshape(B, T, H, D)
    return y, S


def retention_group(pr, s0, pos, gn_w):
    B, T, _ = pr.shape
    f = pr.astype(jnp.float32)
    hd = lambda t: t.reshape(B, T, H_R, HEAD_R)
    q = rope(hd(f[..., :D_R]), pos)
    k = rope(hd(f[..., D_R:2 * D_R]), pos) * (HEAD_R ** -0.5)
    v = hd(f[..., 2 * D_R:3 * D_R])
    g = f[..., 3 * D_R:]
    y, S = retention_chunked(q, k, v, s0)
    y = head_norm(y, GN_EPS_R) * gn_w
    y = jax.nn.silu(g) * y
    return y.astype(pr.dtype), S


def hybrid_layer(x, prev_h, wkv0, ret0, pos, p):
    (norm_g, f1g, f1u, f1d, w_in, mu, w0, w2, a0, a2, g2, k_k, k_a, r_k,
     lnx_w, lnx_b, gn_w, w_out, f2g, f2u, f2d) = p
    x = x + 0.5 * rms_norm(swiglu(rms_norm(x, norm_g[0]), f1g, f1u, f1d), norm_g[1])
    h = rms_norm(x, norm_g[2])
    h_ext = jnp.concatenate([prev_h[:, None, :].astype(h.dtype), h], axis=1)
    ps = h_ext @ w_in[:, :N_SHIFT]
    cur, prv = ps[:, 1:], ps[:, :-1]
    mixed = cur + (prv - cur) * mu
    ya, wkv_new = rwkv7_group(mixed, wkv0, w0, w2, a0, a2, g2, k_k, k_a, r_k, lnx_w, lnx_b)
    pr = h @ w_in[:, N_SHIFT:]
    yr, ret_new = retention_group(pr, ret0, pos, gn_w)
    mix = jnp.concatenate([ya, yr], axis=-1) @ w_out
    x = x + rms_norm(mix, norm_g[3])
    x = x + 0.5 * rms_norm(swiglu(rms_norm(x, norm_g[4]), f2g, f2u, f2d), norm_g[5])
    return x, h[:, -1], wkv_new, ret_new


def setup_inputs(seed: int = 0) -> dict:
    key = jax.random.key(seed)
    ks = jax.random.split(key, 26)
    nrm = lambda k, shape, s: s * jax.random.normal(k, shape, jnp.float32)
    return {
        'x_prompt': nrm(ks[0], (BATCH, SEQ, D_MODEL), 1.0),
        'x_sample': nrm(ks[1], (DEC_BATCH, DEC_SEQ, D_MODEL), 1.0),
        'state_shift': nrm(ks[2], (DEPTH, DEC_BATCH, D_MODEL), 1.0),
        'state_wkv': nrm(ks[3], (DEPTH, DEC_BATCH, H_A, HEAD_A, HEAD_A), 0.3),
        'state_ret': nrm(ks[4], (DEPTH, DEC_BATCH, H_R, HEAD_R, HEAD_R), 1.0),
        'norm_g': 1.0 + nrm(ks[5], (DEPTH, 6, D_MODEL), 0.05),
        'ffn1_wg': nrm(ks[6], (DEPTH, D_MODEL, D_FF), D_MODEL ** -0.5),
        'ffn1_wu': nrm(ks[7], (DEPTH, D_MODEL, D_FF), D_MODEL ** -0.5),
        'ffn1_wd': nrm(ks[8], (DEPTH, D_FF, D_MODEL), D_FF ** -0.5),
        'w_in': nrm(ks[9], (DEPTH, D_MODEL, N_COLS), D_MODEL ** -0.5),
        'mu_shift': jax.random.uniform(ks[10], (DEPTH, N_SHIFT), jnp.float32),
        'w0': jnp.linspace(-5.0, 0.5, D_A, dtype=jnp.float32)[None, :] + nrm(ks[11], (DEPTH, D_A), 0.1),
        'w2': nrm(ks[12], (DEPTH, LORA_W, D_A), LORA_W ** -0.5),
        'a0': nrm(ks[13], (DEPTH, D_A), 0.1),
        'a2': nrm(ks[14], (DEPTH, LORA_A, D_A), LORA_A ** -0.5),
        'g2': nrm(ks[15], (DEPTH, LORA_G, D_A), LORA_G ** -0.5),
        'k_k': 0.85 + nrm(ks[16], (DEPTH, D_A), 0.05),
        'k_a': 1.0 + nrm(ks[17], (DEPTH, D_A), 0.05),
        'r_k': nrm(ks[18], (DEPTH, H_A, HEAD_A), 0.1),
        'lnx_w': 1.0 + nrm(ks[19], (DEPTH, D_A), 0.05),
        'lnx_b': nrm(ks[20], (DEPTH, D_A), 0.02),
        'ret_gn_w': 1.0 + nrm(ks[21], (DEPTH, D_R), 0.05),
        'w_out': nrm(ks[22], (DEPTH, D_MIX, D_MODEL), D_MIX ** -0.5),
        'ffn2_wg': nrm(ks[23], (DEPTH, D_MODEL, D_FF), D_MODEL ** -0.5),
        'ffn2_wu': nrm(ks[24], (DEPTH, D_MODEL, D_FF), D_MODEL ** -0.5),
        'ffn2_wd': nrm(ks[25], (DEPTH, D_FF, D_MODEL), D_FF ** -0.5),
    }


def reference(x_prompt, x_sample, state_shift, state_wkv, state_ret, norm_g, ffn1_wg, ffn1_wu, ffn1_wd,
              w_in, mu_shift, w0, w2, a0, a2, g2, k_k, k_a, r_k, lnx_w, lnx_b, ret_gn_w, w_out,
              ffn2_wg, ffn2_wu, ffn2_wd):
    Bp, Tp, _ = x_prompt.shape
    Ts = x_sample.shape[1]
    pos_p = jnp.arange(Tp, dtype=jnp.int32)
    pos_s = PAST_LEN + jnp.arange(Ts, dtype=jnp.int32)
    yp, ys = x_prompt, x_sample
    sh_p, wk_p, rt_p, sh_s, wk_s, rt_s = [], [], [], [], [], []
    for l in range(DEPTH):
        p = (norm_g[l], ffn1_wg[l], ffn1_wu[l], ffn1_wd[l], w_in[l], mu_shift[l], w0[l], w2[l], a0[l],
             a2[l], g2[l], k_k[l], k_a[l], r_k[l], lnx_w[l], lnx_b[l], ret_gn_w[l], w_out[l],
             ffn2_wg[l], ffn2_wu[l], ffn2_wd[l])
        zero_shift = jnp.zeros((Bp, D_MODEL), x_prompt.dtype)
        zero_wkv = jnp.zeros((Bp, H_A, HEAD_A, HEAD_A), jnp.float32)
        zero_ret = jnp.zeros((Bp, H_R, HEAD_R, HEAD_R), jnp.float32)
        yp, a1, b1, c1 = hybrid_layer(yp, zero_shift, zero_wkv, zero_ret, pos_p, p)
        ys, a2_, b2, c2 = hybrid_layer(ys, state_shift[l], state_wkv[l], state_ret[l], pos_s, p)
        sh_p.append(a1); wk_p.append(b1); rt_p.append(c1)
        sh_s.append(a2_); wk_s.append(b2); rt_s.append(c2)
    shift_prompt = jnp.stack(sh_p)
    wkv_prompt = jnp.stack(wk_p)
    ret_prompt = jnp.stack(rt_p)
    shift_sample = jnp.stack(sh_s)
    wkv_sample = jnp.stack(wk_s)
    ret_sample = jnp.stack(rt_s)
    return (yp, ys, shift_prompt, wkv_prompt, ret_prompt, shift_sample, wkv_sample, ret_sample)
```

```python
import functools

import jax
import jax.numpy as jnp
from jax import lax
from jax.experimental import pallas as pl
from jax.experimental.pallas import tpu as pltpu

F32 = jnp.float32
BF16 = jnp.bfloat16

D_MODEL = 1024
D_A = 512
HEAD_A = 64
H_A = D_A // HEAD_A
D_R = 512
H_R = 4
HEAD_R = D_R // H_R
LORA_W, LORA_A, LORA_G = 64, 64, 128
D_FF = 2816
RET_CHUNK = 128
ROPE_BASE = 10000.0
EPS = 1e-6
GN_EPS_A = 64e-5
GN_EPS_R = 1e-5
N_SHIFT = 3 * D_A + LORA_W + LORA_A + LORA_G
N_COLS = N_SHIFT + 4 * D_R
PAST_LEN = 16384

LANES = 128
SUBLANES = 8
VMEM_LIMIT = 52 * 1024 * 1024

FF_CHUNK = D_FF // 2
WKV_CHUNK = 64
PAIR = 2 * HEAD_A


def _nt(a, b):
    return lax.dot_general(a, b, (((1,), (1,)), ((), ())), preferred_element_type=F32)


def _tn(a, b):
    return lax.dot_general(a, b, (((0,), (0,)), ((), ())), preferred_element_type=F32)


def _mm(a, b):
    return jnp.dot(a, b, preferred_element_type=F32)


def _mm_f32(a, b):
    return jnp.dot(a, b, preferred_element_type=F32, precision=lax.Precision.HIGHEST)


def _split_hi_lo(x):
    hi = x.astype(BF16)
    lo = (x - hi.astype(F32)).astype(BF16)
    return hi, lo


def _mm_exact_rhs(x, m):
    hi, lo = _split_hi_lo(x)
    return _mm(hi, m) + _mm(lo, m)


def _mm_exact_lhs(m, x):
    hi, lo = _split_hi_lo(x)
    return _mm(m, hi) + _mm(m, lo)


def _rms(x, g):
    return x * lax.rsqrt(jnp.mean(x * x, axis=-1, keepdims=True) + EPS) * g


def _softplus(x):
    return jnp.maximum(x, 0.0) + jnp.log(1.0 + jnp.exp(-jnp.abs(x)))


def _sigmoid(x):
    return 1.0 / (1.0 + jnp.exp(-x))


def _resident(shape):
    nd = len(shape)
    return pl.BlockSpec(shape, lambda *_: (0,) * nd, pipeline_mode=pl.Buffered(1))


def _params(n_axes):
    return pltpu.CompilerParams(dimension_semantics=("arbitrary",) * n_axes,
                                vmem_limit_bytes=VMEM_LIMIT)


def _ffn_kernel(with_mix, g_in, g_out, *refs):
    if with_mix:
        x_ref, ya_ref, yr_ref, wo_ref, ng_ref, wg_ref, wu_ref, wd_ref, o_ref = refs
    else:
        x_ref, ng_ref, wg_ref, wu_ref, wd_ref, o_ref = refs
    x = x_ref[...]
    if with_mix:
        mix = (_mm(ya_ref[...].astype(BF16), wo_ref[0:D_A, :])
               + _mm(yr_ref[...].astype(BF16), wo_ref[D_A:, :]))
        x = x + _rms(mix, ng_ref[3:4, :])
    h = _rms(x, ng_ref[g_in:g_in + 1, :]).astype(BF16)
    acc = None
    for c in range(D_FF // FF_CHUNK):
        cols = slice(c * FF_CHUNK, (c + 1) * FF_CHUNK)
        gate = _mm(h, wg_ref[:, cols])
        up = _mm(h, wu_ref[:, cols])
        act = (gate * _sigmoid(gate) * up).astype(BF16)
        part = _mm(act, wd_ref[cols, :])
        acc = part if acc is None else acc + part
    o_ref[...] = x + 0.5 * _rms(acc, ng_ref[g_out:g_out + 1, :])


def _ffn(x, ng, wg, wu, wd, g_in, g_out, tm, mix=None):
    m = x.shape[0]
    row = lambda w: pl.BlockSpec((tm, w), lambda i: (i, 0))
    if mix is None:
        args = (x, ng, wg, wu, wd)
        specs = [row(D_MODEL), _resident(ng.shape), _resident(wg.shape), _resident(wu.shape),
                 _resident(wd.shape)]
    else:
        ya, yr, wo = mix
        args = (x, ya, yr, wo, ng, wg, wu, wd)
        specs = [row(D_MODEL), row(D_A), row(D_R), _resident(wo.shape), _resident(ng.shape),
                 _resident(wg.shape), _resident(wu.shape), _resident(wd.shape)]
    return pl.pallas_call(
        functools.partial(_ffn_kernel, mix is not None, g_in, g_out),
        grid=(m // tm,),
        in_specs=specs,
        out_specs=row(D_MODEL),
        out_shape=jax.ShapeDtypeStruct((m, D_MODEL), F32),
        compiler_params=_params(1),
        name="ffn_mix" if mix is not None else "ffn",
    )(*args)


def _proj_kernel(n_t, lag, pad, tiles_per_seq, has_prev, *refs):
    it = iter(refs)
    x_ref = next(it)
    prev_ref = next(it) if has_prev else None
    (ng_ref, win_ref, mu_ref, w0_ref, w2_ref, a0_ref, a2_ref, g2_ref, kk_ref, ka_ref, bd_ref,
     cos_ref, sin_ref,
     r_o, lw_o, k_o, v_o, kk_o, kka_o, g_o, q_o, kr_o, vr_o, gr_o, hl_o, ps_scr) = tuple(it)

    def load(ref):
        if n_t == 1:
            return ref[...]
        return jnp.concatenate([ref[t] for t in range(n_t)], axis=0)

    def store(ref, val):
        if n_t == 1:
            ref[...] = val.astype(ref.dtype)
        else:
            rows = val.shape[0] // n_t
            for t in range(n_t):
                ref[t] = val[t * rows:(t + 1) * rows].astype(ref.dtype)

    x = load(x_ref)
    tm = x.shape[0]
    h = _rms(x, ng_ref[2:3, :])
    hl_o[0] = h[tm - lag:, :]
    hb = h.astype(BF16)

    seq_start = (pl.program_id(0) % tiles_per_seq) == 0
    if has_prev:
        @pl.when(seq_start)
        def _():
            ps_scr[pad - lag:pad, :] = _mm(prev_ref[0].astype(BF16), win_ref[:, 0:N_SHIFT])
    else:
        @pl.when(seq_start)
        def _():
            ps_scr[pad - lag:pad, :] = jnp.zeros((lag, N_SHIFT), F32)
    ps_scr[pad:pad + tm, :] = _mm(hb, win_ref[:, 0:N_SHIFT])
    cur = ps_scr[pad:pad + tm, :]
    prv = ps_scr[pad - lag:pad - lag + tm, :]
    mixed = cur + (prv - cur) * mu_ref[...]
    ps_scr[pad - lag:pad, :] = cur[tm - lag:, :]

    r = mixed[:, 0:D_A]
    k = mixed[:, D_A:2 * D_A]
    v = mixed[:, 2 * D_A:3 * D_A]
    wa = mixed[:, 3 * D_A:3 * D_A + LORA_W + LORA_A]
    gd = mixed[:, 3 * D_A + LORA_W + LORA_A:N_SHIFT]
    w_pre = w0_ref[...] + _mm(jnp.tanh(wa).astype(BF16), w2_ref[...])
    w_log = -_softplus(-w_pre) - 0.5
    lw = -jnp.exp(w_log)
    a = _sigmoid(a0_ref[...] + _mm(wa.astype(BF16), a2_ref[...]))
    g = _mm(_sigmoid(gd).astype(BF16), g2_ref[...])
    kk = k * kk_ref[...]
    ss = _mm_exact_rhs(kk * kk, bd_ref[...])
    kk = kk / jnp.maximum(jnp.sqrt(ss), 1e-12)
    k_mod = k * (1.0 + (a - 1.0) * ka_ref[...])
    store(r_o, r)
    store(lw_o, lw)
    store(k_o, k_mod)
    store(v_o, v)
    store(kk_o, kk)
    store(kka_o, kk * a)
    store(g_o, g)

    pr = _mm(hb, win_ref[:, N_SHIFT:])
    cos2 = load(cos_ref)
    sin2 = load(sin_ref)

    def rope(t):
        parts = []
        for hh in range(H_R):
            th = t[:, hh * HEAD_R:(hh + 1) * HEAD_R]
            parts.append(th * cos2 + pltpu.roll(th, HEAD_R // 2, 1) * sin2)
        return jnp.concatenate(parts, axis=1)

    store(q_o, rope(pr[:, 0:D_R]))
    store(kr_o, rope(pr[:, D_R:2 * D_R]) * (HEAD_R ** -0.5))
    store(vr_o, pr[:, 2 * D_R:3 * D_R])
    store(gr_o, pr[:, 3 * D_R:])


def _proj(x, prev, ng, win, mu, w0, w2p, a0, a2p, g2, k_k, k_a, bd, cos2, sin2, *,
          n_t, rows_per_t, lag, tiles_per_seq, qkv_dtype):
    tm = n_t * rows_per_t
    pad = max(lag, SUBLANES)
    if n_t == 1:
        m = x.shape[0]
        n_tiles = m // tm
        row = lambda w: pl.BlockSpec((tm, w), lambda i: (i, 0))
        shp = lambda w, dt: jax.ShapeDtypeStruct((m, w), dt)
        tab = pl.BlockSpec((tm, HEAD_R), lambda i: (i % tiles_per_seq, 0))
    else:
        m = x.shape[0] * x.shape[1]
        n_tiles = x.shape[1] // rows_per_t
        row = lambda w: pl.BlockSpec((n_t, rows_per_t, w), lambda i: (0, i, 0))
        shp = lambda w, dt: jax.ShapeDtypeStruct((n_t, m // n_t, w), dt)
        tab = pl.BlockSpec((n_t, rows_per_t, HEAD_R), lambda i: (0, i, 0))
    n_seq = n_tiles // tiles_per_seq
    hl_spec = pl.BlockSpec((1, lag, D_MODEL), lambda i: (i // tiles_per_seq, 0, 0))
    args = [x]
    specs = [row(D_MODEL)]
    if prev is not None:
        args.append(prev)
        specs.append(pl.BlockSpec((1, lag, D_MODEL), lambda i: (i // tiles_per_seq, 0, 0)))
    consts = (ng, win, mu, w0, w2p, a0, a2p, g2, k_k, k_a, bd)
    args += list(consts) + [cos2, sin2]
    specs += [_resident(c.shape) for c in consts] + [tab, tab]
    out_shape = ([shp(D_A, F32)] * 7 + [shp(D_R, qkv_dtype)] * 3 + [shp(D_R, F32)]
                 + [jax.ShapeDtypeStruct((n_seq, lag, D_MODEL), F32)])
    out_specs = [row(D_A)] * 7 + [row(D_R)] * 4 + [hl_spec]
    return pl.pallas_call(
        functools.partial(_proj_kernel, n_t, lag, pad, tiles_per_seq, prev is not None),
        grid=(n_tiles,),
        in_specs=specs,
        out_specs=out_specs,
        out_shape=out_shape,
        scratch_shapes=[pltpu.VMEM((pad + tm, N_SHIFT), F32)],
        compiler_params=_params(1),
        name="proj",
    )(*args)


def _wkv_post(y, r, k, v, g, rk, lw_g, lb_g, bd):
    inv_n = 1.0 / HEAD_A
    mu = _mm_exact_rhs(y, bd) * inv_n
    d = y - mu
    var = _mm_exact_rhs(d * d, bd) * inv_n
    yn = d * lax.rsqrt(var + GN_EPS_A) * lw_g + lb_g
    bonus = _mm_exact_rhs(r * k * rk, bd) * v
    return (yn + bonus) * g


def _wkv_prompt_kernel(tt, r_ref, lw_ref, k_ref, v_ref, kk_ref, kka_ref, g_ref,
                       rk_ref, lnw_ref, lnb_ref, bd_ref, tri_ref,
                       ya_o, s_o, s_scr):
    c = WKV_CHUNK
    j = pl.program_id(1)

    @pl.when(j == 0)
    def _():
        s_scr[...] = jnp.zeros_like(s_scr)

    lane = lax.broadcasted_iota(jnp.int32, (c, PAIR), 1)
    first = lane < HEAD_A

    def stack(x):
        return jnp.concatenate([jnp.where(first, x, 0.0), jnp.where(first, 0.0, x)], axis=0)

    ri = lax.broadcasted_iota(jnp.int32, (2 * c, 2 * c), 0) % c
    ci = lax.broadcasted_iota(jnp.int32, (2 * c, 2 * c), 1) % c
    strict = ri > ci
    incl = ri >= ci
    eye = (lax.broadcasted_iota(jnp.int32, (2 * c, 2 * c), 0)
           == lax.broadcasted_iota(jnp.int32, (2 * c, 2 * c), 1)).astype(F32)
    bd = bd_ref[...]
    tri = tri_ref[...]

    def chunk_body(ch, carry):
        rows = pl.ds(pl.multiple_of(ch * c, c), c)
        for p in range(H_A // 2):
            lanes = slice(p * PAIR, (p + 1) * PAIR)
            r = r_ref[rows, lanes]
            lw = lw_ref[rows, lanes]
            k = k_ref[rows, lanes]
            v = v_ref[rows, lanes]
            kk = kk_ref[rows, lanes]
            kka = kka_ref[rows, lanes]
            cum = _mm_exact_lhs(tri, lw)
            cum_end = cum[c - 1:c, :]
            e_pos = jnp.exp(cum)
            e_neg = jnp.exp(-cum)
            e_end = jnp.exp(cum_end - cum)
            xs = jnp.concatenate([stack(-kk * jnp.exp(cum - lw)), stack(r * e_pos)], axis=0).astype(BF16)
            ws = jnp.concatenate([stack(kka * e_neg), stack(k * e_neg)], axis=0).astype(BF16)
            gram = _nt(xs, ws)
            a_ab = jnp.where(strict, gram[0:2 * c, 0:2 * c], 0.0)
            a_ak = jnp.where(strict, gram[0:2 * c, 2 * c:], 0.0)
            a_rb = jnp.where(incl, gram[2 * c:, 0:2 * c], 0.0)
            a_rk = jnp.where(incl, gram[2 * c:, 2 * c:], 0.0)
            inv = eye + a_ab
            pw = a_ab
            for _ in range(5):
                pw = _mm_f32(pw, pw)
                inv = inv + _mm_f32(inv, pw)
            s = s_scr[p]
            z = _nt(xs, s.astype(BF16))
            vs = stack(v).astype(BF16)
            rhs = z[0:2 * c] + _mm(a_ak.astype(BF16), vs)
            u = _mm(inv.astype(BF16), rhs.astype(BF16)).astype(BF16)
            y2 = z[2 * c:] + _mm(a_rb.astype(BF16), u) + _mm(a_rk.astype(BF16), vs)
            y = y2[0:c] + y2[c:]
            s_scr[p] = (s * jnp.exp(cum_end)
                        + _tn(u, stack(kka * e_end).astype(BF16))
                        + _tn(vs, stack(k * e_end).astype(BF16)))
            out = _wkv_post(y, r, k, v, g_ref[rows, lanes], rk_ref[:, lanes], lnw_ref[:, lanes],
                            lnb_ref[:, lanes], bd)
            ya_o[rows, lanes] = out.astype(ya_o.dtype)
        return carry

    lax.fori_loop(0, tt // c, chunk_body, 0)

    @pl.when(j == pl.num_programs(1) - 1)
    def _():
        for p in range(H_A // 2):
            s = s_scr[p]
            s_o[0, 2 * p] = s[0:HEAD_A, 0:HEAD_A]
            s_o[0, 2 * p + 1] = s[HEAD_A:, HEAD_A:]


def _wkv_prompt(r, lw, k, v, kk, kka, g, rk, lnw, lnb, bd, tri, *, batch, seq, tt):
    tiles = seq // tt
    row = pl.BlockSpec((tt, D_A), lambda b, j: (b * tiles + j, 0))
    consts = (rk, lnw, lnb, bd, tri)
    return pl.pallas_call(
        functools.partial(_wkv_prompt_kernel, tt),
        grid=(batch, tiles),
        in_specs=[row] * 7 + [_resident(x.shape) for x in consts],
        out_specs=[row, pl.BlockSpec((1, H_A, HEAD_A, HEAD_A), lambda b, j: (b, 0, 0, 0))],
        out_shape=[jax.ShapeDtypeStruct((batch * seq, D_A), BF16),
                   jax.ShapeDtypeStruct((batch, H_A, HEAD_A, HEAD_A), F32)],
        scratch_shapes=[pltpu.VMEM((H_A // 2, PAIR, PAIR), F32)],
        compiler_params=_params(2),
        name="wkv_prompt",
    )(r, lw, k, v, kk, kka, g, *consts)


def _wkv_sample_kernel(n_t, bb, r_ref, lw_ref, k_ref, v_ref, kk_ref, kka_ref, g_ref,
                       rk_ref, lnw_ref, lnb_ref, bd_ref, s_ref,
                       ya_o, s_o, y_scr):
    n = HEAD_A
    eye = (lax.broadcasted_iota(jnp.int32, (n, n), 0) == lax.broadcasted_iota(jnp.int32, (n, n), 1))

    for bi in range(bb):
        row = slice(bi, bi + 1)
        for p in range(H_A // 2):
            slab = slice(p * PAIR, (p + 1) * PAIR)
            states = [s_ref[bi, 2 * p], s_ref[bi, 2 * p + 1]]
            for t in range(n_t):
                vecs = [ref[t, row, slab] for ref in (kk_ref, v_ref, lw_ref, kka_ref, k_ref, r_ref)]
                y_rows = []
                for hh in range(2):
                    kk, v, lw, kka, k, r = [jnp.broadcast_to(x[:, hh * n:(hh + 1) * n], (n, n))
                                            for x in vecs]
                    s = states[hh]
                    sa = jnp.sum(s * (-kk), axis=1, keepdims=True)
                    v_col = jnp.sum(jnp.where(eye, v, 0.0), axis=1, keepdims=True)
                    s = s * jnp.exp(lw) + sa * kka + v_col * k
                    y_col = jnp.sum(s * r, axis=1, keepdims=True)
                    y_rows.append(jnp.sum(jnp.where(eye, y_col, 0.0), axis=0, keepdims=True))
                    states[hh] = s
                y_scr[t, row, slab] = jnp.concatenate(y_rows, axis=1)
            s_o[bi, 2 * p] = states[0]
            s_o[bi, 2 * p + 1] = states[1]

    for t in range(n_t):
        out = _wkv_post(y_scr[t], r_ref[t], k_ref[t], v_ref[t], g_ref[t], rk_ref[...],
                        lnw_ref[...], lnb_ref[...], bd_ref[...])
        ya_o[t] = out.astype(ya_o.dtype)


def _wkv_sample(r, lw, k, v, kk, kka, g, rk, lnw, lnb, bd, s0, *, bb):
    n_t, n_b, _ = r.shape
    consts = (rk, lnw, lnb, bd)
    blk = pl.BlockSpec((n_t, bb, D_A), lambda i: (0, i, 0))
    st = pl.BlockSpec((bb, H_A, HEAD_A, HEAD_A), lambda i: (i, 0, 0, 0))
    return pl.pallas_call(
        functools.partial(_wkv_sample_kernel, n_t, bb),
        grid=(n_b // bb,),
        in_specs=[blk] * 7 + [_resident(x.shape) for x in consts] + [st],
        out_specs=[blk, st],
        out_shape=[jax.ShapeDtypeStruct((n_t, n_b, D_A), F32),
                   jax.ShapeDtypeStruct(s0.shape, F32)],
        scratch_shapes=[pltpu.VMEM((n_t, bb, D_A), F32)],
        compiler_params=_params(1),
        name="wkv_sample",
    )(r, lw, k, v, kk, kka, g, *consts, s0)


def _ret_head(q, k, v, g, s, dmask, qdec, kdec, cdec, gn):
    qb, kb, vb = q.astype(BF16), k.astype(BF16), v.astype(BF16)
    inner = _nt(qb, kb) * dmask
    y = _mm(inner.astype(BF16), vb) + _mm((q.astype(F32) * qdec).astype(BF16), s.astype(BF16))
    s_new = s * cdec + _tn((k.astype(F32) * kdec).astype(BF16), vb)
    mu = jnp.mean(y, axis=-1, keepdims=True)
    d = y - mu
    var = jnp.mean(d * d, axis=-1, keepdims=True)
    yn = d * lax.rsqrt(var + GN_EPS_R) * gn
    return g * _sigmoid(g) * yn, s_new


def _ret_prompt_kernel(tt, q_ref, k_ref, v_ref, g_ref, dm_ref, qd_ref, kd_ref, cd_ref, gn_ref,
                       y_o, s_o, s_scr):
    j = pl.program_id(1)

    @pl.when(j == 0)
    def _():
        s_scr[...] = jnp.zeros_like(s_scr)

    c = RET_CHUNK
    for ch in range(tt // c):
        rows = slice(ch * c, (ch + 1) * c)
        for hh in range(H_R):
            lanes = slice(hh * HEAD_R, (hh + 1) * HEAD_R)
            y, s_new = _ret_head(q_ref[rows, lanes], k_ref[rows, lanes], v_ref[rows, lanes],
                                 g_ref[rows, lanes], s_scr[hh], dm_ref[hh], qd_ref[hh], kd_ref[hh],
                                 cd_ref[hh], gn_ref[:, lanes])
            s_scr[hh] = s_new
            y_o[rows, lanes] = y.astype(y_o.dtype)

    @pl.when(j == pl.num_programs(1) - 1)
    def _():
        s_o[0] = s_scr[...]


def _ret_prompt(q, k, v, g, dm, qd, kd, cd, gn, *, batch, seq, tt):
    tiles = seq // tt
    row = pl.BlockSpec((tt, D_R), lambda b, j: (b * tiles + j, 0))
    consts = (dm, qd, kd, cd, gn)
    return pl.pallas_call(
        functools.partial(_ret_prompt_kernel, tt),
        grid=(batch, tiles),
        in_specs=[row] * 4 + [_resident(x.shape) for x in consts],
        out_specs=[row, pl.BlockSpec((1, H_R, HEAD_R, HEAD_R), lambda b, j: (b, 0, 0, 0))],
        out_shape=[jax.ShapeDtypeStruct((batch * seq, D_R), BF16),
                   jax.ShapeDtypeStruct((batch, H_R, HEAD_R, HEAD_R), F32)],
        scratch_shapes=[pltpu.VMEM((H_R, HEAD_R, HEAD_R), F32)],
        compiler_params=_params(2),
        name="ret_prompt",
    )(q, k, v, g, *consts)


def _ret_sample_kernel(n_t, bb, q_ref, k_ref, v_ref, g_ref, dm_ref, qd_ref, kd_ref, cd_ref,
                       gn_ref, s_ref, y_o, s_o):
    rid = lax.broadcasted_iota(jnp.int32, (SUBLANES, HEAD_R), 0)

    def seq_rows(ref, bi, lanes):
        out = jnp.zeros((SUBLANES, HEAD_R), F32)
        for t in range(n_t):
            out = jnp.where(rid == t, jnp.broadcast_to(ref[t, bi:bi + 1, lanes], out.shape), out)
        return out

    for bi in range(bb):
        for hh in range(H_R):
            lanes = slice(hh * HEAD_R, (hh + 1) * HEAD_R)
            y, s_new = _ret_head(seq_rows(q_ref, bi, lanes), seq_rows(k_ref, bi, lanes),
                                 seq_rows(v_ref, bi, lanes), seq_rows(g_ref, bi, lanes),
                                 s_ref[bi, hh], dm_ref[hh], qd_ref[hh], kd_ref[hh], cd_ref[hh],
                                 gn_ref[:, lanes])
            s_o[bi, hh] = s_new
            for t in range(n_t):
                y_o[t, bi:bi + 1, lanes] = y[t:t + 1].astype(y_o.dtype)


def _ret_sample(q, k, v, g, dm, qd, kd, cd, gn, s0, *, bb):
    n_t, n_b, _ = q.shape
    consts = (dm, qd, kd, cd, gn)
    blk = pl.BlockSpec((n_t, bb, D_R), lambda i: (0, i, 0))
    st = pl.BlockSpec((bb, H_R, HEAD_R, HEAD_R), lambda i: (i, 0, 0, 0))
    return pl.pallas_call(
        functools.partial(_ret_sample_kernel, n_t, bb),
        grid=(n_b // bb,),
        in_specs=[blk] * 4 + [_resident(x.shape) for x in consts] + [st],
        out_specs=[blk, st],
        out_shape=[jax.ShapeDtypeStruct((n_t, n_b, D_R), F32),
                   jax.ShapeDtypeStruct(s0.shape, F32)],
        compiler_params=_params(1),
        name="ret_sample",
    )(q, k, v, g, *consts, s0)


def _rope_tables(pos):
    half = HEAD_R // 2
    inv = ROPE_BASE ** (-jnp.arange(half, dtype=F32) / half)
    ang = pos.astype(F32)[:, None] * inv[None, :]
    cos, sin = jnp.cos(ang), jnp.sin(ang)
    return jnp.concatenate([cos, cos], axis=1), jnp.concatenate([-sin, sin], axis=1)


def _ret_tables(c):
    lg = jnp.log1p(-jnp.exp2(-5.0 - jnp.arange(H_R, dtype=F32)))
    idx = jnp.arange(c, dtype=F32)
    diff = idx[:, None] - idx[None, :]
    dmask = jnp.where(diff >= 0, jnp.exp(lg[:, None, None] * jnp.maximum(diff, 0.0)), 0.0)
    ones = jnp.ones((1, 1, HEAD_R), F32)
    qdec = jnp.exp(lg[:, None] * (idx + 1.0))[:, :, None] * ones
    kdec = jnp.exp(lg[:, None] * (c - 1.0 - idx))[:, :, None] * ones
    cdec = jnp.exp(lg * c)[:, None, None] * ones
    extra = -c % SUBLANES
    dmask = jnp.pad(dmask, ((0, 0), (0, extra), (0, extra)))
    qdec = jnp.pad(qdec, ((0, 0), (0, extra), (0, 0)))
    kdec = jnp.pad(kdec, ((0, 0), (0, extra), (0, 0)))
    return dmask, qdec, kdec, cdec


def _block_ones(n, block):
    idx = jnp.arange(n) // block
    return (idx[:, None] == idx[None, :]).astype(BF16)


def kernel(x_prompt, x_sample, state_shift, state_wkv, state_ret, norm_g, ffn1_wg, ffn1_wu, ffn1_wd,
           w_in, mu_shift, w0, w2, a0, a2, g2, k_k, k_a, r_k, lnx_w, lnx_b, ret_gn_w, w_out,
           ffn2_wg, ffn2_wu, ffn2_wd):
    assert norm_g.shape[0] == 1, "single-layer configuration"
    bp, tp, _ = x_prompt.shape
    bs, ts, _ = x_sample.shape
    l = 0
    ng = norm_g[l]
    f1 = (ffn1_wg[l].astype(BF16), ffn1_wu[l].astype(BF16), ffn1_wd[l].astype(BF16))
    f2 = (ffn2_wg[l].astype(BF16), ffn2_wu[l].astype(BF16), ffn2_wd[l].astype(BF16))
    win = w_in[l].astype(BF16)
    wo = w_out[l].astype(BF16)
    row = lambda t: t[l].reshape(1, -1)
    zpad = jnp.zeros((LORA_W, D_A), BF16)
    w2p = jnp.concatenate([w2[l].astype(BF16), zpad], axis=0)
    a2p = jnp.concatenate([zpad, a2[l].astype(BF16)], axis=0)
    proj_consts = (ng, win, row(mu_shift), row(w0), w2p, row(a0), a2p, g2[l].astype(BF16),
                   row(k_k), row(k_a), _block_ones(D_A, HEAD_A))
    rk, lnw, lnb, gn = row(r_k), row(lnx_w), row(lnx_b), row(ret_gn_w)
    bd_pair = _block_ones(PAIR, HEAD_A)
    tri = (jnp.arange(WKV_CHUNK)[:, None] >= jnp.arange(WKV_CHUNK)[None, :]).astype(BF16)

    xp = x_prompt.reshape(bp * tp, D_MODEL)
    x1p = _ffn(xp, ng, *f1, 0, 1, 512)
    cos_p, sin_p = _rope_tables(jnp.arange(tp, dtype=jnp.int32))
    tm_p = 256
    (r, lw, k, v, kk, kka, g, q, kr, vr, gr, hl_p) = _proj(
        x1p, None, *proj_consts, cos_p, sin_p, n_t=1, rows_per_t=tm_p, lag=1,
        tiles_per_seq=tp // tm_p, qkv_dtype=BF16)
    ya_p, wkv_p = _wkv_prompt(r, lw, k, v, kk, kka, g, rk, lnw, lnb, bd_pair, tri,
                              batch=bp, seq=tp, tt=256)
    yr_p, ret_p = _ret_prompt(q, kr, vr, gr, *_ret_tables(RET_CHUNK), gn, batch=bp, seq=tp, tt=512)
    yp = _ffn(x1p, ng, *f2, 4, 5, 512, mix=(ya_p, yr_p, wo))

    m_s = bs * ts
    xs = x_sample.transpose(1, 0, 2)
    x1s = _ffn(xs.reshape(m_s, D_MODEL), ng, *f1, 0, 1, m_s)
    cos_s, sin_s = _rope_tables(PAST_LEN + jnp.arange(ts, dtype=jnp.int32))
    rows_per_t = 32
    tab = lambda t: jnp.broadcast_to(t[:, None, :], (ts, bs, HEAD_R))
    outs = _proj(x1s.reshape(ts, bs, D_MODEL),
                 state_shift[l].reshape(bs // rows_per_t, rows_per_t, D_MODEL),
                 *proj_consts, tab(cos_s), tab(sin_s), n_t=ts, rows_per_t=rows_per_t,
                 lag=rows_per_t, tiles_per_seq=1, qkv_dtype=F32)
    (r, lw, k, v, kk, kka, g, q, kr, vr, gr, hl_s) = outs
    ya_s, wkv_s = _wkv_sample(r, lw, k, v, kk, kka, g, rk, lnw, lnb, _block_ones(D_A, HEAD_A),
                              state_wkv[l], bb=SUBLANES)
    yr_s, ret_s = _ret_sample(q, kr, vr, gr, *_ret_tables(min(RET_CHUNK, ts)), gn, state_ret[l],
                              bb=SUBLANES)
    ys = _ffn(x1s, ng, *f2, 4, 5, m_s,
              mix=(ya_s.reshape(m_s, D_A), yr_s.reshape(m_s, D_R), wo))
    ys = ys.reshape(ts, bs, D_MODEL).transpose(1, 0, 2)

    return (yp.reshape(bp, tp, D_MODEL), ys,
            hl_p.reshape(1, bp, D_MODEL), wkv_p[None], ret_p[None],
            hl_s.reshape(1, bs, D_MODEL), wkv_s[None], ret_s[None])
```

```python
import functools

import jax
import jax.numpy as jnp
from jax import lax
from jax.experimental import pallas as pl
from jax.experimental.pallas import tpu as pltpu

F32 = jnp.float32
BF16 = jnp.bfloat16

D_MODEL = 1024
D_A = 512
HEAD_A = 64
H_A = D_A // HEAD_A
D_R = 512
H_R = 4
HEAD_R = D_R // H_R
LORA_W, LORA_A, LORA_G = 64, 64, 128
D_FF = 2816
RET_CHUNK = 128
ROPE_BASE = 10000.0
EPS = 1e-6
GN_EPS_A = 64e-5
GN_EPS_R = 1e-5
N_SHIFT = 3 * D_A + LORA_W + LORA_A + LORA_G
N_COLS = N_SHIFT + 4 * D_R
PAST_LEN = 16384

LANES = 128
SUBLANES = 8
VMEM_LIMIT = 52 * 1024 * 1024

FF_CHUNK = D_FF // 2
WKV_CHUNK = 64
PAIR = 2 * HEAD_A


def _nt(a, b):
    return lax.dot_general(a, b, (((1,), (1,)), ((), ())), preferred_element_type=F32)


def _tn(a, b):
    return lax.dot_general(a, b, (((0,), (0,)), ((), ())), preferred_element_type=F32)


def _mm(a, b):
    return jnp.dot(a, b, preferred_element_type=F32)


def _split_hi_lo(x):
    hi = x.astype(BF16)
    lo = (x - hi.astype(F32)).astype(BF16)
    return hi, lo


def _mm_exact_rhs(x, m):
    hi, lo = _split_hi_lo(x)
    return _mm(hi, m) + _mm(lo, m)


def _mm_exact_lhs(m, x):
    hi, lo = _split_hi_lo(x)
    return _mm(m, hi) + _mm(m, lo)


def _rms(x, g):
    return x * lax.rsqrt(jnp.mean(x * x, axis=-1, keepdims=True) + EPS) * g


def _softplus(x):
    return jnp.maximum(x, 0.0) + jnp.log(1.0 + jnp.exp(-jnp.abs(x)))


def _sigmoid(x):
    return 1.0 / (1.0 + jnp.exp(-x))


def _resident(shape):
    nd = len(shape)
    return pl.BlockSpec(shape, lambda *_: (0,) * nd, pipeline_mode=pl.Buffered(1))


def _params(n_axes):
    return pltpu.CompilerParams(dimension_semantics=("arbitrary",) * n_axes,
                                vmem_limit_bytes=VMEM_LIMIT)


def _ffn_kernel(with_mix, g_in, g_out, *refs):
    if with_mix:
        x_ref, ya_ref, yr_ref, wo_ref, ng_ref, wg_ref, wu_ref, wd_ref, o_ref = refs
    else:
        x_ref, ng_ref, wg_ref, wu_ref, wd_ref, o_ref = refs
    x = x_ref[...]
    if with_mix:
        mix = (_mm(ya_ref[...].astype(BF16), wo_ref[0:D_A, :])
               + _mm(yr_ref[...].astype(BF16), wo_ref[D_A:, :]))
        x = x + _rms(mix, ng_ref[3:4, :])
    h = _rms(x, ng_ref[g_in:g_in + 1, :]).astype(BF16)
    acc = None
    for c in range(D_FF // FF_CHUNK):
        cols = slice(c * FF_CHUNK, (c + 1) * FF_CHUNK)
        gate = _mm(h, wg_ref[:, cols])
        up = _mm(h, wu_ref[:, cols])
        act = (gate * _sigmoid(gate) * up).astype(BF16)
        part = _mm(act, wd_ref[cols, :])
        acc = part if acc is None else acc + part
    o_ref[...] = x + 0.5 * _rms(acc, ng_ref[g_out:g_out + 1, :])


def _ffn(x, ng, wg, wu, wd, g_in, g_out, tm, mix=None):
    m = x.shape[0]
    row = lambda w: pl.BlockSpec((tm, w), lambda i: (i, 0))
    if mix is None:
        args = (x, ng, wg, wu, wd)
        specs = [row(D_MODEL), _resident(ng.shape), _resident(wg.shape), _resident(wu.shape),
                 _resident(wd.shape)]
    else:
        ya, yr, wo = mix
        args = (x, ya, yr, wo, ng, wg, wu, wd)
        specs = [row(D_MODEL), row(D_A), row(D_R), _resident(wo.shape), _resident(ng.shape),
                 _resident(wg.shape), _resident(wu.shape), _resident(wd.shape)]
    return pl.pallas_call(
        functools.partial(_ffn_kernel, mix is not None, g_in, g_out),
        grid=(m // tm,),
        in_specs=specs,
        out_specs=row(D_MODEL),
        out_shape=jax.ShapeDtypeStruct((m, D_MODEL), F32),
        compiler_params=_params(1),
        name="ffn_mix" if mix is not None else "ffn",
    )(*args)


def _proj_kernel(n_t, lag, pad, tiles_per_seq, has_prev, *refs):
    it = iter(refs)
    x_ref = next(it)
    prev_ref = next(it) if has_prev else None
    (ng_ref, win_ref, mu_ref, w0_ref, w2_ref, a0_ref, a2_ref, g2_ref, kk_ref, ka_ref, bd_ref,
     cos_ref, sin_ref,
     r_o, lw_o, k_o, v_o, kk_o, kka_o, g_o, q_o, kr_o, vr_o, gr_o, hl_o, ps_scr) = tuple(it)

    def load(ref):
        if n_t == 1:
            return ref[...]
        return jnp.concatenate([ref[t] for t in range(n_t)], axis=0)

    def store(ref, val):
        if n_t == 1:
            ref[...] = val.astype(ref.dtype)
        else:
            rows = val.shape[0] // n_t
            for t in range(n_t):
                ref[t] = val[t * rows:(t + 1) * rows].astype(ref.dtype)

    x = load(x_ref)
    tm = x.shape[0]
    h = _rms(x, ng_ref[2:3, :])
    hl_o[0] = h[tm - lag:, :]
    hb = h.astype(BF16)

    seq_start = (pl.program_id(0) % tiles_per_seq) == 0
    if has_prev:
        @pl.when(seq_start)
        def _():
            ps_scr[pad - lag:pad, :] = _mm(prev_ref[0].astype(BF16), win_ref[:, 0:N_SHIFT])
    else:
        @pl.when(seq_start)
        def _():
            ps_scr[pad - lag:pad, :] = jnp.zeros((lag, N_SHIFT), F32)
    ps_scr[pad:pad + tm, :] = _mm(hb, win_ref[:, 0:N_SHIFT])
    cur = ps_scr[pad:pad + tm, :]
    prv = ps_scr[pad - lag:pad - lag + tm, :]
    mixed = cur + (prv - cur) * mu_ref[...]
    ps_scr[pad - lag:pad, :] = cur[tm - lag:, :]

    r = mixed[:, 0:D_A]
    k = mixed[:, D_A:2 * D_A]
    v = mixed[:, 2 * D_A:3 * D_A]
    wa = mixed[:, 3 * D_A:3 * D_A + LORA_W + LORA_A]
    gd = mixed[:, 3 * D_A + LORA_W + LORA_A:N_SHIFT]
    w_pre = w0_ref[...] + _mm(jnp.tanh(wa).astype(BF16), w2_ref[...])
    w_log = -_softplus(-w_pre) - 0.5
    lw = -jnp.exp(w_log)
    a = _sigmoid(a0_ref[...] + _mm(wa.astype(BF16), a2_ref[...]))
    g = _mm(_sigmoid(gd).astype(BF16), g2_ref[...])
    kk = k * kk_ref[...]
    ss = _mm_exact_rhs(kk * kk, bd_ref[...])
    kk = kk / jnp.maximum(jnp.sqrt(ss), 1e-12)
    k_mod = k * (1.0 + (a - 1.0) * ka_ref[...])
    store(r_o, r)
    store(lw_o, lw)
    store(k_o, k_mod)
    store(v_o, v)
    store(kk_o, kk)
    store(kka_o, kk * a)
    store(g_o, g)

    pr = _mm(hb, win_ref[:, N_SHIFT:])
    cos2 = load(cos_ref)
    sin2 = load(sin_ref)

    def rope(t):
        parts = []
        for hh in range(H_R):
            th = t[:, hh * HEAD_R:(hh + 1) * HEAD_R]
            parts.append(th * cos2 + pltpu.roll(th, HEAD_R // 2, 1) * sin2)
        return jnp.concatenate(parts, axis=1)

    store(q_o, rope(pr[:, 0:D_R]))
    store(kr_o, rope(pr[:, D_R:2 * D_R]) * (HEAD_R ** -0.5))
    store(vr_o, pr[:, 2 * D_R:3 * D_R])
    store(gr_o, pr[:, 3 * D_R:])


def _proj(x, prev, ng, win, mu, w0, w2p, a0, a2p, g2, k_k, k_a, bd, cos2, sin2, *,
          n_t, rows_per_t, lag, tiles_per_seq, qkv_dtype):
    tm = n_t * rows_per_t
    pad = max(lag, SUBLANES)
    if n_t == 1:
        m = x.shape[0]
        n_tiles = m // tm
        row = lambda w: pl.BlockSpec((tm, w), lambda i: (i, 0))
        shp = lambda w, dt: jax.ShapeDtypeStruct((m, w), dt)
        tab = pl.BlockSpec((tm, HEAD_R), lambda i: (i % tiles_per_seq, 0))
    else:
        m = x.shape[0] * x.shape[1]
        n_tiles = x.shape[1] // rows_per_t
        row = lambda w: pl.BlockSpec((n_t, rows_per_t, w), lambda i: (0, i, 0))
        shp = lambda w, dt: jax.ShapeDtypeStruct((n_t, m // n_t, w), dt)
        tab = pl.BlockSpec((n_t, rows_per_t, HEAD_R), lambda i: (0, i, 0))
    n_seq = n_tiles // tiles_per_seq
    hl_spec = pl.BlockSpec((1, lag, D_MODEL), lambda i: (i // tiles_per_seq, 0, 0))
    args = [x]
    specs = [row(D_MODEL)]
    if prev is not None:
        args.append(prev)
        specs.append(pl.BlockSpec((1, lag, D_MODEL), lambda i: (i // tiles_per_seq, 0, 0)))
    consts = (ng, win, mu, w0, w2p, a0, a2p, g2, k_k, k_a, bd)
    args += list(consts) + [cos2, sin2]
    specs += [_resident(c.shape) for c in consts] + [tab, tab]
    out_shape = ([shp(D_A, F32)] * 7 + [shp(D_R, qkv_dtype)] * 3 + [shp(D_R, F32)]
                 + [jax.ShapeDtypeStruct((n_seq, lag, D_MODEL), F32)])
    out_specs = [row(D_A)] * 7 + [row(D_R)] * 4 + [hl_spec]
    return pl.pallas_call(
        functools.partial(_proj_kernel, n_t, lag, pad, tiles_per_seq, prev is not None),
        grid=(n_tiles,),
        in_specs=specs,
        out_specs=out_specs,
        out_shape=out_shape,
        scratch_shapes=[pltpu.VMEM((pad + tm, N_SHIFT), F32)],
        compiler_params=_params(1),
        name="proj",
    )(*args)


def _wkv_post(y, r, k, v, g, rk, lw_g, lb_g, bd):
    inv_n = 1.0 / HEAD_A
    mu = _mm_exact_rhs(y, bd) * inv_n
    d = y - mu
    var = _mm_exact_rhs(d * d, bd) * inv_n
    yn = d * lax.rsqrt(var + GN_EPS_A) * lw_g + lb_g
    bonus = _mm_exact_rhs(r * k * rk, bd) * v
    return (yn + bonus) * g


def _wkv_prompt_kernel(tt, r_ref, lw_ref, k_ref, v_ref, kk_ref, kka_ref, g_ref,
                       rk_ref, lnw_ref, lnb_ref, bd_ref, tri_ref,
                       ya_o, s_o, s_scr):
    c = WKV_CHUNK
    j = pl.program_id(1)

    @pl.when(j == 0)
    def _():
        s_scr[...] = jnp.zeros_like(s_scr)

    lane = lax.broadcasted_iota(jnp.int32, (c, PAIR), 1)
    first = lane < HEAD_A

    def stack(x):
        return jnp.concatenate([jnp.where(first, x, 0.0), jnp.where(first, 0.0, x)], axis=0)

    ri = lax.broadcasted_iota(jnp.int32, (2 * c, 2 * c), 0) % c
    ci = lax.broadcasted_iota(jnp.int32, (2 * c, 2 * c), 1) % c
    strict = ri > ci
    incl = ri >= ci
    eye = (lax.broadcasted_iota(jnp.int32, (2 * c, 2 * c), 0)
           == lax.broadcasted_iota(jnp.int32, (2 * c, 2 * c), 1)).astype(F32)
    bd = bd_ref[...]
    tri = tri_ref[...]

    pairs = range(H_A // 2)
    slab = [slice(p * PAIR, (p + 1) * PAIR) for p in pairs]

    def chunk_body(ch, carry):
        rows = pl.ds(pl.multiple_of(ch * c, c), c)
        ld = lambda ref: [ref[rows, slab[p]] for p in pairs]
        r, lw, k, v, kk, kka = ld(r_ref), ld(lw_ref), ld(k_ref), ld(v_ref), ld(kk_ref), ld(kka_ref)
        cum = [_mm_exact_lhs(tri, x) for x in lw]
        cum_end = [x[c - 1:c, :] for x in cum]
        e_pos = [jnp.exp(x) for x in cum]
        e_neg = [jnp.exp(-x) for x in cum]
        e_end = [jnp.exp(x - y) for x, y in zip(cum_end, cum)]
        xs = [jnp.concatenate([stack(-kk[p] * jnp.exp(cum[p] - lw[p])), stack(r[p] * e_pos[p])],
                              axis=0).astype(BF16) for p in pairs]
        ws = [jnp.concatenate([stack(kka[p] * e_neg[p]), stack(k[p] * e_neg[p])],
                              axis=0).astype(BF16) for p in pairs]
        we = [jnp.concatenate([stack(kka[p] * e_end[p]), stack(k[p] * e_end[p])],
                              axis=0).astype(BF16) for p in pairs]
        vs = [stack(x).astype(BF16) for x in v]
        gram = [_nt(xs[p], ws[p]) for p in pairs]
        a_ab = [jnp.where(strict, g[0:2 * c, 0:2 * c], 0.0) for g in gram]
        a_ak = [jnp.where(strict, g[0:2 * c, 2 * c:], 0.0).astype(BF16) for g in gram]
        a_r = [jnp.concatenate([jnp.where(incl, g[2 * c:, 0:2 * c], 0.0),
                                jnp.where(incl, g[2 * c:, 2 * c:], 0.0)], axis=1).astype(BF16)
               for g in gram]
        inv = [eye + a for a in a_ab]
        pw = [x.astype(BF16) for x in a_ab]
        pw = [_mm(x, x).astype(BF16) for x in pw]
        n_lvl = WKV_CHUNK.bit_length() - 2
        for lvl in range(n_lvl):
            if lvl < n_lvl - 1:
                both = [_mm(jnp.concatenate([i.astype(BF16), x], axis=0), x) for i, x in zip(inv, pw)]
                inv = [i + b[0:2 * c] for i, b in zip(inv, both)]
                pw = [b[2 * c:].astype(BF16) for b in both]
            else:
                inv = [i + _mm(i.astype(BF16), x) for i, x in zip(inv, pw)]
        s = [s_scr[p] for p in pairs]
        z = [_nt(xs[p], s[p].astype(BF16)) for p in pairs]
        rhs = [z[p][0:2 * c] + _mm(a_ak[p], vs[p]) for p in pairs]
        u = [_mm(inv[p].astype(BF16), rhs[p].astype(BF16)).astype(BF16) for p in pairs]
        uv = [jnp.concatenate([u[p], vs[p]], axis=0) for p in pairs]
        y2 = [z[p][2 * c:] + _mm(a_r[p], uv[p]) for p in pairs]
        for p in pairs:
            s_scr[p] = s[p] * jnp.exp(cum_end[p]) + _tn(uv[p], we[p])
        for p in pairs:
            y = y2[p][0:c] + y2[p][c:]
            out = _wkv_post(y, r[p], k[p], v[p], g_ref[rows, slab[p]], rk_ref[:, slab[p]],
                            lnw_ref[:, slab[p]], lnb_ref[:, slab[p]], bd)
            ya_o[rows, slab[p]] = out.astype(ya_o.dtype)
        return carry

    lax.fori_loop(0, tt // c, chunk_body, 0, unroll=2)

    @pl.when(j == pl.num_programs(1) - 1)
    def _():
        for p in range(H_A // 2):
            s = s_scr[p]
            s_o[0, 2 * p] = s[0:HEAD_A, 0:HEAD_A]
            s_o[0, 2 * p + 1] = s[HEAD_A:, HEAD_A:]


def _wkv_prompt(r, lw, k, v, kk, kka, g, rk, lnw, lnb, bd, tri, *, batch, seq, tt):
    tiles = seq // tt
    row = pl.BlockSpec((tt, D_A), lambda b, j: (b * tiles + j, 0))
    consts = (rk, lnw, lnb, bd, tri)
    return pl.pallas_call(
        functools.partial(_wkv_prompt_kernel, tt),
        grid=(batch, tiles),
        in_specs=[row] * 7 + [_resident(x.shape) for x in consts],
        out_specs=[row, pl.BlockSpec((1, H_A, HEAD_A, HEAD_A), lambda b, j: (b, 0, 0, 0))],
        out_shape=[jax.ShapeDtypeStruct((batch * seq, D_A), BF16),
                   jax.ShapeDtypeStruct((batch, H_A, HEAD_A, HEAD_A), F32)],
        scratch_shapes=[pltpu.VMEM((H_A // 2, PAIR, PAIR), F32)],
        compiler_params=_params(2),
        name="wkv_prompt",
    )(r, lw, k, v, kk, kka, g, *consts)


def _wkv_sample_kernel(n_t, bb, r_ref, lw_ref, k_ref, v_ref, kk_ref, kka_ref, g_ref,
                       rk_ref, lnw_ref, lnb_ref, bd_ref, s_ref,
                       ya_o, s_o, y_scr):
    n = HEAD_A
    units = [(bi, p) for bi in range(bb) for p in range(H_A // 2)]
    bd2 = bd_ref[0:PAIR, 0:PAIR]
    diag = (lax.broadcasted_iota(jnp.int32, (n, PAIR), 1) % n
            == lax.broadcasted_iota(jnp.int32, (n, PAIR), 0))

    def row_of(ref, t, bi, p):
        return jnp.broadcast_to(ref[t, bi:bi + 1, p * PAIR:(p + 1) * PAIR], (n, PAIR))

    def head_sums(xs, exact):
        tall = jnp.concatenate(xs, axis=0)
        out = _mm_exact_rhs(tall, bd2) if exact else _mm(tall.astype(BF16), bd2)
        return [out[i * n:(i + 1) * n] for i in range(len(xs))]

    s2 = [jnp.concatenate([s_ref[bi, 2 * p], s_ref[bi, 2 * p + 1]], axis=1) for bi, p in units]
    for t in range(n_t):
        sa = head_sums([s * -row_of(kk_ref, t, *u) for s, u in zip(s2, units)], False)
        v_col = head_sums([jnp.where(diag, row_of(v_ref, t, *u), 0.0) for u in units], True)
        s2 = [s * jnp.exp(row_of(lw_ref, t, *u)) + a * row_of(kka_ref, t, *u)
              + vc * row_of(k_ref, t, *u) for s, a, vc, u in zip(s2, sa, v_col, units)]
        y_bc = head_sums([s * row_of(r_ref, t, *u) for s, u in zip(s2, units)], False)
        for y, (bi, p) in zip(y_bc, units):
            y_scr[t, bi:bi + 1, p * PAIR:(p + 1) * PAIR] = jnp.sum(
                jnp.where(diag, y, 0.0), axis=0, keepdims=True)
    for s, (bi, p) in zip(s2, units):
        s_o[bi, 2 * p] = s[:, 0:n]
        s_o[bi, 2 * p + 1] = s[:, n:]

    rows_of = lambda ref: jnp.concatenate([ref[t] for t in range(n_t)], axis=0)
    out = _wkv_post(rows_of(y_scr), rows_of(r_ref), rows_of(k_ref), rows_of(v_ref), rows_of(g_ref),
                    rk_ref[...], lnw_ref[...], lnb_ref[...], bd_ref[...])
    for t in range(n_t):
        ya_o[t] = out[t * bb:(t + 1) * bb].astype(ya_o.dtype)


def _wkv_sample(r, lw, k, v, kk, kka, g, rk, lnw, lnb, bd, s0, *, bb):
    n_t, n_b, _ = r.shape
    consts = (rk, lnw, lnb, bd)
    blk = pl.BlockSpec((n_t, bb, D_A), lambda i: (0, i, 0))
    st = pl.BlockSpec((bb, H_A, HEAD_A, HEAD_A), lambda i: (i, 0, 0, 0))
    return pl.pallas_call(
        functools.partial(_wkv_sample_kernel, n_t, bb),
        grid=(n_b // bb,),
        in_specs=[blk] * 7 + [_resident(x.shape) for x in consts] + [st],
        out_specs=[blk, st],
        out_shape=[jax.ShapeDtypeStruct((n_t, n_b, D_A), F32),
                   jax.ShapeDtypeStruct(s0.shape, F32)],
        scratch_shapes=[pltpu.VMEM((n_t, bb, D_A), F32)],
        compiler_params=_params(1),
        name="wkv_sample",
    )(r, lw, k, v, kk, kka, g, *consts, s0)


def _ret_head(q, k, v, g, s, dmask, qdec, kdec, cdec, gn):
    qb, kb, vb = q.astype(BF16), k.astype(BF16), v.astype(BF16)
    inner = _nt(qb, kb) * dmask
    y = _mm(inner.astype(BF16), vb) + _mm((q.astype(F32) * qdec).astype(BF16), s.astype(BF16))
    s_new = s * cdec + _tn((k.astype(F32) * kdec).astype(BF16), vb)
    mu = jnp.mean(y, axis=-1, keepdims=True)
    d = y - mu
    var = jnp.mean(d * d, axis=-1, keepdims=True)
    yn = d * lax.rsqrt(var + GN_EPS_R) * gn
    return g * _sigmoid(g) * yn, s_new


def _ret_prompt_kernel(tt, q_ref, k_ref, v_ref, g_ref, dm_ref, qd_ref, kd_ref, cd_ref, gn_ref,
                       y_o, s_o, s_scr):
    j = pl.program_id(1)

    @pl.when(j == 0)
    def _():
        s_scr[...] = jnp.zeros_like(s_scr)

    c = RET_CHUNK
    for ch in range(tt // c):
        rows = slice(ch * c, (ch + 1) * c)
        for hh in range(H_R):
            lanes = slice(hh * HEAD_R, (hh + 1) * HEAD_R)
            y, s_new = _ret_head(q_ref[rows, lanes], k_ref[rows, lanes], v_ref[rows, lanes],
                                 g_ref[rows, lanes], s_scr[hh], dm_ref[hh], qd_ref[hh], kd_ref[hh],
                                 cd_ref[hh], gn_ref[:, lanes])
            s_scr[hh] = s_new
            y_o[rows, lanes] = y.astype(y_o.dtype)

    @pl.when(j == pl.num_programs(1) - 1)
    def _():
        s_o[0] = s_scr[...]


def _ret_prompt(q, k, v, g, dm, qd, kd, cd, gn, *, batch, seq, tt):
    tiles = seq // tt
    row = pl.BlockSpec((tt, D_R), lambda b, j: (b * tiles + j, 0))
    consts = (dm, qd, kd, cd, gn)
    return pl.pallas_call(
        functools.partial(_ret_prompt_kernel, tt),
        grid=(batch, tiles),
        in_specs=[row] * 4 + [_resident(x.shape) for x in consts],
        out_specs=[row, pl.BlockSpec((1, H_R, HEAD_R, HEAD_R), lambda b, j: (b, 0, 0, 0))],
        out_shape=[jax.ShapeDtypeStruct((batch * seq, D_R), BF16),
                   jax.ShapeDtypeStruct((batch, H_R, HEAD_R, HEAD_R), F32)],
        scratch_shapes=[pltpu.VMEM((H_R, HEAD_R, HEAD_R), F32)],
        compiler_params=_params(2),
        name="ret_prompt",
    )(q, k, v, g, *consts)


def _ret_sample_kernel(n_t, bb, q_ref, k_ref, v_ref, g_ref, dm_ref, qd_ref, kd_ref, cd_ref,
                       gn_ref, s_ref, y_o, s_o):
    rid = lax.broadcasted_iota(jnp.int32, (SUBLANES, HEAD_R), 0)

    def seq_rows(ref, bi, lanes):
        out = jnp.zeros((SUBLANES, HEAD_R), F32)
        for t in range(n_t):
            out = jnp.where(rid == t, jnp.broadcast_to(ref[t, bi:bi + 1, lanes], out.shape), out)
        return out

    for bi in range(bb):
        for hh in range(H_R):
            lanes = slice(hh * HEAD_R, (hh + 1) * HEAD_R)
            y, s_new = _ret_head(seq_rows(q_ref, bi, lanes), seq_rows(k_ref, bi, lanes),
                                 seq_rows(v_ref, bi, lanes), seq_rows(g_ref, bi, lanes),
                                 s_ref[bi, hh], dm_ref[hh], qd_ref[hh], kd_ref[hh], cd_ref[hh],
                                 gn_ref[:, lanes])
            s_o[bi, hh] = s_new
            for t in range(n_t):
                y_o[t, bi:bi + 1, lanes] = y[t:t + 1].astype(y_o.dtype)


def _ret_sample(q, k, v, g, dm, qd, kd, cd, gn, s0, *, bb):
    n_t, n_b, _ = q.shape
    consts = (dm, qd, kd, cd, gn)
    blk = pl.BlockSpec((n_t, bb, D_R), lambda i: (0, i, 0))
    st = pl.BlockSpec((bb, H_R, HEAD_R, HEAD_R), lambda i: (i, 0, 0, 0))
    return pl.pallas_call(
        functools.partial(_ret_sample_kernel, n_t, bb),
        grid=(n_b // bb,),
        in_specs=[blk] * 4 + [_resident(x.shape) for x in consts] + [st],
        out_specs=[blk, st],
        out_shape=[jax.ShapeDtypeStruct((n_t, n_b, D_R), F32),
                   jax.ShapeDtypeStruct(s0.shape, F32)],
        compiler_params=_params(1),
        name="ret_sample",
    )(q, k, v, g, *consts, s0)


def _rope_tables(pos):
    half = HEAD_R // 2
    inv = ROPE_BASE ** (-jnp.arange(half, dtype=F32) / half)
    ang = pos.astype(F32)[:, None] * inv[None, :]
    cos, sin = jnp.cos(ang), jnp.sin(ang)
    return jnp.concatenate([cos, cos], axis=1), jnp.concatenate([-sin, sin], axis=1)


def _ret_tables(c):
    lg = jnp.log1p(-jnp.exp2(-5.0 - jnp.arange(H_R, dtype=F32)))
    idx = jnp.arange(c, dtype=F32)
    diff = idx[:, None] - idx[None, :]
    dmask = jnp.where(diff >= 0, jnp.exp(lg[:, None, None] * jnp.maximum(diff, 0.0)), 0.0)
    ones = jnp.ones((1, 1, HEAD_R), F32)
    qdec = jnp.exp(lg[:, None] * (idx + 1.0))[:, :, None] * ones
    kdec = jnp.exp(lg[:, None] * (c - 1.0 - idx))[:, :, None] * ones
    cdec = jnp.exp(lg * c)[:, None, None] * ones
    extra = -c % SUBLANES
    dmask = jnp.pad(dmask, ((0, 0), (0, extra), (0, extra)))
    qdec = jnp.pad(qdec, ((0, 0), (0, extra), (0, 0)))
    kdec = jnp.pad(kdec, ((0, 0), (0, extra), (0, 0)))
    return dmask, qdec, kdec, cdec


def _block_ones(n, block):
    idx = jnp.arange(n) // block
    return (idx[:, None] == idx[None, :]).astype(BF16)


def kernel(x_prompt, x_sample, state_shift, state_wkv, state_ret, norm_g, ffn1_wg, ffn1_wu, ffn1_wd,
           w_in, mu_shift, w0, w2, a0, a2, g2, k_k, k_a, r_k, lnx_w, lnx_b, ret_gn_w, w_out,
           ffn2_wg, ffn2_wu, ffn2_wd):
    assert norm_g.shape[0] == 1, "single-layer configuration"
    bp, tp, _ = x_prompt.shape
    bs, ts, _ = x_sample.shape
    l = 0
    ng = norm_g[l]
    f1 = (ffn1_wg[l].astype(BF16), ffn1_wu[l].astype(BF16), ffn1_wd[l].astype(BF16))
    f2 = (ffn2_wg[l].astype(BF16), ffn2_wu[l].astype(BF16), ffn2_wd[l].astype(BF16))
    win = w_in[l].astype(BF16)
    wo = w_out[l].astype(BF16)
    row = lambda t: t[l].reshape(1, -1)
    zpad = jnp.zeros((LORA_W, D_A), BF16)
    w2p = jnp.concatenate([w2[l].astype(BF16), zpad], axis=0)
    a2p = jnp.concatenate([zpad, a2[l].astype(BF16)], axis=0)
    proj_consts = (ng, win, row(mu_shift), row(w0), w2p, row(a0), a2p, g2[l].astype(BF16),
                   row(k_k), row(k_a), _block_ones(D_A, HEAD_A))
    rk, lnw, lnb, gn = row(r_k), row(lnx_w), row(lnx_b), row(ret_gn_w)
    bd_pair = _block_ones(PAIR, HEAD_A)
    tri = (jnp.arange(WKV_CHUNK)[:, None] >= jnp.arange(WKV_CHUNK)[None, :]).astype(BF16)

    xp = x_prompt.reshape(bp * tp, D_MODEL)
    x1p = _ffn(xp, ng, *f1, 0, 1, 512)
    cos_p, sin_p = _rope_tables(jnp.arange(tp, dtype=jnp.int32))
    tm_p = 256
    (r, lw, k, v, kk, kka, g, q, kr, vr, gr, hl_p) = _proj(
        x1p, None, *proj_consts, cos_p, sin_p, n_t=1, rows_per_t=tm_p, lag=1,
        tiles_per_seq=tp // tm_p, qkv_dtype=BF16)
    ya_p, wkv_p = _wkv_prompt(r, lw, k, v, kk, kka, g, rk, lnw, lnb, bd_pair, tri,
                              batch=bp, seq=tp, tt=256)
    yr_p, ret_p = _ret_prompt(q, kr, vr, gr, *_ret_tables(RET_CHUNK), gn, batch=bp, seq=tp, tt=512)
    yp = _ffn(x1p, ng, *f2, 4, 5, 512, mix=(ya_p, yr_p, wo))

    m_s = bs * ts
    xs = x_sample.transpose(1, 0, 2)
    x1s = _ffn(xs.reshape(m_s, D_MODEL), ng, *f1, 0, 1, m_s)
    cos_s, sin_s = _rope_tables(PAST_LEN + jnp.arange(ts, dtype=jnp.int32))
    rows_per_t = 32
    tab = lambda t: jnp.broadcast_to(t[:, None, :], (ts, bs, HEAD_R))
    outs = _proj(x1s.reshape(ts, bs, D_MODEL),
                 state_shift[l].reshape(bs // rows_per_t, rows_per_t, D_MODEL),
                 *proj_consts, tab(cos_s), tab(sin_s), n_t=ts, rows_per_t=rows_per_t,
                 lag=rows_per_t, tiles_per_seq=1, qkv_dtype=F32)
    (r, lw, k, v, kk, kka, g, q, kr, vr, gr, hl_s) = outs
    ya_s, wkv_s = _wkv_sample(r, lw, k, v, kk, kka, g, rk, lnw, lnb, _block_ones(D_A, HEAD_A),
                              state_wkv[l], bb=SUBLANES)
    yr_s, ret_s = _ret_sample(q, kr, vr, gr, *_ret_tables(min(RET_CHUNK, ts)), gn, state_ret[l],
                              bb=SUBLANES)
    ys = _ffn(x1s, ng, *f2, 4, 5, m_s,
              mix=(ya_s.reshape(m_s, D_A), yr_s.reshape(m_s, D_R), wo))
    ys = ys.reshape(ts, bs, D_MODEL).transpose(1, 0, 2)

    return (yp.reshape(bp, tp, D_MODEL), ys,
            hl_p.reshape(1, bp, D_MODEL), wkv_p[None], ret_p[None],
            hl_s.reshape(1, bs, D_MODEL), wkv_s[None], ret_s[None])
```

```python
import functools

import jax
import jax.numpy as jnp
from jax import lax
from jax.experimental import pallas as pl
from jax.experimental.pallas import tpu as pltpu

F32 = jnp.float32
BF16 = jnp.bfloat16

D_MODEL = 1024
D_A = 512
HEAD_A = 64
H_A = D_A // HEAD_A
D_R = 512
H_R = 4
HEAD_R = D_R // H_R
LORA_W, LORA_A, LORA_G = 64, 64, 128
D_FF = 2816
RET_CHUNK = 128
ROPE_BASE = 10000.0
EPS = 1e-6
GN_EPS_A = 64e-5
GN_EPS_R = 1e-5
N_SHIFT = 3 * D_A + LORA_W + LORA_A + LORA_G
N_COLS = N_SHIFT + 4 * D_R
PAST_LEN = 16384

LANES = 128
SUBLANES = 8
VMEM_LIMIT = 52 * 1024 * 1024

FF_CHUNK = D_FF // 2
WKV_CHUNK = 64
PAIR = 2 * HEAD_A


def _nt(a, b):
    return lax.dot_general(a, b, (((1,), (1,)), ((), ())), preferred_element_type=F32)


def _tn(a, b):
    return lax.dot_general(a, b, (((0,), (0,)), ((), ())), preferred_element_type=F32)


def _mm(a, b):
    return jnp.dot(a, b, preferred_element_type=F32)


def _split_hi_lo(x):
    hi = x.astype(BF16)
    lo = (x - hi.astype(F32)).astype(BF16)
    return hi, lo


def _mm_exact_rhs(x, m):
    hi, lo = _split_hi_lo(x)
    return _mm(hi, m) + _mm(lo, m)


def _mm_exact_lhs(m, x):
    hi, lo = _split_hi_lo(x)
    return _mm(m, hi) + _mm(m, lo)


def _rms(x, g):
    return x * lax.rsqrt(jnp.mean(x * x, axis=-1, keepdims=True) + EPS) * g


def _softplus(x):
    return jnp.maximum(x, 0.0) + jnp.log(1.0 + jnp.exp(-jnp.abs(x)))


def _sigmoid(x):
    return 1.0 / (1.0 + jnp.exp(-x))


def _resident(shape):
    nd = len(shape)
    return pl.BlockSpec(shape, lambda *_: (0,) * nd, pipeline_mode=pl.Buffered(1))


def _params(n_axes):
    return pltpu.CompilerParams(dimension_semantics=("arbitrary",) * n_axes,
                                vmem_limit_bytes=VMEM_LIMIT)


def _ffn_kernel(with_mix, g_in, g_out, *refs):
    if with_mix:
        x_ref, ya_ref, yr_ref, wo_ref, ng_ref, wg_ref, wu_ref, wd_ref, o_ref = refs
    else:
        x_ref, ng_ref, wg_ref, wu_ref, wd_ref, o_ref = refs
    x = x_ref[...]
    if with_mix:
        mix = (_mm(ya_ref[...].astype(BF16), wo_ref[0:D_A, :])
               + _mm(yr_ref[...].astype(BF16), wo_ref[D_A:, :]))
        x = x + _rms(mix, ng_ref[3:4, :])
    h = _rms(x, ng_ref[g_in:g_in + 1, :]).astype(BF16)
    acc = None
    for c in range(D_FF // FF_CHUNK):
        cols = slice(c * FF_CHUNK, (c + 1) * FF_CHUNK)
        gate = _mm(h, wg_ref[:, cols])
        up = _mm(h, wu_ref[:, cols])
        act = (gate * _sigmoid(gate) * up).astype(BF16)
        part = _mm(act, wd_ref[cols, :])
        acc = part if acc is None else acc + part
    o_ref[...] = x + 0.5 * _rms(acc, ng_ref[g_out:g_out + 1, :])


def _ffn(x, ng, wg, wu, wd, g_in, g_out, tm, mix=None):
    m = x.shape[0]
    row = lambda w: pl.BlockSpec((tm, w), lambda i: (i, 0))
    if mix is None:
        args = (x, ng, wg, wu, wd)
        specs = [row(D_MODEL), _resident(ng.shape), _resident(wg.shape), _resident(wu.shape),
                 _resident(wd.shape)]
    else:
        ya, yr, wo = mix
        args = (x, ya, yr, wo, ng, wg, wu, wd)
        specs = [row(D_MODEL), row(D_A), row(D_R), _resident(wo.shape), _resident(ng.shape),
                 _resident(wg.shape), _resident(wu.shape), _resident(wd.shape)]
    return pl.pallas_call(
        functools.partial(_ffn_kernel, mix is not None, g_in, g_out),
        grid=(m // tm,),
        in_specs=specs,
        out_specs=row(D_MODEL),
        out_shape=jax.ShapeDtypeStruct((m, D_MODEL), F32),
        compiler_params=_params(1),
        name="ffn_mix" if mix is not None else "ffn",
    )(*args)


def _proj_kernel(n_t, lag, pad, tiles_per_seq, has_prev, *refs):
    it = iter(refs)
    x_ref = next(it)
    prev_ref = next(it) if has_prev else None
    (ng_ref, win_ref, mu_ref, w0_ref, w2_ref, a0_ref, a2_ref, g2_ref, kk_ref, ka_ref, bd_ref,
     cos_ref, sin_ref,
     r_o, lw_o, k_o, v_o, kk_o, kka_o, g_o, q_o, kr_o, vr_o, gr_o, hl_o, ps_scr) = tuple(it)

    def load(ref):
        if n_t == 1:
            return ref[...]
        return jnp.concatenate([ref[t] for t in range(n_t)], axis=0)

    def store(ref, val):
        if n_t == 1:
            ref[...] = val.astype(ref.dtype)
        else:
            rows = val.shape[0] // n_t
            for t in range(n_t):
                ref[t] = val[t * rows:(t + 1) * rows].astype(ref.dtype)

    x = load(x_ref)
    tm = x.shape[0]
    h = _rms(x, ng_ref[2:3, :])
    hl_o[0] = h[tm - lag:, :]
    hb = h.astype(BF16)

    seq_start = (pl.program_id(0) % tiles_per_seq) == 0
    if has_prev:
        @pl.when(seq_start)
        def _():
            ps_scr[pad - lag:pad, :] = _mm(prev_ref[0].astype(BF16), win_ref[:, 0:N_SHIFT])
    else:
        @pl.when(seq_start)
        def _():
            ps_scr[pad - lag:pad, :] = jnp.zeros((lag, N_SHIFT), F32)
    ps_scr[pad:pad + tm, :] = _mm(hb, win_ref[:, 0:N_SHIFT])
    cur = ps_scr[pad:pad + tm, :]
    prv = ps_scr[pad - lag:pad - lag + tm, :]
    mixed = cur + (prv - cur) * mu_ref[...]
    ps_scr[pad - lag:pad, :] = cur[tm - lag:, :]

    r = mixed[:, 0:D_A]
    k = mixed[:, D_A:2 * D_A]
    v = mixed[:, 2 * D_A:3 * D_A]
    wa = mixed[:, 3 * D_A:3 * D_A + LORA_W + LORA_A]
    gd = mixed[:, 3 * D_A + LORA_W + LORA_A:N_SHIFT]
    w_pre = w0_ref[...] + _mm(jnp.tanh(wa).astype(BF16), w2_ref[...])
    w_log = -_softplus(-w_pre) - 0.5
    lw = -jnp.exp(w_log)
    a = _sigmoid(a0_ref[...] + _mm(wa.astype(BF16), a2_ref[...]))
    g = _mm(_sigmoid(gd).astype(BF16), g2_ref[...])
    kk = k * kk_ref[...]
    ss = _mm_exact_rhs(kk * kk, bd_ref[...])
    kk = kk / jnp.maximum(jnp.sqrt(ss), 1e-12)
    k_mod = k * (1.0 + (a - 1.0) * ka_ref[...])
    store(r_o, r)
    store(lw_o, lw)
    store(k_o, k_mod)
    store(v_o, v)
    store(kk_o, kk)
    store(kka_o, kk * a)
    store(g_o, g)

    pr = _mm(hb, win_ref[:, N_SHIFT:])
    cos2 = load(cos_ref)
    sin2 = load(sin_ref)

    def rope(t):
        parts = []
        for hh in range(H_R):
            th = t[:, hh * HEAD_R:(hh + 1) * HEAD_R]
            parts.append(th * cos2 + pltpu.roll(th, HEAD_R // 2, 1) * sin2)
        return jnp.concatenate(parts, axis=1)

    store(q_o, rope(pr[:, 0:D_R]))
    store(kr_o, rope(pr[:, D_R:2 * D_R]) * (HEAD_R ** -0.5))
    store(vr_o, pr[:, 2 * D_R:3 * D_R])
    store(gr_o, pr[:, 3 * D_R:])


def _proj(x, prev, ng, win, mu, w0, w2p, a0, a2p, g2, k_k, k_a, bd, cos2, sin2, *,
          n_t, rows_per_t, lag, tiles_per_seq, qkv_dtype):
    tm = n_t * rows_per_t
    pad = max(lag, SUBLANES)
    if n_t == 1:
        m = x.shape[0]
        n_tiles = m // tm
        row = lambda w: pl.BlockSpec((tm, w), lambda i: (i, 0))
        shp = lambda w, dt: jax.ShapeDtypeStruct((m, w), dt)
        tab = pl.BlockSpec((tm, HEAD_R), lambda i: (i % tiles_per_seq, 0))
    else:
        m = x.shape[0] * x.shape[1]
        n_tiles = x.shape[1] // rows_per_t
        row = lambda w: pl.BlockSpec((n_t, rows_per_t, w), lambda i: (0, i, 0))
        shp = lambda w, dt: jax.ShapeDtypeStruct((n_t, m // n_t, w), dt)
        tab = pl.BlockSpec((n_t, rows_per_t, HEAD_R), lambda i: (0, i, 0))
    n_seq = n_tiles // tiles_per_seq
    hl_spec = pl.BlockSpec((1, lag, D_MODEL), lambda i: (i // tiles_per_seq, 0, 0))
    args = [x]
    specs = [row(D_MODEL)]
    if prev is not None:
        args.append(prev)
        specs.append(pl.BlockSpec((1, lag, D_MODEL), lambda i: (i // tiles_per_seq, 0, 0)))
    consts = (ng, win, mu, w0, w2p, a0, a2p, g2, k_k, k_a, bd)
    args += list(consts) + [cos2, sin2]
    specs += [_resident(c.shape) for c in consts] + [tab, tab]
    out_shape = ([shp(D_A, F32)] * 7 + [shp(D_R, qkv_dtype)] * 3 + [shp(D_R, F32)]
                 + [jax.ShapeDtypeStruct((n_seq, lag, D_MODEL), F32)])
    out_specs = [row(D_A)] * 7 + [row(D_R)] * 4 + [hl_spec]
    return pl.pallas_call(
        functools.partial(_proj_kernel, n_t, lag, pad, tiles_per_seq, prev is not None),
        grid=(n_tiles,),
        in_specs=specs,
        out_specs=out_specs,
        out_shape=out_shape,
        scratch_shapes=[pltpu.VMEM((pad + tm, N_SHIFT), F32)],
        compiler_params=_params(1),
        name="proj",
    )(*args)


def _sums_exact(x, bd):
    return _mm_exact_rhs(x, bd)


def _sums_stacked(x, bd2):
    return _mm(jnp.concatenate(_split_hi_lo(x), axis=1), bd2)


def _wkv_post(y, r, k, v, g, rk, lw_g, lb_g, head_sums):
    inv_n = 1.0 / HEAD_A
    mu = head_sums(y) * inv_n
    d = y - mu
    var = head_sums(d * d) * inv_n
    yn = d * lax.rsqrt(var + GN_EPS_A) * lw_g + lb_g
    bonus = head_sums(r * k * rk) * v
    return (yn + bonus) * g


def _wkv_prompt_kernel(nb, tt, r_ref, lw_ref, k_ref, v_ref, kk_ref, kka_ref, g_ref,
                       rk_ref, lnw_ref, lnb_ref, bd2_ref, tri2_ref,
                       ya_o, s_o, s_scr):
    c = WKV_CHUNK
    j = pl.program_id(1)

    @pl.when(j == 0)
    def _():
        s_scr[...] = jnp.zeros_like(s_scr)

    lane = lax.broadcasted_iota(jnp.int32, (c, PAIR), 1)
    first = lane < HEAD_A

    def stack(x):
        return jnp.concatenate([jnp.where(first, x, 0.0), jnp.where(first, 0.0, x)], axis=0)

    ri = lax.broadcasted_iota(jnp.int32, (2 * c, 2 * c), 0) % c
    ci = lax.broadcasted_iota(jnp.int32, (2 * c, 2 * c), 1) % c
    strict = ri > ci
    incl = ri >= ci
    eye = (lax.broadcasted_iota(jnp.int32, (2 * c, 2 * c), 0)
           == lax.broadcasted_iota(jnp.int32, (2 * c, 2 * c), 1)).astype(F32)
    bd2 = bd2_ref[...]
    tri2 = tri2_ref[...]
    head_sums = functools.partial(_sums_stacked, bd2=bd2)

    n_pairs = H_A // 2
    units = [(bi, p) for bi in range(nb) for p in range(n_pairs)]
    n_u = range(len(units))
    slab = [slice(p * PAIR, (p + 1) * PAIR) for p in range(n_pairs)]

    def chunk_body(ch, carry):
        rows = pl.ds(pl.multiple_of(ch * c, c), c)
        ld = lambda ref: [ref[bi, rows, slab[p]] for bi, p in units]
        r, k, v, kk, kka = ld(r_ref), ld(k_ref), ld(v_ref), ld(kk_ref), ld(kka_ref)
        lw_all = [lw_ref[bi, rows, :] for bi in range(nb)]
        cum_all = [_mm(tri2, jnp.concatenate(_split_hi_lo(x), axis=0)) for x in lw_all]
        lw = [lw_all[bi][:, slab[p]] for bi, p in units]
        cum = [cum_all[bi][:, slab[p]] for bi, p in units]
        cum_end = [x[c - 1:c, :] for x in cum]
        e_pos = [jnp.exp(x) for x in cum]
        e_neg = [jnp.exp(-x) for x in cum]
        e_end = [jnp.exp(x - y) for x, y in zip(cum_end, cum)]
        xs = [jnp.concatenate([stack(-kk[u] * jnp.exp(cum[u] - lw[u])), stack(r[u] * e_pos[u])],
                              axis=0).astype(BF16) for u in n_u]
        ws = [jnp.concatenate([stack(kka[u] * e_neg[u]), stack(k[u] * e_neg[u])],
                              axis=0).astype(BF16) for u in n_u]
        we = [jnp.concatenate([stack(kka[u] * e_end[u]), stack(k[u] * e_end[u])],
                              axis=0).astype(BF16) for u in n_u]
        vs = [stack(x).astype(BF16) for x in v]
        gram = [_nt(xs[u], ws[u]) for u in n_u]
        a_ab = [jnp.where(strict, g[0:2 * c, 0:2 * c], 0.0) for g in gram]
        a_ak = [jnp.where(strict, g[0:2 * c, 2 * c:], 0.0).astype(BF16) for g in gram]
        a_r = [jnp.concatenate([jnp.where(incl, g[2 * c:, 0:2 * c], 0.0),
                                jnp.where(incl, g[2 * c:, 2 * c:], 0.0)], axis=1).astype(BF16)
               for g in gram]
        inv = [eye + a for a in a_ab]
        pw = [x.astype(BF16) for x in a_ab]
        pw = [_mm(x, x).astype(BF16) for x in pw]
        n_lvl = c.bit_length() - 2
        for lvl in range(n_lvl):
            if lvl < n_lvl - 1:
                both = [_mm(x, jnp.concatenate([x, i.astype(BF16)], axis=1)) for i, x in zip(inv, pw)]
                inv = [i + b[:, 2 * c:] for i, b in zip(inv, both)]
                pw = [b[:, 0:2 * c].astype(BF16) for b in both]
            else:
                inv = [i + _mm(x, i.astype(BF16)) for i, x in zip(inv, pw)]
        s = [s_scr[u] for u in n_u]
        z = [_nt(xs[u], s[u].astype(BF16)) for u in n_u]
        rhs = [z[u][0:2 * c] + _mm(a_ak[u], vs[u]) for u in n_u]
        uu = [_mm(inv[u].astype(BF16), rhs[u].astype(BF16)).astype(BF16) for u in n_u]
        uv = [jnp.concatenate([uu[u], vs[u]], axis=0) for u in n_u]
        y2 = [z[u][2 * c:] + _mm(a_r[u], uv[u]) for u in n_u]
        for u in n_u:
            s_scr[u] = s[u] * jnp.exp(cum_end[u]) + _tn(uv[u], we[u])
        tall = lambda xs: jnp.concatenate(xs, axis=0)
        per_pair = lambda ref: tall([jnp.broadcast_to(ref[:, slab[p]], (c, PAIR)) for _, p in units])
        out = _wkv_post(tall([y2[u][0:c] + y2[u][c:] for u in n_u]), tall(r), tall(k), tall(v),
                        tall(ld(g_ref)), per_pair(rk_ref), per_pair(lnw_ref), per_pair(lnb_ref),
                        head_sums)
        for u, (bi, p) in enumerate(units):
            ya_o[bi, rows, slab[p]] = out[u * c:(u + 1) * c].astype(ya_o.dtype)
        return carry

    lax.fori_loop(0, tt // c, chunk_body, 0)

    @pl.when(j == pl.num_programs(1) - 1)
    def _():
        for u, (bi, p) in enumerate(units):
            s = s_scr[u]
            s_o[bi, 2 * p] = s[0:HEAD_A, 0:HEAD_A]
            s_o[bi, 2 * p + 1] = s[HEAD_A:, HEAD_A:]


def _wkv_prompt(r, lw, k, v, kk, kka, g, rk, lnw, lnb, bd2, tri2, *, nb, tt):
    batch, seq, _ = r.shape
    blk = pl.BlockSpec((nb, tt, D_A), lambda b, j: (b, j, 0))
    consts = (rk, lnw, lnb, bd2, tri2)
    return pl.pallas_call(
        functools.partial(_wkv_prompt_kernel, nb, tt),
        grid=(batch // nb, seq // tt),
        in_specs=[blk] * 7 + [_resident(x.shape) for x in consts],
        out_specs=[blk, pl.BlockSpec((nb, H_A, HEAD_A, HEAD_A), lambda b, j: (b, 0, 0, 0))],
        out_shape=[jax.ShapeDtypeStruct((batch, seq, D_A), BF16),
                   jax.ShapeDtypeStruct((batch, H_A, HEAD_A, HEAD_A), F32)],
        scratch_shapes=[pltpu.VMEM((nb * H_A // 2, PAIR, PAIR), F32)],
        compiler_params=_params(2),
        name="wkv_prompt",
    )(r, lw, k, v, kk, kka, g, *consts)


def _wkv_sample_kernel(n_t, bb, r_ref, lw_ref, k_ref, v_ref, kk_ref, kka_ref, g_ref,
                       rk_ref, lnw_ref, lnb_ref, bd_ref, s_ref,
                       ya_o, s_o, y_scr):
    n = HEAD_A
    units = [(bi, p) for bi in range(bb) for p in range(H_A // 2)]
    bd2 = bd_ref[0:PAIR, 0:PAIR]
    diag = (lax.broadcasted_iota(jnp.int32, (n, PAIR), 1) % n
            == lax.broadcasted_iota(jnp.int32, (n, PAIR), 0))

    def row_of(ref, t, bi, p):
        return jnp.broadcast_to(ref[t, bi:bi + 1, p * PAIR:(p + 1) * PAIR], (n, PAIR))

    def head_sums(xs, exact):
        tall = jnp.concatenate(xs, axis=0)
        out = _mm_exact_rhs(tall, bd2) if exact else _mm(tall.astype(BF16), bd2)
        return [out[i * n:(i + 1) * n] for i in range(len(xs))]

    s2 = [jnp.concatenate([s_ref[bi, 2 * p], s_ref[bi, 2 * p + 1]], axis=1) for bi, p in units]
    for t in range(n_t):
        sa = head_sums([s * -row_of(kk_ref, t, *u) for s, u in zip(s2, units)], False)
        v_col = head_sums([jnp.where(diag, row_of(v_ref, t, *u), 0.0) for u in units], True)
        s2 = [s * jnp.exp(row_of(lw_ref, t, *u)) + a * row_of(kka_ref, t, *u)
              + vc * row_of(k_ref, t, *u) for s, a, vc, u in zip(s2, sa, v_col, units)]
        y_bc = head_sums([s * row_of(r_ref, t, *u) for s, u in zip(s2, units)], False)
        for y, (bi, p) in zip(y_bc, units):
            y_scr[t, bi:bi + 1, p * PAIR:(p + 1) * PAIR] = jnp.sum(
                jnp.where(diag, y, 0.0), axis=0, keepdims=True)
    for s, (bi, p) in zip(s2, units):
        s_o[bi, 2 * p] = s[:, 0:n]
        s_o[bi, 2 * p + 1] = s[:, n:]

    rows_of = lambda ref: jnp.concatenate([ref[t] for t in range(n_t)], axis=0)
    out = _wkv_post(rows_of(y_scr), rows_of(r_ref), rows_of(k_ref), rows_of(v_ref), rows_of(g_ref),
                    rk_ref[...], lnw_ref[...], lnb_ref[...],
                    functools.partial(_sums_exact, bd=bd_ref[...]))
    for t in range(n_t):
        ya_o[t] = out[t * bb:(t + 1) * bb].astype(ya_o.dtype)


def _wkv_sample(r, lw, k, v, kk, kka, g, rk, lnw, lnb, bd, s0, *, bb):
    n_t, n_b, _ = r.shape
    consts = (rk, lnw, lnb, bd)
    blk = pl.BlockSpec((n_t, bb, D_A), lambda i: (0, i, 0))
    st = pl.BlockSpec((bb, H_A, HEAD_A, HEAD_A), lambda i: (i, 0, 0, 0))
    return pl.pallas_call(
        functools.partial(_wkv_sample_kernel, n_t, bb),
        grid=(n_b // bb,),
        in_specs=[blk] * 7 + [_resident(x.shape) for x in consts] + [st],
        out_specs=[blk, st],
        out_shape=[jax.ShapeDtypeStruct((n_t, n_b, D_A), F32),
                   jax.ShapeDtypeStruct(s0.shape, F32)],
        scratch_shapes=[pltpu.VMEM((n_t, bb, D_A), F32)],
        compiler_params=_params(1),
        name="wkv_sample",
    )(r, lw, k, v, kk, kka, g, *consts, s0)


def _ret_chunk(q, k, v, g, s, heads, dm_ref, qd_ref, kd_ref, cd_ref, gn_ref):
    n = range(len(q))
    qb = [x.astype(BF16) for x in q]
    kb = [x.astype(BF16) for x in k]
    vb = [x.astype(BF16) for x in v]
    inner = [(_nt(qb[u], kb[u]) * dm_ref[heads[u]]).astype(BF16) for u in n]
    q_dec = [(q[u].astype(F32) * qd_ref[heads[u]]).astype(BF16) for u in n]
    k_dec = [(k[u].astype(F32) * kd_ref[heads[u]]).astype(BF16) for u in n]
    y = [_mm(inner[u], vb[u]) + _mm(q_dec[u], s[u].astype(BF16)) for u in n]
    s_new = [s[u] * cd_ref[heads[u]] + _tn(k_dec[u], vb[u]) for u in n]
    out = []
    for u in n:
        mu = jnp.mean(y[u], axis=-1, keepdims=True)
        d = y[u] - mu
        var = jnp.mean(d * d, axis=-1, keepdims=True)
        lanes = slice(heads[u] * HEAD_R, (heads[u] + 1) * HEAD_R)
        yn = d * lax.rsqrt(var + GN_EPS_R) * gn_ref[:, lanes]
        out.append(g[u] * _sigmoid(g[u]) * yn)
    return out, s_new


def _ret_prompt_kernel(nb, tt, q_ref, k_ref, v_ref, g_ref, dm_ref, qd_ref, kd_ref, cd_ref, gn_ref,
                       y_o, s_o, s_scr):
    j = pl.program_id(1)

    @pl.when(j == 0)
    def _():
        s_scr[...] = jnp.zeros_like(s_scr)

    c = RET_CHUNK
    units = [(bi, hh) for bi in range(nb) for hh in range(H_R)]
    heads = [hh for _, hh in units]
    lanes = [slice(hh * HEAD_R, (hh + 1) * HEAD_R) for hh in heads]
    s = [s_scr[u] for u in range(len(units))]
    for ch in range(tt // c):
        rows = slice(ch * c, (ch + 1) * c)
        ld = lambda ref: [ref[bi, rows, lanes[u]] for u, (bi, _) in enumerate(units)]
        y, s = _ret_chunk(ld(q_ref), ld(k_ref), ld(v_ref), ld(g_ref), s, heads,
                          dm_ref, qd_ref, kd_ref, cd_ref, gn_ref)
        for u, (bi, _) in enumerate(units):
            y_o[bi, rows, lanes[u]] = y[u].astype(y_o.dtype)
    for u in range(len(units)):
        s_scr[u] = s[u]

    @pl.when(j == pl.num_programs(1) - 1)
    def _():
        for u, (bi, hh) in enumerate(units):
            s_o[bi, hh] = s_scr[u]


def _ret_prompt(q, k, v, g, dm, qd, kd, cd, gn, *, nb, tt):
    batch, seq, _ = q.shape
    blk = pl.BlockSpec((nb, tt, D_R), lambda b, j: (b, j, 0))
    consts = (dm, qd, kd, cd, gn)
    return pl.pallas_call(
        functools.partial(_ret_prompt_kernel, nb, tt),
        grid=(batch // nb, seq // tt),
        in_specs=[blk] * 4 + [_resident(x.shape) for x in consts],
        out_specs=[blk, pl.BlockSpec((nb, H_R, HEAD_R, HEAD_R), lambda b, j: (b, 0, 0, 0))],
        out_shape=[jax.ShapeDtypeStruct((batch, seq, D_R), BF16),
                   jax.ShapeDtypeStruct((batch, H_R, HEAD_R, HEAD_R), F32)],
        scratch_shapes=[pltpu.VMEM((nb * H_R, HEAD_R, HEAD_R), F32)],
        compiler_params=_params(2),
        name="ret_prompt",
    )(q, k, v, g, *consts)


def _ret_sample_kernel(n_t, bb, q_ref, k_ref, v_ref, g_ref, dm_ref, qd_ref, kd_ref, cd_ref,
                       gn_ref, s_ref, y_o, s_o):
    rid = lax.broadcasted_iota(jnp.int32, (SUBLANES, HEAD_R), 0)
    units = [(bi, hh) for bi in range(bb) for hh in range(H_R)]
    heads = [hh for _, hh in units]
    lanes = [slice(hh * HEAD_R, (hh + 1) * HEAD_R) for hh in heads]

    def seq_rows(ref):
        outs = []
        for u, (bi, _) in enumerate(units):
            out = jnp.zeros((SUBLANES, HEAD_R), F32)
            for t in range(n_t):
                out = jnp.where(rid == t, jnp.broadcast_to(ref[t, bi:bi + 1, lanes[u]], out.shape), out)
            outs.append(out)
        return outs

    y, s_new = _ret_chunk(seq_rows(q_ref), seq_rows(k_ref), seq_rows(v_ref), seq_rows(g_ref),
                          [s_ref[bi, hh] for bi, hh in units], heads,
                          dm_ref, qd_ref, kd_ref, cd_ref, gn_ref)
    for u, (bi, hh) in enumerate(units):
        s_o[bi, hh] = s_new[u]
        for t in range(n_t):
            y_o[t, bi:bi + 1, lanes[u]] = y[u][t:t + 1].astype(y_o.dtype)


def _ret_sample(q, k, v, g, dm, qd, kd, cd, gn, s0, *, bb):
    n_t, n_b, _ = q.shape
    consts = (dm, qd, kd, cd, gn)
    blk = pl.BlockSpec((n_t, bb, D_R), lambda i: (0, i, 0))
    st = pl.BlockSpec((bb, H_R, HEAD_R, HEAD_R), lambda i: (i, 0, 0, 0))
    return pl.pallas_call(
        functools.partial(_ret_sample_kernel, n_t, bb),
        grid=(n_b // bb,),
        in_specs=[blk] * 4 + [_resident(x.shape) for x in consts] + [st],
        out_specs=[blk, st],
        out_shape=[jax.ShapeDtypeStruct((n_t, n_b, D_R), F32),
                   jax.ShapeDtypeStruct(s0.shape, F32)],
        compiler_params=_params(1),
        name="ret_sample",
    )(q, k, v, g, *consts, s0)


def _rope_tables(pos):
    half = HEAD_R // 2
    inv = ROPE_BASE ** (-jnp.arange(half, dtype=F32) / half)
    ang = pos.astype(F32)[:, None] * inv[None, :]
    cos, sin = jnp.cos(ang), jnp.sin(ang)
    return jnp.concatenate([cos, cos], axis=1), jnp.concatenate([-sin, sin], axis=1)


def _ret_tables(c):
    lg = jnp.log1p(-jnp.exp2(-5.0 - jnp.arange(H_R, dtype=F32)))
    idx = jnp.arange(c, dtype=F32)
    diff = idx[:, None] - idx[None, :]
    dmask = jnp.where(diff >= 0, jnp.exp(lg[:, None, None] * jnp.maximum(diff, 0.0)), 0.0)
    ones = jnp.ones((1, 1, HEAD_R), F32)
    qdec = jnp.exp(lg[:, None] * (idx + 1.0))[:, :, None] * ones
    kdec = jnp.exp(lg[:, None] * (c - 1.0 - idx))[:, :, None] * ones
    cdec = jnp.exp(lg * c)[:, None, None] * ones
    extra = -c % SUBLANES
    dmask = jnp.pad(dmask, ((0, 0), (0, extra), (0, extra)))
    qdec = jnp.pad(qdec, ((0, 0), (0, extra), (0, 0)))
    kdec = jnp.pad(kdec, ((0, 0), (0, extra), (0, 0)))
    return dmask, qdec, kdec, cdec


def _block_ones(n, block):
    idx = jnp.arange(n) // block
    return (idx[:, None] == idx[None, :]).astype(BF16)


def kernel(x_prompt, x_sample, state_shift, state_wkv, state_ret, norm_g, ffn1_wg, ffn1_wu, ffn1_wd,
           w_in, mu_shift, w0, w2, a0, a2, g2, k_k, k_a, r_k, lnx_w, lnx_b, ret_gn_w, w_out,
           ffn2_wg, ffn2_wu, ffn2_wd):
    assert norm_g.shape[0] == 1, "single-layer configuration"
    bp, tp, _ = x_prompt.shape
    bs, ts, _ = x_sample.shape
    l = 0
    ng = norm_g[l]
    f1 = (ffn1_wg[l].astype(BF16), ffn1_wu[l].astype(BF16), ffn1_wd[l].astype(BF16))
    f2 = (ffn2_wg[l].astype(BF16), ffn2_wu[l].astype(BF16), ffn2_wd[l].astype(BF16))
    win = w_in[l].astype(BF16)
    wo = w_out[l].astype(BF16)
    row = lambda t: t[l].reshape(1, -1)
    zpad = jnp.zeros((LORA_W, D_A), BF16)
    w2p = jnp.concatenate([w2[l].astype(BF16), zpad], axis=0)
    a2p = jnp.concatenate([zpad, a2[l].astype(BF16)], axis=0)
    proj_consts = (ng, win, row(mu_shift), row(w0), w2p, row(a0), a2p, g2[l].astype(BF16),
                   row(k_k), row(k_a), _block_ones(D_A, HEAD_A))
    rk, lnw, lnb, gn = row(r_k), row(lnx_w), row(lnx_b), row(ret_gn_w)
    bd_pair = _block_ones(PAIR, HEAD_A)
    bd2 = jnp.concatenate([bd_pair, bd_pair], axis=0)
    tri = (jnp.arange(WKV_CHUNK)[:, None] >= jnp.arange(WKV_CHUNK)[None, :]).astype(BF16)
    tri2 = jnp.concatenate([tri, tri], axis=1)

    xp = x_prompt.reshape(bp * tp, D_MODEL)
    x1p = _ffn(xp, ng, *f1, 0, 1, 512)
    cos_p, sin_p = _rope_tables(jnp.arange(tp, dtype=jnp.int32))
    tm_p = 256
    (r, lw, k, v, kk, kka, g, q, kr, vr, gr, hl_p) = _proj(
        x1p, None, *proj_consts, cos_p, sin_p, n_t=1, rows_per_t=tm_p, lag=1,
        tiles_per_seq=tp // tm_p, qkv_dtype=BF16)
    seq3 = lambda t: t.reshape(bp, tp, D_A)
    ya_p, wkv_p = _wkv_prompt(*map(seq3, (r, lw, k, v, kk, kka, g)), rk, lnw, lnb, bd2, tri2,
                              nb=2, tt=256)
    ya_p = ya_p.reshape(bp * tp, D_A)
    yr_p, ret_p = _ret_prompt(*map(seq3, (q, kr, vr, gr)), *_ret_tables(RET_CHUNK), gn, nb=2, tt=512)
    yr_p = yr_p.reshape(bp * tp, D_R)
    yp = _ffn(x1p, ng, *f2, 4, 5, 512, mix=(ya_p, yr_p, wo))

    m_s = bs * ts
    xs = x_sample.transpose(1, 0, 2)
    x1s = _ffn(xs.reshape(m_s, D_MODEL), ng, *f1, 0, 1, m_s)
    cos_s, sin_s = _rope_tables(PAST_LEN + jnp.arange(ts, dtype=jnp.int32))
    rows_per_t = 32
    tab = lambda t: jnp.broadcast_to(t[:, None, :], (ts, bs, HEAD_R))
    outs = _proj(x1s.reshape(ts, bs, D_MODEL),
                 state_shift[l].reshape(bs // rows_per_t, rows_per_t, D_MODEL),
                 *proj_consts, tab(cos_s), tab(sin_s), n_t=ts, rows_per_t=rows_per_t,
                 lag=rows_per_t, tiles_per_seq=1, qkv_dtype=F32)
    (r, lw, k, v, kk, kka, g, q, kr, vr, gr, hl_s) = outs
    ya_s, wkv_s = _wkv_sample(r, lw, k, v, kk, kka, g, rk, lnw, lnb, _block_ones(D_A, HEAD_A),
                              state_wkv[l], bb=SUBLANES)
    yr_s, ret_s = _ret_sample(q, kr, vr, gr, *_ret_tables(min(RET_CHUNK, ts)), gn, state_ret[l],
                              bb=SUBLANES)
    ys = _ffn(x1s, ng, *f2, 4, 5, m_s,
              mix=(ya_s.reshape(m_s, D_A), yr_s.reshape(m_s, D_R), wo))
    ys = ys.reshape(ts, bs, D_MODEL).transpose(1, 0, 2)

    return (yp.reshape(bp, tp, D_MODEL), ys,
            hl_p.reshape(1, bp, D_MODEL), wkv_p[None], ret_p[None],
            hl_s.reshape(1, bs, D_MODEL), wkv_s[None], ret_s[None])
```

```python
import functools

import jax
import jax.numpy as jnp
from jax import lax
from jax.experimental import pallas as pl
from jax.experimental.pallas import tpu as pltpu

F32 = jnp.float32
BF16 = jnp.bfloat16

D_MODEL = 1024
D_A = 512
HEAD_A = 64
H_A = D_A // HEAD_A
D_R = 512
H_R = 4
HEAD_R = D_R // H_R
LORA_W, LORA_A, LORA_G = 64, 64, 128
D_FF = 2816
RET_CHUNK = 128
ROPE_BASE = 10000.0
EPS = 1e-6
GN_EPS_A = 64e-5
GN_EPS_R = 1e-5
N_SHIFT = 3 * D_A + LORA_W + LORA_A + LORA_G
N_COLS = N_SHIFT + 4 * D_R
PAST_LEN = 16384

LANES = 128
SUBLANES = 8
VMEM_LIMIT = 52 * 1024 * 1024

FF_CHUNK = D_FF // 2
WKV_CHUNK = 64
PAIR = 2 * HEAD_A


def _nt(a, b):
    return lax.dot_general(a, b, (((1,), (1,)), ((), ())), preferred_element_type=F32)


def _tn(a, b):
    return lax.dot_general(a, b, (((0,), (0,)), ((), ())), preferred_element_type=F32)


def _mm(a, b):
    return jnp.dot(a, b, preferred_element_type=F32)


def _split_hi_lo(x):
    hi = x.astype(BF16)
    lo = (x - hi.astype(F32)).astype(BF16)
    return hi, lo


def _mm_exact_rhs(x, m):
    hi, lo = _split_hi_lo(x)
    return _mm(hi, m) + _mm(lo, m)


def _mm_exact_lhs(m, x):
    hi, lo = _split_hi_lo(x)
    return _mm(m, hi) + _mm(m, lo)


def _rms(x, g):
    return x * lax.rsqrt(jnp.mean(x * x, axis=-1, keepdims=True) + EPS) * g


def _softplus(x):
    return jnp.maximum(x, 0.0) + jnp.log(1.0 + jnp.exp(-jnp.abs(x)))


def _sigmoid(x):
    return 1.0 / (1.0 + jnp.exp(-x))


def _resident(shape):
    nd = len(shape)
    return pl.BlockSpec(shape, lambda *_: (0,) * nd, pipeline_mode=pl.Buffered(1))


def _params(n_axes):
    return pltpu.CompilerParams(dimension_semantics=("arbitrary",) * n_axes,
                                vmem_limit_bytes=VMEM_LIMIT)


def _ffn_kernel(with_mix, g_in, g_out, *refs):
    if with_mix:
        x_ref, ya_ref, yr_ref, wo_ref, ng_ref, wg_ref, wu_ref, wd_ref, o_ref = refs
    else:
        x_ref, ng_ref, wg_ref, wu_ref, wd_ref, o_ref = refs
    x = x_ref[...]
    if with_mix:
        mix = (_mm(ya_ref[...].astype(BF16), wo_ref[0:D_A, :])
               + _mm(yr_ref[...].astype(BF16), wo_ref[D_A:, :]))
        x = x + _rms(mix, ng_ref[3:4, :])
    h = _rms(x, ng_ref[g_in:g_in + 1, :]).astype(BF16)
    acc = None
    for c in range(D_FF // FF_CHUNK):
        cols = slice(c * FF_CHUNK, (c + 1) * FF_CHUNK)
        gate = _mm(h, wg_ref[:, cols])
        up = _mm(h, wu_ref[:, cols])
        act = (gate * _sigmoid(gate) * up).astype(BF16)
        part = _mm(act, wd_ref[cols, :])
        acc = part if acc is None else acc + part
    o_ref[...] = x + 0.5 * _rms(acc, ng_ref[g_out:g_out + 1, :])


def _ffn(x, ng, wg, wu, wd, g_in, g_out, tm, mix=None):
    m = x.shape[0]
    row = lambda w: pl.BlockSpec((tm, w), lambda i: (i, 0))
    if mix is None:
        args = (x, ng, wg, wu, wd)
        specs = [row(D_MODEL), _resident(ng.shape), _resident(wg.shape), _resident(wu.shape),
                 _resident(wd.shape)]
    else:
        ya, yr, wo = mix
        args = (x, ya, yr, wo, ng, wg, wu, wd)
        specs = [row(D_MODEL), row(D_A), row(D_R), _resident(wo.shape), _resident(ng.shape),
                 _resident(wg.shape), _resident(wu.shape), _resident(wd.shape)]
    return pl.pallas_call(
        functools.partial(_ffn_kernel, mix is not None, g_in, g_out),
        grid=(m // tm,),
        in_specs=specs,
        out_specs=row(D_MODEL),
        out_shape=jax.ShapeDtypeStruct((m, D_MODEL), F32),
        compiler_params=_params(1),
        name="ffn_mix" if mix is not None else "ffn",
    )(*args)


def _proj_kernel(n_t, lag, pad, tiles_per_seq, has_prev, *refs):
    it = iter(refs)
    x_ref = next(it)
    prev_ref = next(it) if has_prev else None
    (ng_ref, win_ref, mu_ref, w0_ref, w2_ref, a0_ref, a2_ref, g2_ref, kk_ref, ka_ref, bd_ref,
     cos_ref, sin_ref,
     r_o, lw_o, k_o, v_o, kk_o, kka_o, g_o, q_o, kr_o, vr_o, gr_o, hl_o, ps_scr) = tuple(it)

    def load(ref):
        if n_t == 1:
            return ref[...]
        return jnp.concatenate([ref[t] for t in range(n_t)], axis=0)

    def store(ref, val):
        if n_t == 1:
            ref[...] = val.astype(ref.dtype)
        else:
            rows = val.shape[0] // n_t
            for t in range(n_t):
                ref[t] = val[t * rows:(t + 1) * rows].astype(ref.dtype)

    x = load(x_ref)
    tm = x.shape[0]
    h = _rms(x, ng_ref[2:3, :])
    hl_o[0] = h[tm - lag:, :]
    hb = h.astype(BF16)

    seq_start = (pl.program_id(0) % tiles_per_seq) == 0
    if has_prev:
        @pl.when(seq_start)
        def _():
            ps_scr[pad - lag:pad, :] = _mm(prev_ref[0].astype(BF16), win_ref[:, 0:N_SHIFT])
    else:
        @pl.when(seq_start)
        def _():
            ps_scr[pad - lag:pad, :] = jnp.zeros((lag, N_SHIFT), F32)
    ps_scr[pad:pad + tm, :] = _mm(hb, win_ref[:, 0:N_SHIFT])
    cur = ps_scr[pad:pad + tm, :]
    prv = ps_scr[pad - lag:pad - lag + tm, :]
    mixed = cur + (prv - cur) * mu_ref[...]
    ps_scr[pad - lag:pad, :] = cur[tm - lag:, :]

    r = mixed[:, 0:D_A]
    k = mixed[:, D_A:2 * D_A]
    v = mixed[:, 2 * D_A:3 * D_A]
    wa = mixed[:, 3 * D_A:3 * D_A + LORA_W + LORA_A]
    gd = mixed[:, 3 * D_A + LORA_W + LORA_A:N_SHIFT]
    w_pre = w0_ref[...] + _mm(jnp.tanh(wa).astype(BF16), w2_ref[...])
    w_log = -_softplus(-w_pre) - 0.5
    lw = -jnp.exp(w_log)
    a = _sigmoid(a0_ref[...] + _mm(wa.astype(BF16), a2_ref[...]))
    g = _mm(_sigmoid(gd).astype(BF16), g2_ref[...])
    kk = k * kk_ref[...]
    ss = _mm_exact_rhs(kk * kk, bd_ref[...])
    kk = kk / jnp.maximum(jnp.sqrt(ss), 1e-12)
    k_mod = k * (1.0 + (a - 1.0) * ka_ref[...])
    store(r_o, r)
    store(lw_o, lw)
    store(k_o, k_mod)
    store(v_o, v)
    store(kk_o, kk)
    store(kka_o, kk * a)
    store(g_o, g)

    pr = _mm(hb, win_ref[:, N_SHIFT:])
    cos2 = load(cos_ref)
    sin2 = load(sin_ref)

    def rope(t):
        parts = []
        for hh in range(H_R):
            th = t[:, hh * HEAD_R:(hh + 1) * HEAD_R]
            parts.append(th * cos2 + pltpu.roll(th, HEAD_R // 2, 1) * sin2)
        return jnp.concatenate(parts, axis=1)

    store(q_o, rope(pr[:, 0:D_R]))
    store(kr_o, rope(pr[:, D_R:2 * D_R]) * (HEAD_R ** -0.5))
    store(vr_o, pr[:, 2 * D_R:3 * D_R])
    store(gr_o, pr[:, 3 * D_R:])


def _proj(x, prev, ng, win, mu, w0, w2p, a0, a2p, g2, k_k, k_a, bd, cos2, sin2, *,
          n_t, rows_per_t, lag, tiles_per_seq, qkv_dtype):
    tm = n_t * rows_per_t
    pad = max(lag, SUBLANES)
    if n_t == 1:
        m = x.shape[0]
        n_tiles = m // tm
        row = lambda w: pl.BlockSpec((tm, w), lambda i: (i, 0))
        shp = lambda w, dt: jax.ShapeDtypeStruct((m, w), dt)
        tab = pl.BlockSpec((tm, HEAD_R), lambda i: (i % tiles_per_seq, 0))
    else:
        m = x.shape[0] * x.shape[1]
        n_tiles = x.shape[1] // rows_per_t
        row = lambda w: pl.BlockSpec((n_t, rows_per_t, w), lambda i: (0, i, 0))
        shp = lambda w, dt: jax.ShapeDtypeStruct((n_t, m // n_t, w), dt)
        tab = pl.BlockSpec((n_t, rows_per_t, HEAD_R), lambda i: (0, i, 0))
    n_seq = n_tiles // tiles_per_seq
    hl_spec = pl.BlockSpec((1, lag, D_MODEL), lambda i: (i // tiles_per_seq, 0, 0))
    args = [x]
    specs = [row(D_MODEL)]
    if prev is not None:
        args.append(prev)
        specs.append(pl.BlockSpec((1, lag, D_MODEL), lambda i: (i // tiles_per_seq, 0, 0)))
    consts = (ng, win, mu, w0, w2p, a0, a2p, g2, k_k, k_a, bd)
    args += list(consts) + [cos2, sin2]
    specs += [_resident(c.shape) for c in consts] + [tab, tab]
    out_shape = ([shp(D_A, F32)] * 7 + [shp(D_R, qkv_dtype)] * 3 + [shp(D_R, F32)]
                 + [jax.ShapeDtypeStruct((n_seq, lag, D_MODEL), F32)])
    out_specs = [row(D_A)] * 7 + [row(D_R)] * 4 + [hl_spec]
    return pl.pallas_call(
        functools.partial(_proj_kernel, n_t, lag, pad, tiles_per_seq, prev is not None),
        grid=(n_tiles,),
        in_specs=specs,
        out_specs=out_specs,
        out_shape=out_shape,
        scratch_shapes=[pltpu.VMEM((pad + tm, N_SHIFT), F32)],
        compiler_params=_params(1),
        name="proj",
    )(*args)


def _sums_exact(x, bd):
    return _mm_exact_rhs(x, bd)


def _sums_stacked(x, bd2):
    return _mm(jnp.concatenate(_split_hi_lo(x), axis=1), bd2)


def _wkv_post(y, r, k, v, g, rk, lw_g, lb_g, head_sums):
    inv_n = 1.0 / HEAD_A
    mu = head_sums(y) * inv_n
    d = y - mu
    var = head_sums(d * d) * inv_n
    yn = d * lax.rsqrt(var + GN_EPS_A) * lw_g + lb_g
    bonus = head_sums(r * k * rk) * v
    return (yn + bonus) * g


def _wkv_prompt_kernel(nb, tt, r_ref, lw_ref, k_ref, v_ref, kk_ref, kka_ref, g_ref,
                       rk_ref, lnw_ref, lnb_ref, bd2_ref, tri2_ref,
                       ya_o, s_o, s_scr):
    c = WKV_CHUNK
    j = pl.program_id(1)

    @pl.when(j == 0)
    def _():
        s_scr[...] = jnp.zeros_like(s_scr)

    lane = lax.broadcasted_iota(jnp.int32, (c, PAIR), 1)
    first = lane < HEAD_A

    def stack(x):
        return jnp.concatenate([jnp.where(first, x, 0.0), jnp.where(first, 0.0, x)], axis=0)

    ri = lax.broadcasted_iota(jnp.int32, (2 * c, 2 * c), 0) % c
    ci = lax.broadcasted_iota(jnp.int32, (2 * c, 2 * c), 1) % c
    strict = ri > ci
    incl = ri >= ci
    eye = (lax.broadcasted_iota(jnp.int32, (2 * c, 2 * c), 0)
           == lax.broadcasted_iota(jnp.int32, (2 * c, 2 * c), 1)).astype(F32)
    bd2 = bd2_ref[...]
    tri2 = tri2_ref[...]
    head_sums = functools.partial(_sums_stacked, bd2=bd2)

    n_pairs = H_A // 2
    units = [(bi, p) for bi in range(nb) for p in range(n_pairs)]
    n_u = range(len(units))
    slab = [slice(p * PAIR, (p + 1) * PAIR) for p in range(n_pairs)]

    def chunk_body(ch, carry):
        rows = pl.ds(pl.multiple_of(ch * c, c), c)
        ld = lambda ref: [ref[bi, rows, slab[p]] for bi, p in units]
        r, k, v, kk, kka = ld(r_ref), ld(k_ref), ld(v_ref), ld(kk_ref), ld(kka_ref)
        lw_all = [lw_ref[bi, rows, :] for bi in range(nb)]
        cum_all = [_mm(tri2, jnp.concatenate(_split_hi_lo(x), axis=0)) for x in lw_all]
        lw = [lw_all[bi][:, slab[p]] for bi, p in units]
        cum = [cum_all[bi][:, slab[p]] for bi, p in units]
        cum_end = [x[c - 1:c, :] for x in cum]
        e_pos = [jnp.exp(x) for x in cum]
        e_neg = [jnp.exp(-x) for x in cum]
        e_end = [jnp.exp(x - y) for x, y in zip(cum_end, cum)]
        xs = [jnp.concatenate([stack(-kk[u] * jnp.exp(cum[u] - lw[u])), stack(r[u] * e_pos[u])],
                              axis=0).astype(BF16) for u in n_u]
        ws = [jnp.concatenate([stack(kka[u] * e_neg[u]), stack(k[u] * e_neg[u])],
                              axis=0).astype(BF16) for u in n_u]
        we = [jnp.concatenate([stack(kka[u] * e_end[u]), stack(k[u] * e_end[u])],
                              axis=0).astype(BF16) for u in n_u]
        vs = [stack(x).astype(BF16) for x in v]
        gram = [_nt(xs[u], ws[u]) for u in n_u]
        a_ab = [jnp.where(strict, g[0:2 * c, 0:2 * c], 0.0) for g in gram]
        a_ak = [jnp.where(strict, g[0:2 * c, 2 * c:], 0.0).astype(BF16) for g in gram]
        a_r = [jnp.concatenate([jnp.where(incl, g[2 * c:, 0:2 * c], 0.0),
                                jnp.where(incl, g[2 * c:, 2 * c:], 0.0)], axis=1).astype(BF16)
               for g in gram]
        inv = [eye + a for a in a_ab]
        pw = [x.astype(BF16) for x in a_ab]
        pw = [_mm(x, x).astype(BF16) for x in pw]
        n_lvl = c.bit_length() - 2
        for lvl in range(n_lvl):
            if lvl < n_lvl - 1:
                both = [_mm(x, jnp.concatenate([x, i.astype(BF16)], axis=1)) for i, x in zip(inv, pw)]
                inv = [i + b[:, 2 * c:] for i, b in zip(inv, both)]
                pw = [b[:, 0:2 * c].astype(BF16) for b in both]
            else:
                inv = [i + _mm(x, i.astype(BF16)) for i, x in zip(inv, pw)]
        s = [s_scr[u] for u in n_u]
        z = [_nt(xs[u], s[u].astype(BF16)) for u in n_u]
        rhs = [z[u][0:2 * c] + _mm(a_ak[u], vs[u]) for u in n_u]
        uu = [_mm(inv[u].astype(BF16), rhs[u].astype(BF16)).astype(BF16) for u in n_u]
        uv = [jnp.concatenate([uu[u], vs[u]], axis=0) for u in n_u]
        y2 = [z[u][2 * c:] + _mm(a_r[u], uv[u]) for u in n_u]
        for u in n_u:
            s_scr[u] = s[u] * jnp.exp(cum_end[u]) + _tn(uv[u], we[u])
        tall = lambda xs: jnp.concatenate(xs, axis=0)
        per_pair = lambda ref: tall([jnp.broadcast_to(ref[:, slab[p]], (c, PAIR)) for _, p in units])
        out = _wkv_post(tall([y2[u][0:c] + y2[u][c:] for u in n_u]), tall(r), tall(k), tall(v),
                        tall(ld(g_ref)), per_pair(rk_ref), per_pair(lnw_ref), per_pair(lnb_ref),
                        head_sums)
        for u, (bi, p) in enumerate(units):
            ya_o[bi, rows, slab[p]] = out[u * c:(u + 1) * c].astype(ya_o.dtype)
        return carry

    lax.fori_loop(0, tt // c, chunk_body, 0)

    @pl.when(j == pl.num_programs(1) - 1)
    def _():
        for u, (bi, p) in enumerate(units):
            s = s_scr[u]
            s_o[bi, 2 * p] = s[0:HEAD_A, 0:HEAD_A]
            s_o[bi, 2 * p + 1] = s[HEAD_A:, HEAD_A:]


def _wkv_prompt(r, lw, k, v, kk, kka, g, rk, lnw, lnb, bd2, tri2, *, nb, tt):
    batch, seq, _ = r.shape
    blk = pl.BlockSpec((nb, tt, D_A), lambda b, j: (b, j, 0))
    consts = (rk, lnw, lnb, bd2, tri2)
    return pl.pallas_call(
        functools.partial(_wkv_prompt_kernel, nb, tt),
        grid=(batch // nb, seq // tt),
        in_specs=[blk] * 7 + [_resident(x.shape) for x in consts],
        out_specs=[blk, pl.BlockSpec((nb, H_A, HEAD_A, HEAD_A), lambda b, j: (b, 0, 0, 0))],
        out_shape=[jax.ShapeDtypeStruct((batch, seq, D_A), BF16),
                   jax.ShapeDtypeStruct((batch, H_A, HEAD_A, HEAD_A), F32)],
        scratch_shapes=[pltpu.VMEM((nb * H_A // 2, PAIR, PAIR), F32)],
        compiler_params=_params(2),
        name="wkv_prompt",
    )(r, lw, k, v, kk, kka, g, *consts)


def _wkv_sample_kernel(n_t, r_ref, lw_ref, k_ref, v_ref, kk_ref, kka_ref, g_ref,
                       rk_ref, lnw_ref, lnb_ref, bd2_ref, s_ref,
                       ya_o, s_o, yt_scr):
    n = HEAD_A
    tr = lambda ref: [ref[t].T for t in range(n_t)]
    kk_t, kka_t, k_t, r_t, v_t = tr(kk_ref), tr(kka_ref), tr(k_ref), tr(r_ref), tr(v_ref)
    w_t = [jnp.exp(x) for x in tr(lw_ref)]
    n_b = v_t[0].shape[1]
    rid = lax.broadcasted_iota(jnp.int32, (SUBLANES, n_b), 0)

    for hh in range(2):
        keys = slice(hh * n, (hh + 1) * n)
        for ig in range(n // SUBLANES):
            y_tiles = [jnp.zeros((SUBLANES, n_b), F32) for _ in range(n_t)]
            for ii in range(SUBLANES):
                i = ig * SUBLANES + ii
                s = s_ref[hh, i]
                for t in range(n_t):
                    sa = jnp.sum(s * -kk_t[t][keys], axis=0, keepdims=True)
                    v_row = v_t[t][hh * n + i:hh * n + i + 1]
                    s = s * w_t[t][keys] + sa * kka_t[t][keys] + v_row * k_t[t][keys]
                    y_row = jnp.sum(s * r_t[t][keys], axis=0, keepdims=True)
                    y_tiles[t] = jnp.where(rid == ii, y_row, y_tiles[t])
                s_o[hh, i] = s
            for t in range(n_t):
                yt_scr[t, hh * n + ig * SUBLANES:hh * n + (ig + 1) * SUBLANES, :] = y_tiles[t]

    tall = lambda xs: jnp.concatenate(xs, axis=0)
    rows_of = lambda ref: tall([ref[t] for t in range(n_t)])
    out = _wkv_post(tall([yt_scr[t].T for t in range(n_t)]), rows_of(r_ref), rows_of(k_ref),
                    rows_of(v_ref), rows_of(g_ref), rk_ref[...], lnw_ref[...], lnb_ref[...],
                    functools.partial(_sums_stacked, bd2=bd2_ref[...]))
    for t in range(n_t):
        ya_o[t] = out[t * n_b:(t + 1) * n_b].astype(ya_o.dtype)


def _wkv_sample(r, lw, k, v, kk, kka, g, rk, lnw, lnb, bd2, s0):
    n_t, n_b, _ = r.shape
    blk = pl.BlockSpec((n_t, n_b, PAIR), lambda p: (0, 0, p))
    par = pl.BlockSpec((1, PAIR), lambda p: (0, p))
    st = pl.BlockSpec((2, HEAD_A, HEAD_A, n_b), lambda p: (p, 0, 0, 0))
    return pl.pallas_call(
        functools.partial(_wkv_sample_kernel, n_t),
        grid=(H_A // 2,),
        in_specs=[blk] * 7 + [par] * 3 + [_resident(bd2.shape), st],
        out_specs=[blk, st],
        out_shape=[jax.ShapeDtypeStruct((n_t, n_b, D_A), F32),
                   jax.ShapeDtypeStruct(s0.shape, F32)],
        scratch_shapes=[pltpu.VMEM((n_t, PAIR, n_b), F32)],
        compiler_params=_params(1),
        name="wkv_sample",
    )(r, lw, k, v, kk, kka, g, rk, lnw, lnb, bd2, s0)


def _ret_chunk(q, k, v, g, s, heads, dm_ref, qd_ref, kd_ref, cd_ref, gn_ref):
    n = range(len(q))
    qb = [x.astype(BF16) for x in q]
    kb = [x.astype(BF16) for x in k]
    vb = [x.astype(BF16) for x in v]
    inner = [(_nt(qb[u], kb[u]) * dm_ref[heads[u]]).astype(BF16) for u in n]
    q_dec = [(q[u].astype(F32) * qd_ref[heads[u]]).astype(BF16) for u in n]
    k_dec = [(k[u].astype(F32) * kd_ref[heads[u]]).astype(BF16) for u in n]
    y = [_mm(inner[u], vb[u]) + _mm(q_dec[u], s[u].astype(BF16)) for u in n]
    s_new = [s[u] * cd_ref[heads[u]] + _tn(k_dec[u], vb[u]) for u in n]
    out = []
    for u in n:
        mu = jnp.mean(y[u], axis=-1, keepdims=True)
        d = y[u] - mu
        var = jnp.mean(d * d, axis=-1, keepdims=True)
        lanes = slice(heads[u] * HEAD_R, (heads[u] + 1) * HEAD_R)
        yn = d * lax.rsqrt(var + GN_EPS_R) * gn_ref[:, lanes]
        out.append(g[u] * _sigmoid(g[u]) * yn)
    return out, s_new


def _ret_prompt_kernel(nb, tt, q_ref, k_ref, v_ref, g_ref, dm_ref, qd_ref, kd_ref, cd_ref, gn_ref,
                       y_o, s_o, s_scr):
    j = pl.program_id(1)

    @pl.when(j == 0)
    def _():
        s_scr[...] = jnp.zeros_like(s_scr)

    c = RET_CHUNK
    units = [(bi, hh) for bi in range(nb) for hh in range(H_R)]
    heads = [hh for _, hh in units]
    lanes = [slice(hh * HEAD_R, (hh + 1) * HEAD_R) for hh in heads]
    s = [s_scr[u] for u in range(len(units))]
    for ch in range(tt // c):
        rows = slice(ch * c, (ch + 1) * c)
        ld = lambda ref: [ref[bi, rows, lanes[u]] for u, (bi, _) in enumerate(units)]
        y, s = _ret_chunk(ld(q_ref), ld(k_ref), ld(v_ref), ld(g_ref), s, heads,
                          dm_ref, qd_ref, kd_ref, cd_ref, gn_ref)
        for u, (bi, _) in enumerate(units):
            y_o[bi, rows, lanes[u]] = y[u].astype(y_o.dtype)
    for u in range(len(units)):
        s_scr[u] = s[u]

    @pl.when(j == pl.num_programs(1) - 1)
    def _():
        for u, (bi, hh) in enumerate(units):
            s_o[bi, hh] = s_scr[u]


def _ret_prompt(q, k, v, g, dm, qd, kd, cd, gn, *, nb, tt):
    batch, seq, _ = q.shape
    blk = pl.BlockSpec((nb, tt, D_R), lambda b, j: (b, j, 0))
    consts = (dm, qd, kd, cd, gn)
    return pl.pallas_call(
        functools.partial(_ret_prompt_kernel, nb, tt),
        grid=(batch // nb, seq // tt),
        in_specs=[blk] * 4 + [_resident(x.shape) for x in consts],
        out_specs=[blk, pl.BlockSpec((nb, H_R, HEAD_R, HEAD_R), lambda b, j: (b, 0, 0, 0))],
        out_shape=[jax.ShapeDtypeStruct((batch, seq, D_R), BF16),
                   jax.ShapeDtypeStruct((batch, H_R, HEAD_R, HEAD_R), F32)],
        scratch_shapes=[pltpu.VMEM((nb * H_R, HEAD_R, HEAD_R), F32)],
        compiler_params=_params(2),
        name="ret_prompt",
    )(q, k, v, g, *consts)


def _ret_sample_kernel(n_t, bb, q_ref, k_ref, v_ref, g_ref, dm_ref, qd_ref, kd_ref, cd_ref,
                       gn_ref, s_ref, y_o, s_o):
    rid = lax.broadcasted_iota(jnp.int32, (SUBLANES, HEAD_R), 0)
    units = [(bi, hh) for bi in range(bb) for hh in range(H_R)]
    heads = [hh for _, hh in units]
    lanes = [slice(hh * HEAD_R, (hh + 1) * HEAD_R) for hh in heads]

    def seq_rows(ref):
        outs = []
        for u, (bi, _) in enumerate(units):
            out = jnp.zeros((SUBLANES, HEAD_R), F32)
            for t in range(n_t):
                out = jnp.where(rid == t, jnp.broadcast_to(ref[t, bi:bi + 1, lanes[u]], out.shape), out)
            outs.append(out)
        return outs

    y, s_new = _ret_chunk(seq_rows(q_ref), seq_rows(k_ref), seq_rows(v_ref), seq_rows(g_ref),
                          [s_ref[bi, hh] for bi, hh in units], heads,
                          dm_ref, qd_ref, kd_ref, cd_ref, gn_ref)
    for u, (bi, hh) in enumerate(units):
        s_o[bi, hh] = s_new[u]
        for t in range(n_t):
            y_o[t, bi:bi + 1, lanes[u]] = y[u][t:t + 1].astype(y_o.dtype)


def _ret_sample(q, k, v, g, dm, qd, kd, cd, gn, s0, *, bb):
    n_t, n_b, _ = q.shape
    consts = (dm, qd, kd, cd, gn)
    blk = pl.BlockSpec((n_t, bb, D_R), lambda i: (0, i, 0))
    st = pl.BlockSpec((bb, H_R, HEAD_R, HEAD_R), lambda i: (i, 0, 0, 0))
    return pl.pallas_call(
        functools.partial(_ret_sample_kernel, n_t, bb),
        grid=(n_b // bb,),
        in_specs=[blk] * 4 + [_resident(x.shape) for x in consts] + [st],
        out_specs=[blk, st],
        out_shape=[jax.ShapeDtypeStruct((n_t, n_b, D_R), F32),
                   jax.ShapeDtypeStruct(s0.shape, F32)],
        compiler_params=_params(1),
        name="ret_sample",
    )(q, k, v, g, *consts, s0)


def _rope_tables(pos):
    half = HEAD_R // 2
    inv = ROPE_BASE ** (-jnp.arange(half, dtype=F32) / half)
    ang = pos.astype(F32)[:, None] * inv[None, :]
    cos, sin = jnp.cos(ang), jnp.sin(ang)
    return jnp.concatenate([cos, cos], axis=1), jnp.concatenate([-sin, sin], axis=1)


def _ret_tables(c):
    lg = jnp.log1p(-jnp.exp2(-5.0 - jnp.arange(H_R, dtype=F32)))
    idx = jnp.arange(c, dtype=F32)
    diff = idx[:, None] - idx[None, :]
    dmask = jnp.where(diff >= 0, jnp.exp(lg[:, None, None] * jnp.maximum(diff, 0.0)), 0.0)
    ones = jnp.ones((1, 1, HEAD_R), F32)
    qdec = jnp.exp(lg[:, None] * (idx + 1.0))[:, :, None] * ones
    kdec = jnp.exp(lg[:, None] * (c - 1.0 - idx))[:, :, None] * ones
    cdec = jnp.exp(lg * c)[:, None, None] * ones
    extra = -c % SUBLANES
    dmask = jnp.pad(dmask, ((0, 0), (0, extra), (0, extra)))
    qdec = jnp.pad(qdec, ((0, 0), (0, extra), (0, 0)))
    kdec = jnp.pad(kdec, ((0, 0), (0, extra), (0, 0)))
    return dmask, qdec, kdec, cdec


def _block_ones(n, block):
    idx = jnp.arange(n) // block
    return (idx[:, None] == idx[None, :]).astype(BF16)


def kernel(x_prompt, x_sample, state_shift, state_wkv, state_ret, norm_g, ffn1_wg, ffn1_wu, ffn1_wd,
           w_in, mu_shift, w0, w2, a0, a2, g2, k_k, k_a, r_k, lnx_w, lnx_b, ret_gn_w, w_out,
           ffn2_wg, ffn2_wu, ffn2_wd):
    assert norm_g.shape[0] == 1, "single-layer configuration"
    bp, tp, _ = x_prompt.shape
    bs, ts, _ = x_sample.shape
    l = 0
    ng = norm_g[l]
    f1 = (ffn1_wg[l].astype(BF16), ffn1_wu[l].astype(BF16), ffn1_wd[l].astype(BF16))
    f2 = (ffn2_wg[l].astype(BF16), ffn2_wu[l].astype(BF16), ffn2_wd[l].astype(BF16))
    win = w_in[l].astype(BF16)
    wo = w_out[l].astype(BF16)
    row = lambda t: t[l].reshape(1, -1)
    zpad = jnp.zeros((LORA_W, D_A), BF16)
    w2p = jnp.concatenate([w2[l].astype(BF16), zpad], axis=0)
    a2p = jnp.concatenate([zpad, a2[l].astype(BF16)], axis=0)
    proj_consts = (ng, win, row(mu_shift), row(w0), w2p, row(a0), a2p, g2[l].astype(BF16),
                   row(k_k), row(k_a), _block_ones(D_A, HEAD_A))
    rk, lnw, lnb, gn = row(r_k), row(lnx_w), row(lnx_b), row(ret_gn_w)
    bd_pair = _block_ones(PAIR, HEAD_A)
    bd2 = jnp.concatenate([bd_pair, bd_pair], axis=0)
    tri = (jnp.arange(WKV_CHUNK)[:, None] >= jnp.arange(WKV_CHUNK)[None, :]).astype(BF16)
    tri2 = jnp.concatenate([tri, tri], axis=1)

    xp = x_prompt.reshape(bp * tp, D_MODEL)
    x1p = _ffn(xp, ng, *f1, 0, 1, 512)
    cos_p, sin_p = _rope_tables(jnp.arange(tp, dtype=jnp.int32))
    tm_p = 256
    (r, lw, k, v, kk, kka, g, q, kr, vr, gr, hl_p) = _proj(
        x1p, None, *proj_consts, cos_p, sin_p, n_t=1, rows_per_t=tm_p, lag=1,
        tiles_per_seq=tp // tm_p, qkv_dtype=BF16)
    seq3 = lambda t: t.reshape(bp, tp, D_A)
    ya_p, wkv_p = _wkv_prompt(*map(seq3, (r, lw, k, v, kk, kka, g)), rk, lnw, lnb, bd2, tri2,
                              nb=2, tt=256)
    ya_p = ya_p.reshape(bp * tp, D_A)
    yr_p, ret_p = _ret_prompt(*map(seq3, (q, kr, vr, gr)), *_ret_tables(RET_CHUNK), gn, nb=2, tt=512)
    yr_p = yr_p.reshape(bp * tp, D_R)
    yp = _ffn(x1p, ng, *f2, 4, 5, 512, mix=(ya_p, yr_p, wo))

    m_s = bs * ts
    xs = x_sample.transpose(1, 0, 2)
    x1s = _ffn(xs.reshape(m_s, D_MODEL), ng, *f1, 0, 1, m_s)
    cos_s, sin_s = _rope_tables(PAST_LEN + jnp.arange(ts, dtype=jnp.int32))
    rows_per_t = 32
    tab = lambda t: jnp.broadcast_to(t[:, None, :], (ts, bs, HEAD_R))
    outs = _proj(x1s.reshape(ts, bs, D_MODEL),
                 state_shift[l].reshape(bs // rows_per_t, rows_per_t, D_MODEL),
                 *proj_consts, tab(cos_s), tab(sin_s), n_t=ts, rows_per_t=rows_per_t,
                 lag=rows_per_t, tiles_per_seq=1, qkv_dtype=F32)
    (r, lw, k, v, kk, kka, g, q, kr, vr, gr, hl_s) = outs
    ya_s, wkv_s = _wkv_sample(r, lw, k, v, kk, kka, g, rk, lnw, lnb, bd2,
                              state_wkv[l].transpose(1, 2, 3, 0))
    wkv_s = wkv_s.transpose(3, 0, 1, 2)
    yr_s, ret_s = _ret_sample(q, kr, vr, gr, *_ret_tables(min(RET_CHUNK, ts)), gn, state_ret[l],
                              bb=SUBLANES)
    ys = _ffn(x1s, ng, *f2, 4, 5, m_s,
              mix=(ya_s.reshape(m_s, D_A), yr_s.reshape(m_s, D_R), wo))
    ys = ys.reshape(ts, bs, D_MODEL).transpose(1, 0, 2)

    return (yp.reshape(bp, tp, D_MODEL), ys,
            hl_p.reshape(1, bp, D_MODEL), wkv_p[None], ret_p[None],
            hl_s.reshape(1, bs, D_MODEL), wkv_s[None], ret_s[None])
```

```python
import functools

import jax
import jax.numpy as jnp
from jax import lax
from jax.experimental import pallas as pl
from jax.experimental.pallas import tpu as pltpu

F32 = jnp.float32
BF16 = jnp.bfloat16

D_MODEL = 1024
D_A = 512
HEAD_A = 64
H_A = D_A // HEAD_A
D_R = 512
H_R = 4
HEAD_R = D_R // H_R
LORA_W, LORA_A, LORA_G = 64, 64, 128
D_FF = 2816
RET_CHUNK = 128
ROPE_BASE = 10000.0
EPS = 1e-6
GN_EPS_A = 64e-5
GN_EPS_R = 1e-5
N_SHIFT = 3 * D_A + LORA_W + LORA_A + LORA_G
N_COLS = N_SHIFT + 4 * D_R
PAST_LEN = 16384

LANES = 128
SUBLANES = 8
VMEM_LIMIT = 52 * 1024 * 1024

FF_CHUNK = D_FF // 2
WKV_CHUNK = 64
PAIR = 2 * HEAD_A


def _nt(a, b):
    return lax.dot_general(a, b, (((1,), (1,)), ((), ())), preferred_element_type=F32)


def _tn(a, b):
    return lax.dot_general(a, b, (((0,), (0,)), ((), ())), preferred_element_type=F32)


def _mm(a, b):
    return jnp.dot(a, b, preferred_element_type=F32)


def _split_hi_lo(x):
    hi = x.astype(BF16)
    lo = (x - hi.astype(F32)).astype(BF16)
    return hi, lo


def _mm_exact_rhs(x, m):
    hi, lo = _split_hi_lo(x)
    return _mm(hi, m) + _mm(lo, m)


def _mm_exact_lhs(m, x):
    hi, lo = _split_hi_lo(x)
    return _mm(m, hi) + _mm(m, lo)


def _rms(x, g):
    return x * lax.rsqrt(jnp.mean(x * x, axis=-1, keepdims=True) + EPS) * g


def _softplus(x):
    return jnp.maximum(x, 0.0) + jnp.log(1.0 + jnp.exp(-jnp.abs(x)))


def _sigmoid(x):
    return 1.0 / (1.0 + jnp.exp(-x))


def _resident(shape):
    nd = len(shape)
    return pl.BlockSpec(shape, lambda *_: (0,) * nd, pipeline_mode=pl.Buffered(1))


def _params(n_axes):
    return pltpu.CompilerParams(dimension_semantics=("arbitrary",) * n_axes,
                                vmem_limit_bytes=VMEM_LIMIT)


def _ffn_kernel(with_mix, g_in, g_out, *refs):
    if with_mix:
        x_ref, ya_ref, yr_ref, wo_ref, ng_ref, wg_ref, wu_ref, wd_ref, o_ref = refs
    else:
        x_ref, ng_ref, wg_ref, wu_ref, wd_ref, o_ref = refs
    x = x_ref[...]
    if with_mix:
        mix = (_mm(ya_ref[...].astype(BF16), wo_ref[0:D_A, :])
               + _mm(yr_ref[...].astype(BF16), wo_ref[D_A:, :]))
        x = x + _rms(mix, ng_ref[3:4, :])
    h = _rms(x, ng_ref[g_in:g_in + 1, :]).astype(BF16)
    acc = None
    for c in range(D_FF // FF_CHUNK):
        cols = slice(c * FF_CHUNK, (c + 1) * FF_CHUNK)
        gate = _mm(h, wg_ref[:, cols])
        up = _mm(h, wu_ref[:, cols])
        act = (gate * _sigmoid(gate) * up).astype(BF16)
        part = _mm(act, wd_ref[cols, :])
        acc = part if acc is None else acc + part
    o_ref[...] = x + 0.5 * _rms(acc, ng_ref[g_out:g_out + 1, :])


def _ffn(x, ng, wg, wu, wd, g_in, g_out, tm, mix=None):
    m = x.shape[0]
    row = lambda w: pl.BlockSpec((tm, w), lambda i: (i, 0))
    if mix is None:
        args = (x, ng, wg, wu, wd)
        specs = [row(D_MODEL), _resident(ng.shape), _resident(wg.shape), _resident(wu.shape),
                 _resident(wd.shape)]
    else:
        ya, yr, wo = mix
        args = (x, ya, yr, wo, ng, wg, wu, wd)
        specs = [row(D_MODEL), row(D_A), row(D_R), _resident(wo.shape), _resident(ng.shape),
                 _resident(wg.shape), _resident(wu.shape), _resident(wd.shape)]
    return pl.pallas_call(
        functools.partial(_ffn_kernel, mix is not None, g_in, g_out),
        grid=(m // tm,),
        in_specs=specs,
        out_specs=row(D_MODEL),
        out_shape=jax.ShapeDtypeStruct((m, D_MODEL), F32),
        compiler_params=_params(1),
        name="ffn_mix" if mix is not None else "ffn",
    )(*args)


def _proj_kernel(n_t, lag, pad, tiles_per_seq, has_prev, *refs):
    it = iter(refs)
    x_ref = next(it)
    prev_ref = next(it) if has_prev else None
    (ng_ref, win_ref, mu_ref, w0_ref, w2_ref, a0_ref, a2_ref, g2_ref, kk_ref, ka_ref, bd_ref,
     cos_ref, sin_ref,
     r_o, lw_o, k_o, v_o, kk_o, kka_o, g_o, q_o, kr_o, vr_o, gr_o, hl_o, ps_scr) = tuple(it)

    def load(ref):
        if n_t == 1:
            return ref[...]
        return jnp.concatenate([ref[t] for t in range(n_t)], axis=0)

    def store(ref, val):
        if n_t == 1:
            ref[...] = val.astype(ref.dtype)
        else:
            rows = val.shape[0] // n_t
            for t in range(n_t):
                ref[t] = val[t * rows:(t + 1) * rows].astype(ref.dtype)

    x = load(x_ref)
    tm = x.shape[0]
    h = _rms(x, ng_ref[2:3, :])
    hl_o[0] = h[tm - lag:, :]
    hb = h.astype(BF16)

    seq_start = (pl.program_id(0) % tiles_per_seq) == 0
    if has_prev:
        @pl.when(seq_start)
        def _():
            ps_scr[pad - lag:pad, :] = _mm(prev_ref[0].astype(BF16), win_ref[:, 0:N_SHIFT])
    else:
        @pl.when(seq_start)
        def _():
            ps_scr[pad - lag:pad, :] = jnp.zeros((lag, N_SHIFT), F32)
    ps_scr[pad:pad + tm, :] = _mm(hb, win_ref[:, 0:N_SHIFT])
    cur = ps_scr[pad:pad + tm, :]
    prv = ps_scr[pad - lag:pad - lag + tm, :]
    mixed = cur + (prv - cur) * mu_ref[...]
    ps_scr[pad - lag:pad, :] = cur[tm - lag:, :]

    r = mixed[:, 0:D_A]
    k = mixed[:, D_A:2 * D_A]
    v = mixed[:, 2 * D_A:3 * D_A]
    wa = mixed[:, 3 * D_A:3 * D_A + LORA_W + LORA_A]
    gd = mixed[:, 3 * D_A + LORA_W + LORA_A:N_SHIFT]
    w_pre = w0_ref[...] + _mm(jnp.tanh(wa).astype(BF16), w2_ref[...])
    w_log = -_softplus(-w_pre) - 0.5
    lw = -jnp.exp(w_log)
    a = _sigmoid(a0_ref[...] + _mm(wa.astype(BF16), a2_ref[...]))
    g = _mm(_sigmoid(gd).astype(BF16), g2_ref[...])
    kk = k * kk_ref[...]
    ss = _mm_exact_rhs(kk * kk, bd_ref[...])
    kk = kk / jnp.maximum(jnp.sqrt(ss), 1e-12)
    k_mod = k * (1.0 + (a - 1.0) * ka_ref[...])
    store(r_o, r)
    store(lw_o, lw)
    store(k_o, k_mod)
    store(v_o, v)
    store(kk_o, kk)
    store(kka_o, kk * a)
    store(g_o, g)

    pr = _mm(hb, win_ref[:, N_SHIFT:])
    cos2 = load(cos_ref)
    sin2 = load(sin_ref)

    def rope(t):
        parts = []
        for hh in range(H_R):
            th = t[:, hh * HEAD_R:(hh + 1) * HEAD_R]
            parts.append(th * cos2 + pltpu.roll(th, HEAD_R // 2, 1) * sin2)
        return jnp.concatenate(parts, axis=1)

    store(q_o, rope(pr[:, 0:D_R]))
    store(kr_o, rope(pr[:, D_R:2 * D_R]) * (HEAD_R ** -0.5))
    store(vr_o, pr[:, 2 * D_R:3 * D_R])
    store(gr_o, pr[:, 3 * D_R:])


def _proj(x, prev, ng, win, mu, w0, w2p, a0, a2p, g2, k_k, k_a, bd, cos2, sin2, *,
          n_t, rows_per_t, lag, tiles_per_seq, qkv_dtype):
    tm = n_t * rows_per_t
    pad = max(lag, SUBLANES)
    if n_t == 1:
        m = x.shape[0]
        n_tiles = m // tm
        row = lambda w: pl.BlockSpec((tm, w), lambda i: (i, 0))
        shp = lambda w, dt: jax.ShapeDtypeStruct((m, w), dt)
        tab = pl.BlockSpec((tm, HEAD_R), lambda i: (i % tiles_per_seq, 0))
    else:
        m = x.shape[0] * x.shape[1]
        n_tiles = x.shape[1] // rows_per_t
        row = lambda w: pl.BlockSpec((n_t, rows_per_t, w), lambda i: (0, i, 0))
        shp = lambda w, dt: jax.ShapeDtypeStruct((n_t, m // n_t, w), dt)
        tab = pl.BlockSpec((n_t, rows_per_t, HEAD_R), lambda i: (0, i, 0))
    n_seq = n_tiles // tiles_per_seq
    hl_spec = pl.BlockSpec((1, lag, D_MODEL), lambda i: (i // tiles_per_seq, 0, 0))
    args = [x]
    specs = [row(D_MODEL)]
    if prev is not None:
        args.append(prev)
        specs.append(pl.BlockSpec((1, lag, D_MODEL), lambda i: (i // tiles_per_seq, 0, 0)))
    consts = (ng, win, mu, w0, w2p, a0, a2p, g2, k_k, k_a, bd)
    args += list(consts) + [cos2, sin2]
    specs += [_resident(c.shape) for c in consts] + [tab, tab]
    out_shape = ([shp(D_A, F32)] * 7 + [shp(D_R, qkv_dtype)] * 3 + [shp(D_R, F32)]
                 + [jax.ShapeDtypeStruct((n_seq, lag, D_MODEL), F32)])
    out_specs = [row(D_A)] * 7 + [row(D_R)] * 4 + [hl_spec]
    return pl.pallas_call(
        functools.partial(_proj_kernel, n_t, lag, pad, tiles_per_seq, prev is not None),
        grid=(n_tiles,),
        in_specs=specs,
        out_specs=out_specs,
        out_shape=out_shape,
        scratch_shapes=[pltpu.VMEM((pad + tm, N_SHIFT), F32)],
        compiler_params=_params(1),
        name="proj",
    )(*args)


def _sums_exact(x, bd):
    return _mm_exact_rhs(x, bd)


def _sums_stacked(x, bd2):
    return _mm(jnp.concatenate(_split_hi_lo(x), axis=1), bd2)


def _wkv_post(y, r, k, v, g, rk, lw_g, lb_g, head_sums):
    inv_n = 1.0 / HEAD_A
    mu = head_sums(y) * inv_n
    d = y - mu
    var = head_sums(d * d) * inv_n
    yn = d * lax.rsqrt(var + GN_EPS_A) * lw_g + lb_g
    bonus = head_sums(r * k * rk) * v
    return (yn + bonus) * g


def _wkv_prompt_kernel(nb, tt, r_ref, lw_ref, k_ref, v_ref, kk_ref, kka_ref, g_ref,
                       rk_ref, lnw_ref, lnb_ref, bd2_ref, tri2_ref,
                       ya_o, s_o, s_scr):
    c = WKV_CHUNK
    j = pl.program_id(1)

    @pl.when(j == 0)
    def _():
        s_scr[...] = jnp.zeros_like(s_scr)

    lane = lax.broadcasted_iota(jnp.int32, (c, PAIR), 1)
    first = lane < HEAD_A

    def stack(x):
        return jnp.concatenate([jnp.where(first, x, 0.0), jnp.where(first, 0.0, x)], axis=0)

    ri = lax.broadcasted_iota(jnp.int32, (2 * c, 2 * c), 0) % c
    ci = lax.broadcasted_iota(jnp.int32, (2 * c, 2 * c), 1) % c
    strict = ri > ci
    incl = ri >= ci
    eye = (lax.broadcasted_iota(jnp.int32, (2 * c, 2 * c), 0)
           == lax.broadcasted_iota(jnp.int32, (2 * c, 2 * c), 1)).astype(F32)
    bd2 = bd2_ref[...]
    tri2 = tri2_ref[...]
    head_sums = functools.partial(_sums_stacked, bd2=bd2)

    n_pairs = H_A // 2
    units = [(bi, p) for bi in range(nb) for p in range(n_pairs)]
    n_u = range(len(units))
    slab = [slice(p * PAIR, (p + 1) * PAIR) for p in range(n_pairs)]

    def chunk_body(ch, carry):
        rows = pl.ds(pl.multiple_of(ch * c, c), c)
        ld = lambda ref: [ref[bi, rows, slab[p]] for bi, p in units]
        r, k, v, kk, kka = ld(r_ref), ld(k_ref), ld(v_ref), ld(kk_ref), ld(kka_ref)
        lw_all = [lw_ref[bi, rows, :] for bi in range(nb)]
        cum_all = [_mm(tri2, jnp.concatenate(_split_hi_lo(x), axis=0)) for x in lw_all]
        lw = [lw_all[bi][:, slab[p]] for bi, p in units]
        cum = [cum_all[bi][:, slab[p]] for bi, p in units]
        cum_end = [x[c - 1:c, :] for x in cum]
        e_pos = [jnp.exp(x) for x in cum]
        e_neg = [jnp.exp(-x) for x in cum]
        e_end = [jnp.exp(x - y) for x, y in zip(cum_end, cum)]
        xs = [jnp.concatenate([stack(-kk[u] * jnp.exp(cum[u] - lw[u])), stack(r[u] * e_pos[u])],
                              axis=0).astype(BF16) for u in n_u]
        ws = [jnp.concatenate([stack(kka[u] * e_neg[u]), stack(k[u] * e_neg[u])],
                              axis=0).astype(BF16) for u in n_u]
        we = [jnp.concatenate([stack(kka[u] * e_end[u]), stack(k[u] * e_end[u])],
                              axis=0).astype(BF16) for u in n_u]
        vs = [stack(x).astype(BF16) for x in v]
        gram = [_nt(xs[u], ws[u]) for u in n_u]
        a_ab = [jnp.where(strict, g[0:2 * c, 0:2 * c], 0.0) for g in gram]
        a_ak = [jnp.where(strict, g[0:2 * c, 2 * c:], 0.0).astype(BF16) for g in gram]
        a_r = [jnp.concatenate([jnp.where(incl, g[2 * c:, 0:2 * c], 0.0),
                                jnp.where(incl, g[2 * c:, 2 * c:], 0.0)], axis=1).astype(BF16)
               for g in gram]
        inv = [eye + a for a in a_ab]
        pw = [x.astype(BF16) for x in a_ab]
        pw = [_mm(x, x).astype(BF16) for x in pw]
        n_lvl = c.bit_length() - 2
        for lvl in range(n_lvl):
            if lvl < n_lvl - 1:
                both = [_mm(x, jnp.concatenate([x, i.astype(BF16)], axis=1)) for i, x in zip(inv, pw)]
                inv = [i + b[:, 2 * c:] for i, b in zip(inv, both)]
                pw = [b[:, 0:2 * c].astype(BF16) for b in both]
            else:
                inv = [i + _mm(x, i.astype(BF16)) for i, x in zip(inv, pw)]
        s = [s_scr[u] for u in n_u]
        z = [_nt(xs[u], s[u].astype(BF16)) for u in n_u]
        rhs = [z[u][0:2 * c] + _mm(a_ak[u], vs[u]) for u in n_u]
        uu = [_mm(inv[u].astype(BF16), rhs[u].astype(BF16)).astype(BF16) for u in n_u]
        uv = [jnp.concatenate([uu[u], vs[u]], axis=0) for u in n_u]
        y2 = [z[u][2 * c:] + _mm(a_r[u], uv[u]) for u in n_u]
        for u in n_u:
            s_scr[u] = s[u] * jnp.exp(cum_end[u]) + _tn(uv[u], we[u])
        tall = lambda xs: jnp.concatenate(xs, axis=0)
        per_pair = lambda ref: tall([jnp.broadcast_to(ref[:, slab[p]], (c, PAIR)) for _, p in units])
        out = _wkv_post(tall([y2[u][0:c] + y2[u][c:] for u in n_u]), tall(r), tall(k), tall(v),
                        tall(ld(g_ref)), per_pair(rk_ref), per_pair(lnw_ref), per_pair(lnb_ref),
                        head_sums)
        for u, (bi, p) in enumerate(units):
            ya_o[bi, rows, slab[p]] = out[u * c:(u + 1) * c].astype(ya_o.dtype)
        return carry

    lax.fori_loop(0, tt // c, chunk_body, 0)

    @pl.when(j == pl.num_programs(1) - 1)
    def _():
        for u, (bi, p) in enumerate(units):
            s = s_scr[u]
            s_o[bi, 2 * p] = s[0:HEAD_A, 0:HEAD_A]
            s_o[bi, 2 * p + 1] = s[HEAD_A:, HEAD_A:]


def _wkv_prompt(r, lw, k, v, kk, kka, g, rk, lnw, lnb, bd2, tri2, *, nb, tt):
    batch, seq, _ = r.shape
    blk = pl.BlockSpec((nb, tt, D_A), lambda b, j: (b, j, 0))
    consts = (rk, lnw, lnb, bd2, tri2)
    return pl.pallas_call(
        functools.partial(_wkv_prompt_kernel, nb, tt),
        grid=(batch // nb, seq // tt),
        in_specs=[blk] * 7 + [_resident(x.shape) for x in consts],
        out_specs=[blk, pl.BlockSpec((nb, H_A, HEAD_A, HEAD_A), lambda b, j: (b, 0, 0, 0))],
        out_shape=[jax.ShapeDtypeStruct((batch, seq, D_A), BF16),
                   jax.ShapeDtypeStruct((batch, H_A, HEAD_A, HEAD_A), F32)],
        scratch_shapes=[pltpu.VMEM((nb * H_A // 2, PAIR, PAIR), F32)],
        compiler_params=_params(2),
        name="wkv_prompt",
    )(r, lw, k, v, kk, kka, g, *consts)


def _wkv_sample_kernel(n_t, r_ref, lw_ref, k_ref, v_ref, kk_ref, kka_ref, g_ref,
                       rk_ref, lnw_ref, lnb_ref, bd2_ref, s_ref,
                       ya_o, s_o, yt_scr):
    n = HEAD_A
    tr = lambda ref: [ref[t].T for t in range(n_t)]
    kk_t, kka_t, k_t, r_t, v_t = tr(kk_ref), tr(kka_ref), tr(k_ref), tr(r_ref), tr(v_ref)
    w_t = [jnp.exp(x) for x in tr(lw_ref)]
    n_b = v_t[0].shape[1]
    rid = lax.broadcasted_iota(jnp.int32, (SUBLANES, n_b), 0)

    for hh in range(2):
        keys = slice(hh * n, (hh + 1) * n)
        for ig in range(n // SUBLANES):
            y_tiles = [jnp.zeros((SUBLANES, n_b), F32) for _ in range(n_t)]
            for ii in range(SUBLANES):
                i = ig * SUBLANES + ii
                s = s_ref[hh, i]
                for t in range(n_t):
                    sa = jnp.sum(s * -kk_t[t][keys], axis=0, keepdims=True)
                    v_row = v_t[t][hh * n + i:hh * n + i + 1]
                    s = s * w_t[t][keys] + sa * kka_t[t][keys] + v_row * k_t[t][keys]
                    y_row = jnp.sum(s * r_t[t][keys], axis=0, keepdims=True)
                    y_tiles[t] = jnp.where(rid == ii, y_row, y_tiles[t])
                s_o[hh, i] = s
            for t in range(n_t):
                yt_scr[t, hh * n + ig * SUBLANES:hh * n + (ig + 1) * SUBLANES, :] = y_tiles[t]

    tall = lambda xs: jnp.concatenate(xs, axis=0)
    rows_of = lambda ref: tall([ref[t] for t in range(n_t)])
    out = _wkv_post(tall([yt_scr[t].T for t in range(n_t)]), rows_of(r_ref), rows_of(k_ref),
                    rows_of(v_ref), rows_of(g_ref), rk_ref[...], lnw_ref[...], lnb_ref[...],
                    functools.partial(_sums_stacked, bd2=bd2_ref[...]))
    for t in range(n_t):
        ya_o[t] = out[t * n_b:(t + 1) * n_b].astype(ya_o.dtype)


def _wkv_sample(r, lw, k, v, kk, kka, g, rk, lnw, lnb, bd2, s0):
    n_t, n_b, _ = r.shape
    blk = pl.BlockSpec((n_t, n_b, PAIR), lambda p: (0, 0, p))
    par = pl.BlockSpec((1, PAIR), lambda p: (0, p))
    st = pl.BlockSpec((2, HEAD_A, HEAD_A, n_b), lambda p: (p, 0, 0, 0))
    return pl.pallas_call(
        functools.partial(_wkv_sample_kernel, n_t),
        grid=(H_A // 2,),
        in_specs=[blk] * 7 + [par] * 3 + [_resident(bd2.shape), st],
        out_specs=[blk, st],
        out_shape=[jax.ShapeDtypeStruct((n_t, n_b, D_A), F32),
                   jax.ShapeDtypeStruct(s0.shape, F32)],
        scratch_shapes=[pltpu.VMEM((n_t, PAIR, n_b), F32)],
        compiler_params=_params(1),
        name="wkv_sample",
    )(r, lw, k, v, kk, kka, g, rk, lnw, lnb, bd2, s0)


def _ret_chunk(q, k, v, g, s, heads, dm_ref, qd_ref, kd_ref, cd_ref, gn_ref):
    n = range(len(q))
    qb = [x.astype(BF16) for x in q]
    kb = [x.astype(BF16) for x in k]
    vb = [x.astype(BF16) for x in v]
    inner = [(_nt(qb[u], kb[u]) * dm_ref[heads[u]]).astype(BF16) for u in n]
    q_dec = [(q[u].astype(F32) * qd_ref[heads[u]]).astype(BF16) for u in n]
    k_dec = [(k[u].astype(F32) * kd_ref[heads[u]]).astype(BF16) for u in n]
    if inner[0].shape[1] % LANES == 0:
        y = [_mm(jnp.concatenate([inner[u], q_dec[u]], axis=1),
                 jnp.concatenate([vb[u], s[u].astype(BF16)], axis=0)) for u in n]
    else:
        y = [_mm(inner[u], vb[u]) + _mm(q_dec[u], s[u].astype(BF16)) for u in n]
    s_new = [s[u] * cd_ref[heads[u]] + _tn(k_dec[u], vb[u]) for u in n]
    out = []
    for u in n:
        mu = jnp.mean(y[u], axis=-1, keepdims=True)
        d = y[u] - mu
        var = jnp.mean(d * d, axis=-1, keepdims=True)
        lanes = slice(heads[u] * HEAD_R, (heads[u] + 1) * HEAD_R)
        yn = d * lax.rsqrt(var + GN_EPS_R) * gn_ref[:, lanes]
        out.append(g[u] * _sigmoid(g[u]) * yn)
    return out, s_new


def _ret_prompt_kernel(nb, tt, q_ref, k_ref, v_ref, g_ref, dm_ref, qd_ref, kd_ref, cd_ref, gn_ref,
                       y_o, s_o, s_scr):
    j = pl.program_id(1)

    @pl.when(j == 0)
    def _():
        s_scr[...] = jnp.zeros_like(s_scr)

    c = RET_CHUNK
    units = [(bi, hh) for bi in range(nb) for hh in range(H_R)]
    heads = [hh for _, hh in units]
    lanes = [slice(hh * HEAD_R, (hh + 1) * HEAD_R) for hh in heads]
    s = [s_scr[u] for u in range(len(units))]
    for ch in range(tt // c):
        rows = slice(ch * c, (ch + 1) * c)
        ld = lambda ref: [ref[bi, rows, lanes[u]] for u, (bi, _) in enumerate(units)]
        y, s = _ret_chunk(ld(q_ref), ld(k_ref), ld(v_ref), ld(g_ref), s, heads,
                          dm_ref, qd_ref, kd_ref, cd_ref, gn_ref)
        for u, (bi, _) in enumerate(units):
            y_o[bi, rows, lanes[u]] = y[u].astype(y_o.dtype)
    for u in range(len(units)):
        s_scr[u] = s[u]

    @pl.when(j == pl.num_programs(1) - 1)
    def _():
        for u, (bi, hh) in enumerate(units):
            s_o[bi, hh] = s_scr[u]


def _ret_prompt(q, k, v, g, dm, qd, kd, cd, gn, *, nb, tt):
    batch, seq, _ = q.shape
    blk = pl.BlockSpec((nb, tt, D_R), lambda b, j: (b, j, 0))
    consts = (dm, qd, kd, cd, gn)
    return pl.pallas_call(
        functools.partial(_ret_prompt_kernel, nb, tt),
        grid=(batch // nb, seq // tt),
        in_specs=[blk] * 4 + [_resident(x.shape) for x in consts],
        out_specs=[blk, pl.BlockSpec((nb, H_R, HEAD_R, HEAD_R), lambda b, j: (b, 0, 0, 0))],
        out_shape=[jax.ShapeDtypeStruct((batch, seq, D_R), BF16),
                   jax.ShapeDtypeStruct((batch, H_R, HEAD_R, HEAD_R), F32)],
        scratch_shapes=[pltpu.VMEM((nb * H_R, HEAD_R, HEAD_R), F32)],
        compiler_params=_params(2),
        name="ret_prompt",
    )(q, k, v, g, *consts)


def _ret_sample_kernel(n_t, bb, q_ref, k_ref, v_ref, g_ref, dm_ref, qd_ref, kd_ref, cd_ref,
                       gn_ref, s_ref, y_o, s_o):
    rid = lax.broadcasted_iota(jnp.int32, (SUBLANES, HEAD_R), 0)
    units = [(bi, hh) for bi in range(bb) for hh in range(H_R)]
    heads = [hh for _, hh in units]
    lanes = [slice(hh * HEAD_R, (hh + 1) * HEAD_R) for hh in heads]

    def seq_rows(ref):
        outs = []
        for u, (bi, _) in enumerate(units):
            out = jnp.zeros((SUBLANES, HEAD_R), F32)
            for t in range(n_t):
                out = jnp.where(rid == t, jnp.broadcast_to(ref[t, bi:bi + 1, lanes[u]], out.shape), out)
            outs.append(out)
        return outs

    y, s_new = _ret_chunk(seq_rows(q_ref), seq_rows(k_ref), seq_rows(v_ref), seq_rows(g_ref),
                          [s_ref[bi, hh] for bi, hh in units], heads,
                          dm_ref, qd_ref, kd_ref, cd_ref, gn_ref)
    for u, (bi, hh) in enumerate(units):
        s_o[bi, hh] = s_new[u]
        for t in range(n_t):
            y_o[t, bi:bi + 1, lanes[u]] = y[u][t:t + 1].astype(y_o.dtype)


def _ret_sample(q, k, v, g, dm, qd, kd, cd, gn, s0, *, bb):
    n_t, n_b, _ = q.shape
    consts = (dm, qd, kd, cd, gn)
    blk = pl.BlockSpec((n_t, bb, D_R), lambda i: (0, i, 0))
    st = pl.BlockSpec((bb, H_R, HEAD_R, HEAD_R), lambda i: (i, 0, 0, 0))
    return pl.pallas_call(
        functools.partial(_ret_sample_kernel, n_t, bb),
        grid=(n_b // bb,),
        in_specs=[blk] * 4 + [_resident(x.shape) for x in consts] + [st],
        out_specs=[blk, st],
        out_shape=[jax.ShapeDtypeStruct((n_t, n_b, D_R), F32),
                   jax.ShapeDtypeStruct(s0.shape, F32)],
        compiler_params=_params(1),
        name="ret_sample",
    )(q, k, v, g, *consts, s0)


def _rope_tables(pos):
    half = HEAD_R // 2
    inv = ROPE_BASE ** (-jnp.arange(half, dtype=F32) / half)
    ang = pos.astype(F32)[:, None] * inv[None, :]
    cos, sin = jnp.cos(ang), jnp.sin(ang)
    return jnp.concatenate([cos, cos], axis=1), jnp.concatenate([-sin, sin], axis=1)


def _ret_tables(c):
    lg = jnp.log1p(-jnp.exp2(-5.0 - jnp.arange(H_R, dtype=F32)))
    idx = jnp.arange(c, dtype=F32)
    diff = idx[:, None] - idx[None, :]
    dmask = jnp.where(diff >= 0, jnp.exp(lg[:, None, None] * jnp.maximum(diff, 0.0)), 0.0)
    ones = jnp.ones((1, 1, HEAD_R), F32)
    qdec = jnp.exp(lg[:, None] * (idx + 1.0))[:, :, None] * ones
    kdec = jnp.exp(lg[:, None] * (c - 1.0 - idx))[:, :, None] * ones
    cdec = jnp.exp(lg * c)[:, None, None] * ones
    extra = -c % SUBLANES
    dmask = jnp.pad(dmask, ((0, 0), (0, extra), (0, extra)))
    qdec = jnp.pad(qdec, ((0, 0), (0, extra), (0, 0)))
    kdec = jnp.pad(kdec, ((0, 0), (0, extra), (0, 0)))
    return dmask, qdec, kdec, cdec


def _block_ones(n, block):
    idx = jnp.arange(n) // block
    return (idx[:, None] == idx[None, :]).astype(BF16)


def kernel(x_prompt, x_sample, state_shift, state_wkv, state_ret, norm_g, ffn1_wg, ffn1_wu, ffn1_wd,
           w_in, mu_shift, w0, w2, a0, a2, g2, k_k, k_a, r_k, lnx_w, lnx_b, ret_gn_w, w_out,
           ffn2_wg, ffn2_wu, ffn2_wd):
    assert norm_g.shape[0] == 1, "single-layer configuration"
    bp, tp, _ = x_prompt.shape
    bs, ts, _ = x_sample.shape
    l = 0
    ng = norm_g[l]
    f1 = (ffn1_wg[l].astype(BF16), ffn1_wu[l].astype(BF16), ffn1_wd[l].astype(BF16))
    f2 = (ffn2_wg[l].astype(BF16), ffn2_wu[l].astype(BF16), ffn2_wd[l].astype(BF16))
    win = w_in[l].astype(BF16)
    wo = w_out[l].astype(BF16)
    row = lambda t: t[l].reshape(1, -1)
    zpad = jnp.zeros((LORA_W, D_A), BF16)
    w2p = jnp.concatenate([w2[l].astype(BF16), zpad], axis=0)
    a2p = jnp.concatenate([zpad, a2[l].astype(BF16)], axis=0)
    proj_consts = (ng, win, row(mu_shift), row(w0), w2p, row(a0), a2p, g2[l].astype(BF16),
                   row(k_k), row(k_a), _block_ones(D_A, HEAD_A))
    rk, lnw, lnb, gn = row(r_k), row(lnx_w), row(lnx_b), row(ret_gn_w)
    bd_pair = _block_ones(PAIR, HEAD_A)
    bd2 = jnp.concatenate([bd_pair, bd_pair], axis=0)
    tri = (jnp.arange(WKV_CHUNK)[:, None] >= jnp.arange(WKV_CHUNK)[None, :]).astype(BF16)
    tri2 = jnp.concatenate([tri, tri], axis=1)

    xp = x_prompt.reshape(bp * tp, D_MODEL)
    x1p = _ffn(xp, ng, *f1, 0, 1, 512)
    cos_p, sin_p = _rope_tables(jnp.arange(tp, dtype=jnp.int32))
    tm_p = 256
    (r, lw, k, v, kk, kka, g, q, kr, vr, gr, hl_p) = _proj(
        x1p, None, *proj_consts, cos_p, sin_p, n_t=1, rows_per_t=tm_p, lag=1,
        tiles_per_seq=tp // tm_p, qkv_dtype=BF16)
    seq3 = lambda t: t.reshape(bp, tp, D_A)
    ya_p, wkv_p = _wkv_prompt(*map(seq3, (r, lw, k, v, kk, kka, g)), rk, lnw, lnb, bd2, tri2,
                              nb=4, tt=256)
    ya_p = ya_p.reshape(bp * tp, D_A)
    yr_p, ret_p = _ret_prompt(*map(seq3, (q, kr, vr, gr)), *_ret_tables(RET_CHUNK), gn, nb=4, tt=256)
    yr_p = yr_p.reshape(bp * tp, D_R)
    yp = _ffn(x1p, ng, *f2, 4, 5, 512, mix=(ya_p, yr_p, wo))

    m_s = bs * ts
    xs = x_sample.transpose(1, 0, 2)
    x1s = _ffn(xs.reshape(m_s, D_MODEL), ng, *f1, 0, 1, m_s)
    cos_s, sin_s = _rope_tables(PAST_LEN + jnp.arange(ts, dtype=jnp.int32))
    rows_per_t = 32
    tab = lambda t: jnp.broadcast_to(t[:, None, :], (ts, bs, HEAD_R))
    outs = _proj(x1s.reshape(ts, bs, D_MODEL),
                 state_shift[l].reshape(bs // rows_per_t, rows_per_t, D_MODEL),
                 *proj_consts, tab(cos_s), tab(sin_s), n_t=ts, rows_per_t=rows_per_t,
                 lag=rows_per_t, tiles_per_seq=1, qkv_dtype=F32)
    (r, lw, k, v, kk, kka, g, q, kr, vr, gr, hl_s) = outs
    ya_s, wkv_s = _wkv_sample(r, lw, k, v, kk, kka, g, rk, lnw, lnb, bd2,
                              state_wkv[l].transpose(1, 2, 3, 0))
    wkv_s = wkv_s.transpose(3, 0, 1, 2)
    yr_s, ret_s = _ret_sample(q, kr, vr, gr, *_ret_tables(min(RET_CHUNK, ts)), gn, state_ret[l],
                              bb=SUBLANES)
    ys = _ffn(x1s, ng, *f2, 4, 5, m_s,
              mix=(ya_s.reshape(m_s, D_A), yr_s.reshape(m_s, D_R), wo))
    ys = ys.reshape(ts, bs, D_MODEL).transpose(1, 0, 2)

    return (yp.reshape(bp, tp, D_MODEL), ys,
            hl_p.reshape(1, bp, D_MODEL), wkv_p[None], ret_p[None],
            hl_s.reshape(1, bs, D_MODEL), wkv_s[None], ret_s[None])
```

```python
import functools

import jax
import jax.numpy as jnp
from jax import lax
from jax.experimental import pallas as pl
from jax.experimental.pallas import tpu as pltpu

F32 = jnp.float32
BF16 = jnp.bfloat16

D_MODEL = 1024
D_A = 512
HEAD_A = 64
H_A = D_A // HEAD_A
D_R = 512
H_R = 4
HEAD_R = D_R // H_R
LORA_W, LORA_A, LORA_G = 64, 64, 128
D_FF = 2816
RET_CHUNK = 128
ROPE_BASE = 10000.0
EPS = 1e-6
GN_EPS_A = 64e-5
GN_EPS_R = 1e-5
N_SHIFT = 3 * D_A + LORA_W + LORA_A + LORA_G
N_COLS = N_SHIFT + 4 * D_R
PAST_LEN = 16384

LANES = 128
SUBLANES = 8
VMEM_LIMIT = 52 * 1024 * 1024

FF_CHUNK = D_FF // 2
WKV_CHUNK = 64
PAIR = 2 * HEAD_A


def _nt(a, b):
    return lax.dot_general(a, b, (((1,), (1,)), ((), ())), preferred_element_type=F32)


def _tn(a, b):
    return lax.dot_general(a, b, (((0,), (0,)), ((), ())), preferred_element_type=F32)


def _mm(a, b):
    return jnp.dot(a, b, preferred_element_type=F32)


def _split_hi_lo(x):
    hi = x.astype(BF16)
    lo = (x - hi.astype(F32)).astype(BF16)
    return hi, lo


def _rms(x, g):
    return x * lax.rsqrt(jnp.mean(x * x, axis=-1, keepdims=True) + EPS) * g


def _softplus(x):
    return jnp.maximum(x, 0.0) + jnp.log(1.0 + jnp.exp(-jnp.abs(x)))


def _sigmoid(x):
    return 1.0 / (1.0 + jnp.exp(-x))


def _resident(shape):
    nd = len(shape)
    return pl.BlockSpec(shape, lambda *_: (0,) * nd, pipeline_mode=pl.Buffered(1))


def _params(n_axes):
    return pltpu.CompilerParams(dimension_semantics=("arbitrary",) * n_axes,
                                vmem_limit_bytes=VMEM_LIMIT)


def _ffn_kernel(with_mix, g_in, g_out, *refs):
    if with_mix:
        x_ref, ya_ref, yr_ref, wo_ref, ng_ref, wg_ref, wu_ref, wd_ref, o_ref = refs
    else:
        x_ref, ng_ref, wg_ref, wu_ref, wd_ref, o_ref = refs
    x = x_ref[...]
    if with_mix:
        mix = (_mm(ya_ref[...].astype(BF16), wo_ref[0:D_A, :])
               + _mm(yr_ref[...].astype(BF16), wo_ref[D_A:, :]))
        x = x + _rms(mix, ng_ref[3:4, :])
    h = _rms(x, ng_ref[g_in:g_in + 1, :]).astype(BF16)
    acc = None
    for c in range(D_FF // FF_CHUNK):
        cols = slice(c * FF_CHUNK, (c + 1) * FF_CHUNK)
        gate = _mm(h, wg_ref[:, cols])
        up = _mm(h, wu_ref[:, cols])
        act = (gate * _sigmoid(gate) * up).astype(BF16)
        part = _mm(act, wd_ref[cols, :])
        acc = part if acc is None else acc + part
    o_ref[...] = x + 0.5 * _rms(acc, ng_ref[g_out:g_out + 1, :])


def _ffn(x, ng, wg, wu, wd, g_in, g_out, tm, mix=None):
    m = x.shape[0]
    row = lambda w: pl.BlockSpec((tm, w), lambda i: (i, 0))
    if mix is None:
        args = (x, ng, wg, wu, wd)
        specs = [row(D_MODEL), _resident(ng.shape), _resident(wg.shape), _resident(wu.shape),
                 _resident(wd.shape)]
    else:
        ya, yr, wo = mix
        args = (x, ya, yr, wo, ng, wg, wu, wd)
        specs = [row(D_MODEL), row(D_A), row(D_R), _resident(wo.shape), _resident(ng.shape),
                 _resident(wg.shape), _resident(wu.shape), _resident(wd.shape)]
    return pl.pallas_call(
        functools.partial(_ffn_kernel, mix is not None, g_in, g_out),
        grid=(m // tm,),
        in_specs=specs,
        out_specs=row(D_MODEL),
        out_shape=jax.ShapeDtypeStruct((m, D_MODEL), F32),
        compiler_params=_params(1),
        name="ffn_mix" if mix is not None else "ffn",
    )(*args)


def _lag_rows(cur, first, lag):
    if lag % SUBLANES == 0:
        return jnp.concatenate([first, cur[:-lag]], axis=0)
    assert lag == 1
    rolled = pltpu.roll(cur, 1, 0)
    rid = lax.broadcasted_iota(jnp.int32, (SUBLANES, cur.shape[1]), 0)
    head = jnp.where(rid == 0, first, rolled[0:SUBLANES])
    return jnp.concatenate([head, rolled[SUBLANES:]], axis=0)


def _proj_kernel(n_t, lag, tiles_per_seq, has_prev, *refs):
    it = iter(refs)
    x_ref = next(it)
    prev_ref = next(it) if has_prev else None
    (ng_ref, win_ref, mu_ref, cos_ref, sin_ref,
     mix_o, q_o, kr_o, vr_o, gr_o, hl_o, carry_scr) = tuple(it)

    def load(ref):
        if n_t == 1:
            return ref[...]
        return jnp.concatenate([ref[t] for t in range(n_t)], axis=0)

    def store(ref, val, cols=slice(None)):
        if n_t == 1:
            ref[:, cols] = val.astype(ref.dtype)
        else:
            rows = val.shape[0] // n_t
            for t in range(n_t):
                ref[t, :, cols] = val[t * rows:(t + 1) * rows].astype(ref.dtype)

    x = load(x_ref)
    tm = x.shape[0]
    h = _rms(x, ng_ref[2:3, :])
    hl_o[0] = h[tm - lag:, :]
    hb = h.astype(BF16)
    seq_start = (pl.program_id(0) % tiles_per_seq) == 0
    prev_b = prev_ref[0].astype(BF16) if has_prev else None

    for c0 in range(0, N_SHIFT, D_A):
        cols = slice(c0, min(c0 + D_A, N_SHIFT))
        cur = _mm(hb, win_ref[:, cols])
        if has_prev:
            first = _mm(prev_b, win_ref[:, cols])
        else:
            first = jnp.zeros((lag, cur.shape[1]), F32)
        if tiles_per_seq > 1:
            first = jnp.where(seq_start, first, carry_scr[0:lag, cols])
            carry_scr[0:lag, cols] = cur[tm - lag:, :]
        prv = _lag_rows(cur, first, lag)
        store(mix_o, cur + (prv - cur) * mu_ref[:, cols], cols)

    cos2 = load(cos_ref)
    sin2 = load(sin_ref)

    def rope(t):
        parts = []
        for hh in range(H_R):
            th = t[:, hh * HEAD_R:(hh + 1) * HEAD_R]
            parts.append(th * cos2 + pltpu.roll(th, HEAD_R // 2, 1) * sin2)
        return jnp.concatenate(parts, axis=1)

    ret = lambda c: _mm(hb, win_ref[:, N_SHIFT + c * D_R:N_SHIFT + (c + 1) * D_R])
    q = ret(0)
    kr = ret(1)
    store(q_o, rope(q))
    vr = ret(2)
    store(kr_o, rope(kr) * (HEAD_R ** -0.5))
    gr = ret(3)
    store(vr_o, vr)
    store(gr_o, gr)


def _proj(x, prev, ng, win, mu, cos2, sin2, *, n_t, rows_per_t, lag, tiles_per_seq, qkv_dtype):
    tm = n_t * rows_per_t
    if n_t == 1:
        m = x.shape[0]
        n_tiles = m // tm
        row = lambda w: pl.BlockSpec((tm, w), lambda i: (i, 0))
        shp = lambda w, dt: jax.ShapeDtypeStruct((m, w), dt)
        tab = pl.BlockSpec((tm, HEAD_R), lambda i: (i % tiles_per_seq, 0))
    else:
        m = x.shape[0] * x.shape[1]
        n_tiles = x.shape[1] // rows_per_t
        row = lambda w: pl.BlockSpec((n_t, rows_per_t, w), lambda i: (0, i, 0))
        shp = lambda w, dt: jax.ShapeDtypeStruct((n_t, m // n_t, w), dt)
        tab = pl.BlockSpec((n_t, rows_per_t, HEAD_R), lambda i: (0, i, 0))
    n_seq = n_tiles // tiles_per_seq
    hl_spec = pl.BlockSpec((1, lag, D_MODEL), lambda i: (i // tiles_per_seq, 0, 0))
    args = [x]
    specs = [row(D_MODEL)]
    if prev is not None:
        args.append(prev)
        specs.append(pl.BlockSpec((1, lag, D_MODEL), lambda i: (i // tiles_per_seq, 0, 0)))
    consts = (ng, win, mu)
    args += list(consts) + [cos2, sin2]
    specs += [_resident(c.shape) for c in consts] + [tab, tab]
    out_shape = ([shp(N_SHIFT, F32)] + [shp(D_R, qkv_dtype)] * 3 + [shp(D_R, F32)]
                 + [jax.ShapeDtypeStruct((n_seq, lag, D_MODEL), F32)])
    out_specs = [row(N_SHIFT)] + [row(D_R)] * 4 + [hl_spec]
    return pl.pallas_call(
        functools.partial(_proj_kernel, n_t, lag, tiles_per_seq, prev is not None),
        grid=(n_tiles,),
        in_specs=specs,
        out_specs=out_specs,
        out_shape=out_shape,
        scratch_shapes=[pltpu.VMEM((max(lag, SUBLANES), N_SHIFT), F32)],
        compiler_params=_params(1),
        name="proj",
    )(*args)


def _sums_stacked(x, bd2):
    return _mm(jnp.concatenate(_split_hi_lo(x), axis=1), bd2)


def _wkv_post(y, r, k, v, g, rk, lw_g, lb_g, head_sums):
    inv_n = 1.0 / HEAD_A
    mu = head_sums(y) * inv_n
    d = y - mu
    var = head_sums(d * d) * inv_n
    yn = d * lax.rsqrt(var + GN_EPS_A) * lw_g + lb_g
    bonus = head_sums(r * k * rk) * v
    return (yn + bonus) * g


def _wkv_decay_gate(wa, gd, w0, w2p, a0, a2p, g2):
    w_pre = w0 + _mm(jnp.tanh(wa).astype(BF16), w2p)
    lw = -jnp.exp(-_softplus(-w_pre) - 0.5)
    a = _sigmoid(a0 + _mm(wa.astype(BF16), a2p))
    g = _mm(_sigmoid(gd).astype(BF16), g2)
    return lw, a, g


def _wkv_keys(k, a, k_k, k_a, head_sums):
    kk = k * k_k
    kk = kk / jnp.maximum(jnp.sqrt(head_sums(kk * kk)), 1e-12)
    return k * (1.0 + (a - 1.0) * k_a), kk, kk * a


def _wkv_prompt_kernel(nb, tt, mix_ref, w0_ref, w2_ref, a0_ref, a2_ref, g2_ref, kk_ref, ka_ref,
                       rk_ref, lnw_ref, lnb_ref, bd2_ref, tri2_ref,
                       ya_o, s_o, s_scr):
    c = WKV_CHUNK
    j = pl.program_id(1)

    @pl.when(j == 0)
    def _():
        s_scr[...] = jnp.zeros_like(s_scr)

    lane = lax.broadcasted_iota(jnp.int32, (c, PAIR), 1)
    first = lane < HEAD_A

    def stack(x):
        return jnp.concatenate([jnp.where(first, x, 0.0), jnp.where(first, 0.0, x)], axis=0)

    ri = lax.broadcasted_iota(jnp.int32, (2 * c, 2 * c), 0) % c
    ci = lax.broadcasted_iota(jnp.int32, (2 * c, 2 * c), 1) % c
    strict = ri > ci
    incl = ri >= ci
    eye = (lax.broadcasted_iota(jnp.int32, (2 * c, 2 * c), 0)
           == lax.broadcasted_iota(jnp.int32, (2 * c, 2 * c), 1)).astype(F32)
    bd2 = bd2_ref[...]
    tri2 = tri2_ref[...]
    head_sums = functools.partial(_sums_stacked, bd2=bd2)

    n_pairs = H_A // 2
    units = [(bi, p) for bi in range(nb) for p in range(n_pairs)]
    n_u = range(len(units))
    slab = [slice(p * PAIR, (p + 1) * PAIR) for p in range(n_pairs)]

    tall = lambda xs: jnp.concatenate(xs, axis=0)
    per_pair = lambda ref: tall([jnp.broadcast_to(ref[:, slab[p]], (c, PAIR)) for _, p in units])
    split = lambda x: [x[u * c:(u + 1) * c] for u in n_u]

    def prep(ch):
        rows = slice(ch * c, (ch + 1) * c)
        ld = lambda c0: [mix_ref[bi, rows, c0 + p * PAIR:c0 + (p + 1) * PAIR] for bi, p in units]
        r, k_raw, v = ld(0), ld(D_A), ld(2 * D_A)
        lora = [_wkv_decay_gate(mix_ref[bi, rows, 3 * D_A:3 * D_A + LORA_W + LORA_A],
                                mix_ref[bi, rows, 3 * D_A + LORA_W + LORA_A:N_SHIFT],
                                w0_ref[...], w2_ref[...], a0_ref[...], a2_ref[...], g2_ref[...])
                for bi in range(nb)]
        lw_all = [x[0] for x in lora]
        a = tall([lora[bi][1][:, slab[p]] for bi, p in units])
        gate = tall([lora[bi][2][:, slab[p]] for bi, p in units])
        k, kk, kka = map(split, _wkv_keys(tall(k_raw), a, per_pair(kk_ref), per_pair(ka_ref),
                                          head_sums))
        return r, v, k, kk, kka, gate, lw_all

    def gram_stage(pre):
        r, v, k, kk, kka, gate, lw_all = pre
        cum_all = [_mm(tri2, jnp.concatenate(_split_hi_lo(x), axis=0)) for x in lw_all]
        lw = [lw_all[bi][:, slab[p]] for bi, p in units]
        cum = [cum_all[bi][:, slab[p]] for bi, p in units]
        cum_end = [x[c - 1:c, :] for x in cum]
        e_pos = [jnp.exp(x) for x in cum]
        e_neg = [jnp.exp(-x) for x in cum]
        e_end = [jnp.exp(x - y) for x, y in zip(cum_end, cum)]
        xs = [jnp.concatenate([stack(-kk[u] * jnp.exp(cum[u] - lw[u])), stack(r[u] * e_pos[u])],
                              axis=0).astype(BF16) for u in n_u]
        ws = [jnp.concatenate([stack(kka[u] * e_neg[u]), stack(k[u] * e_neg[u])],
                              axis=0).astype(BF16) for u in n_u]
        we = [jnp.concatenate([stack(kka[u] * e_end[u]), stack(k[u] * e_end[u])],
                              axis=0).astype(BF16) for u in n_u]
        vs = [stack(x).astype(BF16) for x in v]
        gram = [_nt(xs[u], ws[u]) for u in n_u]
        a_ab = [jnp.where(strict, g[0:2 * c, 0:2 * c], 0.0) for g in gram]
        a_ak = [jnp.where(strict, g[0:2 * c, 2 * c:], 0.0).astype(BF16) for g in gram]
        a_r = [jnp.concatenate([jnp.where(incl, g[2 * c:, 0:2 * c], 0.0),
                                jnp.where(incl, g[2 * c:, 2 * c:], 0.0)], axis=1).astype(BF16)
               for g in gram]
        return xs, we, vs, cum_end, a_ab, a_ak, a_r

    def solve_stage(ch, pre, gs):
        rows = slice(ch * c, (ch + 1) * c)
        r, v, k, kk, kka, gate, lw_all = pre
        xs, we, vs, cum_end, a_ab, a_ak, a_r = gs
        inv = [eye + a for a in a_ab]
        pw = [x.astype(BF16) for x in a_ab]
        pw = [_mm(x, x).astype(BF16) for x in pw]
        n_lvl = c.bit_length() - 2
        for lvl in range(n_lvl):
            if lvl < n_lvl - 1:
                both = [_mm(x, jnp.concatenate([x, i.astype(BF16)], axis=1)) for i, x in zip(inv, pw)]
                inv = [i + b[:, 2 * c:] for i, b in zip(inv, both)]
                pw = [b[:, 0:2 * c].astype(BF16) for b in both]
            else:
                inv = [i + _mm(x, i.astype(BF16)) for i, x in zip(inv, pw)]
        s = [s_scr[u] for u in n_u]
        z = [_nt(xs[u], s[u].astype(BF16)) for u in n_u]
        rhs = [z[u][0:2 * c] + _mm(a_ak[u], vs[u]) for u in n_u]
        uu = [_mm(inv[u].astype(BF16), rhs[u].astype(BF16)).astype(BF16) for u in n_u]
        uv = [jnp.concatenate([uu[u], vs[u]], axis=0) for u in n_u]
        y2 = [z[u][2 * c:] + _mm(a_r[u], uv[u]) for u in n_u]
        for u in n_u:
            s_scr[u] = s[u] * jnp.exp(cum_end[u]) + _tn(uv[u], we[u])
        out = _wkv_post(tall([y2[u][0:c] + y2[u][c:] for u in n_u]), tall(r), tall(k), tall(v),
                        gate, per_pair(rk_ref), per_pair(lnw_ref), per_pair(lnb_ref), head_sums)
        for u, (bi, p) in enumerate(units):
            ya_o[bi, rows, slab[p]] = out[u * c:(u + 1) * c].astype(ya_o.dtype)

    n_chunks = tt // c
    pre = prep(0)
    for ch in range(n_chunks):
        gs = gram_stage(pre)
        nxt = prep(ch + 1) if ch + 1 < n_chunks else None
        solve_stage(ch, pre, gs)
        pre = nxt

    @pl.when(j == pl.num_programs(1) - 1)
    def _():
        for u, (bi, p) in enumerate(units):
            s = s_scr[u]
            s_o[bi, 2 * p] = s[0:HEAD_A, 0:HEAD_A]
            s_o[bi, 2 * p + 1] = s[HEAD_A:, HEAD_A:]


def _wkv_prompt(mixed, consts, *, nb, tt):
    batch, seq, _ = mixed.shape
    blk = lambda w: pl.BlockSpec((nb, tt, w), lambda b, j: (b, j, 0))
    return pl.pallas_call(
        functools.partial(_wkv_prompt_kernel, nb, tt),
        grid=(batch // nb, seq // tt),
        in_specs=[blk(N_SHIFT)] + [_resident(x.shape) for x in consts],
        out_specs=[blk(D_A), pl.BlockSpec((nb, H_A, HEAD_A, HEAD_A), lambda b, j: (b, 0, 0, 0))],
        out_shape=[jax.ShapeDtypeStruct((batch, seq, D_A), BF16),
                   jax.ShapeDtypeStruct((batch, H_A, HEAD_A, HEAD_A), F32)],
        scratch_shapes=[pltpu.VMEM((nb * H_A // 2, PAIR, PAIR), F32)],
        compiler_params=_params(2),
        name="wkv_prompt",
    )(mixed, *consts)


def _wkv_sample_kernel(n_t, r_ref, k_ref, v_ref, lora_ref,
                       w0_ref, w2_ref, a0_ref, a2_ref, g2_ref, kk_ref, ka_ref,
                       rk_ref, lnw_ref, lnb_ref, bd2_ref, s_ref,
                       ya_o, s_o, yt_scr):
    n = HEAD_A
    n_b = r_ref.shape[1]
    tall = lambda xs: jnp.concatenate(xs, axis=0)
    rows_of = lambda ref: tall([ref[t] for t in range(n_t)])
    head_sums = functools.partial(_sums_stacked, bd2=bd2_ref[...])
    r, v, lora = rows_of(r_ref), rows_of(v_ref), rows_of(lora_ref)
    lw, a, gate = _wkv_decay_gate(lora[:, 0:LORA_W + LORA_A], lora[:, LORA_W + LORA_A:],
                                  w0_ref[...], w2_ref[...], a0_ref[...], a2_ref[...], g2_ref[...])
    k, kk, kka = _wkv_keys(rows_of(k_ref), a, kk_ref[...], ka_ref[...], head_sums)

    tr = lambda x: [x[t * n_b:(t + 1) * n_b].T for t in range(n_t)]
    kk_t, kka_t, k_t, r_t, v_t, w_t = tr(kk), tr(kka), tr(k), tr(r), tr(v), tr(jnp.exp(lw))
    rid = lax.broadcasted_iota(jnp.int32, (SUBLANES, n_b), 0)

    for hh in range(2):
        keys = slice(hh * n, (hh + 1) * n)
        for ig in range(n // SUBLANES):
            y_tiles = [jnp.zeros((SUBLANES, n_b), F32) for _ in range(n_t)]
            for ii in range(SUBLANES):
                i = ig * SUBLANES + ii
                s = s_ref[hh, i]
                for t in range(n_t):
                    sa = jnp.sum(s * -kk_t[t][keys], axis=0, keepdims=True)
                    v_row = v_t[t][hh * n + i:hh * n + i + 1]
                    s = s * w_t[t][keys] + sa * kka_t[t][keys] + v_row * k_t[t][keys]
                    y_row = jnp.sum(s * r_t[t][keys], axis=0, keepdims=True)
                    y_tiles[t] = jnp.where(rid == ii, y_row, y_tiles[t])
                s_o[hh, i] = s
            for t in range(n_t):
                yt_scr[t, hh * n + ig * SUBLANES:hh * n + (ig + 1) * SUBLANES, :] = y_tiles[t]

    out = _wkv_post(tall([yt_scr[t].T for t in range(n_t)]), r, k, v, gate,
                    rk_ref[...], lnw_ref[...], lnb_ref[...], head_sums)
    for t in range(n_t):
        ya_o[t] = out[t * n_b:(t + 1) * n_b].astype(ya_o.dtype)


def _wkv_sample(mixed, w0, w2p, a0, a2p, g2, k_k, k_a, rk, lnw, lnb, bd2, s0):
    n_t, n_b, _ = mixed.shape
    slab = lambda first: pl.BlockSpec((n_t, n_b, PAIR), lambda p: (0, 0, first + p))
    lora = pl.BlockSpec((n_t, n_b, 2 * PAIR), lambda p: (0, 0, 3 * D_A // (2 * PAIR)))
    par = pl.BlockSpec((1, PAIR), lambda p: (0, p))
    low = pl.BlockSpec((PAIR, PAIR), lambda p: (0, p))
    st = pl.BlockSpec((2, HEAD_A, HEAD_A, n_b), lambda p: (p, 0, 0, 0))
    n_slab = D_A // PAIR
    return pl.pallas_call(
        functools.partial(_wkv_sample_kernel, n_t),
        grid=(H_A // 2,),
        in_specs=[slab(0), slab(n_slab), slab(2 * n_slab), lora,
                  par, low, par, low, low, par, par, par, par, par, _resident(bd2.shape), st],
        out_specs=[slab(0), st],
        out_shape=[jax.ShapeDtypeStruct((n_t, n_b, D_A), F32),
                   jax.ShapeDtypeStruct(s0.shape, F32)],
        scratch_shapes=[pltpu.VMEM((n_t, PAIR, n_b), F32)],
        compiler_params=_params(1),
        name="wkv_sample",
    )(mixed, mixed, mixed, mixed, w0, w2p, a0, a2p, g2, k_k, k_a, rk, lnw, lnb, bd2, s0)


def _ret_chunk(q, k, v, g, s, heads, dm_ref, qd_ref, kd_ref, cd_ref, gn_ref):
    n = range(len(q))
    qb = [x.astype(BF16) for x in q]
    kb = [x.astype(BF16) for x in k]
    vb = [x.astype(BF16) for x in v]
    inner = [(_nt(qb[u], kb[u]) * dm_ref[heads[u]]).astype(BF16) for u in n]
    q_dec = [(q[u].astype(F32) * qd_ref[heads[u]]).astype(BF16) for u in n]
    k_dec = [(k[u].astype(F32) * kd_ref[heads[u]]).astype(BF16) for u in n]
    if inner[0].shape[1] % LANES == 0:
        y = [_mm(jnp.concatenate([inner[u], q_dec[u]], axis=1),
                 jnp.concatenate([vb[u], s[u].astype(BF16)], axis=0)) for u in n]
    else:
        y = [_mm(inner[u], vb[u]) + _mm(q_dec[u], s[u].astype(BF16)) for u in n]
    s_new = [s[u] * cd_ref[heads[u]] + _tn(k_dec[u], vb[u]) for u in n]
    out = []
    for u in n:
        mu = jnp.mean(y[u], axis=-1, keepdims=True)
        d = y[u] - mu
        var = jnp.mean(d * d, axis=-1, keepdims=True)
        lanes = slice(heads[u] * HEAD_R, (heads[u] + 1) * HEAD_R)
        yn = d * lax.rsqrt(var + GN_EPS_R) * gn_ref[:, lanes]
        out.append(g[u] * _sigmoid(g[u]) * yn)
    return out, s_new


def _ret_prompt_kernel(nb, tt, q_ref, k_ref, v_ref, g_ref, dm_ref, qd_ref, kd_ref, cd_ref, gn_ref,
                       y_o, s_o, s_scr):
    j = pl.program_id(1)

    @pl.when(j == 0)
    def _():
        s_scr[...] = jnp.zeros_like(s_scr)

    c = RET_CHUNK
    units = [(bi, hh) for bi in range(nb) for hh in range(H_R)]
    heads = [hh for _, hh in units]
    lanes = [slice(hh * HEAD_R, (hh + 1) * HEAD_R) for hh in heads]
    s = [s_scr[u] for u in range(len(units))]
    for ch in range(tt // c):
        rows = slice(ch * c, (ch + 1) * c)
        ld = lambda ref: [ref[bi, rows, lanes[u]] for u, (bi, _) in enumerate(units)]
        y, s = _ret_chunk(ld(q_ref), ld(k_ref), ld(v_ref), ld(g_ref), s, heads,
                          dm_ref, qd_ref, kd_ref, cd_ref, gn_ref)
        for u, (bi, _) in enumerate(units):
            y_o[bi, rows, lanes[u]] = y[u].astype(y_o.dtype)
    for u in range(len(units)):
        s_scr[u] = s[u]

    @pl.when(j == pl.num_programs(1) - 1)
    def _():
        for u, (bi, hh) in enumerate(units):
            s_o[bi, hh] = s_scr[u]


def _ret_prompt(q, k, v, g, dm, qd, kd, cd, gn, *, nb, tt):
    batch, seq, _ = q.shape
    blk = pl.BlockSpec((nb, tt, D_R), lambda b, j: (b, j, 0))
    consts = (dm, qd, kd, cd, gn)
    return pl.pallas_call(
        functools.partial(_ret_prompt_kernel, nb, tt),
        grid=(batch // nb, seq // tt),
        in_specs=[blk] * 4 + [_resident(x.shape) for x in consts],
        out_specs=[blk, pl.BlockSpec((nb, H_R, HEAD_R, HEAD_R), lambda b, j: (b, 0, 0, 0))],
        out_shape=[jax.ShapeDtypeStruct((batch, seq, D_R), BF16),
                   jax.ShapeDtypeStruct((batch, H_R, HEAD_R, HEAD_R), F32)],
        scratch_shapes=[pltpu.VMEM((nb * H_R, HEAD_R, HEAD_R), F32)],
        compiler_params=_params(2),
        name="ret_prompt",
    )(q, k, v, g, *consts)


def _ret_sample_kernel(n_t, bb, q_ref, k_ref, v_ref, g_ref, dm_ref, qd_ref, kd_ref, cd_ref,
                       gn_ref, s_ref, y_o, s_o):
    rid = lax.broadcasted_iota(jnp.int32, (SUBLANES, HEAD_R), 0)
    units = [(bi, hh) for bi in range(bb) for hh in range(H_R)]
    heads = [hh for _, hh in units]
    lanes = [slice(hh * HEAD_R, (hh + 1) * HEAD_R) for hh in heads]

    def seq_rows(ref):
        outs = []
        for u, (bi, _) in enumerate(units):
            out = jnp.zeros((SUBLANES, HEAD_R), F32)
            for t in range(n_t):
                out = jnp.where(rid == t, jnp.broadcast_to(ref[t, bi:bi + 1, lanes[u]], out.shape), out)
            outs.append(out)
        return outs

    y, s_new = _ret_chunk(seq_rows(q_ref), seq_rows(k_ref), seq_rows(v_ref), seq_rows(g_ref),
                          [s_ref[bi, hh] for bi, hh in units], heads,
                          dm_ref, qd_ref, kd_ref, cd_ref, gn_ref)
    for u, (bi, hh) in enumerate(units):
        s_o[bi, hh] = s_new[u]
        for t in range(n_t):
            y_o[t, bi:bi + 1, lanes[u]] = y[u][t:t + 1].astype(y_o.dtype)


def _ret_sample(q, k, v, g, dm, qd, kd, cd, gn, s0, *, bb):
    n_t, n_b, _ = q.shape
    consts = (dm, qd, kd, cd, gn)
    blk = pl.BlockSpec((n_t, bb, D_R), lambda i: (0, i, 0))
    st = pl.BlockSpec((bb, H_R, HEAD_R, HEAD_R), lambda i: (i, 0, 0, 0))
    return pl.pallas_call(
        functools.partial(_ret_sample_kernel, n_t, bb),
        grid=(n_b // bb,),
        in_specs=[blk] * 4 + [_resident(x.shape) for x in consts] + [st],
        out_specs=[blk, st],
        out_shape=[jax.ShapeDtypeStruct((n_t, n_b, D_R), F32),
                   jax.ShapeDtypeStruct(s0.shape, F32)],
        compiler_params=_params(1),
        name="ret_sample",
    )(q, k, v, g, *consts, s0)


def _rope_tables(pos):
    half = HEAD_R // 2
    inv = ROPE_BASE ** (-jnp.arange(half, dtype=F32) / half)
    ang = pos.astype(F32)[:, None] * inv[None, :]
    cos, sin = jnp.cos(ang), jnp.sin(ang)
    return jnp.concatenate([cos, cos], axis=1), jnp.concatenate([-sin, sin], axis=1)


def _ret_tables(c):
    lg = jnp.log1p(-jnp.exp2(-5.0 - jnp.arange(H_R, dtype=F32)))
    idx = jnp.arange(c, dtype=F32)
    diff = idx[:, None] - idx[None, :]
    dmask = jnp.where(diff >= 0, jnp.exp(lg[:, None, None] * jnp.maximum(diff, 0.0)), 0.0)
    ones = jnp.ones((1, 1, HEAD_R), F32)
    qdec = jnp.exp(lg[:, None] * (idx + 1.0))[:, :, None] * ones
    kdec = jnp.exp(lg[:, None] * (c - 1.0 - idx))[:, :, None] * ones
    cdec = jnp.exp(lg * c)[:, None, None] * ones
    extra = -c % SUBLANES
    dmask = jnp.pad(dmask, ((0, 0), (0, extra), (0, extra)))
    qdec = jnp.pad(qdec, ((0, 0), (0, extra), (0, 0)))
    kdec = jnp.pad(kdec, ((0, 0), (0, extra), (0, 0)))
    return dmask, qdec, kdec, cdec


def _block_ones(n, block):
    idx = jnp.arange(n) // block
    return (idx[:, None] == idx[None, :]).astype(BF16)


def kernel(x_prompt, x_sample, state_shift, state_wkv, state_ret, norm_g, ffn1_wg, ffn1_wu, ffn1_wd,
           w_in, mu_shift, w0, w2, a0, a2, g2, k_k, k_a, r_k, lnx_w, lnx_b, ret_gn_w, w_out,
           ffn2_wg, ffn2_wu, ffn2_wd):
    assert norm_g.shape[0] == 1, "single-layer configuration"
    bp, tp, _ = x_prompt.shape
    bs, ts, _ = x_sample.shape
    l = 0
    ng = norm_g[l]
    f1 = (ffn1_wg[l].astype(BF16), ffn1_wu[l].astype(BF16), ffn1_wd[l].astype(BF16))
    f2 = (ffn2_wg[l].astype(BF16), ffn2_wu[l].astype(BF16), ffn2_wd[l].astype(BF16))
    win = w_in[l].astype(BF16)
    wo = w_out[l].astype(BF16)
    row = lambda t: t[l].reshape(1, -1)
    zpad = jnp.zeros((LORA_W, D_A), BF16)
    w2p = jnp.concatenate([w2[l].astype(BF16), zpad], axis=0)
    a2p = jnp.concatenate([zpad, a2[l].astype(BF16)], axis=0)
    proj_consts = (ng, win, row(mu_shift))
    rk, lnw, lnb, gn = row(r_k), row(lnx_w), row(lnx_b), row(ret_gn_w)
    wkv_params = (row(w0), w2p, row(a0), a2p, g2[l].astype(BF16), row(k_k), row(k_a), rk, lnw, lnb)
    bd_pair = _block_ones(PAIR, HEAD_A)
    bd2 = jnp.concatenate([bd_pair, bd_pair], axis=0)
    tri = (jnp.arange(WKV_CHUNK)[:, None] >= jnp.arange(WKV_CHUNK)[None, :]).astype(BF16)
    tri2 = jnp.concatenate([tri, tri], axis=1)

    xp = x_prompt.reshape(bp * tp, D_MODEL)
    x1p = _ffn(xp, ng, *f1, 0, 1, 512)
    cos_p, sin_p = _rope_tables(jnp.arange(tp, dtype=jnp.int32))
    tm_p = 256
    (mixed, q, kr, vr, gr, hl_p) = _proj(
        x1p, None, *proj_consts, cos_p, sin_p, n_t=1, rows_per_t=tm_p, lag=1,
        tiles_per_seq=tp // tm_p, qkv_dtype=BF16)
    seq3 = lambda t: t.reshape(bp, tp, -1)
    ya_p, wkv_p = _wkv_prompt(seq3(mixed), (*wkv_params, bd2, tri2), nb=4, tt=256)
    ya_p = ya_p.reshape(bp * tp, D_A)
    yr_p, ret_p = _ret_prompt(*map(seq3, (q, kr, vr, gr)), *_ret_tables(RET_CHUNK), gn, nb=4, tt=256)
    yr_p = yr_p.reshape(bp * tp, D_R)
    yp = _ffn(x1p, ng, *f2, 4, 5, 512, mix=(ya_p, yr_p, wo))

    m_s = bs * ts
    xs = x_sample.transpose(1, 0, 2)
    x1s = _ffn(xs.reshape(m_s, D_MODEL), ng, *f1, 0, 1, m_s)
    cos_s, sin_s = _rope_tables(PAST_LEN + jnp.arange(ts, dtype=jnp.int32))
    rows_per_t = 32
    tab = lambda t: jnp.broadcast_to(t[:, None, :], (ts, bs, HEAD_R))
    outs = _proj(x1s.reshape(ts, bs, D_MODEL),
                 state_shift[l].reshape(bs // rows_per_t, rows_per_t, D_MODEL),
                 *proj_consts, tab(cos_s), tab(sin_s), n_t=ts, rows_per_t=rows_per_t,
                 lag=rows_per_t, tiles_per_seq=1, qkv_dtype=F32)
    (mixed, q, kr, vr, gr, hl_s) = outs
    ya_s, wkv_s = _wkv_sample(mixed, *wkv_params, bd2, state_wkv[l].transpose(1, 2, 3, 0))
    wkv_s = wkv_s.transpose(3, 0, 1, 2)
    yr_s, ret_s = _ret_sample(q, kr, vr, gr, *_ret_tables(min(RET_CHUNK, ts)), gn, state_ret[l],
                              bb=SUBLANES)
    ys = _ffn(x1s, ng, *f2, 4, 5, m_s,
              mix=(ya_s.reshape(m_s, D_A), yr_s.reshape(m_s, D_R), wo))
    ys = ys.reshape(ts, bs, D_MODEL).transpose(1, 0, 2)

    return (yp.reshape(bp, tp, D_MODEL), ys,
            hl_p.reshape(1, bp, D_MODEL), wkv_p[None], ret_p[None],
            hl_s.reshape(1, bs, D_MODEL), wkv_s[None], ret_s[None])
```

```python
import functools

import jax
import jax.numpy as jnp
from jax import lax
from jax.experimental import pallas as pl
from jax.experimental.pallas import tpu as pltpu

F32 = jnp.float32
BF16 = jnp.bfloat16

D_MODEL = 1024
D_A = 512
HEAD_A = 64
H_A = D_A // HEAD_A
D_R = 512
H_R = 4
HEAD_R = D_R // H_R
LORA_W, LORA_A, LORA_G = 64, 64, 128
D_FF = 2816
RET_CHUNK = 128
ROPE_BASE = 10000.0
EPS = 1e-6
GN_EPS_A = 64e-5
GN_EPS_R = 1e-5
N_SHIFT = 3 * D_A + LORA_W + LORA_A + LORA_G
N_COLS = N_SHIFT + 4 * D_R
PAST_LEN = 16384

LANES = 128
SUBLANES = 8
VMEM_LIMIT = 52 * 1024 * 1024

MXU_DIM = 256
FF_SPLIT = 6 * MXU_DIM
WKV_CHUNK = 64
PAIR = 2 * HEAD_A


def _nt(a, b):
    return lax.dot_general(a, b, (((1,), (1,)), ((), ())), preferred_element_type=F32)


def _tn(a, b):
    return lax.dot_general(a, b, (((0,), (0,)), ((), ())), preferred_element_type=F32)


def _mm(a, b):
    return jnp.dot(a, b, preferred_element_type=F32)


def _split_hi_lo(x):
    hi = x.astype(BF16)
    lo = (x - hi.astype(F32)).astype(BF16)
    return hi, lo


def _rms(x, g):
    return x * lax.rsqrt(jnp.mean(x * x, axis=-1, keepdims=True) + EPS) * g


def _softplus(x):
    return jnp.maximum(x, 0.0) + jnp.log(1.0 + jnp.exp(-jnp.abs(x)))


def _sigmoid(x):
    return 1.0 / (1.0 + jnp.exp(-x))


def _resident(shape):
    nd = len(shape)
    return pl.BlockSpec(shape, lambda *_: (0,) * nd, pipeline_mode=pl.Buffered(1))


def _params(n_axes):
    return pltpu.CompilerParams(dimension_semantics=("arbitrary",) * n_axes,
                                vmem_limit_bytes=VMEM_LIMIT)


def _ffn_kernel(with_mix, g_in, g_out, *refs):
    if with_mix:
        x_ref, ya_ref, yr_ref, wo_ref, ng_ref, wg_ref, wu_ref, wd_ref, o_ref = refs
    else:
        x_ref, ng_ref, wg_ref, wu_ref, wd_ref, o_ref = refs
    x = x_ref[...]
    if with_mix:
        mix = (_mm(ya_ref[...].astype(BF16), wo_ref[0:D_A, :])
               + _mm(yr_ref[...].astype(BF16), wo_ref[D_A:, :]))
        x = x + _rms(mix, ng_ref[3:4, :])
    h = _rms(x, ng_ref[g_in:g_in + 1, :]).astype(BF16)
    acc = None
    for cols in (slice(0, FF_SPLIT), slice(FF_SPLIT, D_FF)):
        gate = _mm(h, wg_ref[:, cols])
        up = _mm(h, wu_ref[:, cols])
        act = (gate * _sigmoid(gate) * up).astype(BF16)
        part = _mm(act, wd_ref[cols, :])
        acc = part if acc is None else acc + part
    o_ref[...] = x + 0.5 * _rms(acc, ng_ref[g_out:g_out + 1, :])


def _ffn(x, ng, wg, wu, wd, g_in, g_out, tm, mix=None):
    m = x.shape[0]
    row = lambda w: pl.BlockSpec((tm, w), lambda i: (i, 0))
    if mix is None:
        args = (x, ng, wg, wu, wd)
        specs = [row(D_MODEL), _resident(ng.shape), _resident(wg.shape), _resident(wu.shape),
                 _resident(wd.shape)]
    else:
        ya, yr, wo = mix
        args = (x, ya, yr, wo, ng, wg, wu, wd)
        specs = [row(D_MODEL), row(D_A), row(D_R), _resident(wo.shape), _resident(ng.shape),
                 _resident(wg.shape), _resident(wu.shape), _resident(wd.shape)]
    return pl.pallas_call(
        functools.partial(_ffn_kernel, mix is not None, g_in, g_out),
        grid=(m // tm,),
        in_specs=specs,
        out_specs=row(D_MODEL),
        out_shape=jax.ShapeDtypeStruct((m, D_MODEL), F32),
        compiler_params=_params(1),
        name="ffn_mix" if mix is not None else "ffn",
    )(*args)


def _lag_rows(cur, first, lag):
    if lag % SUBLANES == 0:
        return jnp.concatenate([first, cur[:-lag]], axis=0)
    assert lag == 1
    rolled = pltpu.roll(cur, 1, 0)
    rid = lax.broadcasted_iota(jnp.int32, (SUBLANES, cur.shape[1]), 0)
    head = jnp.where(rid == 0, first, rolled[0:SUBLANES])
    return jnp.concatenate([head, rolled[SUBLANES:]], axis=0)


def _proj_kernel(n_t, lag, tiles_per_seq, has_prev, *refs):
    it = iter(refs)
    x_ref = next(it)
    prev_ref = next(it) if has_prev else None
    (ng_ref, win_ref, mu_ref, cos_ref, sin_ref,
     mix_o, q_o, kr_o, vr_o, gr_o, hl_o, carry_scr) = tuple(it)

    def load(ref):
        if n_t == 1:
            return ref[...]
        return jnp.concatenate([ref[t] for t in range(n_t)], axis=0)

    def store(ref, val, cols=slice(None)):
        if n_t == 1:
            ref[:, cols] = val.astype(ref.dtype)
        else:
            rows = val.shape[0] // n_t
            for t in range(n_t):
                ref[t, :, cols] = val[t * rows:(t + 1) * rows].astype(ref.dtype)

    x = load(x_ref)
    tm = x.shape[0]
    h = _rms(x, ng_ref[2:3, :])
    hl_o[0] = h[tm - lag:, :]
    hb = h.astype(BF16)
    seq_start = (pl.program_id(0) % tiles_per_seq) == 0
    prev_b = prev_ref[0].astype(BF16) if has_prev else None

    for c0 in range(0, N_SHIFT, D_A):
        cols = slice(c0, min(c0 + D_A, N_SHIFT))
        cur = _mm(hb, win_ref[:, cols])
        if has_prev:
            first = _mm(prev_b, win_ref[:, cols])
        else:
            first = jnp.zeros((lag, cur.shape[1]), F32)
        if tiles_per_seq > 1:
            first = jnp.where(seq_start, first, carry_scr[0:lag, cols])
            carry_scr[0:lag, cols] = cur[tm - lag:, :]
        prv = _lag_rows(cur, first, lag)
        store(mix_o, cur + (prv - cur) * mu_ref[:, cols], cols)

    cos2 = load(cos_ref)
    sin2 = load(sin_ref)

    def rope(t):
        parts = []
        for hh in range(H_R):
            th = t[:, hh * HEAD_R:(hh + 1) * HEAD_R]
            parts.append(th * cos2 + pltpu.roll(th, HEAD_R // 2, 1) * sin2)
        return jnp.concatenate(parts, axis=1)

    ret = lambda c: _mm(hb, win_ref[:, N_SHIFT + c * D_R:N_SHIFT + (c + 1) * D_R])
    q = ret(0)
    kr = ret(1)
    store(q_o, rope(q))
    vr = ret(2)
    store(kr_o, rope(kr) * (HEAD_R ** -0.5))
    gr = ret(3)
    store(vr_o, vr)
    store(gr_o, gr)


def _proj(x, prev, ng, win, mu, cos2, sin2, *, n_t, rows_per_t, lag, tiles_per_seq, qkv_dtype):
    tm = n_t * rows_per_t
    if n_t == 1:
        m = x.shape[0]
        n_tiles = m // tm
        row = lambda w: pl.BlockSpec((tm, w), lambda i: (i, 0))
        shp = lambda w, dt: jax.ShapeDtypeStruct((m, w), dt)
        tab = pl.BlockSpec((tm, HEAD_R), lambda i: (i % tiles_per_seq, 0))
    else:
        m = x.shape[0] * x.shape[1]
        n_tiles = x.shape[1] // rows_per_t
        row = lambda w: pl.BlockSpec((n_t, rows_per_t, w), lambda i: (0, i, 0))
        shp = lambda w, dt: jax.ShapeDtypeStruct((n_t, m // n_t, w), dt)
        tab = pl.BlockSpec((n_t, rows_per_t, HEAD_R), lambda i: (0, i, 0))
    n_seq = n_tiles // tiles_per_seq
    hl_spec = pl.BlockSpec((1, lag, D_MODEL), lambda i: (i // tiles_per_seq, 0, 0))
    args = [x]
    specs = [row(D_MODEL)]
    if prev is not None:
        args.append(prev)
        specs.append(pl.BlockSpec((1, lag, D_MODEL), lambda i: (i // tiles_per_seq, 0, 0)))
    consts = (ng, win, mu)
    args += list(consts) + [cos2, sin2]
    specs += [_resident(c.shape) for c in consts] + [tab, tab]
    out_shape = ([shp(N_SHIFT, F32)] + [shp(D_R, qkv_dtype)] * 3 + [shp(D_R, F32)]
                 + [jax.ShapeDtypeStruct((n_seq, lag, D_MODEL), F32)])
    out_specs = [row(N_SHIFT)] + [row(D_R)] * 4 + [hl_spec]
    return pl.pallas_call(
        functools.partial(_proj_kernel, n_t, lag, tiles_per_seq, prev is not None),
        grid=(n_tiles,),
        in_specs=specs,
        out_specs=out_specs,
        out_shape=out_shape,
        scratch_shapes=[pltpu.VMEM((max(lag, SUBLANES), N_SHIFT), F32)],
        compiler_params=_params(1),
        name="proj",
    )(*args)


def _sums_stacked(x, bd2):
    return _mm(jnp.concatenate(_split_hi_lo(x), axis=1), bd2)


def _wkv_post(y, r, k, v, g, rk, lw_g, lb_g, head_sums):
    inv_n = 1.0 / HEAD_A
    mu = head_sums(y) * inv_n
    d = y - mu
    var = head_sums(d * d) * inv_n
    yn = d * lax.rsqrt(var + GN_EPS_A) * lw_g + lb_g
    bonus = head_sums(r * k * rk) * v
    return (yn + bonus) * g


def _wkv_decay_gate(wa, gd, w0, w2p, a0, a2p, g2):
    w_pre = w0 + _mm(jnp.tanh(wa).astype(BF16), w2p)
    lw = -jnp.exp(-_softplus(-w_pre) - 0.5)
    a = _sigmoid(a0 + _mm(wa.astype(BF16), a2p))
    g = _mm(_sigmoid(gd).astype(BF16), g2)
    return lw, a, g


def _wkv_keys(k, a, k_k, k_a, head_sums):
    kk = k * k_k
    kk = kk / jnp.maximum(jnp.sqrt(head_sums(kk * kk)), 1e-12)
    return k * (1.0 + (a - 1.0) * k_a), kk, kk * a


def _wkv_prompt_kernel(nb, tt, mix_ref, w0_ref, w2_ref, a0_ref, a2_ref, g2_ref, kk_ref, ka_ref,
                       rk_ref, lnw_ref, lnb_ref, bd2_ref, tri2_ref,
                       ya_o, s_o, s_scr):
    c = WKV_CHUNK
    j = pl.program_id(1)

    @pl.when(j == 0)
    def _():
        s_scr[...] = jnp.zeros_like(s_scr)

    lane = lax.broadcasted_iota(jnp.int32, (c, PAIR), 1)
    first = lane < HEAD_A

    def stack(x):
        return jnp.concatenate([jnp.where(first, x, 0.0), jnp.where(first, 0.0, x)], axis=0)

    ri = lax.broadcasted_iota(jnp.int32, (2 * c, 2 * c), 0) % c
    ci = lax.broadcasted_iota(jnp.int32, (2 * c, 2 * c), 1) % c
    strict = ri > ci
    incl = ri >= ci
    eye = (lax.broadcasted_iota(jnp.int32, (2 * c, 2 * c), 0)
           == lax.broadcasted_iota(jnp.int32, (2 * c, 2 * c), 1)).astype(F32)
    bd2 = bd2_ref[...]
    tri2 = tri2_ref[...]
    head_sums = functools.partial(_sums_stacked, bd2=bd2)

    n_pairs = H_A // 2
    units = [(bi, p) for bi in range(nb) for p in range(n_pairs)]
    n_u = range(len(units))
    slab = [slice(p * PAIR, (p + 1) * PAIR) for p in range(n_pairs)]

    tall = lambda xs: jnp.concatenate(xs, axis=0)
    per_pair = lambda ref: tall([jnp.broadcast_to(ref[:, slab[p]], (c, PAIR)) for _, p in units])
    split = lambda x: [x[u * c:(u + 1) * c] for u in n_u]

    def prep(ch):
        rows = slice(ch * c, (ch + 1) * c)
        ld = lambda c0: [mix_ref[bi, rows, c0 + p * PAIR:c0 + (p + 1) * PAIR] for bi, p in units]
        r, k_raw, v = ld(0), ld(D_A), ld(2 * D_A)
        lora = [_wkv_decay_gate(mix_ref[bi, rows, 3 * D_A:3 * D_A + LORA_W + LORA_A],
                                mix_ref[bi, rows, 3 * D_A + LORA_W + LORA_A:N_SHIFT],
                                w0_ref[...], w2_ref[...], a0_ref[...], a2_ref[...], g2_ref[...])
                for bi in range(nb)]
        lw_all = [x[0] for x in lora]
        a = tall([lora[bi][1][:, slab[p]] for bi, p in units])
        gate = tall([lora[bi][2][:, slab[p]] for bi, p in units])
        k, kk, kka = map(split, _wkv_keys(tall(k_raw), a, per_pair(kk_ref), per_pair(ka_ref),
                                          head_sums))
        return r, v, k, kk, kka, gate, lw_all

    def gram_stage(pre):
        r, v, k, kk, kka, gate, lw_all = pre
        cum_all = [_mm(tri2, jnp.concatenate(_split_hi_lo(x), axis=0)) for x in lw_all]
        lw = [lw_all[bi][:, slab[p]] for bi, p in units]
        cum = [cum_all[bi][:, slab[p]] for bi, p in units]
        cum_end = [x[c - 1:c, :] for x in cum]
        e_pos = [jnp.exp(x) for x in cum]
        e_neg = [jnp.exp(-x) for x in cum]
        e_end = [jnp.exp(x - y) for x, y in zip(cum_end, cum)]
        xs = [jnp.concatenate([stack(-kk[u] * jnp.exp(cum[u] - lw[u])), stack(r[u] * e_pos[u])],
                              axis=0).astype(BF16) for u in n_u]
        ws = [jnp.concatenate([stack(kka[u] * e_neg[u]), stack(k[u] * e_neg[u])],
                              axis=0).astype(BF16) for u in n_u]
        we = [jnp.concatenate([stack(kka[u] * e_end[u]), stack(k[u] * e_end[u])],
                              axis=0).astype(BF16) for u in n_u]
        vs = [stack(x).astype(BF16) for x in v]
        gram = [_nt(xs[u], ws[u]) for u in n_u]
        a_ab = [jnp.where(strict, g[0:2 * c, 0:2 * c], 0.0) for g in gram]
        a_ak = [jnp.where(strict, g[0:2 * c, 2 * c:], 0.0).astype(BF16) for g in gram]
        a_r = [jnp.concatenate([jnp.where(incl, g[2 * c:, 0:2 * c], 0.0),
                                jnp.where(incl, g[2 * c:, 2 * c:], 0.0)], axis=1).astype(BF16)
               for g in gram]
        return xs, we, vs, cum_end, a_ab, a_ak, a_r

    def solve_stage(ch, pre, gs):
        rows = slice(ch * c, (ch + 1) * c)
        r, v, k, kk, kka, gate, lw_all = pre
        xs, we, vs, cum_end, a_ab, a_ak, a_r = gs
        inv = [eye + a for a in a_ab]
        pw = [x.astype(BF16) for x in a_ab]
        pw = [_mm(x, x).astype(BF16) for x in pw]
        n_lvl = c.bit_length() - 2
        for lvl in range(n_lvl):
            if lvl < n_lvl - 1:
                both = [_mm(x, jnp.concatenate([x, i.astype(BF16)], axis=1)) for i, x in zip(inv, pw)]
                inv = [i + b[:, 2 * c:] for i, b in zip(inv, both)]
                pw = [b[:, 0:2 * c].astype(BF16) for b in both]
            else:
                inv = [i + _mm(x, i.astype(BF16)) for i, x in zip(inv, pw)]
        s = [s_scr[u] for u in n_u]
        z = [_nt(xs[u], s[u].astype(BF16)) for u in n_u]
        rhs = [z[u][0:2 * c] + _mm(a_ak[u], vs[u]) for u in n_u]
        uu = [_mm(inv[u].astype(BF16), rhs[u].astype(BF16)).astype(BF16) for u in n_u]
        uv = [jnp.concatenate([uu[u], vs[u]], axis=0) for u in n_u]
        y2 = [z[u][2 * c:] + _mm(a_r[u], uv[u]) for u in n_u]
        for u in n_u:
            s_scr[u] = s[u] * jnp.exp(cum_end[u]) + _tn(uv[u], we[u])
        out = _wkv_post(tall([y2[u][0:c] + y2[u][c:] for u in n_u]), tall(r), tall(k), tall(v),
                        gate, per_pair(rk_ref), per_pair(lnw_ref), per_pair(lnb_ref), head_sums)
        for u, (bi, p) in enumerate(units):
            ya_o[bi, rows, slab[p]] = out[u * c:(u + 1) * c].astype(ya_o.dtype)

    n_chunks = tt // c
    pre = prep(0)
    for ch in range(n_chunks):
        gs = gram_stage(pre)
        nxt = prep(ch + 1) if ch + 1 < n_chunks else None
        solve_stage(ch, pre, gs)
        pre = nxt

    @pl.when(j == pl.num_programs(1) - 1)
    def _():
        for u, (bi, p) in enumerate(units):
            s = s_scr[u]
            s_o[bi, 2 * p] = s[0:HEAD_A, 0:HEAD_A]
            s_o[bi, 2 * p + 1] = s[HEAD_A:, HEAD_A:]


def _wkv_prompt(mixed, consts, *, nb, tt):
    batch, seq, _ = mixed.shape
    blk = lambda w: pl.BlockSpec((nb, tt, w), lambda b, j: (b, j, 0))
    return pl.pallas_call(
        functools.partial(_wkv_prompt_kernel, nb, tt),
        grid=(batch // nb, seq // tt),
        in_specs=[blk(N_SHIFT)] + [_resident(x.shape) for x in consts],
        out_specs=[blk(D_A), pl.BlockSpec((nb, H_A, HEAD_A, HEAD_A), lambda b, j: (b, 0, 0, 0))],
        out_shape=[jax.ShapeDtypeStruct((batch, seq, D_A), BF16),
                   jax.ShapeDtypeStruct((batch, H_A, HEAD_A, HEAD_A), F32)],
        scratch_shapes=[pltpu.VMEM((nb * H_A // 2, PAIR, PAIR), F32)],
        compiler_params=_params(2),
        name="wkv_prompt",
    )(mixed, *consts)


def _wkv_sample_kernel(n_t, r_ref, k_ref, v_ref, lora_ref,
                       w0_ref, w2_ref, a0_ref, a2_ref, g2_ref, kk_ref, ka_ref,
                       rk_ref, lnw_ref, lnb_ref, bd2_ref, s_ref,
                       ya_o, s_o, yt_scr):
    n = HEAD_A
    n_b = r_ref.shape[1]
    tall = lambda xs: jnp.concatenate(xs, axis=0)
    rows_of = lambda ref: tall([ref[t] for t in range(n_t)])
    head_sums = functools.partial(_sums_stacked, bd2=bd2_ref[...])
    r, v, lora = rows_of(r_ref), rows_of(v_ref), rows_of(lora_ref)
    lw, a, gate = _wkv_decay_gate(lora[:, 0:LORA_W + LORA_A], lora[:, LORA_W + LORA_A:],
                                  w0_ref[...], w2_ref[...], a0_ref[...], a2_ref[...], g2_ref[...])
    k, kk, kka = _wkv_keys(rows_of(k_ref), a, kk_ref[...], ka_ref[...], head_sums)

    tr = lambda x: [x[t * n_b:(t + 1) * n_b].T for t in range(n_t)]
    kk_t, kka_t, k_t, r_t, v_t, w_t = tr(kk), tr(kka), tr(k), tr(r), tr(v), tr(jnp.exp(lw))
    rid = lax.broadcasted_iota(jnp.int32, (SUBLANES, n_b), 0)

    for hh in range(2):
        keys = slice(hh * n, (hh + 1) * n)
        for ig in range(n // SUBLANES):
            y_tiles = [jnp.zeros((SUBLANES, n_b), F32) for _ in range(n_t)]
            for ii in range(SUBLANES):
                i = ig * SUBLANES + ii
                s = s_ref[hh, i]
                for t in range(n_t):
                    sa = jnp.sum(s * -kk_t[t][keys], axis=0, keepdims=True)
                    v_row = v_t[t][hh * n + i:hh * n + i + 1]
                    s = s * w_t[t][keys] + sa * kka_t[t][keys] + v_row * k_t[t][keys]
                    y_row = jnp.sum(s * r_t[t][keys], axis=0, keepdims=True)
                    y_tiles[t] = jnp.where(rid == ii, y_row, y_tiles[t])
                s_o[hh, i] = s
            for t in range(n_t):
                yt_scr[t, hh * n + ig * SUBLANES:hh * n + (ig + 1) * SUBLANES, :] = y_tiles[t]

    out = _wkv_post(tall([yt_scr[t].T for t in range(n_t)]), r, k, v, gate,
                    rk_ref[...], lnw_ref[...], lnb_ref[...], head_sums)
    for t in range(n_t):
        ya_o[t] = out[t * n_b:(t + 1) * n_b].astype(ya_o.dtype)


def _wkv_sample(mixed, w0, w2p, a0, a2p, g2, k_k, k_a, rk, lnw, lnb, bd2, s0):
    n_t, n_b, _ = mixed.shape
    slab = lambda first: pl.BlockSpec((n_t, n_b, PAIR), lambda p: (0, 0, first + p))
    lora = pl.BlockSpec((n_t, n_b, 2 * PAIR), lambda p: (0, 0, 3 * D_A // (2 * PAIR)))
    par = pl.BlockSpec((1, PAIR), lambda p: (0, p))
    low = pl.BlockSpec((PAIR, PAIR), lambda p: (0, p))
    st = pl.BlockSpec((2, HEAD_A, HEAD_A, n_b), lambda p: (p, 0, 0, 0))
    n_slab = D_A // PAIR
    return pl.pallas_call(
        functools.partial(_wkv_sample_kernel, n_t),
        grid=(H_A // 2,),
        in_specs=[slab(0), slab(n_slab), slab(2 * n_slab), lora,
                  par, low, par, low, low, par, par, par, par, par, _resident(bd2.shape), st],
        out_specs=[slab(0), st],
        out_shape=[jax.ShapeDtypeStruct((n_t, n_b, D_A), F32),
                   jax.ShapeDtypeStruct(s0.shape, F32)],
        scratch_shapes=[pltpu.VMEM((n_t, PAIR, n_b), F32)],
        compiler_params=_params(1),
        name="wkv_sample",
    )(mixed, mixed, mixed, mixed, w0, w2p, a0, a2p, g2, k_k, k_a, rk, lnw, lnb, bd2, s0)


def _ret_chunk(q, k, v, g, s, heads, dm_ref, qd_ref, kd_ref, cd_ref, gn_ref):
    n = range(len(q))
    qb = [x.astype(BF16) for x in q]
    kb = [x.astype(BF16) for x in k]
    vb = [x.astype(BF16) for x in v]
    inner = [(_nt(qb[u], kb[u]) * dm_ref[heads[u]]).astype(BF16) for u in n]
    q_dec = [(q[u].astype(F32) * qd_ref[heads[u]]).astype(BF16) for u in n]
    k_dec = [(k[u].astype(F32) * kd_ref[heads[u]]).astype(BF16) for u in n]
    if inner[0].shape[1] % LANES == 0:
        y = [_mm(jnp.concatenate([inner[u], q_dec[u]], axis=1),
                 jnp.concatenate([vb[u], s[u].astype(BF16)], axis=0)) for u in n]
    else:
        y = [_mm(inner[u], vb[u]) + _mm(q_dec[u], s[u].astype(BF16)) for u in n]
    s_new = [s[u] * cd_ref[heads[u]] + _tn(k_dec[u], vb[u]) for u in n]
    out = []
    for u in n:
        mu = jnp.mean(y[u], axis=-1, keepdims=True)
        d = y[u] - mu
        var = jnp.mean(d * d, axis=-1, keepdims=True)
        lanes = slice(heads[u] * HEAD_R, (heads[u] + 1) * HEAD_R)
        yn = d * lax.rsqrt(var + GN_EPS_R) * gn_ref[:, lanes]
        out.append(g[u] * _sigmoid(g[u]) * yn)
    return out, s_new


def _ret_prompt_kernel(nb, tt, q_ref, k_ref, v_ref, g_ref, dm_ref, qd_ref, kd_ref, cd_ref, gn_ref,
                       y_o, s_o, s_scr):
    j = pl.program_id(1)

    @pl.when(j == 0)
    def _():
        s_scr[...] = jnp.zeros_like(s_scr)

    c = RET_CHUNK
    units = [(bi, hh) for bi in range(nb) for hh in range(H_R)]
    heads = [hh for _, hh in units]
    lanes = [slice(hh * HEAD_R, (hh + 1) * HEAD_R) for hh in heads]
    s = [s_scr[u] for u in range(len(units))]
    for ch in range(tt // c):
        rows = slice(ch * c, (ch + 1) * c)
        ld = lambda ref: [ref[bi, rows, lanes[u]] for u, (bi, _) in enumerate(units)]
        y, s = _ret_chunk(ld(q_ref), ld(k_ref), ld(v_ref), ld(g_ref), s, heads,
                          dm_ref, qd_ref, kd_ref, cd_ref, gn_ref)
        for u, (bi, _) in enumerate(units):
            y_o[bi, rows, lanes[u]] = y[u].astype(y_o.dtype)
    for u in range(len(units)):
        s_scr[u] = s[u]

    @pl.when(j == pl.num_programs(1) - 1)
    def _():
        for u, (bi, hh) in enumerate(units):
            s_o[bi, hh] = s_scr[u]


def _ret_prompt(q, k, v, g, dm, qd, kd, cd, gn, *, nb, tt):
    batch, seq, _ = q.shape
    blk = pl.BlockSpec((nb, tt, D_R), lambda b, j: (b, j, 0))
    consts = (dm, qd, kd, cd, gn)
    return pl.pallas_call(
        functools.partial(_ret_prompt_kernel, nb, tt),
        grid=(batch // nb, seq // tt),
        in_specs=[blk] * 4 + [_resident(x.shape) for x in consts],
        out_specs=[blk, pl.BlockSpec((nb, H_R, HEAD_R, HEAD_R), lambda b, j: (b, 0, 0, 0))],
        out_shape=[jax.ShapeDtypeStruct((batch, seq, D_R), BF16),
                   jax.ShapeDtypeStruct((batch, H_R, HEAD_R, HEAD_R), F32)],
        scratch_shapes=[pltpu.VMEM((nb * H_R, HEAD_R, HEAD_R), F32)],
        compiler_params=_params(2),
        name="ret_prompt",
    )(q, k, v, g, *consts)


def _ret_sample_kernel(n_t, bb, q_ref, k_ref, v_ref, g_ref, dm_ref, qd_ref, kd_ref, cd_ref,
                       gn_ref, s_ref, y_o, s_o):
    rid = lax.broadcasted_iota(jnp.int32, (SUBLANES, HEAD_R), 0)
    units = [(bi, hh) for bi in range(bb) for hh in range(H_R)]
    heads = [hh for _, hh in units]
    lanes = [slice(hh * HEAD_R, (hh + 1) * HEAD_R) for hh in heads]

    def seq_rows(ref):
        outs = []
        for u, (bi, _) in enumerate(units):
            out = jnp.zeros((SUBLANES, HEAD_R), F32)
            for t in range(n_t):
                out = jnp.where(rid == t, jnp.broadcast_to(ref[t, bi:bi + 1, lanes[u]], out.shape), out)
            outs.append(out)
        return outs

    y, s_new = _ret_chunk(seq_rows(q_ref), seq_rows(k_ref), seq_rows(v_ref), seq_rows(g_ref),
                          [s_ref[bi, hh] for bi, hh in units], heads,
                          dm_ref, qd_ref, kd_ref, cd_ref, gn_ref)
    for u, (bi, hh) in enumerate(units):
        s_o[bi, hh] = s_new[u]
        for t in range(n_t):
            y_o[t, bi:bi + 1, lanes[u]] = y[u][t:t + 1].astype(y_o.dtype)


def _ret_sample(q, k, v, g, dm, qd, kd, cd, gn, s0, *, bb):
    n_t, n_b, _ = q.shape
    consts = (dm, qd, kd, cd, gn)
    blk = pl.BlockSpec((n_t, bb, D_R), lambda i: (0, i, 0))
    st = pl.BlockSpec((bb, H_R, HEAD_R, HEAD_R), lambda i: (i, 0, 0, 0))
    return pl.pallas_call(
        functools.partial(_ret_sample_kernel, n_t, bb),
        grid=(n_b // bb,),
        in_specs=[blk] * 4 + [_resident(x.shape) for x in consts] + [st],
        out_specs=[blk, st],
        out_shape=[jax.ShapeDtypeStruct((n_t, n_b, D_R), F32),
                   jax.ShapeDtypeStruct(s0.shape, F32)],
        compiler_params=_params(1),
        name="ret_sample",
    )(q, k, v, g, *consts, s0)


def _rope_tables(pos):
    half = HEAD_R // 2
    inv = ROPE_BASE ** (-jnp.arange(half, dtype=F32) / half)
    ang = pos.astype(F32)[:, None] * inv[None, :]
    cos, sin = jnp.cos(ang), jnp.sin(ang)
    return jnp.concatenate([cos, cos], axis=1), jnp.concatenate([-sin, sin], axis=1)


def _ret_tables(c):
    lg = jnp.log1p(-jnp.exp2(-5.0 - jnp.arange(H_R, dtype=F32)))
    idx = jnp.arange(c, dtype=F32)
    diff = idx[:, None] - idx[None, :]
    dmask = jnp.where(diff >= 0, jnp.exp(lg[:, None, None] * jnp.maximum(diff, 0.0)), 0.0)
    ones = jnp.ones((1, 1, HEAD_R), F32)
    qdec = jnp.exp(lg[:, None] * (idx + 1.0))[:, :, None] * ones
    kdec = jnp.exp(lg[:, None] * (c - 1.0 - idx))[:, :, None] * ones
    cdec = jnp.exp(lg * c)[:, None, None] * ones
    extra = -c % SUBLANES
    dmask = jnp.pad(dmask, ((0, 0), (0, extra), (0, extra)))
    qdec = jnp.pad(qdec, ((0, 0), (0, extra), (0, 0)))
    kdec = jnp.pad(kdec, ((0, 0), (0, extra), (0, 0)))
    return dmask, qdec, kdec, cdec


def _block_ones(n, block):
    idx = jnp.arange(n) // block
    return (idx[:, None] == idx[None, :]).astype(BF16)


def kernel(x_prompt, x_sample, state_shift, state_wkv, state_ret, norm_g, ffn1_wg, ffn1_wu, ffn1_wd,
           w_in, mu_shift, w0, w2, a0, a2, g2, k_k, k_a, r_k, lnx_w, lnx_b, ret_gn_w, w_out,
           ffn2_wg, ffn2_wu, ffn2_wd):
    assert norm_g.shape[0] == 1, "single-layer configuration"
    bp, tp, _ = x_prompt.shape
    bs, ts, _ = x_sample.shape
    l = 0
    ng = norm_g[l]
    f1 = (ffn1_wg[l].astype(BF16), ffn1_wu[l].astype(BF16), ffn1_wd[l].astype(BF16))
    f2 = (ffn2_wg[l].astype(BF16), ffn2_wu[l].astype(BF16), ffn2_wd[l].astype(BF16))
    win = w_in[l].astype(BF16)
    wo = w_out[l].astype(BF16)
    row = lambda t: t[l].reshape(1, -1)
    zpad = jnp.zeros((LORA_W, D_A), BF16)
    w2p = jnp.concatenate([w2[l].astype(BF16), zpad], axis=0)
    a2p = jnp.concatenate([zpad, a2[l].astype(BF16)], axis=0)
    proj_consts = (ng, win, row(mu_shift))
    rk, lnw, lnb, gn = row(r_k), row(lnx_w), row(lnx_b), row(ret_gn_w)
    wkv_params = (row(w0), w2p, row(a0), a2p, g2[l].astype(BF16), row(k_k), row(k_a), rk, lnw, lnb)
    bd_pair = _block_ones(PAIR, HEAD_A)
    bd2 = jnp.concatenate([bd_pair, bd_pair], axis=0)
    tri = (jnp.arange(WKV_CHUNK)[:, None] >= jnp.arange(WKV_CHUNK)[None, :]).astype(BF16)
    tri2 = jnp.concatenate([tri, tri], axis=1)

    xp = x_prompt.reshape(bp * tp, D_MODEL)
    x1p = _ffn(xp, ng, *f1, 0, 1, 512)
    cos_p, sin_p = _rope_tables(jnp.arange(tp, dtype=jnp.int32))
    tm_p = 256
    (mixed, q, kr, vr, gr, hl_p) = _proj(
        x1p, None, *proj_consts, cos_p, sin_p, n_t=1, rows_per_t=tm_p, lag=1,
        tiles_per_seq=tp // tm_p, qkv_dtype=BF16)
    seq3 = lambda t: t.reshape(bp, tp, -1)
    ya_p, wkv_p = _wkv_prompt(seq3(mixed), (*wkv_params, bd2, tri2), nb=4, tt=256)
    ya_p = ya_p.reshape(bp * tp, D_A)
    yr_p, ret_p = _ret_prompt(*map(seq3, (q, kr, vr, gr)), *_ret_tables(RET_CHUNK), gn, nb=4, tt=256)
    yr_p = yr_p.reshape(bp * tp, D_R)
    yp = _ffn(x1p, ng, *f2, 4, 5, 512, mix=(ya_p, yr_p, wo))

    m_s = bs * ts
    xs = x_sample.transpose(1, 0, 2)
    x1s = _ffn(xs.reshape(m_s, D_MODEL), ng, *f1, 0, 1, m_s)
    cos_s, sin_s = _rope_tables(PAST_LEN + jnp.arange(ts, dtype=jnp.int32))
    rows_per_t = 32
    tab = lambda t: jnp.broadcast_to(t[:, None, :], (ts, bs, HEAD_R))
    outs = _proj(x1s.reshape(ts, bs, D_MODEL),
                 state_shift[l].reshape(bs // rows_per_t, rows_per_t, D_MODEL),
                 *proj_consts, tab(cos_s), tab(sin_s), n_t=ts, rows_per_t=rows_per_t,
                 lag=rows_per_t, tiles_per_seq=1, qkv_dtype=F32)
    (mixed, q, kr, vr, gr, hl_s) = outs
    ya_s, wkv_s = _wkv_sample(mixed, *wkv_params, bd2, state_wkv[l].transpose(1, 2, 3, 0))
    wkv_s = wkv_s.transpose(3, 0, 1, 2)
    yr_s, ret_s = _ret_sample(q, kr, vr, gr, *_ret_tables(min(RET_CHUNK, ts)), gn, state_ret[l],
                              bb=SUBLANES)
    ys = _ffn(x1s, ng, *f2, 4, 5, m_s,
              mix=(ya_s.reshape(m_s, D_A), yr_s.reshape(m_s, D_R), wo))
    ys = ys.reshape(ts, bs, D_MODEL).transpose(1, 0, 2)

    return (yp.reshape(bp, tp, D_MODEL), ys,
            hl_p.reshape(1, bp, D_MODEL), wkv_p[None], ret_p[None],
            hl_s.reshape(1, bs, D_MODEL), wkv_s[None], ret_s[None])
```

```python
import functools

import jax
import jax.numpy as jnp
from jax import lax
from jax.experimental import pallas as pl
from jax.experimental.pallas import tpu as pltpu

F32 = jnp.float32
BF16 = jnp.bfloat16

D_MODEL = 1024
D_A = 512
HEAD_A = 64
H_A = D_A // HEAD_A
D_R = 512
H_R = 4
HEAD_R = D_R // H_R
LORA_W, LORA_A, LORA_G = 64, 64, 128
D_FF = 2816
RET_CHUNK = 128
ROPE_BASE = 10000.0
EPS = 1e-6
GN_EPS_A = 64e-5
GN_EPS_R = 1e-5
N_SHIFT = 3 * D_A + LORA_W + LORA_A + LORA_G
N_COLS = N_SHIFT + 4 * D_R
PAST_LEN = 16384

LANES = 128
SUBLANES = 8
VMEM_LIMIT = 52 * 1024 * 1024

MXU_DIM = 256
FF_SPLIT = 6 * MXU_DIM
WKV_CHUNK = 64
PAIR = 2 * HEAD_A


def _nt(a, b):
    return lax.dot_general(a, b, (((1,), (1,)), ((), ())), preferred_element_type=F32)


def _tn(a, b):
    return lax.dot_general(a, b, (((0,), (0,)), ((), ())), preferred_element_type=F32)


def _mm(a, b):
    return jnp.dot(a, b, preferred_element_type=F32)


def _split_hi_lo(x):
    hi = x.astype(BF16)
    lo = (x - hi.astype(F32)).astype(BF16)
    return hi, lo


def _rms(x, g):
    return x * lax.rsqrt(jnp.mean(x * x, axis=-1, keepdims=True) + EPS) * g


def _softplus(x):
    return jnp.maximum(x, 0.0) + jnp.log(1.0 + jnp.exp(-jnp.abs(x)))


def _sigmoid(x):
    return 1.0 / (1.0 + jnp.exp(-x))


def _resident(shape):
    nd = len(shape)
    return pl.BlockSpec(shape, lambda *_: (0,) * nd, pipeline_mode=pl.Buffered(1))


def _params(n_axes):
    return pltpu.CompilerParams(dimension_semantics=("arbitrary",) * n_axes,
                                vmem_limit_bytes=VMEM_LIMIT)


def _ffn_kernel(with_mix, g_in, g_out, *refs):
    if with_mix:
        x_ref, ya_ref, yr_ref, wo_ref, ng_ref, wg_ref, wu_ref, wd_ref, o_ref = refs
    else:
        x_ref, ng_ref, wg_ref, wu_ref, wd_ref, o_ref = refs
    x = x_ref[...]
    if with_mix:
        mix = (_mm(ya_ref[...].astype(BF16), wo_ref[0:D_A, :])
               + _mm(yr_ref[...].astype(BF16), wo_ref[D_A:, :]))
        x = x + _rms(mix, ng_ref[3:4, :])
    h = _rms(x, ng_ref[g_in:g_in + 1, :]).astype(BF16)
    acc = None
    for cols in (slice(0, FF_SPLIT), slice(FF_SPLIT, D_FF)):
        gate = _mm(h, wg_ref[:, cols])
        up = _mm(h, wu_ref[:, cols])
        act = (gate * _sigmoid(gate) * up).astype(BF16)
        part = _mm(act, wd_ref[cols, :])
        acc = part if acc is None else acc + part
    o_ref[...] = x + 0.5 * _rms(acc, ng_ref[g_out:g_out + 1, :])


def _ffn(x, ng, wg, wu, wd, g_in, g_out, tm, mix=None):
    m = x.shape[0]
    row = lambda w: pl.BlockSpec((tm, w), lambda i: (i, 0))
    if mix is None:
        args = (x, ng, wg, wu, wd)
        specs = [row(D_MODEL), _resident(ng.shape), _resident(wg.shape), _resident(wu.shape),
                 _resident(wd.shape)]
    else:
        ya, yr, wo = mix
        args = (x, ya, yr, wo, ng, wg, wu, wd)
        specs = [row(D_MODEL), row(D_A), row(D_R), _resident(wo.shape), _resident(ng.shape),
                 _resident(wg.shape), _resident(wu.shape), _resident(wd.shape)]
    return pl.pallas_call(
        functools.partial(_ffn_kernel, mix is not None, g_in, g_out),
        grid=(m // tm,),
        in_specs=specs,
        out_specs=row(D_MODEL),
        out_shape=jax.ShapeDtypeStruct((m, D_MODEL), F32),
        compiler_params=_params(1),
        name="ffn_mix" if mix is not None else "ffn",
    )(*args)


def _lag_rows(cur, first, lag):
    if lag % SUBLANES == 0:
        return jnp.concatenate([first, cur[:-lag]], axis=0)
    assert lag == 1
    rolled = pltpu.roll(cur, 1, 0)
    rid = lax.broadcasted_iota(jnp.int32, (SUBLANES, cur.shape[1]), 0)
    head = jnp.where(rid == 0, first, rolled[0:SUBLANES])
    return jnp.concatenate([head, rolled[SUBLANES:]], axis=0)


def _proj_kernel(n_t, lag, tiles_per_seq, has_prev, *refs):
    it = iter(refs)
    x_ref = next(it)
    prev_ref = next(it) if has_prev else None
    (ng_ref, win_ref, mu_ref, cos_ref, sin_ref,
     mix_o, q_o, kr_o, vr_o, gr_o, hl_o, carry_scr) = tuple(it)

    def load(ref):
        if n_t == 1:
            return ref[...]
        return jnp.concatenate([ref[t] for t in range(n_t)], axis=0)

    def store(ref, val, cols=slice(None)):
        if n_t == 1:
            ref[:, cols] = val.astype(ref.dtype)
        else:
            rows = val.shape[0] // n_t
            for t in range(n_t):
                ref[t, :, cols] = val[t * rows:(t + 1) * rows].astype(ref.dtype)

    x = load(x_ref)
    tm = x.shape[0]
    h = _rms(x, ng_ref[2:3, :])
    hl_o[0] = h[tm - lag:, :]
    hb = h.astype(BF16)
    seq_start = (pl.program_id(0) % tiles_per_seq) == 0
    prev_b = prev_ref[0].astype(BF16) if has_prev else None

    for c0 in range(0, N_SHIFT, D_A):
        cols = slice(c0, min(c0 + D_A, N_SHIFT))
        cur = _mm(hb, win_ref[:, cols])
        if has_prev:
            first = _mm(prev_b, win_ref[:, cols])
        else:
            first = jnp.zeros((lag, cur.shape[1]), F32)
        if tiles_per_seq > 1:
            first = jnp.where(seq_start, first, carry_scr[0:lag, cols])
            carry_scr[0:lag, cols] = cur[tm - lag:, :]
        prv = _lag_rows(cur, first, lag)
        store(mix_o, cur + (prv - cur) * mu_ref[:, cols], cols)

    cos2 = load(cos_ref)
    sin2 = load(sin_ref)

    def rope(t):
        parts = []
        for hh in range(H_R):
            th = t[:, hh * HEAD_R:(hh + 1) * HEAD_R]
            parts.append(th * cos2 + pltpu.roll(th, HEAD_R // 2, 1) * sin2)
        return jnp.concatenate(parts, axis=1)

    ret = lambda c: _mm(hb, win_ref[:, N_SHIFT + c * D_R:N_SHIFT + (c + 1) * D_R])
    q = ret(0)
    kr = ret(1)
    store(q_o, rope(q))
    vr = ret(2)
    store(kr_o, rope(kr) * (HEAD_R ** -0.5))
    gr = ret(3)
    store(vr_o, vr)
    store(gr_o, gr)


def _proj(x, prev, ng, win, mu, cos2, sin2, *, n_t, rows_per_t, lag, tiles_per_seq, qkv_dtype):
    tm = n_t * rows_per_t
    if n_t == 1:
        m = x.shape[0]
        n_tiles = m // tm
        row = lambda w: pl.BlockSpec((tm, w), lambda i: (i, 0))
        shp = lambda w, dt: jax.ShapeDtypeStruct((m, w), dt)
        tab = pl.BlockSpec((tm, HEAD_R), lambda i: (i % tiles_per_seq, 0))
    else:
        m = x.shape[0] * x.shape[1]
        n_tiles = x.shape[1] // rows_per_t
        row = lambda w: pl.BlockSpec((n_t, rows_per_t, w), lambda i: (0, i, 0))
        shp = lambda w, dt: jax.ShapeDtypeStruct((n_t, m // n_t, w), dt)
        tab = pl.BlockSpec((n_t, rows_per_t, HEAD_R), lambda i: (0, i, 0))
    n_seq = n_tiles // tiles_per_seq
    hl_spec = pl.BlockSpec((1, lag, D_MODEL), lambda i: (i // tiles_per_seq, 0, 0))
    args = [x]
    specs = [row(D_MODEL)]
    if prev is not None:
        args.append(prev)
        specs.append(pl.BlockSpec((1, lag, D_MODEL), lambda i: (i // tiles_per_seq, 0, 0)))
    consts = (ng, win, mu)
    args += list(consts) + [cos2, sin2]
    specs += [_resident(c.shape) for c in consts] + [tab, tab]
    out_shape = ([shp(N_SHIFT, F32)] + [shp(D_R, qkv_dtype)] * 3 + [shp(D_R, F32)]
                 + [jax.ShapeDtypeStruct((n_seq, lag, D_MODEL), F32)])
    out_specs = [row(N_SHIFT)] + [row(D_R)] * 4 + [hl_spec]
    return pl.pallas_call(
        functools.partial(_proj_kernel, n_t, lag, tiles_per_seq, prev is not None),
        grid=(n_tiles,),
        in_specs=specs,
        out_specs=out_specs,
        out_shape=out_shape,
        scratch_shapes=[pltpu.VMEM((max(lag, SUBLANES), N_SHIFT), F32)],
        compiler_params=_params(1),
        name="proj",
    )(*args)


def _sums_stacked(x, bd2):
    return _mm(jnp.concatenate(_split_hi_lo(x), axis=1), bd2)


def _wkv_post(y, r, k, v, g, rk, lw_g, lb_g, head_sums):
    inv_n = 1.0 / HEAD_A
    mu = head_sums(y) * inv_n
    d = y - mu
    var = head_sums(d * d) * inv_n
    yn = d * lax.rsqrt(var + GN_EPS_A) * lw_g + lb_g
    bonus = head_sums(r * k * rk) * v
    return (yn + bonus) * g


def _wkv_decay_gate(wa, gd, w0, w2p, a0, a2p, g2):
    w_pre = w0 + _mm(jnp.tanh(wa).astype(BF16), w2p)
    lw = -jnp.exp(-_softplus(-w_pre) - 0.5)
    a = _sigmoid(a0 + _mm(wa.astype(BF16), a2p))
    g = _mm(_sigmoid(gd).astype(BF16), g2)
    return lw, a, g


def _wkv_keys(k, a, k_k, k_a, head_sums):
    kk = k * k_k
    kk = kk / jnp.maximum(jnp.sqrt(head_sums(kk * kk)), 1e-12)
    return k * (1.0 + (a - 1.0) * k_a), kk, kk * a


def _wkv_prompt_kernel(nb, tt, mix_ref, w0_ref, w2_ref, a0_ref, a2_ref, g2_ref, kk_ref, ka_ref,
                       rk_ref, lnw_ref, lnb_ref, bd2_ref, tri2_ref,
                       ya_o, s_o, s_scr):
    c = WKV_CHUNK
    j = pl.program_id(1)

    @pl.when(j == 0)
    def _():
        s_scr[...] = jnp.zeros_like(s_scr)

    lane = lax.broadcasted_iota(jnp.int32, (c, PAIR), 1)
    first = lane < HEAD_A

    def stack(x):
        return jnp.concatenate([jnp.where(first, x, 0.0), jnp.where(first, 0.0, x)], axis=0)

    ri = lax.broadcasted_iota(jnp.int32, (2 * c, 2 * c), 0) % c
    ci = lax.broadcasted_iota(jnp.int32, (2 * c, 2 * c), 1) % c
    strict = ri > ci
    incl = ri >= ci
    eye = (lax.broadcasted_iota(jnp.int32, (2 * c, 2 * c), 0)
           == lax.broadcasted_iota(jnp.int32, (2 * c, 2 * c), 1)).astype(F32)
    bd2 = bd2_ref[...]
    tri2 = tri2_ref[...]
    head_sums = functools.partial(_sums_stacked, bd2=bd2)

    n_pairs = H_A // 2
    units = [(bi, p) for bi in range(nb) for p in range(n_pairs)]
    n_u = range(len(units))
    slab = [slice(p * PAIR, (p + 1) * PAIR) for p in range(n_pairs)]

    tall = lambda xs: jnp.concatenate(xs, axis=0)
    per_pair = lambda ref: tall([jnp.broadcast_to(ref[:, slab[p]], (c, PAIR)) for _, p in units])
    split = lambda x: [x[u * c:(u + 1) * c] for u in n_u]

    def prep(ch):
        rows = slice(ch * c, (ch + 1) * c)
        ld = lambda c0: [mix_ref[bi, rows, c0 + p * PAIR:c0 + (p + 1) * PAIR] for bi, p in units]
        r, k_raw, v = ld(0), ld(D_A), ld(2 * D_A)
        lora = [_wkv_decay_gate(mix_ref[bi, rows, 3 * D_A:3 * D_A + LORA_W + LORA_A],
                                mix_ref[bi, rows, 3 * D_A + LORA_W + LORA_A:N_SHIFT],
                                w0_ref[...], w2_ref[...], a0_ref[...], a2_ref[...], g2_ref[...])
                for bi in range(nb)]
        lw_all = [x[0] for x in lora]
        a = tall([lora[bi][1][:, slab[p]] for bi, p in units])
        gate = tall([lora[bi][2][:, slab[p]] for bi, p in units])
        k, kk, kka = map(split, _wkv_keys(tall(k_raw), a, per_pair(kk_ref), per_pair(ka_ref),
                                          head_sums))
        return r, v, k, kk, kka, gate, lw_all

    def gram_stage(pre):
        r, v, k, kk, kka, gate, lw_all = pre
        cum_all = [_mm(tri2, jnp.concatenate(_split_hi_lo(x), axis=0)) for x in lw_all]
        lw = [lw_all[bi][:, slab[p]] for bi, p in units]
        cum = [cum_all[bi][:, slab[p]] for bi, p in units]
        cum_end = [x[c - 1:c, :] for x in cum]
        e_pos = [jnp.exp(x) for x in cum]
        e_neg = [jnp.exp(-x) for x in cum]
        e_end = [jnp.exp(x - y) for x, y in zip(cum_end, cum)]
        xs = [jnp.concatenate([stack(-kk[u] * jnp.exp(cum[u] - lw[u])), stack(r[u] * e_pos[u])],
                              axis=0).astype(BF16) for u in n_u]
        ws = [jnp.concatenate([stack(kka[u] * e_neg[u]), stack(k[u] * e_neg[u])],
                              axis=0).astype(BF16) for u in n_u]
        we = [jnp.concatenate([stack(kka[u] * e_end[u]), stack(k[u] * e_end[u])],
                              axis=0).astype(BF16) for u in n_u]
        vs = [stack(x).astype(BF16) for x in v]
        gram = [_nt(xs[u], ws[u]) for u in n_u]
        a_ab = [jnp.where(strict, g[0:2 * c, 0:2 * c], 0.0) for g in gram]
        a_ak = [jnp.where(strict, g[0:2 * c, 2 * c:], 0.0).astype(BF16) for g in gram]
        a_r = [jnp.concatenate([jnp.where(incl, g[2 * c:, 0:2 * c], 0.0),
                                jnp.where(incl, g[2 * c:, 2 * c:], 0.0)], axis=1).astype(BF16)
               for g in gram]
        return xs, we, vs, cum_end, a_ab, a_ak, a_r

    def solve_stage(ch, pre, gs):
        rows = slice(ch * c, (ch + 1) * c)
        r, v, k, kk, kka, gate, lw_all = pre
        xs, we, vs, cum_end, a_ab, a_ak, a_r = gs
        inv = [eye + a for a in a_ab]
        pw = [x.astype(BF16) for x in a_ab]
        pw = [_mm(x, x).astype(BF16) for x in pw]
        n_lvl = c.bit_length() - 2
        for lvl in range(n_lvl):
            if lvl < n_lvl - 1:
                both = [_mm(x, jnp.concatenate([x, i.astype(BF16)], axis=1)) for i, x in zip(inv, pw)]
                inv = [i + b[:, 2 * c:] for i, b in zip(inv, both)]
                pw = [b[:, 0:2 * c].astype(BF16) for b in both]
            else:
                inv = [i + _mm(x, i.astype(BF16)) for i, x in zip(inv, pw)]
        s = [s_scr[u] for u in n_u]
        z = [_nt(xs[u], s[u].astype(BF16)) for u in n_u]
        rhs = [z[u][0:2 * c] + _mm(a_ak[u], vs[u]) for u in n_u]
        uu = [_mm(inv[u].astype(BF16), rhs[u].astype(BF16)).astype(BF16) for u in n_u]
        uv = [jnp.concatenate([uu[u], vs[u]], axis=0) for u in n_u]
        y2 = [z[u][2 * c:] + _mm(a_r[u], uv[u]) for u in n_u]
        for u in n_u:
            s_scr[u] = s[u] * jnp.exp(cum_end[u]) + _tn(uv[u], we[u])
        out = _wkv_post(tall([y2[u][0:c] + y2[u][c:] for u in n_u]), tall(r), tall(k), tall(v),
                        gate, per_pair(rk_ref), per_pair(lnw_ref), per_pair(lnb_ref), head_sums)
        for u, (bi, p) in enumerate(units):
            ya_o[bi, rows, slab[p]] = out[u * c:(u + 1) * c].astype(ya_o.dtype)

    n_chunks = tt // c
    pre = prep(0)
    for ch in range(n_chunks):
        gs = gram_stage(pre)
        nxt = prep(ch + 1) if ch + 1 < n_chunks else None
        solve_stage(ch, pre, gs)
        pre = nxt

    @pl.when(j == pl.num_programs(1) - 1)
    def _():
        for u, (bi, p) in enumerate(units):
            s = s_scr[u]
            s_o[bi, 2 * p] = s[0:HEAD_A, 0:HEAD_A]
            s_o[bi, 2 * p + 1] = s[HEAD_A:, HEAD_A:]


def _wkv_prompt(mixed, consts, *, nb, tt):
    batch, seq, _ = mixed.shape
    blk = lambda w: pl.BlockSpec((nb, tt, w), lambda b, j: (b, j, 0))
    return pl.pallas_call(
        functools.partial(_wkv_prompt_kernel, nb, tt),
        grid=(batch // nb, seq // tt),
        in_specs=[blk(N_SHIFT)] + [_resident(x.shape) for x in consts],
        out_specs=[blk(D_A), pl.BlockSpec((nb, H_A, HEAD_A, HEAD_A), lambda b, j: (b, 0, 0, 0))],
        out_shape=[jax.ShapeDtypeStruct((batch, seq, D_A), BF16),
                   jax.ShapeDtypeStruct((batch, H_A, HEAD_A, HEAD_A), F32)],
        scratch_shapes=[pltpu.VMEM((nb * H_A // 2, PAIR, PAIR), F32)],
        compiler_params=_params(2),
        name="wkv_prompt",
    )(mixed, *consts)


def _wkv_sample_kernel(n_t, r_ref, k_ref, v_ref, lora_ref,
                       w0_ref, w2_ref, a0_ref, a2_ref, g2_ref, kk_ref, ka_ref,
                       rk_ref, lnw_ref, lnb_ref, bd2_ref, s_ref,
                       ya_o, s_o, yt_scr):
    n = HEAD_A
    n_b = r_ref.shape[1]
    tall = lambda xs: jnp.concatenate(xs, axis=0)
    rows_of = lambda ref: tall([ref[t] for t in range(n_t)])
    head_sums = functools.partial(_sums_stacked, bd2=bd2_ref[...])
    r, v, lora = rows_of(r_ref), rows_of(v_ref), rows_of(lora_ref)
    lw, a, gate = _wkv_decay_gate(lora[:, 0:LORA_W + LORA_A], lora[:, LORA_W + LORA_A:],
                                  w0_ref[...], w2_ref[...], a0_ref[...], a2_ref[...], g2_ref[...])
    k, kk, kka = _wkv_keys(rows_of(k_ref), a, kk_ref[...], ka_ref[...], head_sums)

    tr = lambda x: [x[t * n_b:(t + 1) * n_b].T for t in range(n_t)]
    kk_t, kka_t, k_t, r_t, v_t, w_t = tr(kk), tr(kka), tr(k), tr(r), tr(v), tr(jnp.exp(lw))
    rid = lax.broadcasted_iota(jnp.int32, (SUBLANES, n_b), 0)

    for hh in range(2):
        keys = slice(hh * n, (hh + 1) * n)
        for ig in range(n // SUBLANES):
            y_tiles = [jnp.zeros((SUBLANES, n_b), F32) for _ in range(n_t)]
            for ii in range(SUBLANES):
                i = ig * SUBLANES + ii
                s = s_ref[hh, i]
                for t in range(n_t):
                    sa = jnp.sum(s * -kk_t[t][keys], axis=0, keepdims=True)
                    v_row = v_t[t][hh * n + i:hh * n + i + 1]
                    s = s * w_t[t][keys] + sa * kka_t[t][keys] + v_row * k_t[t][keys]
                    y_row = jnp.sum(s * r_t[t][keys], axis=0, keepdims=True)
                    y_tiles[t] = jnp.where(rid == ii, y_row, y_tiles[t])
                s_o[hh, i] = s
            for t in range(n_t):
                yt_scr[t, hh * n + ig * SUBLANES:hh * n + (ig + 1) * SUBLANES, :] = y_tiles[t]

    out = _wkv_post(tall([yt_scr[t].T for t in range(n_t)]), r, k, v, gate,
                    rk_ref[...], lnw_ref[...], lnb_ref[...], head_sums)
    for t in range(n_t):
        ya_o[t] = out[t * n_b:(t + 1) * n_b].astype(ya_o.dtype)


def _wkv_sample(mixed, w0, w2p, a0, a2p, g2, k_k, k_a, rk, lnw, lnb, bd2, s0):
    n_t, n_b, _ = mixed.shape
    slab = lambda first: pl.BlockSpec((n_t, n_b, PAIR), lambda p: (0, 0, first + p))
    lora = pl.BlockSpec((n_t, n_b, 2 * PAIR), lambda p: (0, 0, 3 * D_A // (2 * PAIR)))
    par = pl.BlockSpec((1, PAIR), lambda p: (0, p))
    low = pl.BlockSpec((PAIR, PAIR), lambda p: (0, p))
    st = pl.BlockSpec((2, HEAD_A, HEAD_A, n_b), lambda p: (p, 0, 0, 0))
    n_slab = D_A // PAIR
    return pl.pallas_call(
        functools.partial(_wkv_sample_kernel, n_t),
        grid=(H_A // 2,),
        in_specs=[slab(0), slab(n_slab), slab(2 * n_slab), lora,
                  par, low, par, low, low, par, par, par, par, par, _resident(bd2.shape), st],
        out_specs=[slab(0), st],
        out_shape=[jax.ShapeDtypeStruct((n_t, n_b, D_A), F32),
                   jax.ShapeDtypeStruct(s0.shape, F32)],
        scratch_shapes=[pltpu.VMEM((n_t, PAIR, n_b), F32)],
        compiler_params=_params(1),
        name="wkv_sample",
    )(mixed, mixed, mixed, mixed, w0, w2p, a0, a2p, g2, k_k, k_a, rk, lnw, lnb, bd2, s0)


def _ret_chunk(q, k, v, g, s, heads, dm_ref, qd_ref, kd_ref, cd_ref, gn_ref):
    n = range(len(q))
    qb = [x.astype(BF16) for x in q]
    kb = [x.astype(BF16) for x in k]
    vb = [x.astype(BF16) for x in v]
    inner = [(_nt(qb[u], kb[u]) * dm_ref[heads[u]]).astype(BF16) for u in n]
    q_dec = [(q[u].astype(F32) * qd_ref[heads[u]]).astype(BF16) for u in n]
    k_dec = [(k[u].astype(F32) * kd_ref[heads[u]]).astype(BF16) for u in n]
    if inner[0].shape[1] % LANES == 0:
        y = [_mm(jnp.concatenate([inner[u], q_dec[u]], axis=1),
                 jnp.concatenate([vb[u], s[u].astype(BF16)], axis=0)) for u in n]
    else:
        y = [_mm(inner[u], vb[u]) + _mm(q_dec[u], s[u].astype(BF16)) for u in n]
    s_new = [s[u] * cd_ref[heads[u]] + _tn(k_dec[u], vb[u]) for u in n]
    out = []
    for u in n:
        mu = jnp.mean(y[u], axis=-1, keepdims=True)
        d = y[u] - mu
        var = jnp.mean(d * d, axis=-1, keepdims=True)
        lanes = slice(heads[u] * HEAD_R, (heads[u] + 1) * HEAD_R)
        yn = d * lax.rsqrt(var + GN_EPS_R) * gn_ref[:, lanes]
        out.append(g[u] * _sigmoid(g[u]) * yn)
    return out, s_new


def _ret_prompt_kernel(nb, tt, q_ref, k_ref, v_ref, g_ref, dm_ref, qd_ref, kd_ref, cd_ref, gn_ref,
                       y_o, s_o, s_scr):
    j = pl.program_id(1)

    @pl.when(j == 0)
    def _():
        s_scr[...] = jnp.zeros_like(s_scr)

    c = RET_CHUNK
    units = [(bi, hh) for bi in range(nb) for hh in range(H_R)]
    heads = [hh for _, hh in units]
    lanes = [slice(hh * HEAD_R, (hh + 1) * HEAD_R) for hh in heads]
    for g0 in range(0, len(units), H_R):
        us = list(range(g0, g0 + H_R))
        s = [s_scr[u] for u in us]
        for ch in range(tt // c):
            rows = slice(ch * c, (ch + 1) * c)
            ld = lambda ref: [ref[units[u][0], rows, lanes[u]] for u in us]
            y, s = _ret_chunk(ld(q_ref), ld(k_ref), ld(v_ref), ld(g_ref), s, [heads[u] for u in us],
                              dm_ref, qd_ref, kd_ref, cd_ref, gn_ref)
            for i, u in enumerate(us):
                y_o[units[u][0], rows, lanes[u]] = y[i].astype(y_o.dtype)
        for i, u in enumerate(us):
            s_scr[u] = s[i]

    @pl.when(j == pl.num_programs(1) - 1)
    def _():
        for u, (bi, hh) in enumerate(units):
            s_o[bi, hh] = s_scr[u]


def _ret_prompt(q, k, v, g, dm, qd, kd, cd, gn, *, nb, tt):
    batch, seq, _ = q.shape
    blk = pl.BlockSpec((nb, tt, D_R), lambda b, j: (b, j, 0))
    consts = (dm, qd, kd, cd, gn)
    return pl.pallas_call(
        functools.partial(_ret_prompt_kernel, nb, tt),
        grid=(batch // nb, seq // tt),
        in_specs=[blk] * 4 + [_resident(x.shape) for x in consts],
        out_specs=[blk, pl.BlockSpec((nb, H_R, HEAD_R, HEAD_R), lambda b, j: (b, 0, 0, 0))],
        out_shape=[jax.ShapeDtypeStruct((batch, seq, D_R), BF16),
                   jax.ShapeDtypeStruct((batch, H_R, HEAD_R, HEAD_R), F32)],
        scratch_shapes=[pltpu.VMEM((nb * H_R, HEAD_R, HEAD_R), F32)],
        compiler_params=_params(2),
        name="ret_prompt",
    )(q, k, v, g, *consts)


def _ret_sample_kernel(n_t, bb, q_ref, k_ref, v_ref, g_ref, dm_ref, qd_ref, kd_ref, cd_ref,
                       gn_ref, s_ref, y_o, s_o):
    rid = lax.broadcasted_iota(jnp.int32, (SUBLANES, HEAD_R), 0)
    units = [(bi, hh) for bi in range(bb) for hh in range(H_R)]
    heads = [hh for _, hh in units]
    lanes = [slice(hh * HEAD_R, (hh + 1) * HEAD_R) for hh in heads]

    def seq_rows(ref):
        outs = []
        for u, (bi, _) in enumerate(units):
            out = jnp.zeros((SUBLANES, HEAD_R), F32)
            for t in range(n_t):
                out = jnp.where(rid == t, jnp.broadcast_to(ref[t, bi:bi + 1, lanes[u]], out.shape), out)
            outs.append(out)
        return outs

    y, s_new = _ret_chunk(seq_rows(q_ref), seq_rows(k_ref), seq_rows(v_ref), seq_rows(g_ref),
                          [s_ref[bi, hh] for bi, hh in units], heads,
                          dm_ref, qd_ref, kd_ref, cd_ref, gn_ref)
    for u, (bi, hh) in enumerate(units):
        s_o[bi, hh] = s_new[u]
        for t in range(n_t):
            y_o[t, bi:bi + 1, lanes[u]] = y[u][t:t + 1].astype(y_o.dtype)


def _ret_sample(q, k, v, g, dm, qd, kd, cd, gn, s0, *, bb):
    n_t, n_b, _ = q.shape
    consts = (dm, qd, kd, cd, gn)
    blk = pl.BlockSpec((n_t, bb, D_R), lambda i: (0, i, 0))
    st = pl.BlockSpec((bb, H_R, HEAD_R, HEAD_R), lambda i: (i, 0, 0, 0))
    return pl.pallas_call(
        functools.partial(_ret_sample_kernel, n_t, bb),
        grid=(n_b // bb,),
        in_specs=[blk] * 4 + [_resident(x.shape) for x in consts] + [st],
        out_specs=[blk, st],
        out_shape=[jax.ShapeDtypeStruct((n_t, n_b, D_R), F32),
                   jax.ShapeDtypeStruct(s0.shape, F32)],
        compiler_params=_params(1),
        name="ret_sample",
    )(q, k, v, g, *consts, s0)


def _rope_tables(pos):
    half = HEAD_R // 2
    inv = ROPE_BASE ** (-jnp.arange(half, dtype=F32) / half)
    ang = pos.astype(F32)[:, None] * inv[None, :]
    cos, sin = jnp.cos(ang), jnp.sin(ang)
    return jnp.concatenate([cos, cos], axis=1), jnp.concatenate([-sin, sin], axis=1)


def _ret_tables(c):
    lg = jnp.log1p(-jnp.exp2(-5.0 - jnp.arange(H_R, dtype=F32)))
    idx = jnp.arange(c, dtype=F32)
    diff = idx[:, None] - idx[None, :]
    dmask = jnp.where(diff >= 0, jnp.exp(lg[:, None, None] * jnp.maximum(diff, 0.0)), 0.0)
    ones = jnp.ones((1, 1, HEAD_R), F32)
    qdec = jnp.exp(lg[:, None] * (idx + 1.0))[:, :, None] * ones
    kdec = jnp.exp(lg[:, None] * (c - 1.0 - idx))[:, :, None] * ones
    cdec = jnp.exp(lg * c)[:, None, None] * ones
    extra = -c % SUBLANES
    dmask = jnp.pad(dmask, ((0, 0), (0, extra), (0, extra)))
    qdec = jnp.pad(qdec, ((0, 0), (0, extra), (0, 0)))
    kdec = jnp.pad(kdec, ((0, 0), (0, extra), (0, 0)))
    return dmask, qdec, kdec, cdec


def _block_ones(n, block):
    idx = jnp.arange(n) // block
    return (idx[:, None] == idx[None, :]).astype(BF16)


def kernel(x_prompt, x_sample, state_shift, state_wkv, state_ret, norm_g, ffn1_wg, ffn1_wu, ffn1_wd,
           w_in, mu_shift, w0, w2, a0, a2, g2, k_k, k_a, r_k, lnx_w, lnx_b, ret_gn_w, w_out,
           ffn2_wg, ffn2_wu, ffn2_wd):
    assert norm_g.shape[0] == 1, "single-layer configuration"
    bp, tp, _ = x_prompt.shape
    bs, ts, _ = x_sample.shape
    l = 0
    ng = norm_g[l]
    f1 = (ffn1_wg[l].astype(BF16), ffn1_wu[l].astype(BF16), ffn1_wd[l].astype(BF16))
    f2 = (ffn2_wg[l].astype(BF16), ffn2_wu[l].astype(BF16), ffn2_wd[l].astype(BF16))
    win = w_in[l].astype(BF16)
    wo = w_out[l].astype(BF16)
    row = lambda t: t[l].reshape(1, -1)
    zpad = jnp.zeros((LORA_W, D_A), BF16)
    w2p = jnp.concatenate([w2[l].astype(BF16), zpad], axis=0)
    a2p = jnp.concatenate([zpad, a2[l].astype(BF16)], axis=0)
    proj_consts = (ng, win, row(mu_shift))
    rk, lnw, lnb, gn = row(r_k), row(lnx_w), row(lnx_b), row(ret_gn_w)
    wkv_params = (row(w0), w2p, row(a0), a2p, g2[l].astype(BF16), row(k_k), row(k_a), rk, lnw, lnb)
    bd_pair = _block_ones(PAIR, HEAD_A)
    bd2 = jnp.concatenate([bd_pair, bd_pair], axis=0)
    tri = (jnp.arange(WKV_CHUNK)[:, None] >= jnp.arange(WKV_CHUNK)[None, :]).astype(BF16)
    tri2 = jnp.concatenate([tri, tri], axis=1)

    xp = x_prompt.reshape(bp * tp, D_MODEL)
    x1p = _ffn(xp, ng, *f1, 0, 1, 512)
    cos_p, sin_p = _rope_tables(jnp.arange(tp, dtype=jnp.int32))
    tm_p = 256
    (mixed, q, kr, vr, gr, hl_p) = _proj(
        x1p, None, *proj_consts, cos_p, sin_p, n_t=1, rows_per_t=tm_p, lag=1,
        tiles_per_seq=tp // tm_p, qkv_dtype=BF16)
    seq3 = lambda t: t.reshape(bp, tp, -1)
    ya_p, wkv_p = _wkv_prompt(seq3(mixed), (*wkv_params, bd2, tri2), nb=4, tt=256)
    ya_p = ya_p.reshape(bp * tp, D_A)
    yr_p, ret_p = _ret_prompt(*map(seq3, (q, kr, vr, gr)), *_ret_tables(RET_CHUNK), gn, nb=4, tt=256)
    yr_p = yr_p.reshape(bp * tp, D_R)
    yp = _ffn(x1p, ng, *f2, 4, 5, 512, mix=(ya_p, yr_p, wo))

    m_s = bs * ts
    xs = x_sample.transpose(1, 0, 2)
    x1s = _ffn(xs.reshape(m_s, D_MODEL), ng, *f1, 0, 1, m_s)
    cos_s, sin_s = _rope_tables(PAST_LEN + jnp.arange(ts, dtype=jnp.int32))
    rows_per_t = 32
    tab = lambda t: jnp.broadcast_to(t[:, None, :], (ts, bs, HEAD_R))
    outs = _proj(x1s.reshape(ts, bs, D_MODEL),
                 state_shift[l].reshape(bs // rows_per_t, rows_per_t, D_MODEL),
                 *proj_consts, tab(cos_s), tab(sin_s), n_t=ts, rows_per_t=rows_per_t,
                 lag=rows_per_t, tiles_per_seq=1, qkv_dtype=F32)
    (mixed, q, kr, vr, gr, hl_s) = outs
    ya_s, wkv_s = _wkv_sample(mixed, *wkv_params, bd2, state_wkv[l].transpose(1, 2, 3, 0))
    wkv_s = wkv_s.transpose(3, 0, 1, 2)
    yr_s, ret_s = _ret_sample(q, kr, vr, gr, *_ret_tables(min(RET_CHUNK, ts)), gn, state_ret[l],
                              bb=2 * SUBLANES)
    ys = _ffn(x1s, ng, *f2, 4, 5, m_s,
              mix=(ya_s.reshape(m_s, D_A), yr_s.reshape(m_s, D_R), wo))
    ys = ys.reshape(ts, bs, D_MODEL).transpose(1, 0, 2)

    return (yp.reshape(bp, tp, D_MODEL), ys,
            hl_p.reshape(1, bp, D_MODEL), wkv_p[None], ret_p[None],
            hl_s.reshape(1, bs, D_MODEL), wkv_s[None], ret_s[None])
```

```python
import functools

import jax
import jax.numpy as jnp
from jax import lax
from jax.experimental import pallas as pl
from jax.experimental.pallas import tpu as pltpu

F32 = jnp.float32
BF16 = jnp.bfloat16

D_MODEL = 1024
D_A = 512
HEAD_A = 64
H_A = D_A // HEAD_A
D_R = 512
H_R = 4
HEAD_R = D_R // H_R
LORA_W, LORA_A, LORA_G = 64, 64, 128
D_FF = 2816
RET_CHUNK = 128
ROPE_BASE = 10000.0
EPS = 1e-6
GN_EPS_A = 64e-5
GN_EPS_R = 1e-5
N_SHIFT = 3 * D_A + LORA_W + LORA_A + LORA_G
N_COLS = N_SHIFT + 4 * D_R
PAST_LEN = 16384

LANES = 128
SUBLANES = 8
VMEM_LIMIT = 52 * 1024 * 1024

MXU_DIM = 256
FF_SPLIT = 6 * MXU_DIM
WKV_CHUNK = 64
PAIR = 2 * HEAD_A


def _nt(a, b):
    return lax.dot_general(a, b, (((1,), (1,)), ((), ())), preferred_element_type=F32)


def _tn(a, b):
    return lax.dot_general(a, b, (((0,), (0,)), ((), ())), preferred_element_type=F32)


def _mm(a, b):
    return jnp.dot(a, b, preferred_element_type=F32)


def _split_hi_lo(x):
    hi = x.astype(BF16)
    lo = (x - hi.astype(F32)).astype(BF16)
    return hi, lo


def _rms(x, g):
    return x * lax.rsqrt(jnp.mean(x * x, axis=-1, keepdims=True) + EPS) * g


def _softplus(x):
    return jnp.maximum(x, 0.0) + jnp.log(1.0 + jnp.exp(-jnp.abs(x)))


def _sigmoid(x):
    return 1.0 / (1.0 + jnp.exp(-x))


def _resident(shape):
    nd = len(shape)
    return pl.BlockSpec(shape, lambda *_: (0,) * nd, pipeline_mode=pl.Buffered(1))


def _params(n_axes):
    return pltpu.CompilerParams(dimension_semantics=("arbitrary",) * n_axes,
                                vmem_limit_bytes=VMEM_LIMIT)


def _ffn_kernel(with_mix, n_cast, g_in, g_out, *refs):
    n_in = (8 if with_mix else 5) + n_cast
    ins, outs = refs[:n_in], refs[n_in:]
    if with_mix:
        x_ref, ya_ref, yr_ref, wo_ref, ng_ref, wg_ref, wu_ref, wd_ref = ins[:8]
    else:
        x_ref, ng_ref, wg_ref, wu_ref, wd_ref = ins[:5]
    o_ref = outs[0]
    for src, dst in zip(ins[n_in - n_cast:], outs[1:]):
        dst[...] = src[...].astype(dst.dtype)
    x = x_ref[...]
    if with_mix:
        mix = (_mm(ya_ref[...].astype(BF16), wo_ref[0:D_A, :])
               + _mm(yr_ref[...].astype(BF16), wo_ref[D_A:, :]))
        x = x + _rms(mix, ng_ref[3:4, :])
    h = _rms(x, ng_ref[g_in:g_in + 1, :]).astype(BF16)
    acc = None
    for cols in (slice(0, FF_SPLIT), slice(FF_SPLIT, D_FF)):
        gate = _mm(h, wg_ref[:, cols])
        up = _mm(h, wu_ref[:, cols])
        act = (gate * _sigmoid(gate) * up).astype(BF16)
        part = _mm(act, wd_ref[cols, :])
        acc = part if acc is None else acc + part
    o_ref[...] = x + 0.5 * _rms(acc, ng_ref[g_out:g_out + 1, :])


BF16_ROWS = 2 * SUBLANES


def _cast_spec(rows, cols, steps):
    rep = 1
    while (rows * rep) % steps or (rows * rep // steps) % BF16_ROWS:
        rep *= 2
    return pl.BlockSpec((rows * rep // steps, cols), lambda i: (i // rep, 0))


def _ffn(x, ng, wg, wu, wd, g_in, g_out, tm, mix=None, cast=()):
    m = x.shape[0]
    steps = m // tm
    row = lambda w: pl.BlockSpec((tm, w), lambda i: (i, 0))
    if mix is None:
        args = (x, ng, wg, wu, wd)
        specs = [row(D_MODEL), _resident(ng.shape), _resident(wg.shape), _resident(wu.shape),
                 _resident(wd.shape)]
    else:
        ya, yr, wo = mix
        args = (x, ya, yr, wo, ng, wg, wu, wd)
        specs = [row(D_MODEL), row(D_A), row(D_R), _resident(wo.shape), _resident(ng.shape),
                 _resident(wg.shape), _resident(wu.shape), _resident(wd.shape)]
    cast_specs = [_cast_spec(*w.shape, steps) for w in cast]
    out = pl.pallas_call(
        functools.partial(_ffn_kernel, mix is not None, len(cast), g_in, g_out),
        grid=(steps,),
        in_specs=specs + cast_specs,
        out_specs=[row(D_MODEL)] + cast_specs,
        out_shape=[jax.ShapeDtypeStruct((m, D_MODEL), F32)]
        + [jax.ShapeDtypeStruct(w.shape, BF16) for w in cast],
        compiler_params=_params(1),
        name="ffn_mix" if mix is not None else "ffn",
    )(*args, *cast)
    return out if cast else out[0]


def _lag_rows(cur, first, lag):
    if lag % SUBLANES == 0:
        return jnp.concatenate([first, cur[:-lag]], axis=0)
    assert lag == 1
    rolled = pltpu.roll(cur, 1, 0)
    rid = lax.broadcasted_iota(jnp.int32, (SUBLANES, cur.shape[1]), 0)
    head = jnp.where(rid == 0, first, rolled[0:SUBLANES])
    return jnp.concatenate([head, rolled[SUBLANES:]], axis=0)


def _proj_kernel(n_t, lag, tiles_per_seq, has_prev, *refs):
    it = iter(refs)
    x_ref = next(it)
    prev_ref = next(it) if has_prev else None
    (ng_ref, win_ref, mu_ref, cos_ref, sin_ref,
     mix_o, q_o, kr_o, vr_o, gr_o, hl_o, carry_scr) = tuple(it)

    def load(ref):
        if n_t == 1:
            return ref[...]
        return jnp.concatenate([ref[t] for t in range(n_t)], axis=0)

    def store(ref, val, cols=slice(None)):
        if n_t == 1:
            ref[:, cols] = val.astype(ref.dtype)
        else:
            rows = val.shape[0] // n_t
            for t in range(n_t):
                ref[t, :, cols] = val[t * rows:(t + 1) * rows].astype(ref.dtype)

    x = load(x_ref)
    tm = x.shape[0]
    h = _rms(x, ng_ref[2:3, :])
    hl_o[0] = h[tm - lag:, :]
    hb = h.astype(BF16)
    seq_start = (pl.program_id(0) % tiles_per_seq) == 0
    prev_b = prev_ref[0].astype(BF16) if has_prev else None

    for c0 in range(0, N_SHIFT, D_A):
        cols = slice(c0, min(c0 + D_A, N_SHIFT))
        cur = _mm(hb, win_ref[:, cols])
        if has_prev:
            first = _mm(prev_b, win_ref[:, cols])
        else:
            first = jnp.zeros((lag, cur.shape[1]), F32)
        if tiles_per_seq > 1:
            first = jnp.where(seq_start, first, carry_scr[0:lag, cols])
            carry_scr[0:lag, cols] = cur[tm - lag:, :]
        prv = _lag_rows(cur, first, lag)
        store(mix_o, cur + (prv - cur) * mu_ref[:, cols], cols)

    cos2 = load(cos_ref)
    sin2 = load(sin_ref)

    def rope(t):
        parts = []
        for hh in range(H_R):
            th = t[:, hh * HEAD_R:(hh + 1) * HEAD_R]
            parts.append(th * cos2 + pltpu.roll(th, HEAD_R // 2, 1) * sin2)
        return jnp.concatenate(parts, axis=1)

    ret = lambda c: _mm(hb, win_ref[:, N_SHIFT + c * D_R:N_SHIFT + (c + 1) * D_R])
    q = ret(0)
    kr = ret(1)
    store(q_o, rope(q))
    vr = ret(2)
    store(kr_o, rope(kr) * (HEAD_R ** -0.5))
    gr = ret(3)
    store(vr_o, vr)
    store(gr_o, gr)


def _proj(x, prev, ng, win, mu, cos2, sin2, *, n_t, rows_per_t, lag, tiles_per_seq, qkv_dtype):
    tm = n_t * rows_per_t
    if n_t == 1:
        m = x.shape[0]
        n_tiles = m // tm
        row = lambda w: pl.BlockSpec((tm, w), lambda i: (i, 0))
        shp = lambda w, dt: jax.ShapeDtypeStruct((m, w), dt)
        tab = pl.BlockSpec((tm, HEAD_R), lambda i: (i % tiles_per_seq, 0))
    else:
        m = x.shape[0] * x.shape[1]
        n_tiles = x.shape[1] // rows_per_t
        row = lambda w: pl.BlockSpec((n_t, rows_per_t, w), lambda i: (0, i, 0))
        shp = lambda w, dt: jax.ShapeDtypeStruct((n_t, m // n_t, w), dt)
        tab = pl.BlockSpec((n_t, rows_per_t, HEAD_R), lambda i: (0, i, 0))
    n_seq = n_tiles // tiles_per_seq
    hl_spec = pl.BlockSpec((1, lag, D_MODEL), lambda i: (i // tiles_per_seq, 0, 0))
    args = [x]
    specs = [row(D_MODEL)]
    if prev is not None:
        args.append(prev)
        specs.append(pl.BlockSpec((1, lag, D_MODEL), lambda i: (i // tiles_per_seq, 0, 0)))
    consts = (ng, win, mu)
    args += list(consts) + [cos2, sin2]
    specs += [_resident(c.shape) for c in consts] + [tab, tab]
    out_shape = ([shp(N_SHIFT, F32)] + [shp(D_R, qkv_dtype)] * 3 + [shp(D_R, F32)]
                 + [jax.ShapeDtypeStruct((n_seq, lag, D_MODEL), F32)])
    out_specs = [row(N_SHIFT)] + [row(D_R)] * 4 + [hl_spec]
    return pl.pallas_call(
        functools.partial(_proj_kernel, n_t, lag, tiles_per_seq, prev is not None),
        grid=(n_tiles,),
        in_specs=specs,
        out_specs=out_specs,
        out_shape=out_shape,
        scratch_shapes=[pltpu.VMEM((max(lag, SUBLANES), N_SHIFT), F32)],
        compiler_params=_params(1),
        name="proj",
    )(*args)


def _sums_stacked(x, bd2):
    return _mm(jnp.concatenate(_split_hi_lo(x), axis=1), bd2)


def _wkv_post(y, r, k, v, g, rk, lw_g, lb_g, head_sums):
    inv_n = 1.0 / HEAD_A
    mu = head_sums(y) * inv_n
    d = y - mu
    var = head_sums(d * d) * inv_n
    yn = d * lax.rsqrt(var + GN_EPS_A) * lw_g + lb_g
    bonus = head_sums(r * k * rk) * v
    return (yn + bonus) * g


def _wkv_decay_gate(wa, gd, w0, w2p, a0, a2p, g2):
    w_pre = w0 + _mm(jnp.tanh(wa).astype(BF16), w2p)
    lw = -jnp.exp(-_softplus(-w_pre) - 0.5)
    a = _sigmoid(a0 + _mm(wa.astype(BF16), a2p))
    g = _mm(_sigmoid(gd).astype(BF16), g2)
    return lw, a, g


def _wkv_keys(k, a, k_k, k_a, head_sums):
    kk = k * k_k
    kk = kk / jnp.maximum(jnp.sqrt(head_sums(kk * kk)), 1e-12)
    return k * (1.0 + (a - 1.0) * k_a), kk, kk * a


def _wkv_prompt_kernel(nb, tt, mix_ref, w0_ref, w2_ref, a0_ref, a2_ref, g2_ref, kk_ref, ka_ref,
                       rk_ref, lnw_ref, lnb_ref, bd2_ref, tri2_ref,
                       ya_o, s_o, s_scr):
    c = WKV_CHUNK
    j = pl.program_id(1)

    @pl.when(j == 0)
    def _():
        s_scr[...] = jnp.zeros_like(s_scr)

    lane = lax.broadcasted_iota(jnp.int32, (c, PAIR), 1)
    first = lane < HEAD_A

    def stack(x):
        return jnp.concatenate([jnp.where(first, x, 0.0), jnp.where(first, 0.0, x)], axis=0)

    ri = lax.broadcasted_iota(jnp.int32, (2 * c, 2 * c), 0) % c
    ci = lax.broadcasted_iota(jnp.int32, (2 * c, 2 * c), 1) % c
    strict = ri > ci
    incl = ri >= ci
    eye = (lax.broadcasted_iota(jnp.int32, (2 * c, 2 * c), 0)
           == lax.broadcasted_iota(jnp.int32, (2 * c, 2 * c), 1)).astype(F32)
    bd2 = bd2_ref[...]
    tri2 = tri2_ref[...]
    head_sums = functools.partial(_sums_stacked, bd2=bd2)

    n_pairs = H_A // 2
    units = [(bi, p) for bi in range(nb) for p in range(n_pairs)]
    n_u = range(len(units))
    slab = [slice(p * PAIR, (p + 1) * PAIR) for p in range(n_pairs)]

    tall = lambda xs: jnp.concatenate(xs, axis=0)
    per_pair = lambda ref: tall([jnp.broadcast_to(ref[:, slab[p]], (c, PAIR)) for _, p in units])
    split = lambda x: [x[u * c:(u + 1) * c] for u in n_u]

    def prep(ch):
        rows = slice(ch * c, (ch + 1) * c)
        ld = lambda c0: [mix_ref[bi, rows, c0 + p * PAIR:c0 + (p + 1) * PAIR] for bi, p in units]
        r, k_raw, v = ld(0), ld(D_A), ld(2 * D_A)
        lora = [_wkv_decay_gate(mix_ref[bi, rows, 3 * D_A:3 * D_A + LORA_W + LORA_A],
                                mix_ref[bi, rows, 3 * D_A + LORA_W + LORA_A:N_SHIFT],
                                w0_ref[...], w2_ref[...], a0_ref[...], a2_ref[...], g2_ref[...])
                for bi in range(nb)]
        lw_all = [x[0] for x in lora]
        a = tall([lora[bi][1][:, slab[p]] for bi, p in units])
        gate = tall([lora[bi][2][:, slab[p]] for bi, p in units])
        k, kk, kka = map(split, _wkv_keys(tall(k_raw), a, per_pair(kk_ref), per_pair(ka_ref),
                                          head_sums))
        return r, v, k, kk, kka, gate, lw_all

    def gram_stage(pre):
        r, v, k, kk, kka, gate, lw_all = pre
        cum_all = [_mm(tri2, jnp.concatenate(_split_hi_lo(x), axis=0)) for x in lw_all]
        lw = [lw_all[bi][:, slab[p]] for bi, p in units]
        cum = [cum_all[bi][:, slab[p]] for bi, p in units]
        cum_end = [x[c - 1:c, :] for x in cum]
        e_pos = [jnp.exp(x) for x in cum]
        e_neg = [jnp.exp(-x) for x in cum]
        e_end = [jnp.exp(x - y) for x, y in zip(cum_end, cum)]
        xs = [jnp.concatenate([stack(-kk[u] * jnp.exp(cum[u] - lw[u])), stack(r[u] * e_pos[u])],
                              axis=0).astype(BF16) for u in n_u]
        ws = [jnp.concatenate([stack(kka[u] * e_neg[u]), stack(k[u] * e_neg[u])],
                              axis=0).astype(BF16) for u in n_u]
        we = [jnp.concatenate([stack(kka[u] * e_end[u]), stack(k[u] * e_end[u])],
                              axis=0).astype(BF16) for u in n_u]
        vs = [stack(x).astype(BF16) for x in v]
        gram = [_nt(xs[u], ws[u]) for u in n_u]
        a_ab = [jnp.where(strict, g[0:2 * c, 0:2 * c], 0.0) for g in gram]
        a_ak = [jnp.where(strict, g[0:2 * c, 2 * c:], 0.0).astype(BF16) for g in gram]
        a_r = [jnp.concatenate([jnp.where(incl, g[2 * c:, 0:2 * c], 0.0),
                                jnp.where(incl, g[2 * c:, 2 * c:], 0.0)], axis=1).astype(BF16)
               for g in gram]
        return xs, we, vs, cum_end, a_ab, a_ak, a_r

    def solve_stage(ch, pre, gs):
        rows = slice(ch * c, (ch + 1) * c)
        r, v, k, kk, kka, gate, lw_all = pre
        xs, we, vs, cum_end, a_ab, a_ak, a_r = gs
        inv = [eye + a for a in a_ab]
        pw = [x.astype(BF16) for x in a_ab]
        pw = [_mm(x, x).astype(BF16) for x in pw]
        n_lvl = c.bit_length() - 2
        for lvl in range(n_lvl):
            if lvl < n_lvl - 1:
                both = [_mm(x, jnp.concatenate([x, i.astype(BF16)], axis=1)) for i, x in zip(inv, pw)]
                inv = [i + b[:, 2 * c:] for i, b in zip(inv, both)]
                pw = [b[:, 0:2 * c].astype(BF16) for b in both]
            else:
                inv = [i + _mm(x, i.astype(BF16)) for i, x in zip(inv, pw)]
        s = [s_scr[u] for u in n_u]
        z = [_nt(xs[u], s[u].astype(BF16)) for u in n_u]
        rhs = [z[u][0:2 * c] + _mm(a_ak[u], vs[u]) for u in n_u]
        uu = [_mm(inv[u].astype(BF16), rhs[u].astype(BF16)).astype(BF16) for u in n_u]
        uv = [jnp.concatenate([uu[u], vs[u]], axis=0) for u in n_u]
        y2 = [z[u][2 * c:] + _mm(a_r[u], uv[u]) for u in n_u]
        for u in n_u:
            s_scr[u] = s[u] * jnp.exp(cum_end[u]) + _tn(uv[u], we[u])
        out = _wkv_post(tall([y2[u][0:c] + y2[u][c:] for u in n_u]), tall(r), tall(k), tall(v),
                        gate, per_pair(rk_ref), per_pair(lnw_ref), per_pair(lnb_ref), head_sums)
        for u, (bi, p) in enumerate(units):
            ya_o[bi, rows, slab[p]] = out[u * c:(u + 1) * c].astype(ya_o.dtype)

    n_chunks = tt // c
    pre = prep(0)
    for ch in range(n_chunks):
        gs = gram_stage(pre)
        nxt = prep(ch + 1) if ch + 1 < n_chunks else None
        solve_stage(ch, pre, gs)
        pre = nxt

    @pl.when(j == pl.num_programs(1) - 1)
    def _():
        for u, (bi, p) in enumerate(units):
            s = s_scr[u]
            s_o[bi, 2 * p] = s[0:HEAD_A, 0:HEAD_A]
            s_o[bi, 2 * p + 1] = s[HEAD_A:, HEAD_A:]


def _wkv_prompt(mixed, consts, *, nb, tt):
    batch, seq, _ = mixed.shape
    blk = lambda w: pl.BlockSpec((nb, tt, w), lambda b, j: (b, j, 0))
    return pl.pallas_call(
        functools.partial(_wkv_prompt_kernel, nb, tt),
        grid=(batch // nb, seq // tt),
        in_specs=[blk(N_SHIFT)] + [_resident(x.shape) for x in consts],
        out_specs=[blk(D_A), pl.BlockSpec((nb, H_A, HEAD_A, HEAD_A), lambda b, j: (b, 0, 0, 0))],
        out_shape=[jax.ShapeDtypeStruct((batch, seq, D_A), BF16),
                   jax.ShapeDtypeStruct((batch, H_A, HEAD_A, HEAD_A), F32)],
        scratch_shapes=[pltpu.VMEM((nb * H_A // 2, PAIR, PAIR), F32)],
        compiler_params=_params(2),
        name="wkv_prompt",
    )(mixed, *consts)


def _wkv_sample_kernel(n_t, r_ref, k_ref, v_ref, lora_ref,
                       w0_ref, w2_ref, a0_ref, a2_ref, g2_ref, kk_ref, ka_ref,
                       rk_ref, lnw_ref, lnb_ref, bd2_ref, s_ref,
                       ya_o, s_o, yt_scr):
    n = HEAD_A
    n_b = r_ref.shape[1]
    tall = lambda xs: jnp.concatenate(xs, axis=0)
    rows_of = lambda ref: tall([ref[t] for t in range(n_t)])
    head_sums = functools.partial(_sums_stacked, bd2=bd2_ref[...])
    r, v, lora = rows_of(r_ref), rows_of(v_ref), rows_of(lora_ref)
    lw, a, gate = _wkv_decay_gate(lora[:, 0:LORA_W + LORA_A], lora[:, LORA_W + LORA_A:],
                                  w0_ref[...], w2_ref[...], a0_ref[...], a2_ref[...], g2_ref[...])
    k, kk, kka = _wkv_keys(rows_of(k_ref), a, kk_ref[...], ka_ref[...], head_sums)

    tr = lambda x: [x[t * n_b:(t + 1) * n_b].T for t in range(n_t)]
    kk_t, kka_t, k_t, r_t, v_t, w_t = tr(kk), tr(kka), tr(k), tr(r), tr(v), tr(jnp.exp(lw))
    rid = lax.broadcasted_iota(jnp.int32, (SUBLANES, n_b), 0)

    for hh in range(2):
        keys = slice(hh * n, (hh + 1) * n)
        for ig in range(n // SUBLANES):
            y_tiles = [jnp.zeros((SUBLANES, n_b), F32) for _ in range(n_t)]
            for ii in range(SUBLANES):
                i = ig * SUBLANES + ii
                s = s_ref[hh, i]
                for t in range(n_t):
                    sa = jnp.sum(s * -kk_t[t][keys], axis=0, keepdims=True)
                    v_row = v_t[t][hh * n + i:hh * n + i + 1]
                    s = s * w_t[t][keys] + sa * kka_t[t][keys] + v_row * k_t[t][keys]
                    y_row = jnp.sum(s * r_t[t][keys], axis=0, keepdims=True)
                    y_tiles[t] = jnp.where(rid == ii, y_row, y_tiles[t])
                s_o[hh, i] = s
            for t in range(n_t):
                yt_scr[t, hh * n + ig * SUBLANES:hh * n + (ig + 1) * SUBLANES, :] = y_tiles[t]

    out = _wkv_post(tall([yt_scr[t].T for t in range(n_t)]), r, k, v, gate,
                    rk_ref[...], lnw_ref[...], lnb_ref[...], head_sums)
    for t in range(n_t):
        ya_o[t] = out[t * n_b:(t + 1) * n_b].astype(ya_o.dtype)


def _wkv_sample(mixed, w0, w2p, a0, a2p, g2, k_k, k_a, rk, lnw, lnb, bd2, s0):
    n_t, n_b, _ = mixed.shape
    slab = lambda first: pl.BlockSpec((n_t, n_b, PAIR), lambda p: (0, 0, first + p))
    lora = pl.BlockSpec((n_t, n_b, 2 * PAIR), lambda p: (0, 0, 3 * D_A // (2 * PAIR)))
    par = pl.BlockSpec((1, PAIR), lambda p: (0, p))
    low = pl.BlockSpec((PAIR, PAIR), lambda p: (0, p))
    st = pl.BlockSpec((2, HEAD_A, HEAD_A, n_b), lambda p: (p, 0, 0, 0))
    n_slab = D_A // PAIR
    return pl.pallas_call(
        functools.partial(_wkv_sample_kernel, n_t),
        grid=(H_A // 2,),
        in_specs=[slab(0), slab(n_slab), slab(2 * n_slab), lora,
                  par, low, par, low, low, par, par, par, par, par, _resident(bd2.shape), st],
        out_specs=[slab(0), st],
        out_shape=[jax.ShapeDtypeStruct((n_t, n_b, D_A), F32),
                   jax.ShapeDtypeStruct(s0.shape, F32)],
        scratch_shapes=[pltpu.VMEM((n_t, PAIR, n_b), F32)],
        compiler_params=_params(1),
        name="wkv_sample",
    )(mixed, mixed, mixed, mixed, w0, w2p, a0, a2p, g2, k_k, k_a, rk, lnw, lnb, bd2, s0)


def _ret_chunk(q, k, v, g, s, heads, dm_ref, qd_ref, kd_ref, cd_ref, gn_ref):
    n = range(len(q))
    qb = [x.astype(BF16) for x in q]
    kb = [x.astype(BF16) for x in k]
    vb = [x.astype(BF16) for x in v]
    inner = [(_nt(qb[u], kb[u]) * dm_ref[heads[u]]).astype(BF16) for u in n]
    q_dec = [(q[u].astype(F32) * qd_ref[heads[u]]).astype(BF16) for u in n]
    k_dec = [(k[u].astype(F32) * kd_ref[heads[u]]).astype(BF16) for u in n]
    if inner[0].shape[1] % LANES == 0:
        y = [_mm(jnp.concatenate([inner[u], q_dec[u]], axis=1),
                 jnp.concatenate([vb[u], s[u].astype(BF16)], axis=0)) for u in n]
    else:
        y = [_mm(inner[u], vb[u]) + _mm(q_dec[u], s[u].astype(BF16)) for u in n]
    s_new = [s[u] * cd_ref[heads[u]] + _tn(k_dec[u], vb[u]) for u in n]
    out = []
    for u in n:
        mu = jnp.mean(y[u], axis=-1, keepdims=True)
        d = y[u] - mu
        var = jnp.mean(d * d, axis=-1, keepdims=True)
        lanes = slice(heads[u] * HEAD_R, (heads[u] + 1) * HEAD_R)
        yn = d * lax.rsqrt(var + GN_EPS_R) * gn_ref[:, lanes]
        out.append(g[u] * _sigmoid(g[u]) * yn)
    return out, s_new


def _ret_prompt_kernel(nb, tt, q_ref, k_ref, v_ref, g_ref, dm_ref, qd_ref, kd_ref, cd_ref, gn_ref,
                       y_o, s_o, s_scr):
    j = pl.program_id(1)

    @pl.when(j == 0)
    def _():
        s_scr[...] = jnp.zeros_like(s_scr)

    c = RET_CHUNK
    units = [(bi, hh) for bi in range(nb) for hh in range(H_R)]
    heads = [hh for _, hh in units]
    lanes = [slice(hh * HEAD_R, (hh + 1) * HEAD_R) for hh in heads]
    for g0 in range(0, len(units), H_R):
        us = list(range(g0, g0 + H_R))
        s = [s_scr[u] for u in us]
        for ch in range(tt // c):
            rows = slice(ch * c, (ch + 1) * c)
            ld = lambda ref: [ref[units[u][0], rows, lanes[u]] for u in us]
            y, s = _ret_chunk(ld(q_ref), ld(k_ref), ld(v_ref), ld(g_ref), s, [heads[u] for u in us],
                              dm_ref, qd_ref, kd_ref, cd_ref, gn_ref)
            for i, u in enumerate(us):
                y_o[units[u][0], rows, lanes[u]] = y[i].astype(y_o.dtype)
        for i, u in enumerate(us):
            s_scr[u] = s[i]

    @pl.when(j == pl.num_programs(1) - 1)
    def _():
        for u, (bi, hh) in enumerate(units):
            s_o[bi, hh] = s_scr[u]


def _ret_prompt(q, k, v, g, dm, qd, kd, cd, gn, *, nb, tt):
    batch, seq, _ = q.shape
    blk = pl.BlockSpec((nb, tt, D_R), lambda b, j: (b, j, 0))
    consts = (dm, qd, kd, cd, gn)
    return pl.pallas_call(
        functools.partial(_ret_prompt_kernel, nb, tt),
        grid=(batch // nb, seq // tt),
        in_specs=[blk] * 4 + [_resident(x.shape) for x in consts],
        out_specs=[blk, pl.BlockSpec((nb, H_R, HEAD_R, HEAD_R), lambda b, j: (b, 0, 0, 0))],
        out_shape=[jax.ShapeDtypeStruct((batch, seq, D_R), BF16),
                   jax.ShapeDtypeStruct((batch, H_R, HEAD_R, HEAD_R), F32)],
        scratch_shapes=[pltpu.VMEM((nb * H_R, HEAD_R, HEAD_R), F32)],
        compiler_params=_params(2),
        name="ret_prompt",
    )(q, k, v, g, *consts)


def _ret_sample_kernel(n_t, bb, q_ref, k_ref, v_ref, g_ref, dm_ref, qd_ref, kd_ref, cd_ref,
                       gn_ref, s_ref, y_o, s_o):
    rid = lax.broadcasted_iota(jnp.int32, (SUBLANES, HEAD_R), 0)
    units = [(bi, hh) for bi in range(bb) for hh in range(H_R)]
    heads = [hh for _, hh in units]
    lanes = [slice(hh * HEAD_R, (hh + 1) * HEAD_R) for hh in heads]

    def seq_rows(ref):
        outs = []
        for u, (bi, _) in enumerate(units):
            out = jnp.zeros((SUBLANES, HEAD_R), F32)
            for t in range(n_t):
                out = jnp.where(rid == t, jnp.broadcast_to(ref[t, bi:bi + 1, lanes[u]], out.shape), out)
            outs.append(out)
        return outs

    y, s_new = _ret_chunk(seq_rows(q_ref), seq_rows(k_ref), seq_rows(v_ref), seq_rows(g_ref),
                          [s_ref[bi, hh] for bi, hh in units], heads,
                          dm_ref, qd_ref, kd_ref, cd_ref, gn_ref)
    for u, (bi, hh) in enumerate(units):
        s_o[bi, hh] = s_new[u]
        for t in range(n_t):
            y_o[t, bi:bi + 1, lanes[u]] = y[u][t:t + 1].astype(y_o.dtype)


def _ret_sample(q, k, v, g, dm, qd, kd, cd, gn, s0, *, bb):
    n_t, n_b, _ = q.shape
    consts = (dm, qd, kd, cd, gn)
    blk = pl.BlockSpec((n_t, bb, D_R), lambda i: (0, i, 0))
    st = pl.BlockSpec((bb, H_R, HEAD_R, HEAD_R), lambda i: (i, 0, 0, 0))
    return pl.pallas_call(
        functools.partial(_ret_sample_kernel, n_t, bb),
        grid=(n_b // bb,),
        in_specs=[blk] * 4 + [_resident(x.shape) for x in consts] + [st],
        out_specs=[blk, st],
        out_shape=[jax.ShapeDtypeStruct((n_t, n_b, D_R), F32),
                   jax.ShapeDtypeStruct(s0.shape, F32)],
        compiler_params=_params(1),
        name="ret_sample",
    )(q, k, v, g, *consts, s0)


def _rope_tables(pos):
    half = HEAD_R // 2
    inv = ROPE_BASE ** (-jnp.arange(half, dtype=F32) / half)
    ang = pos.astype(F32)[:, None] * inv[None, :]
    cos, sin = jnp.cos(ang), jnp.sin(ang)
    return jnp.concatenate([cos, cos], axis=1), jnp.concatenate([-sin, sin], axis=1)


def _ret_tables(c):
    lg = jnp.log1p(-jnp.exp2(-5.0 - jnp.arange(H_R, dtype=F32)))
    idx = jnp.arange(c, dtype=F32)
    diff = idx[:, None] - idx[None, :]
    dmask = jnp.where(diff >= 0, jnp.exp(lg[:, None, None] * jnp.maximum(diff, 0.0)), 0.0)
    ones = jnp.ones((1, 1, HEAD_R), F32)
    qdec = jnp.exp(lg[:, None] * (idx + 1.0))[:, :, None] * ones
    kdec = jnp.exp(lg[:, None] * (c - 1.0 - idx))[:, :, None] * ones
    cdec = jnp.exp(lg * c)[:, None, None] * ones
    extra = -c % SUBLANES
    dmask = jnp.pad(dmask, ((0, 0), (0, extra), (0, extra)))
    qdec = jnp.pad(qdec, ((0, 0), (0, extra), (0, 0)))
    kdec = jnp.pad(kdec, ((0, 0), (0, extra), (0, 0)))
    return dmask, qdec, kdec, cdec


def _block_ones(n, block):
    idx = jnp.arange(n) // block
    return (idx[:, None] == idx[None, :]).astype(BF16)


def kernel(x_prompt, x_sample, state_shift, state_wkv, state_ret, norm_g, ffn1_wg, ffn1_wu, ffn1_wd,
           w_in, mu_shift, w0, w2, a0, a2, g2, k_k, k_a, r_k, lnx_w, lnx_b, ret_gn_w, w_out,
           ffn2_wg, ffn2_wu, ffn2_wd):
    assert norm_g.shape[0] == 1, "single-layer configuration"
    bp, tp, _ = x_prompt.shape
    bs, ts, _ = x_sample.shape
    l = 0
    ng = norm_g[l]
    f1 = (ffn1_wg[l].astype(BF16), ffn1_wu[l].astype(BF16), ffn1_wd[l].astype(BF16))
    row = lambda t: t[l].reshape(1, -1)
    zpad = jnp.zeros((LORA_W, D_A), BF16)
    w2p = jnp.concatenate([w2[l].astype(BF16), zpad], axis=0)
    a2p = jnp.concatenate([zpad, a2[l].astype(BF16)], axis=0)
    rk, lnw, lnb, gn = row(r_k), row(lnx_w), row(lnx_b), row(ret_gn_w)
    wkv_params = (row(w0), w2p, row(a0), a2p, g2[l].astype(BF16), row(k_k), row(k_a), rk, lnw, lnb)
    bd_pair = _block_ones(PAIR, HEAD_A)
    bd2 = jnp.concatenate([bd_pair, bd_pair], axis=0)
    tri = (jnp.arange(WKV_CHUNK)[:, None] >= jnp.arange(WKV_CHUNK)[None, :]).astype(BF16)
    tri2 = jnp.concatenate([tri, tri], axis=1)

    xp = x_prompt.reshape(bp * tp, D_MODEL)
    x1p, *f2, win, wo = _ffn(xp, ng, *f1, 0, 1, 512,
                             cast=(ffn2_wg[l], ffn2_wu[l], ffn2_wd[l], w_in[l], w_out[l]))
    proj_consts = (ng, win, row(mu_shift))
    cos_p, sin_p = _rope_tables(jnp.arange(tp, dtype=jnp.int32))
    tm_p = 256
    (mixed, q, kr, vr, gr, hl_p) = _proj(
        x1p, None, *proj_consts, cos_p, sin_p, n_t=1, rows_per_t=tm_p, lag=1,
        tiles_per_seq=tp // tm_p, qkv_dtype=BF16)
    seq3 = lambda t: t.reshape(bp, tp, -1)
    ya_p, wkv_p = _wkv_prompt(seq3(mixed), (*wkv_params, bd2, tri2), nb=4, tt=256)
    ya_p = ya_p.reshape(bp * tp, D_A)
    yr_p, ret_p = _ret_prompt(*map(seq3, (q, kr, vr, gr)), *_ret_tables(RET_CHUNK), gn, nb=4, tt=256)
    yr_p = yr_p.reshape(bp * tp, D_R)
    yp = _ffn(x1p, ng, *f2, 4, 5, 512, mix=(ya_p, yr_p, wo))

    m_s = bs * ts
    xs = x_sample.transpose(1, 0, 2)
    x1s = _ffn(xs.reshape(m_s, D_MODEL), ng, *f1, 0, 1, m_s)
    cos_s, sin_s = _rope_tables(PAST_LEN + jnp.arange(ts, dtype=jnp.int32))
    rows_per_t = bs
    tab = lambda t: jnp.broadcast_to(t[:, None, :], (ts, bs, HEAD_R))
    outs = _proj(x1s.reshape(ts, bs, D_MODEL),
                 state_shift[l].reshape(bs // rows_per_t, rows_per_t, D_MODEL),
                 *proj_consts, tab(cos_s), tab(sin_s), n_t=ts, rows_per_t=rows_per_t,
                 lag=rows_per_t, tiles_per_seq=1, qkv_dtype=F32)
    (mixed, q, kr, vr, gr, hl_s) = outs
    ya_s, wkv_s = _wkv_sample(mixed, *wkv_params, bd2, state_wkv[l].transpose(1, 2, 3, 0))
    wkv_s = wkv_s.transpose(3, 0, 1, 2)
    yr_s, ret_s = _ret_sample(q, kr, vr, gr, *_ret_tables(min(RET_CHUNK, ts)), gn, state_ret[l],
                              bb=2 * SUBLANES)
    ys = _ffn(x1s, ng, *f2, 4, 5, m_s,
              mix=(ya_s.reshape(m_s, D_A), yr_s.reshape(m_s, D_R), wo))
    ys = ys.reshape(ts, bs, D_MODEL).transpose(1, 0, 2)

    return (yp.reshape(bp, tp, D_MODEL), ys,
            hl_p.reshape(1, bp, D_MODEL), wkv_p[None], ret_p[None],
            hl_s.reshape(1, bs, D_MODEL), wkv_s[None], ret_s[None])
```

```python
import functools

import jax
import jax.numpy as jnp
from jax import lax
from jax.experimental import pallas as pl
from jax.experimental.pallas import tpu as pltpu

F32 = jnp.float32
BF16 = jnp.bfloat16

D_MODEL = 1024
D_A = 512
HEAD_A = 64
H_A = D_A // HEAD_A
D_R = 512
H_R = 4
HEAD_R = D_R // H_R
LORA_W, LORA_A, LORA_G = 64, 64, 128
D_FF = 2816
RET_CHUNK = 128
ROPE_BASE = 10000.0
EPS = 1e-6
GN_EPS_A = 64e-5
GN_EPS_R = 1e-5
N_SHIFT = 3 * D_A + LORA_W + LORA_A + LORA_G
N_COLS = N_SHIFT + 4 * D_R
PAST_LEN = 16384

LANES = 128
SUBLANES = 8
VMEM_LIMIT = 52 * 1024 * 1024

MXU_DIM = 256
FF_SPLIT = 6 * MXU_DIM
WKV_CHUNK = 64
PAIR = 2 * HEAD_A


def _nt(a, b):
    return lax.dot_general(a, b, (((1,), (1,)), ((), ())), preferred_element_type=F32)


def _tn(a, b):
    return lax.dot_general(a, b, (((0,), (0,)), ((), ())), preferred_element_type=F32)


def _mm(a, b):
    return jnp.dot(a, b, preferred_element_type=F32)


def _split_hi_lo(x):
    hi = x.astype(BF16)
    lo = (x - hi.astype(F32)).astype(BF16)
    return hi, lo


def _rms(x, g):
    return x * lax.rsqrt(jnp.mean(x * x, axis=-1, keepdims=True) + EPS) * g


def _softplus(x):
    return jnp.maximum(x, 0.0) + jnp.log(1.0 + jnp.exp(-jnp.abs(x)))


def _sigmoid(x):
    return 1.0 / (1.0 + jnp.exp(-x))


def _resident(shape):
    nd = len(shape)
    return pl.BlockSpec(shape, lambda *_: (0,) * nd, pipeline_mode=pl.Buffered(1))


def _params(n_axes):
    return pltpu.CompilerParams(dimension_semantics=("arbitrary",) * n_axes,
                                vmem_limit_bytes=VMEM_LIMIT)


def _ffn_kernel(with_mix, n_cast, g_in, g_out, *refs):
    n_in = (8 if with_mix else 5) + n_cast
    ins, outs = refs[:n_in], refs[n_in:]
    if with_mix:
        x_ref, ya_ref, yr_ref, wo_ref, ng_ref, wg_ref, wu_ref, wd_ref = ins[:8]
    else:
        x_ref, ng_ref, wg_ref, wu_ref, wd_ref = ins[:5]
    o_ref = outs[0]
    for src, dst in zip(ins[n_in - n_cast:], outs[1:]):
        dst[...] = src[...].astype(dst.dtype)
    half = x_ref.shape[0] // 2
    halves = [slice(0, half), slice(half, 2 * half)]
    x = [x_ref[r, :] for r in halves]
    if with_mix:
        mix = [_mm(ya_ref[r, :].astype(BF16), wo_ref[0:D_A, :])
               + _mm(yr_ref[r, :].astype(BF16), wo_ref[D_A:, :]) for r in halves]
        x = [xi + _rms(m, ng_ref[3:4, :]) for xi, m in zip(x, mix)]
    h = [_rms(xi, ng_ref[g_in:g_in + 1, :]).astype(BF16) for xi in x]
    for i, r in enumerate(halves):
        acc = None
        for cols in (slice(0, FF_SPLIT), slice(FF_SPLIT, D_FF)):
            gate = _mm(h[i], wg_ref[:, cols])
            up = _mm(h[i], wu_ref[:, cols])
            act = (gate * _sigmoid(gate) * up).astype(BF16)
            part = _mm(act, wd_ref[cols, :])
            acc = part if acc is None else acc + part
        o_ref[r, :] = x[i] + 0.5 * _rms(acc, ng_ref[g_out:g_out + 1, :])


BF16_ROWS = 2 * SUBLANES


def _cast_spec(rows, cols, steps):
    rep = 1
    while (rows * rep) % steps or (rows * rep // steps) % BF16_ROWS:
        rep *= 2
    return pl.BlockSpec((rows * rep // steps, cols), lambda i: (i // rep, 0))


def _ffn(x, ng, wg, wu, wd, g_in, g_out, tm, mix=None, cast=()):
    m = x.shape[0]
    steps = m // tm
    row = lambda w: pl.BlockSpec((tm, w), lambda i: (i, 0))
    if mix is None:
        args = (x, ng, wg, wu, wd)
        specs = [row(D_MODEL), _resident(ng.shape), _resident(wg.shape), _resident(wu.shape),
                 _resident(wd.shape)]
    else:
        ya, yr, wo = mix
        args = (x, ya, yr, wo, ng, wg, wu, wd)
        specs = [row(D_MODEL), row(D_A), row(D_R), _resident(wo.shape), _resident(ng.shape),
                 _resident(wg.shape), _resident(wu.shape), _resident(wd.shape)]
    cast_specs = [_cast_spec(*w.shape, steps) for w in cast]
    out = pl.pallas_call(
        functools.partial(_ffn_kernel, mix is not None, len(cast), g_in, g_out),
        grid=(steps,),
        in_specs=specs + cast_specs,
        out_specs=[row(D_MODEL)] + cast_specs,
        out_shape=[jax.ShapeDtypeStruct((m, D_MODEL), F32)]
        + [jax.ShapeDtypeStruct(w.shape, BF16) for w in cast],
        compiler_params=_params(1),
        name="ffn_mix" if mix is not None else "ffn",
    )(*args, *cast)
    return out if cast else out[0]


def _lag_rows(cur, first, lag):
    if lag % SUBLANES == 0:
        return jnp.concatenate([first, cur[:-lag]], axis=0)
    assert lag == 1
    rolled = pltpu.roll(cur, 1, 0)
    rid = lax.broadcasted_iota(jnp.int32, (SUBLANES, cur.shape[1]), 0)
    head = jnp.where(rid == 0, first, rolled[0:SUBLANES])
    return jnp.concatenate([head, rolled[SUBLANES:]], axis=0)


def _proj_kernel(n_t, lag, tiles_per_seq, has_prev, *refs):
    it = iter(refs)
    x_ref = next(it)
    prev_ref = next(it) if has_prev else None
    (ng_ref, win_ref, mu_ref, cos_ref, sin_ref,
     mix_o, q_o, kr_o, vr_o, gr_o, hl_o, carry_scr) = tuple(it)

    def load(ref):
        if n_t == 1:
            return ref[...]
        return jnp.concatenate([ref[t] for t in range(n_t)], axis=0)

    def store(ref, val, cols=slice(None)):
        if n_t == 1:
            ref[:, cols] = val.astype(ref.dtype)
        else:
            rows = val.shape[0] // n_t
            for t in range(n_t):
                ref[t, :, cols] = val[t * rows:(t + 1) * rows].astype(ref.dtype)

    x = load(x_ref)
    tm = x.shape[0]
    h = _rms(x, ng_ref[2:3, :])
    hl_o[0] = h[tm - lag:, :]
    hb = h.astype(BF16)
    seq_start = (pl.program_id(0) % tiles_per_seq) == 0
    prev_b = prev_ref[0].astype(BF16) if has_prev else None

    for c0 in range(0, N_SHIFT, D_A):
        cols = slice(c0, min(c0 + D_A, N_SHIFT))
        cur = _mm(hb, win_ref[:, cols])
        if has_prev:
            first = _mm(prev_b, win_ref[:, cols])
        else:
            first = jnp.zeros((lag, cur.shape[1]), F32)
        if tiles_per_seq > 1:
            first = jnp.where(seq_start, first, carry_scr[0:lag, cols])
            carry_scr[0:lag, cols] = cur[tm - lag:, :]
        prv = _lag_rows(cur, first, lag)
        store(mix_o, cur + (prv - cur) * mu_ref[:, cols], cols)

    cos2 = load(cos_ref)
    sin2 = load(sin_ref)

    def rope(t):
        parts = []
        for hh in range(H_R):
            th = t[:, hh * HEAD_R:(hh + 1) * HEAD_R]
            parts.append(th * cos2 + pltpu.roll(th, HEAD_R // 2, 1) * sin2)
        return jnp.concatenate(parts, axis=1)

    ret = lambda c: _mm(hb, win_ref[:, N_SHIFT + c * D_R:N_SHIFT + (c + 1) * D_R])
    q = ret(0)
    kr = ret(1)
    store(q_o, rope(q))
    vr = ret(2)
    store(kr_o, rope(kr) * (HEAD_R ** -0.5))
    gr = ret(3)
    store(vr_o, vr)
    store(gr_o, gr)


def _proj(x, prev, ng, win, mu, cos2, sin2, *, n_t, rows_per_t, lag, tiles_per_seq, qkv_dtype):
    tm = n_t * rows_per_t
    if n_t == 1:
        m = x.shape[0]
        n_tiles = m // tm
        row = lambda w: pl.BlockSpec((tm, w), lambda i: (i, 0))
        shp = lambda w, dt: jax.ShapeDtypeStruct((m, w), dt)
        tab = pl.BlockSpec((tm, HEAD_R), lambda i: (i % tiles_per_seq, 0))
    else:
        m = x.shape[0] * x.shape[1]
        n_tiles = x.shape[1] // rows_per_t
        row = lambda w: pl.BlockSpec((n_t, rows_per_t, w), lambda i: (0, i, 0))
        shp = lambda w, dt: jax.ShapeDtypeStruct((n_t, m // n_t, w), dt)
        tab = pl.BlockSpec((n_t, rows_per_t, HEAD_R), lambda i: (0, i, 0))
    n_seq = n_tiles // tiles_per_seq
    hl_spec = pl.BlockSpec((1, lag, D_MODEL), lambda i: (i // tiles_per_seq, 0, 0))
    args = [x]
    specs = [row(D_MODEL)]
    if prev is not None:
        args.append(prev)
        specs.append(pl.BlockSpec((1, lag, D_MODEL), lambda i: (i // tiles_per_seq, 0, 0)))
    consts = (ng, win, mu)
    args += list(consts) + [cos2, sin2]
    specs += [_resident(c.shape) for c in consts] + [tab, tab]
    out_shape = ([shp(N_SHIFT, F32)] + [shp(D_R, qkv_dtype)] * 3 + [shp(D_R, F32)]
                 + [jax.ShapeDtypeStruct((n_seq, lag, D_MODEL), F32)])
    out_specs = [row(N_SHIFT)] + [row(D_R)] * 4 + [hl_spec]
    return pl.pallas_call(
        functools.partial(_proj_kernel, n_t, lag, tiles_per_seq, prev is not None),
        grid=(n_tiles,),
        in_specs=specs,
        out_specs=out_specs,
        out_shape=out_shape,
        scratch_shapes=[pltpu.VMEM((max(lag, SUBLANES), N_SHIFT), F32)],
        compiler_params=_params(1),
        name="proj",
    )(*args)


def _sums_stacked(x, bd2):
    return _mm(jnp.concatenate(_split_hi_lo(x), axis=1), bd2)


def _wkv_post(y, r, k, v, g, rk, lw_g, lb_g, head_sums):
    inv_n = 1.0 / HEAD_A
    mu = head_sums(y) * inv_n
    d = y - mu
    var = head_sums(d * d) * inv_n
    yn = d * lax.rsqrt(var + GN_EPS_A) * lw_g + lb_g
    bonus = head_sums(r * k * rk) * v
    return (yn + bonus) * g


def _wkv_decay_gate(wa, gd, w0, w2p, a0, a2p, g2):
    w_pre = w0 + _mm(jnp.tanh(wa).astype(BF16), w2p)
    lw = -jnp.exp(-_softplus(-w_pre) - 0.5)
    a = _sigmoid(a0 + _mm(wa.astype(BF16), a2p))
    g = _mm(_sigmoid(gd).astype(BF16), g2)
    return lw, a, g


def _wkv_keys(k, a, k_k, k_a, head_sums):
    kk = k * k_k
    kk = kk / jnp.maximum(jnp.sqrt(head_sums(kk * kk)), 1e-12)
    return k * (1.0 + (a - 1.0) * k_a), kk, kk * a


def _wkv_prompt_kernel(nb, tt, mix_ref, w0_ref, w2_ref, a0_ref, a2_ref, g2_ref, kk_ref, ka_ref,
                       rk_ref, lnw_ref, lnb_ref, bd2_ref, tri2_ref,
                       ya_o, s_o, s_scr):
    c = WKV_CHUNK
    j = pl.program_id(1)

    @pl.when(j == 0)
    def _():
        s_scr[...] = jnp.zeros_like(s_scr)

    lane = lax.broadcasted_iota(jnp.int32, (c, PAIR), 1)
    first = lane < HEAD_A

    def stack(x):
        return jnp.concatenate([jnp.where(first, x, 0.0), jnp.where(first, 0.0, x)], axis=0)

    ri = lax.broadcasted_iota(jnp.int32, (2 * c, 2 * c), 0) % c
    ci = lax.broadcasted_iota(jnp.int32, (2 * c, 2 * c), 1) % c
    strict = ri > ci
    incl = ri >= ci
    eye = (lax.broadcasted_iota(jnp.int32, (2 * c, 2 * c), 0)
           == lax.broadcasted_iota(jnp.int32, (2 * c, 2 * c), 1)).astype(F32)
    bd2 = bd2_ref[...]
    tri2 = tri2_ref[...]
    head_sums = functools.partial(_sums_stacked, bd2=bd2)

    n_pairs = H_A // 2
    units = [(bi, p) for bi in range(nb) for p in range(n_pairs)]
    n_u = range(len(units))
    slab = [slice(p * PAIR, (p + 1) * PAIR) for p in range(n_pairs)]

    tall = lambda xs: jnp.concatenate(xs, axis=0)
    per_pair = lambda ref: tall([jnp.broadcast_to(ref[:, slab[p]], (c, PAIR)) for _, p in units])
    split = lambda x: [x[u * c:(u + 1) * c] for u in n_u]

    def prep(ch):
        rows = slice(ch * c, (ch + 1) * c)
        ld = lambda c0: [mix_ref[bi, rows, c0 + p * PAIR:c0 + (p + 1) * PAIR] for bi, p in units]
        r, k_raw, v = ld(0), ld(D_A), ld(2 * D_A)
        lora = [_wkv_decay_gate(mix_ref[bi, rows, 3 * D_A:3 * D_A + LORA_W + LORA_A],
                                mix_ref[bi, rows, 3 * D_A + LORA_W + LORA_A:N_SHIFT],
                                w0_ref[...], w2_ref[...], a0_ref[...], a2_ref[...], g2_ref[...])
                for bi in range(nb)]
        lw_all = [x[0] for x in lora]
        a = tall([lora[bi][1][:, slab[p]] for bi, p in units])
        gate = tall([lora[bi][2][:, slab[p]] for bi, p in units])
        k, kk, kka = map(split, _wkv_keys(tall(k_raw), a, per_pair(kk_ref), per_pair(ka_ref),
                                          head_sums))
        return r, v, k, kk, kka, gate, lw_all

    def gram_stage(pre):
        r, v, k, kk, kka, gate, lw_all = pre
        cum_all = [_mm(tri2, jnp.concatenate(_split_hi_lo(x), axis=0)) for x in lw_all]
        lw = [lw_all[bi][:, slab[p]] for bi, p in units]
        cum = [cum_all[bi][:, slab[p]] for bi, p in units]
        cum_end = [x[c - 1:c, :] for x in cum]
        e_pos = [jnp.exp(x) for x in cum]
        e_neg = [jnp.exp(-x) for x in cum]
        e_end = [jnp.exp(x - y) for x, y in zip(cum_end, cum)]
        xs = [jnp.concatenate([stack(-kk[u] * jnp.exp(cum[u] - lw[u])), stack(r[u] * e_pos[u])],
                              axis=0).astype(BF16) for u in n_u]
        ws = [jnp.concatenate([stack(kka[u] * e_neg[u]), stack(k[u] * e_neg[u])],
                              axis=0).astype(BF16) for u in n_u]
        we = [jnp.concatenate([stack(kka[u] * e_end[u]), stack(k[u] * e_end[u])],
                              axis=0).astype(BF16) for u in n_u]
        vs = [stack(x).astype(BF16) for x in v]
        gram = [_nt(xs[u], ws[u]) for u in n_u]
        a_ab = [jnp.where(strict, g[0:2 * c, 0:2 * c], 0.0) for g in gram]
        a_ak = [jnp.where(strict, g[0:2 * c, 2 * c:], 0.0).astype(BF16) for g in gram]
        a_r = [jnp.concatenate([jnp.where(incl, g[2 * c:, 0:2 * c], 0.0),
                                jnp.where(incl, g[2 * c:, 2 * c:], 0.0)], axis=1).astype(BF16)
               for g in gram]
        return xs, we, vs, cum_end, a_ab, a_ak, a_r

    def solve_stage(ch, pre, gs):
        rows = slice(ch * c, (ch + 1) * c)
        r, v, k, kk, kka, gate, lw_all = pre
        xs, we, vs, cum_end, a_ab, a_ak, a_r = gs
        inv = [eye + a for a in a_ab]
        pw = [x.astype(BF16) for x in a_ab]
        pw = [_mm(x, x).astype(BF16) for x in pw]
        n_lvl = c.bit_length() - 2
        for lvl in range(n_lvl):
            if lvl < n_lvl - 1:
                both = [_mm(x, jnp.concatenate([x, i.astype(BF16)], axis=1)) for i, x in zip(inv, pw)]
                inv = [i + b[:, 2 * c:] for i, b in zip(inv, both)]
                pw = [b[:, 0:2 * c].astype(BF16) for b in both]
            else:
                inv = [i + _mm(x, i.astype(BF16)) for i, x in zip(inv, pw)]
        s = [s_scr[u] for u in n_u]
        z = [_nt(xs[u], s[u].astype(BF16)) for u in n_u]
        rhs = [z[u][0:2 * c] + _mm(a_ak[u], vs[u]) for u in n_u]
        uu = [_mm(inv[u].astype(BF16), rhs[u].astype(BF16)).astype(BF16) for u in n_u]
        uv = [jnp.concatenate([uu[u], vs[u]], axis=0) for u in n_u]
        y2 = [z[u][2 * c:] + _mm(a_r[u], uv[u]) for u in n_u]
        for u in n_u:
            s_scr[u] = s[u] * jnp.exp(cum_end[u]) + _tn(uv[u], we[u])
        out = _wkv_post(tall([y2[u][0:c] + y2[u][c:] for u in n_u]), tall(r), tall(k), tall(v),
                        gate, per_pair(rk_ref), per_pair(lnw_ref), per_pair(lnb_ref), head_sums)
        for u, (bi, p) in enumerate(units):
            ya_o[bi, rows, slab[p]] = out[u * c:(u + 1) * c].astype(ya_o.dtype)

    n_chunks = tt // c
    pre = prep(0)
    for ch in range(n_chunks):
        gs = gram_stage(pre)
        nxt = prep(ch + 1) if ch + 1 < n_chunks else None
        solve_stage(ch, pre, gs)
        pre = nxt

    @pl.when(j == pl.num_programs(1) - 1)
    def _():
        for u, (bi, p) in enumerate(units):
            s = s_scr[u]
            s_o[bi, 2 * p] = s[0:HEAD_A, 0:HEAD_A]
            s_o[bi, 2 * p + 1] = s[HEAD_A:, HEAD_A:]


def _wkv_prompt(mixed, consts, *, nb, tt):
    batch, seq, _ = mixed.shape
    blk = lambda w: pl.BlockSpec((nb, tt, w), lambda b, j: (b, j, 0))
    return pl.pallas_call(
        functools.partial(_wkv_prompt_kernel, nb, tt),
        grid=(batch // nb, seq // tt),
        in_specs=[blk(N_SHIFT)] + [_resident(x.shape) for x in consts],
        out_specs=[blk(D_A), pl.BlockSpec((nb, H_A, HEAD_A, HEAD_A), lambda b, j: (b, 0, 0, 0))],
        out_shape=[jax.ShapeDtypeStruct((batch, seq, D_A), BF16),
                   jax.ShapeDtypeStruct((batch, H_A, HEAD_A, HEAD_A), F32)],
        scratch_shapes=[pltpu.VMEM((nb * H_A // 2, PAIR, PAIR), F32)],
        compiler_params=_params(2),
        name="wkv_prompt",
    )(mixed, *consts)


def _wkv_sample_kernel(n_t, r_ref, k_ref, v_ref, lora_ref,
                       w0_ref, w2_ref, a0_ref, a2_ref, g2_ref, kk_ref, ka_ref,
                       rk_ref, lnw_ref, lnb_ref, bd2_ref, s_ref,
                       ya_o, s_o, yt_scr):
    n = HEAD_A
    n_b = r_ref.shape[1]
    tall = lambda xs: jnp.concatenate(xs, axis=0)
    rows_of = lambda ref: tall([ref[t] for t in range(n_t)])
    head_sums = functools.partial(_sums_stacked, bd2=bd2_ref[...])
    r, v, lora = rows_of(r_ref), rows_of(v_ref), rows_of(lora_ref)
    lw, a, gate = _wkv_decay_gate(lora[:, 0:LORA_W + LORA_A], lora[:, LORA_W + LORA_A:],
                                  w0_ref[...], w2_ref[...], a0_ref[...], a2_ref[...], g2_ref[...])
    k, kk, kka = _wkv_keys(rows_of(k_ref), a, kk_ref[...], ka_ref[...], head_sums)

    tr = lambda x: [x[t * n_b:(t + 1) * n_b].T for t in range(n_t)]
    kk_t, kka_t, k_t, r_t, v_t, w_t = tr(kk), tr(kka), tr(k), tr(r), tr(v), tr(jnp.exp(lw))
    rid = lax.broadcasted_iota(jnp.int32, (SUBLANES, n_b), 0)

    for hh in range(2):
        keys = slice(hh * n, (hh + 1) * n)
        for ig in range(n // SUBLANES):
            y_tiles = [jnp.zeros((SUBLANES, n_b), F32) for _ in range(n_t)]
            for ii in range(SUBLANES):
                i = ig * SUBLANES + ii
                s = s_ref[hh, i]
                for t in range(n_t):
                    sa = jnp.sum(s * -kk_t[t][keys], axis=0, keepdims=True)
                    v_row = v_t[t][hh * n + i:hh * n + i + 1]
                    s = s * w_t[t][keys] + sa * kka_t[t][keys] + v_row * k_t[t][keys]
                    y_row = jnp.sum(s * r_t[t][keys], axis=0, keepdims=True)
                    y_tiles[t] = jnp.where(rid == ii, y_row, y_tiles[t])
                s_o[hh, i] = s
            for t in range(n_t):
                yt_scr[t, hh * n + ig * SUBLANES:hh * n + (ig + 1) * SUBLANES, :] = y_tiles[t]

    out = _wkv_post(tall([yt_scr[t].T for t in range(n_t)]), r, k, v, gate,
                    rk_ref[...], lnw_ref[...], lnb_ref[...], head_sums)
    for t in range(n_t):
        ya_o[t] = out[t * n_b:(t + 1) * n_b].astype(ya_o.dtype)


def _wkv_sample(mixed, w0, w2p, a0, a2p, g2, k_k, k_a, rk, lnw, lnb, bd2, s0):
    n_t, n_b, _ = mixed.shape
    slab = lambda first: pl.BlockSpec((n_t, n_b, PAIR), lambda p: (0, 0, first + p))
    lora = pl.BlockSpec((n_t, n_b, 2 * PAIR), lambda p: (0, 0, 3 * D_A // (2 * PAIR)))
    par = pl.BlockSpec((1, PAIR), lambda p: (0, p))
    low = pl.BlockSpec((PAIR, PAIR), lambda p: (0, p))
    st = pl.BlockSpec((2, HEAD_A, HEAD_A, n_b), lambda p: (p, 0, 0, 0))
    n_slab = D_A // PAIR
    return pl.pallas_call(
        functools.partial(_wkv_sample_kernel, n_t),
        grid=(H_A // 2,),
        in_specs=[slab(0), slab(n_slab), slab(2 * n_slab), lora,
                  par, low, par, low, low, par, par, par, par, par, _resident(bd2.shape), st],
        out_specs=[slab(0), st],
        out_shape=[jax.ShapeDtypeStruct((n_t, n_b, D_A), F32),
                   jax.ShapeDtypeStruct(s0.shape, F32)],
        scratch_shapes=[pltpu.VMEM((n_t, PAIR, n_b), F32)],
        compiler_params=_params(1),
        name="wkv_sample",
    )(mixed, mixed, mixed, mixed, w0, w2p, a0, a2p, g2, k_k, k_a, rk, lnw, lnb, bd2, s0)


def _ret_chunk(q, k, v, g, s, heads, dm_ref, qd_ref, kd_ref, cd_ref, gn_ref):
    n = range(len(q))
    qb = [x.astype(BF16) for x in q]
    kb = [x.astype(BF16) for x in k]
    vb = [x.astype(BF16) for x in v]
    inner = [(_nt(qb[u], kb[u]) * dm_ref[heads[u]]).astype(BF16) for u in n]
    q_dec = [(q[u].astype(F32) * qd_ref[heads[u]]).astype(BF16) for u in n]
    k_dec = [(k[u].astype(F32) * kd_ref[heads[u]]).astype(BF16) for u in n]
    if inner[0].shape[1] % LANES == 0:
        y = [_mm(jnp.concatenate([inner[u], q_dec[u]], axis=1),
                 jnp.concatenate([vb[u], s[u].astype(BF16)], axis=0)) for u in n]
    else:
        y = [_mm(inner[u], vb[u]) + _mm(q_dec[u], s[u].astype(BF16)) for u in n]
    s_new = [s[u] * cd_ref[heads[u]] + _tn(k_dec[u], vb[u]) for u in n]
    out = []
    for u in n:
        mu = jnp.mean(y[u], axis=-1, keepdims=True)
        d = y[u] - mu
        var = jnp.mean(d * d, axis=-1, keepdims=True)
        lanes = slice(heads[u] * HEAD_R, (heads[u] + 1) * HEAD_R)
        yn = d * lax.rsqrt(var + GN_EPS_R) * gn_ref[:, lanes]
        out.append(g[u] * _sigmoid(g[u]) * yn)
    return out, s_new


def _ret_prompt_kernel(nb, tt, q_ref, k_ref, v_ref, g_ref, dm_ref, qd_ref, kd_ref, cd_ref, gn_ref,
                       y_o, s_o, s_scr):
    j = pl.program_id(1)

    @pl.when(j == 0)
    def _():
        s_scr[...] = jnp.zeros_like(s_scr)

    c = RET_CHUNK
    units = [(bi, hh) for bi in range(nb) for hh in range(H_R)]
    heads = [hh for _, hh in units]
    lanes = [slice(hh * HEAD_R, (hh + 1) * HEAD_R) for hh in heads]
    for g0 in range(0, len(units), H_R):
        us = list(range(g0, g0 + H_R))
        s = [s_scr[u] for u in us]
        for ch in range(tt // c):
            rows = slice(ch * c, (ch + 1) * c)
            ld = lambda ref: [ref[units[u][0], rows, lanes[u]] for u in us]
            y, s = _ret_chunk(ld(q_ref), ld(k_ref), ld(v_ref), ld(g_ref), s, [heads[u] for u in us],
                              dm_ref, qd_ref, kd_ref, cd_ref, gn_ref)
            for i, u in enumerate(us):
                y_o[units[u][0], rows, lanes[u]] = y[i].astype(y_o.dtype)
        for i, u in enumerate(us):
            s_scr[u] = s[i]

    @pl.when(j == pl.num_programs(1) - 1)
    def _():
        for u, (bi, hh) in enumerate(units):
            s_o[bi, hh] = s_scr[u]


def _ret_prompt(q, k, v, g, dm, qd, kd, cd, gn, *, nb, tt):
    batch, seq, _ = q.shape
    blk = pl.BlockSpec((nb, tt, D_R), lambda b, j: (b, j, 0))
    consts = (dm, qd, kd, cd, gn)
    return pl.pallas_call(
        functools.partial(_ret_prompt_kernel, nb, tt),
        grid=(batch // nb, seq // tt),
        in_specs=[blk] * 4 + [_resident(x.shape) for x in consts],
        out_specs=[blk, pl.BlockSpec((nb, H_R, HEAD_R, HEAD_R), lambda b, j: (b, 0, 0, 0))],
        out_shape=[jax.ShapeDtypeStruct((batch, seq, D_R), BF16),
                   jax.ShapeDtypeStruct((batch, H_R, HEAD_R, HEAD_R), F32)],
        scratch_shapes=[pltpu.VMEM((nb * H_R, HEAD_R, HEAD_R), F32)],
        compiler_params=_params(2),
        name="ret_prompt",
    )(q, k, v, g, *consts)


def _ret_sample_kernel(n_t, bb, q_ref, k_ref, v_ref, g_ref, dm_ref, qd_ref, kd_ref, cd_ref,
                       gn_ref, s_ref, y_o, s_o):
    rid = lax.broadcasted_iota(jnp.int32, (SUBLANES, HEAD_R), 0)
    units = [(bi, hh) for bi in range(bb) for hh in range(H_R)]
    heads = [hh for _, hh in units]
    lanes = [slice(hh * HEAD_R, (hh + 1) * HEAD_R) for hh in heads]

    def seq_rows(ref):
        outs = []
        for u, (bi, _) in enumerate(units):
            out = jnp.zeros((SUBLANES, HEAD_R), F32)
            for t in range(n_t):
                out = jnp.where(rid == t, jnp.broadcast_to(ref[t, bi:bi + 1, lanes[u]], out.shape), out)
            outs.append(out)
        return outs

    y, s_new = _ret_chunk(seq_rows(q_ref), seq_rows(k_ref), seq_rows(v_ref), seq_rows(g_ref),
                          [s_ref[bi, hh] for bi, hh in units], heads,
                          dm_ref, qd_ref, kd_ref, cd_ref, gn_ref)
    for u, (bi, hh) in enumerate(units):
        s_o[bi, hh] = s_new[u]
        for t in range(n_t):
            y_o[t, bi:bi + 1, lanes[u]] = y[u][t:t + 1].astype(y_o.dtype)


def _ret_sample(q, k, v, g, dm, qd, kd, cd, gn, s0, *, bb):
    n_t, n_b, _ = q.shape
    consts = (dm, qd, kd, cd, gn)
    blk = pl.BlockSpec((n_t, bb, D_R), lambda i: (0, i, 0))
    st = pl.BlockSpec((bb, H_R, HEAD_R, HEAD_R), lambda i: (i, 0, 0, 0))
    return pl.pallas_call(
        functools.partial(_ret_sample_kernel, n_t, bb),
        grid=(n_b // bb,),
        in_specs=[blk] * 4 + [_resident(x.shape) for x in consts] + [st],
        out_specs=[blk, st],
        out_shape=[jax.ShapeDtypeStruct((n_t, n_b, D_R), F32),
                   jax.ShapeDtypeStruct(s0.shape, F32)],
        compiler_params=_params(1),
        name="ret_sample",
    )(q, k, v, g, *consts, s0)


def _rope_tables(pos):
    half = HEAD_R // 2
    inv = ROPE_BASE ** (-jnp.arange(half, dtype=F32) / half)
    ang = pos.astype(F32)[:, None] * inv[None, :]
    cos, sin = jnp.cos(ang), jnp.sin(ang)
    return jnp.concatenate([cos, cos], axis=1), jnp.concatenate([-sin, sin], axis=1)


def _ret_tables(c):
    lg = jnp.log1p(-jnp.exp2(-5.0 - jnp.arange(H_R, dtype=F32)))
    idx = jnp.arange(c, dtype=F32)
    diff = idx[:, None] - idx[None, :]
    dmask = jnp.where(diff >= 0, jnp.exp(lg[:, None, None] * jnp.maximum(diff, 0.0)), 0.0)
    ones = jnp.ones((1, 1, HEAD_R), F32)
    qdec = jnp.exp(lg[:, None] * (idx + 1.0))[:, :, None] * ones
    kdec = jnp.exp(lg[:, None] * (c - 1.0 - idx))[:, :, None] * ones
    cdec = jnp.exp(lg * c)[:, None, None] * ones
    extra = -c % SUBLANES
    dmask = jnp.pad(dmask, ((0, 0), (0, extra), (0, extra)))
    qdec = jnp.pad(qdec, ((0, 0), (0, extra), (0, 0)))
    kdec = jnp.pad(kdec, ((0, 0), (0, extra), (0, 0)))
    return dmask, qdec, kdec, cdec


def _block_ones(n, block):
    idx = jnp.arange(n) // block
    return (idx[:, None] == idx[None, :]).astype(BF16)


def kernel(x_prompt, x_sample, state_shift, state_wkv, state_ret, norm_g, ffn1_wg, ffn1_wu, ffn1_wd,
           w_in, mu_shift, w0, w2, a0, a2, g2, k_k, k_a, r_k, lnx_w, lnx_b, ret_gn_w, w_out,
           ffn2_wg, ffn2_wu, ffn2_wd):
    assert norm_g.shape[0] == 1, "single-layer configuration"
    bp, tp, _ = x_prompt.shape
    bs, ts, _ = x_sample.shape
    l = 0
    ng = norm_g[l]
    f1 = (ffn1_wg[l].astype(BF16), ffn1_wu[l].astype(BF16), ffn1_wd[l].astype(BF16))
    row = lambda t: t[l].reshape(1, -1)
    zpad = jnp.zeros((LORA_W, D_A), BF16)
    w2p = jnp.concatenate([w2[l].astype(BF16), zpad], axis=0)
    a2p = jnp.concatenate([zpad, a2[l].astype(BF16)], axis=0)
    rk, lnw, lnb, gn = row(r_k), row(lnx_w), row(lnx_b), row(ret_gn_w)
    wkv_params = (row(w0), w2p, row(a0), a2p, g2[l].astype(BF16), row(k_k), row(k_a), rk, lnw, lnb)
    bd_pair = _block_ones(PAIR, HEAD_A)
    bd2 = jnp.concatenate([bd_pair, bd_pair], axis=0)
    tri = (jnp.arange(WKV_CHUNK)[:, None] >= jnp.arange(WKV_CHUNK)[None, :]).astype(BF16)
    tri2 = jnp.concatenate([tri, tri], axis=1)

    xp = x_prompt.reshape(bp * tp, D_MODEL)
    x1p, *f2, win, wo = _ffn(xp, ng, *f1, 0, 1, 1024,
                             cast=(ffn2_wg[l], ffn2_wu[l], ffn2_wd[l], w_in[l], w_out[l]))
    proj_consts = (ng, win, row(mu_shift))
    cos_p, sin_p = _rope_tables(jnp.arange(tp, dtype=jnp.int32))
    tm_p = 512
    (mixed, q, kr, vr, gr, hl_p) = _proj(
        x1p, None, *proj_consts, cos_p, sin_p, n_t=1, rows_per_t=tm_p, lag=1,
        tiles_per_seq=tp // tm_p, qkv_dtype=BF16)
    seq3 = lambda t: t.reshape(bp, tp, -1)
    ya_p, wkv_p = _wkv_prompt(seq3(mixed), (*wkv_params, bd2, tri2), nb=4, tt=256)
    ya_p = ya_p.reshape(bp * tp, D_A)
    yr_p, ret_p = _ret_prompt(*map(seq3, (q, kr, vr, gr)), *_ret_tables(RET_CHUNK), gn, nb=4, tt=256)
    yr_p = yr_p.reshape(bp * tp, D_R)
    yp = _ffn(x1p, ng, *f2, 4, 5, 1024, mix=(ya_p, yr_p, wo))

    m_s = bs * ts
    xs = x_sample.transpose(1, 0, 2)
    x1s = _ffn(xs.reshape(m_s, D_MODEL), ng, *f1, 0, 1, m_s)
    cos_s, sin_s = _rope_tables(PAST_LEN + jnp.arange(ts, dtype=jnp.int32))
    rows_per_t = bs
    tab = lambda t: jnp.broadcast_to(t[:, None, :], (ts, bs, HEAD_R))
    outs = _proj(x1s.reshape(ts, bs, D_MODEL),
                 state_shift[l].reshape(bs // rows_per_t, rows_per_t, D_MODEL),
                 *proj_consts, tab(cos_s), tab(sin_s), n_t=ts, rows_per_t=rows_per_t,
                 lag=rows_per_t, tiles_per_seq=1, qkv_dtype=F32)
    (mixed, q, kr, vr, gr, hl_s) = outs
    ya_s, wkv_s = _wkv_sample(mixed, *wkv_params, bd2, state_wkv[l].transpose(1, 2, 3, 0))
    wkv_s = wkv_s.transpose(3, 0, 1, 2)
    yr_s, ret_s = _ret_sample(q, kr, vr, gr, *_ret_tables(min(RET_CHUNK, ts)), gn, state_ret[l],
                              bb=2 * SUBLANES)
    ys = _ffn(x1s, ng, *f2, 4, 5, m_s,
              mix=(ya_s.reshape(m_s, D_A), yr_s.reshape(m_s, D_R), wo))
    ys = ys.reshape(ts, bs, D_MODEL).transpose(1, 0, 2)

    return (yp.reshape(bp, tp, D_MODEL), ys,
            hl_p.reshape(1, bp, D_MODEL), wkv_p[None], ret_p[None],
            hl_s.reshape(1, bs, D_MODEL), wkv_s[None], ret_s[None])
```

```python
import functools

import jax
import jax.numpy as jnp
from jax import lax
from jax.experimental import pallas as pl
from jax.experimental.pallas import tpu as pltpu

F32 = jnp.float32
BF16 = jnp.bfloat16

D_MODEL = 1024
D_A = 512
HEAD_A = 64
H_A = D_A // HEAD_A
D_R = 512
H_R = 4
HEAD_R = D_R // H_R
LORA_W, LORA_A, LORA_G = 64, 64, 128
D_FF = 2816
RET_CHUNK = 128
ROPE_BASE = 10000.0
EPS = 1e-6
GN_EPS_A = 64e-5
GN_EPS_R = 1e-5
N_SHIFT = 3 * D_A + LORA_W + LORA_A + LORA_G
N_COLS = N_SHIFT + 4 * D_R
PAST_LEN = 16384

LANES = 128
SUBLANES = 8
VMEM_LIMIT = 52 * 1024 * 1024

MXU_DIM = 256
FF_SPLIT = 6 * MXU_DIM
WKV_CHUNK = 64
PAIR = 2 * HEAD_A


def _nt(a, b):
    return lax.dot_general(a, b, (((1,), (1,)), ((), ())), preferred_element_type=F32)


def _tn(a, b):
    return lax.dot_general(a, b, (((0,), (0,)), ((), ())), preferred_element_type=F32)


def _mm(a, b):
    return jnp.dot(a, b, preferred_element_type=F32)


def _split_hi_lo(x):
    hi = x.astype(BF16)
    lo = (x - hi.astype(F32)).astype(BF16)
    return hi, lo


def _rms(x, g):
    return x * lax.rsqrt(jnp.mean(x * x, axis=-1, keepdims=True) + EPS) * g


def _softplus(x):
    return jnp.maximum(x, 0.0) + jnp.log(1.0 + jnp.exp(-jnp.abs(x)))


def _sigmoid(x):
    return 1.0 / (1.0 + jnp.exp(-x))


def _resident(shape):
    nd = len(shape)
    return pl.BlockSpec(shape, lambda *_: (0,) * nd, pipeline_mode=pl.Buffered(1))


def _params(n_axes):
    return pltpu.CompilerParams(dimension_semantics=("arbitrary",) * n_axes,
                                vmem_limit_bytes=VMEM_LIMIT)


def _ffn_kernel(with_mix, n_cast, g_in, g_out, *refs):
    n_in = (8 if with_mix else 5) + n_cast
    ins, outs = refs[:n_in], refs[n_in:]
    if with_mix:
        x_ref, ya_ref, yr_ref, wo_ref, ng_ref, wg_ref, wu_ref, wd_ref = ins[:8]
    else:
        x_ref, ng_ref, wg_ref, wu_ref, wd_ref = ins[:5]
    o_ref = outs[0]
    for src, dst in zip(ins[n_in - n_cast:], outs[1:]):
        dst[...] = src[...].astype(dst.dtype)
    half = x_ref.shape[0] // 2
    halves = [slice(0, half), slice(half, 2 * half)]
    x = [x_ref[r, :] for r in halves]
    if with_mix:
        mix = [_mm(ya_ref[r, :].astype(BF16), wo_ref[0:D_A, :])
               + _mm(yr_ref[r, :].astype(BF16), wo_ref[D_A:, :]) for r in halves]
        x = [xi + _rms(m, ng_ref[3:4, :]) for xi, m in zip(x, mix)]
    h = [_rms(xi, ng_ref[g_in:g_in + 1, :]).astype(BF16) for xi in x]
    for i, r in enumerate(halves):
        acc = None
        for cols in (slice(0, FF_SPLIT), slice(FF_SPLIT, D_FF)):
            gate = _mm(h[i], wg_ref[:, cols])
            up = _mm(h[i], wu_ref[:, cols])
            act = (gate * _sigmoid(gate) * up).astype(BF16)
            part = _mm(act, wd_ref[cols, :])
            acc = part if acc is None else acc + part
        o_ref[r, :] = x[i] + 0.5 * _rms(acc, ng_ref[g_out:g_out + 1, :])


BF16_ROWS = 2 * SUBLANES


def _cast_spec(rows, cols, steps):
    rep = 1
    while (rows * rep) % steps or (rows * rep // steps) % BF16_ROWS:
        rep *= 2
    return pl.BlockSpec((rows * rep // steps, cols), lambda i: (i // rep, 0))


def _ffn(x, ng, wg, wu, wd, g_in, g_out, tm, mix=None, cast=()):
    m = x.shape[0]
    steps = m // tm
    row = lambda w: pl.BlockSpec((tm, w), lambda i: (i, 0))
    if mix is None:
        args = (x, ng, wg, wu, wd)
        specs = [row(D_MODEL), _resident(ng.shape), _resident(wg.shape), _resident(wu.shape),
                 _resident(wd.shape)]
    else:
        ya, yr, wo = mix
        args = (x, ya, yr, wo, ng, wg, wu, wd)
        specs = [row(D_MODEL), row(D_A), row(D_R), _resident(wo.shape), _resident(ng.shape),
                 _resident(wg.shape), _resident(wu.shape), _resident(wd.shape)]
    cast_specs = [_cast_spec(*w.shape, steps) for w in cast]
    out = pl.pallas_call(
        functools.partial(_ffn_kernel, mix is not None, len(cast), g_in, g_out),
        grid=(steps,),
        in_specs=specs + cast_specs,
        out_specs=[row(D_MODEL)] + cast_specs,
        out_shape=[jax.ShapeDtypeStruct((m, D_MODEL), F32)]
        + [jax.ShapeDtypeStruct(w.shape, BF16) for w in cast],
        compiler_params=_params(1),
        name="ffn_mix" if mix is not None else "ffn",
    )(*args, *cast)
    return out if cast else out[0]


def _lag_rows(cur, first, lag):
    if lag % SUBLANES == 0:
        return jnp.concatenate([first, cur[:-lag]], axis=0)
    assert lag == 1
    rolled = pltpu.roll(cur, 1, 0)
    rid = lax.broadcasted_iota(jnp.int32, (SUBLANES, cur.shape[1]), 0)
    head = jnp.where(rid == 0, first, rolled[0:SUBLANES])
    return jnp.concatenate([head, rolled[SUBLANES:]], axis=0)


def _proj_kernel(n_t, lag, tiles_per_seq, has_prev, fuse_ret, *refs):
    it = iter(refs)
    x_ref = next(it)
    prev_ref = next(it) if has_prev else None
    ng_ref, win_ref, mu_ref = next(it), next(it), next(it)
    if fuse_ret:
        dm_ref, qd_ref, kd_ref, cd_ref, gn_ref = (next(it) for _ in range(5))
        cos_ref, sin_ref, mix_o, yr_o, s_o, hl_o, carry_scr, s_scr = tuple(it)
    else:
        cos_ref, sin_ref, mix_o, q_o, kr_o, vr_o, gr_o, hl_o, carry_scr = tuple(it)

    def load(ref):
        if n_t == 1:
            return ref[...]
        return jnp.concatenate([ref[t] for t in range(n_t)], axis=0)

    def store(ref, val, cols=slice(None)):
        if n_t == 1:
            ref[:, cols] = val.astype(ref.dtype)
        else:
            rows = val.shape[0] // n_t
            for t in range(n_t):
                ref[t, :, cols] = val[t * rows:(t + 1) * rows].astype(ref.dtype)

    x = load(x_ref)
    tm = x.shape[0]
    h = _rms(x, ng_ref[2:3, :])
    hl_o[0] = h[tm - lag:, :]
    hb = h.astype(BF16)
    seq_start = (pl.program_id(0) % tiles_per_seq) == 0
    prev_b = prev_ref[0].astype(BF16) if has_prev else None

    ret = lambda c: _mm(hb, win_ref[:, N_SHIFT + c * D_R:N_SHIFT + (c + 1) * D_R])
    cos2 = load(cos_ref)
    sin2 = load(sin_ref)

    def rope(t):
        parts = []
        for hh in range(H_R):
            th = t[:, hh * HEAD_R:(hh + 1) * HEAD_R]
            parts.append(th * cos2 + pltpu.roll(th, HEAD_R // 2, 1) * sin2)
        return jnp.concatenate(parts, axis=1)

    if fuse_ret:
        rq = rope(ret(0)).astype(BF16)
        rk = (rope(ret(1)) * (HEAD_R ** -0.5)).astype(BF16)
        rv = ret(2).astype(BF16)
        rg = ret(3)
        heads = list(range(H_R))
        lanes = [slice(hh * HEAD_R, (hh + 1) * HEAD_R) for hh in heads]
        r_state = [jnp.where(seq_start, 0.0, s_scr[hh]) for hh in heads]

    for slab, c0 in enumerate(range(0, N_SHIFT, D_A)):
        cols = slice(c0, min(c0 + D_A, N_SHIFT))
        cur = _mm(hb, win_ref[:, cols])
        if has_prev:
            first = _mm(prev_b, win_ref[:, cols])
        else:
            first = jnp.zeros((lag, cur.shape[1]), F32)
        if tiles_per_seq > 1:
            first = jnp.where(seq_start, first, carry_scr[0:lag, cols])
            carry_scr[0:lag, cols] = cur[tm - lag:, :]
        prv = _lag_rows(cur, first, lag)
        store(mix_o, cur + (prv - cur) * mu_ref[:, cols], cols)
        if fuse_ret:
            rows = slice(slab * RET_CHUNK, (slab + 1) * RET_CHUNK)
            pick = lambda t: [t[rows, ln] for ln in lanes]
            y, r_state = _ret_chunk(pick(rq), pick(rk), pick(rv), pick(rg), r_state, heads,
                                    dm_ref, qd_ref, kd_ref, cd_ref, gn_ref)
            for hh in heads:
                yr_o[rows, lanes[hh]] = y[hh].astype(yr_o.dtype)

    if fuse_ret:
        for hh in heads:
            s_scr[hh] = r_state[hh]
            s_o[0, hh] = r_state[hh]
        return

    q = ret(0)
    kr = ret(1)
    store(q_o, rope(q))
    vr = ret(2)
    store(kr_o, rope(kr) * (HEAD_R ** -0.5))
    gr = ret(3)
    store(vr_o, vr)
    store(gr_o, gr)


def _proj(x, prev, ng, win, mu, cos2, sin2, *, n_t, rows_per_t, lag, tiles_per_seq, qkv_dtype,
          ret_tables=None):
    tm = n_t * rows_per_t
    fuse_ret = ret_tables is not None
    assert not fuse_ret or (n_t == 1 and tm == pl.cdiv(N_SHIFT, D_A) * RET_CHUNK)
    if n_t == 1:
        m = x.shape[0]
        n_tiles = m // tm
        row = lambda w: pl.BlockSpec((tm, w), lambda i: (i, 0))
        shp = lambda w, dt: jax.ShapeDtypeStruct((m, w), dt)
        tab = pl.BlockSpec((tm, HEAD_R), lambda i: (i % tiles_per_seq, 0))
    else:
        m = x.shape[0] * x.shape[1]
        n_tiles = x.shape[1] // rows_per_t
        row = lambda w: pl.BlockSpec((n_t, rows_per_t, w), lambda i: (0, i, 0))
        shp = lambda w, dt: jax.ShapeDtypeStruct((n_t, m // n_t, w), dt)
        tab = pl.BlockSpec((n_t, rows_per_t, HEAD_R), lambda i: (0, i, 0))
    n_seq = n_tiles // tiles_per_seq
    hl_spec = pl.BlockSpec((1, lag, D_MODEL), lambda i: (i // tiles_per_seq, 0, 0))
    args = [x]
    specs = [row(D_MODEL)]
    if prev is not None:
        args.append(prev)
        specs.append(pl.BlockSpec((1, lag, D_MODEL), lambda i: (i // tiles_per_seq, 0, 0)))
    consts = (ng, win, mu) + (tuple(ret_tables) if fuse_ret else ())
    args += list(consts) + [cos2, sin2]
    specs += [_resident(c.shape) for c in consts] + [tab, tab]
    hl_shape = jax.ShapeDtypeStruct((n_seq, lag, D_MODEL), F32)
    scratch = [pltpu.VMEM((max(lag, SUBLANES), N_SHIFT), F32)]
    if fuse_ret:
        out_shape = [shp(N_SHIFT, F32), shp(D_R, BF16),
                     jax.ShapeDtypeStruct((n_seq, H_R, HEAD_R, HEAD_R), F32), hl_shape]
        out_specs = [row(N_SHIFT), row(D_R),
                     pl.BlockSpec((1, H_R, HEAD_R, HEAD_R), lambda i: (i // tiles_per_seq, 0, 0, 0)),
                     hl_spec]
        scratch.append(pltpu.VMEM((H_R, HEAD_R, HEAD_R), F32))
    else:
        out_shape = [shp(N_SHIFT, F32)] + [shp(D_R, qkv_dtype)] * 3 + [shp(D_R, F32), hl_shape]
        out_specs = [row(N_SHIFT)] + [row(D_R)] * 4 + [hl_spec]
    return pl.pallas_call(
        functools.partial(_proj_kernel, n_t, lag, tiles_per_seq, prev is not None, fuse_ret),
        grid=(n_tiles,),
        in_specs=specs,
        out_specs=out_specs,
        out_shape=out_shape,
        scratch_shapes=scratch,
        compiler_params=_params(1),
        name="proj",
    )(*args)


def _sums_stacked(x, bd2):
    return _mm(jnp.concatenate(_split_hi_lo(x), axis=1), bd2)


def _wkv_post(y, r, k, v, g, rk, lw_g, lb_g, head_sums):
    inv_n = 1.0 / HEAD_A
    mu = head_sums(y) * inv_n
    d = y - mu
    var = head_sums(d * d) * inv_n
    yn = d * lax.rsqrt(var + GN_EPS_A) * lw_g + lb_g
    bonus = head_sums(r * k * rk) * v
    return (yn + bonus) * g


def _wkv_decay_gate(wa, gd, w0, w2p, a0, a2p, g2):
    w_pre = w0 + _mm(jnp.tanh(wa).astype(BF16), w2p)
    lw = -jnp.exp(-_softplus(-w_pre) - 0.5)
    a = _sigmoid(a0 + _mm(wa.astype(BF16), a2p))
    g = _mm(_sigmoid(gd).astype(BF16), g2)
    return lw, a, g


def _wkv_keys(k, a, k_k, k_a, head_sums):
    kk = k * k_k
    kk = kk / jnp.maximum(jnp.sqrt(head_sums(kk * kk)), 1e-12)
    return k * (1.0 + (a - 1.0) * k_a), kk, kk * a


def _wkv_prompt_kernel(nb, tt, mix_ref, w0_ref, w2_ref, a0_ref, a2_ref, g2_ref, kk_ref, ka_ref,
                       rk_ref, lnw_ref, lnb_ref, bd2_ref, tri2_ref,
                       ya_o, s_o, s_scr):
    c = WKV_CHUNK
    j = pl.program_id(1)

    @pl.when(j == 0)
    def _():
        s_scr[...] = jnp.zeros_like(s_scr)

    lane = lax.broadcasted_iota(jnp.int32, (c, PAIR), 1)
    first = lane < HEAD_A

    def stack(x):
        return jnp.concatenate([jnp.where(first, x, 0.0), jnp.where(first, 0.0, x)], axis=0)

    ri = lax.broadcasted_iota(jnp.int32, (2 * c, 2 * c), 0) % c
    ci = lax.broadcasted_iota(jnp.int32, (2 * c, 2 * c), 1) % c
    strict = ri > ci
    incl = ri >= ci
    eye = (lax.broadcasted_iota(jnp.int32, (2 * c, 2 * c), 0)
           == lax.broadcasted_iota(jnp.int32, (2 * c, 2 * c), 1)).astype(F32)
    bd2 = bd2_ref[...]
    tri2 = tri2_ref[...]
    head_sums = functools.partial(_sums_stacked, bd2=bd2)

    n_pairs = H_A // 2
    units = [(bi, p) for bi in range(nb) for p in range(n_pairs)]
    n_u = range(len(units))
    slab = [slice(p * PAIR, (p + 1) * PAIR) for p in range(n_pairs)]

    tall = lambda xs: jnp.concatenate(xs, axis=0)
    per_pair = lambda ref: tall([jnp.broadcast_to(ref[:, slab[p]], (c, PAIR)) for _, p in units])
    split = lambda x: [x[u * c:(u + 1) * c] for u in n_u]

    def prep(ch):
        rows = slice(ch * c, (ch + 1) * c)
        ld = lambda c0: [mix_ref[bi, rows, c0 + p * PAIR:c0 + (p + 1) * PAIR] for bi, p in units]
        r, k_raw, v = ld(0), ld(D_A), ld(2 * D_A)
        lora = [_wkv_decay_gate(mix_ref[bi, rows, 3 * D_A:3 * D_A + LORA_W + LORA_A],
                                mix_ref[bi, rows, 3 * D_A + LORA_W + LORA_A:N_SHIFT],
                                w0_ref[...], w2_ref[...], a0_ref[...], a2_ref[...], g2_ref[...])
                for bi in range(nb)]
        lw_all = [x[0] for x in lora]
        a = tall([lora[bi][1][:, slab[p]] for bi, p in units])
        gate = tall([lora[bi][2][:, slab[p]] for bi, p in units])
        k, kk, kka = map(split, _wkv_keys(tall(k_raw), a, per_pair(kk_ref), per_pair(ka_ref),
                                          head_sums))
        return r, v, k, kk, kka, gate, lw_all

    def gram_stage(pre):
        r, v, k, kk, kka, gate, lw_all = pre
        cum_all = [_mm(tri2, jnp.concatenate(_split_hi_lo(x), axis=0)) for x in lw_all]
        lw = [lw_all[bi][:, slab[p]] for bi, p in units]
        cum = [cum_all[bi][:, slab[p]] for bi, p in units]
        cum_end = [x[c - 1:c, :] for x in cum]
        e_pos = [jnp.exp(x) for x in cum]
        e_neg = [jnp.exp(-x) for x in cum]
        e_end = [jnp.exp(x - y) for x, y in zip(cum_end, cum)]
        xs = [jnp.concatenate([stack(-kk[u] * jnp.exp(cum[u] - lw[u])), stack(r[u] * e_pos[u])],
                              axis=0).astype(BF16) for u in n_u]
        ws = [jnp.concatenate([stack(kka[u] * e_neg[u]), stack(k[u] * e_neg[u])],
                              axis=0).astype(BF16) for u in n_u]
        we = [jnp.concatenate([stack(kka[u] * e_end[u]), stack(k[u] * e_end[u])],
                              axis=0).astype(BF16) for u in n_u]
        vs = [stack(x).astype(BF16) for x in v]
        gram = [_nt(xs[u], ws[u]) for u in n_u]
        a_ab = [jnp.where(strict, g[0:2 * c, 0:2 * c], 0.0) for g in gram]
        a_ak = [jnp.where(strict, g[0:2 * c, 2 * c:], 0.0).astype(BF16) for g in gram]
        a_r = [jnp.concatenate([jnp.where(incl, g[2 * c:, 0:2 * c], 0.0),
                                jnp.where(incl, g[2 * c:, 2 * c:], 0.0)], axis=1).astype(BF16)
               for g in gram]
        return xs, we, vs, cum_end, a_ab, a_ak, a_r

    def solve_stage(ch, pre, gs):
        rows = slice(ch * c, (ch + 1) * c)
        r, v, k, kk, kka, gate, lw_all = pre
        xs, we, vs, cum_end, a_ab, a_ak, a_r = gs
        inv = [eye + a for a in a_ab]
        pw = [x.astype(BF16) for x in a_ab]
        pw = [_mm(x, x).astype(BF16) for x in pw]
        n_lvl = c.bit_length() - 2
        for lvl in range(n_lvl):
            if lvl < n_lvl - 1:
                both = [_mm(x, jnp.concatenate([x, i.astype(BF16)], axis=1)) for i, x in zip(inv, pw)]
                inv = [i + b[:, 2 * c:] for i, b in zip(inv, both)]
                pw = [b[:, 0:2 * c].astype(BF16) for b in both]
            else:
                inv = [i + _mm(x, i.astype(BF16)) for i, x in zip(inv, pw)]
        s = [s_scr[u] for u in n_u]
        z = [_nt(xs[u], s[u].astype(BF16)) for u in n_u]
        rhs = [z[u][0:2 * c] + _mm(a_ak[u], vs[u]) for u in n_u]
        uu = [_mm(inv[u].astype(BF16), rhs[u].astype(BF16)).astype(BF16) for u in n_u]
        uv = [jnp.concatenate([uu[u], vs[u]], axis=0) for u in n_u]
        y2 = [z[u][2 * c:] + _mm(a_r[u], uv[u]) for u in n_u]
        for u in n_u:
            s_scr[u] = s[u] * jnp.exp(cum_end[u]) + _tn(uv[u], we[u])
        out = _wkv_post(tall([y2[u][0:c] + y2[u][c:] for u in n_u]), tall(r), tall(k), tall(v),
                        gate, per_pair(rk_ref), per_pair(lnw_ref), per_pair(lnb_ref), head_sums)
        for u, (bi, p) in enumerate(units):
            ya_o[bi, rows, slab[p]] = out[u * c:(u + 1) * c].astype(ya_o.dtype)

    n_chunks = tt // c
    pre = prep(0)
    for ch in range(n_chunks):
        gs = gram_stage(pre)
        nxt = prep(ch + 1) if ch + 1 < n_chunks else None
        solve_stage(ch, pre, gs)
        pre = nxt

    @pl.when(j == pl.num_programs(1) - 1)
    def _():
        for u, (bi, p) in enumerate(units):
            s = s_scr[u]
            s_o[bi, 2 * p] = s[0:HEAD_A, 0:HEAD_A]
            s_o[bi, 2 * p + 1] = s[HEAD_A:, HEAD_A:]


def _wkv_prompt(mixed, consts, *, nb, tt):
    batch, seq, _ = mixed.shape
    blk = lambda w: pl.BlockSpec((nb, tt, w), lambda b, j: (b, j, 0))
    return pl.pallas_call(
        functools.partial(_wkv_prompt_kernel, nb, tt),
        grid=(batch // nb, seq // tt),
        in_specs=[blk(N_SHIFT)] + [_resident(x.shape) for x in consts],
        out_specs=[blk(D_A), pl.BlockSpec((nb, H_A, HEAD_A, HEAD_A), lambda b, j: (b, 0, 0, 0))],
        out_shape=[jax.ShapeDtypeStruct((batch, seq, D_A), BF16),
                   jax.ShapeDtypeStruct((batch, H_A, HEAD_A, HEAD_A), F32)],
        scratch_shapes=[pltpu.VMEM((nb * H_A // 2, PAIR, PAIR), F32)],
        compiler_params=_params(2),
        name="wkv_prompt",
    )(mixed, *consts)


def _wkv_sample_kernel(n_t, r_ref, k_ref, v_ref, lora_ref,
                       w0_ref, w2_ref, a0_ref, a2_ref, g2_ref, kk_ref, ka_ref,
                       rk_ref, lnw_ref, lnb_ref, bd2_ref, s_ref,
                       ya_o, s_o, yt_scr):
    n = HEAD_A
    n_b = r_ref.shape[1]
    tall = lambda xs: jnp.concatenate(xs, axis=0)
    rows_of = lambda ref: tall([ref[t] for t in range(n_t)])
    head_sums = functools.partial(_sums_stacked, bd2=bd2_ref[...])
    r, v, lora = rows_of(r_ref), rows_of(v_ref), rows_of(lora_ref)
    lw, a, gate = _wkv_decay_gate(lora[:, 0:LORA_W + LORA_A], lora[:, LORA_W + LORA_A:],
                                  w0_ref[...], w2_ref[...], a0_ref[...], a2_ref[...], g2_ref[...])
    k, kk, kka = _wkv_keys(rows_of(k_ref), a, kk_ref[...], ka_ref[...], head_sums)

    tr = lambda x: [x[t * n_b:(t + 1) * n_b].T for t in range(n_t)]
    nkk_t, kka_t, k_t, r_t, v_t, w_t = tr(-kk), tr(kka), tr(k), tr(r), tr(v), tr(jnp.exp(lw))
    rid = lax.broadcasted_iota(jnp.int32, (SUBLANES, n_b), 0)

    for hh in range(2):
        keys = slice(hh * n, (hh + 1) * n)
        for ig in range(n // SUBLANES):
            y_tiles = [jnp.zeros((SUBLANES, n_b), F32) for _ in range(n_t)]
            for ii in range(SUBLANES):
                i = ig * SUBLANES + ii
                s = s_ref[hh, i]
                for t in range(n_t):
                    sa = jnp.sum(s * nkk_t[t][keys], axis=0, keepdims=True)
                    v_row = v_t[t][hh * n + i:hh * n + i + 1]
                    s = s * w_t[t][keys] + sa * kka_t[t][keys] + v_row * k_t[t][keys]
                    y_row = jnp.sum(s * r_t[t][keys], axis=0, keepdims=True)
                    y_tiles[t] = jnp.where(rid == ii, y_row, y_tiles[t])
                s_o[hh, i] = s
            for t in range(n_t):
                yt_scr[t, hh * n + ig * SUBLANES:hh * n + (ig + 1) * SUBLANES, :] = y_tiles[t]

    out = _wkv_post(tall([yt_scr[t].T for t in range(n_t)]), r, k, v, gate,
                    rk_ref[...], lnw_ref[...], lnb_ref[...], head_sums)
    for t in range(n_t):
        ya_o[t] = out[t * n_b:(t + 1) * n_b].astype(ya_o.dtype)


def _wkv_sample(mixed, w0, w2p, a0, a2p, g2, k_k, k_a, rk, lnw, lnb, bd2, s0):
    n_t, n_b, _ = mixed.shape
    slab = lambda first: pl.BlockSpec((n_t, n_b, PAIR), lambda p: (0, 0, first + p))
    lora = pl.BlockSpec((n_t, n_b, 2 * PAIR), lambda p: (0, 0, 3 * D_A // (2 * PAIR)))
    par = pl.BlockSpec((1, PAIR), lambda p: (0, p))
    low = pl.BlockSpec((PAIR, PAIR), lambda p: (0, p))
    st = pl.BlockSpec((2, HEAD_A, HEAD_A, n_b), lambda p: (p, 0, 0, 0))
    n_slab = D_A // PAIR
    return pl.pallas_call(
        functools.partial(_wkv_sample_kernel, n_t),
        grid=(H_A // 2,),
        in_specs=[slab(0), slab(n_slab), slab(2 * n_slab), lora,
                  par, low, par, low, low, par, par, par, par, par, _resident(bd2.shape), st],
        out_specs=[slab(0), st],
        out_shape=[jax.ShapeDtypeStruct((n_t, n_b, D_A), F32),
                   jax.ShapeDtypeStruct(s0.shape, F32)],
        scratch_shapes=[pltpu.VMEM((n_t, PAIR, n_b), F32)],
        compiler_params=_params(1),
        name="wkv_sample",
    )(mixed, mixed, mixed, mixed, w0, w2p, a0, a2p, g2, k_k, k_a, rk, lnw, lnb, bd2, s0)


def _ret_chunk(q, k, v, g, s, heads, dm_ref, qd_ref, kd_ref, cd_ref, gn_ref):
    n = range(len(q))
    qb = [x.astype(BF16) for x in q]
    kb = [x.astype(BF16) for x in k]
    vb = [x.astype(BF16) for x in v]
    inner = [(_nt(qb[u], kb[u]) * dm_ref[heads[u]]).astype(BF16) for u in n]
    q_dec = [(q[u].astype(F32) * qd_ref[heads[u]]).astype(BF16) for u in n]
    k_dec = [(k[u].astype(F32) * kd_ref[heads[u]]).astype(BF16) for u in n]
    if inner[0].shape[1] % LANES == 0:
        y = [_mm(jnp.concatenate([inner[u], q_dec[u]], axis=1),
                 jnp.concatenate([vb[u], s[u].astype(BF16)], axis=0)) for u in n]
    else:
        y = [_mm(inner[u], vb[u]) + _mm(q_dec[u], s[u].astype(BF16)) for u in n]
    s_new = [s[u] * cd_ref[heads[u]] + _tn(k_dec[u], vb[u]) for u in n]
    out = []
    for u in n:
        mu = jnp.mean(y[u], axis=-1, keepdims=True)
        d = y[u] - mu
        var = jnp.mean(d * d, axis=-1, keepdims=True)
        lanes = slice(heads[u] * HEAD_R, (heads[u] + 1) * HEAD_R)
        yn = d * lax.rsqrt(var + GN_EPS_R) * gn_ref[:, lanes]
        out.append(g[u] * _sigmoid(g[u]) * yn)
    return out, s_new


def _ret_sample_kernel(n_t, bb, q_ref, k_ref, v_ref, g_ref, dm_ref, qd_ref, kd_ref, cd_ref,
                       gn_ref, s_ref, y_o, s_o):
    rid = lax.broadcasted_iota(jnp.int32, (SUBLANES, HEAD_R), 0)
    units = [(bi, hh) for bi in range(bb) for hh in range(H_R)]
    heads = [hh for _, hh in units]
    lanes = [slice(hh * HEAD_R, (hh + 1) * HEAD_R) for hh in heads]

    def seq_rows(ref):
        outs = []
        for u, (bi, _) in enumerate(units):
            out = jnp.zeros((SUBLANES, HEAD_R), F32)
            for t in range(n_t):
                out = jnp.where(rid == t, jnp.broadcast_to(ref[t, bi:bi + 1, lanes[u]], out.shape), out)
            outs.append(out)
        return outs

    y, s_new = _ret_chunk(seq_rows(q_ref), seq_rows(k_ref), seq_rows(v_ref), seq_rows(g_ref),
                          [s_ref[bi, hh] for bi, hh in units], heads,
                          dm_ref, qd_ref, kd_ref, cd_ref, gn_ref)
    for u, (bi, hh) in enumerate(units):
        s_o[bi, hh] = s_new[u]
        for t in range(n_t):
            y_o[t, bi:bi + 1, lanes[u]] = y[u][t:t + 1].astype(y_o.dtype)


def _ret_sample(q, k, v, g, dm, qd, kd, cd, gn, s0, *, bb):
    n_t, n_b, _ = q.shape
    consts = (dm, qd, kd, cd, gn)
    blk = pl.BlockSpec((n_t, bb, D_R), lambda i: (0, i, 0))
    st = pl.BlockSpec((bb, H_R, HEAD_R, HEAD_R), lambda i: (i, 0, 0, 0))
    return pl.pallas_call(
        functools.partial(_ret_sample_kernel, n_t, bb),
        grid=(n_b // bb,),
        in_specs=[blk] * 4 + [_resident(x.shape) for x in consts] + [st],
        out_specs=[blk, st],
        out_shape=[jax.ShapeDtypeStruct((n_t, n_b, D_R), F32),
                   jax.ShapeDtypeStruct(s0.shape, F32)],
        compiler_params=_params(1),
        name="ret_sample",
    )(q, k, v, g, *consts, s0)


def _rope_tables(pos):
    half = HEAD_R // 2
    inv = ROPE_BASE ** (-jnp.arange(half, dtype=F32) / half)
    ang = pos.astype(F32)[:, None] * inv[None, :]
    cos, sin = jnp.cos(ang), jnp.sin(ang)
    return jnp.concatenate([cos, cos], axis=1), jnp.concatenate([-sin, sin], axis=1)


def _ret_tables(c):
    lg = jnp.log1p(-jnp.exp2(-5.0 - jnp.arange(H_R, dtype=F32)))
    idx = jnp.arange(c, dtype=F32)
    diff = idx[:, None] - idx[None, :]
    dmask = jnp.where(diff >= 0, jnp.exp(lg[:, None, None] * jnp.maximum(diff, 0.0)), 0.0)
    ones = jnp.ones((1, 1, HEAD_R), F32)
    qdec = jnp.exp(lg[:, None] * (idx + 1.0))[:, :, None] * ones
    kdec = jnp.exp(lg[:, None] * (c - 1.0 - idx))[:, :, None] * ones
    cdec = jnp.exp(lg * c)[:, None, None] * ones
    extra = -c % SUBLANES
    dmask = jnp.pad(dmask, ((0, 0), (0, extra), (0, extra)))
    qdec = jnp.pad(qdec, ((0, 0), (0, extra), (0, 0)))
    kdec = jnp.pad(kdec, ((0, 0), (0, extra), (0, 0)))
    return dmask, qdec, kdec, cdec


def _block_ones(n, block):
    idx = jnp.arange(n) // block
    return (idx[:, None] == idx[None, :]).astype(BF16)


def kernel(x_prompt, x_sample, state_shift, state_wkv, state_ret, norm_g, ffn1_wg, ffn1_wu, ffn1_wd,
           w_in, mu_shift, w0, w2, a0, a2, g2, k_k, k_a, r_k, lnx_w, lnx_b, ret_gn_w, w_out,
           ffn2_wg, ffn2_wu, ffn2_wd):
    assert norm_g.shape[0] == 1, "single-layer configuration"
    bp, tp, _ = x_prompt.shape
    bs, ts, _ = x_sample.shape
    l = 0
    ng = norm_g[l]
    f1 = (ffn1_wg[l].astype(BF16), ffn1_wu[l].astype(BF16), ffn1_wd[l].astype(BF16))
    row = lambda t: t[l].reshape(1, -1)
    zpad = jnp.zeros((LORA_W, D_A), BF16)
    w2p = jnp.concatenate([w2[l].astype(BF16), zpad], axis=0)
    a2p = jnp.concatenate([zpad, a2[l].astype(BF16)], axis=0)
    rk, lnw, lnb, gn = row(r_k), row(lnx_w), row(lnx_b), row(ret_gn_w)
    wkv_params = (row(w0), w2p, row(a0), a2p, g2[l].astype(BF16), row(k_k), row(k_a), rk, lnw, lnb)
    bd_pair = _block_ones(PAIR, HEAD_A)
    bd2 = jnp.concatenate([bd_pair, bd_pair], axis=0)
    tri = (jnp.arange(WKV_CHUNK)[:, None] >= jnp.arange(WKV_CHUNK)[None, :]).astype(BF16)
    tri2 = jnp.concatenate([tri, tri], axis=1)

    xp = x_prompt.reshape(bp * tp, D_MODEL)
    x1p, *f2, win, wo = _ffn(xp, ng, *f1, 0, 1, 1024,
                             cast=(ffn2_wg[l], ffn2_wu[l], ffn2_wd[l], w_in[l], w_out[l]))
    proj_consts = (ng, win, row(mu_shift))
    cos_p, sin_p = _rope_tables(jnp.arange(tp, dtype=jnp.int32))
    tm_p = 512
    (mixed, yr_p, ret_p, hl_p) = _proj(
        x1p, None, *proj_consts, cos_p, sin_p, n_t=1, rows_per_t=tm_p, lag=1,
        tiles_per_seq=tp // tm_p, qkv_dtype=BF16, ret_tables=(*_ret_tables(RET_CHUNK), gn))
    seq3 = lambda t: t.reshape(bp, tp, -1)
    ya_p, wkv_p = _wkv_prompt(seq3(mixed), (*wkv_params, bd2, tri2), nb=4, tt=256)
    ya_p = ya_p.reshape(bp * tp, D_A)
    yp = _ffn(x1p, ng, *f2, 4, 5, 1024, mix=(ya_p, yr_p, wo))

    m_s = bs * ts
    xs = x_sample.transpose(1, 0, 2)
    x1s = _ffn(xs.reshape(m_s, D_MODEL), ng, *f1, 0, 1, m_s)
    cos_s, sin_s = _rope_tables(PAST_LEN + jnp.arange(ts, dtype=jnp.int32))
    rows_per_t = bs
    tab = lambda t: jnp.broadcast_to(t[:, None, :], (ts, bs, HEAD_R))
    outs = _proj(x1s.reshape(ts, bs, D_MODEL),
                 state_shift[l].reshape(bs // rows_per_t, rows_per_t, D_MODEL),
                 *proj_consts, tab(cos_s), tab(sin_s), n_t=ts, rows_per_t=rows_per_t,
                 lag=rows_per_t, tiles_per_seq=1, qkv_dtype=F32)
    (mixed, q, kr, vr, gr, hl_s) = outs
    ya_s, wkv_s = _wkv_sample(mixed, *wkv_params, bd2, state_wkv[l].transpose(1, 2, 3, 0))
    wkv_s = wkv_s.transpose(3, 0, 1, 2)
    yr_s, ret_s = _ret_sample(q, kr, vr, gr, *_ret_tables(min(RET_CHUNK, ts)), gn, state_ret[l],
                              bb=2 * SUBLANES)
    ys = _ffn(x1s, ng, *f2, 4, 5, m_s,
              mix=(ya_s.reshape(m_s, D_A), yr_s.reshape(m_s, D_R), wo))
    ys = ys.reshape(ts, bs, D_MODEL).transpose(1, 0, 2)

    return (yp.reshape(bp, tp, D_MODEL), ys,
            hl_p.reshape(1, bp, D_MODEL), wkv_p[None], ret_p[None],
            hl_s.reshape(1, bs, D_MODEL), wkv_s[None], ret_s[None])
```

```python
import functools

import jax
import jax.numpy as jnp
from jax import lax
from jax.experimental import pallas as pl
from jax.experimental.pallas import tpu as pltpu

F32 = jnp.float32
BF16 = jnp.bfloat16

D_MODEL = 1024
D_A = 512
HEAD_A = 64
H_A = D_A // HEAD_A
D_R = 512
H_R = 4
HEAD_R = D_R // H_R
LORA_W, LORA_A, LORA_G = 64, 64, 128
D_FF = 2816
RET_CHUNK = 128
ROPE_BASE = 10000.0
EPS = 1e-6
GN_EPS_A = 64e-5
GN_EPS_R = 1e-5
N_SHIFT = 3 * D_A + LORA_W + LORA_A + LORA_G
N_COLS = N_SHIFT + 4 * D_R
PAST_LEN = 16384

LANES = 128
SUBLANES = 8
VMEM_LIMIT = 52 * 1024 * 1024

MXU_DIM = 256
FF_SPLIT = 6 * MXU_DIM
WKV_CHUNK = 64
PAIR = 2 * HEAD_A


def _nt(a, b):
    return lax.dot_general(a, b, (((1,), (1,)), ((), ())), preferred_element_type=F32)


def _tn(a, b):
    return lax.dot_general(a, b, (((0,), (0,)), ((), ())), preferred_element_type=F32)


def _mm(a, b):
    return jnp.dot(a, b, preferred_element_type=F32)


def _split_hi_lo(x):
    hi = x.astype(BF16)
    lo = (x - hi.astype(F32)).astype(BF16)
    return hi, lo


def _rms(x, g):
    return x * lax.rsqrt(jnp.mean(x * x, axis=-1, keepdims=True) + EPS) * g


def _softplus(x):
    return jnp.maximum(x, 0.0) + jnp.log(1.0 + jnp.exp(-jnp.abs(x)))


def _sigmoid(x):
    return 1.0 / (1.0 + jnp.exp(-x))


def _resident(shape):
    nd = len(shape)
    return pl.BlockSpec(shape, lambda *_: (0,) * nd, pipeline_mode=pl.Buffered(1))


def _params(n_axes):
    return pltpu.CompilerParams(dimension_semantics=("arbitrary",) * n_axes,
                                vmem_limit_bytes=VMEM_LIMIT)


def _ffn_kernel(with_mix, n_cast, g_in, g_out, *refs):
    n_in = (8 if with_mix else 5) + n_cast
    ins, outs = refs[:n_in], refs[n_in:]
    if with_mix:
        x_ref, ya_ref, yr_ref, wo_ref, ng_ref, wg_ref, wu_ref, wd_ref = ins[:8]
    else:
        x_ref, ng_ref, wg_ref, wu_ref, wd_ref = ins[:5]
    o_ref = outs[0]
    for src, dst in zip(ins[n_in - n_cast:], outs[1:]):
        dst[...] = src[...].astype(dst.dtype)
    half = x_ref.shape[0] // 2
    halves = [slice(0, half), slice(half, 2 * half)]
    x = [x_ref[r, :] for r in halves]
    if with_mix:
        mix = [_mm(ya_ref[r, :].astype(BF16), wo_ref[0:D_A, :])
               + _mm(yr_ref[r, :].astype(BF16), wo_ref[D_A:, :]) for r in halves]
        x = [xi + _rms(m, ng_ref[3:4, :]) for xi, m in zip(x, mix)]
    h = [_rms(xi, ng_ref[g_in:g_in + 1, :]).astype(BF16) for xi in x]
    for i, r in enumerate(halves):
        acc = None
        for cols in (slice(0, FF_SPLIT), slice(FF_SPLIT, D_FF)):
            gate = _mm(h[i], wg_ref[:, cols])
            up = _mm(h[i], wu_ref[:, cols])
            act = (gate * _sigmoid(gate) * up).astype(BF16)
            part = _mm(act, wd_ref[cols, :])
            acc = part if acc is None else acc + part
        o_ref[r, :] = x[i] + 0.5 * _rms(acc, ng_ref[g_out:g_out + 1, :])


BF16_ROWS = 2 * SUBLANES


def _cast_spec(rows, cols, steps):
    rep = 1
    while (rows * rep) % steps or (rows * rep // steps) % BF16_ROWS:
        rep *= 2
    return pl.BlockSpec((rows * rep // steps, cols), lambda i: (i // rep, 0))


def _ffn(x, ng, wg, wu, wd, g_in, g_out, tm, mix=None, cast=()):
    m = x.shape[0]
    steps = m // tm
    row = lambda w: pl.BlockSpec((tm, w), lambda i: (i, 0))
    if mix is None:
        args = (x, ng, wg, wu, wd)
        specs = [row(D_MODEL), _resident(ng.shape), _resident(wg.shape), _resident(wu.shape),
                 _resident(wd.shape)]
    else:
        ya, yr, wo = mix
        args = (x, ya, yr, wo, ng, wg, wu, wd)
        specs = [row(D_MODEL), row(D_A), row(D_R), _resident(wo.shape), _resident(ng.shape),
                 _resident(wg.shape), _resident(wu.shape), _resident(wd.shape)]
    cast_specs = [_cast_spec(*w.shape, steps) for w in cast]
    out = pl.pallas_call(
        functools.partial(_ffn_kernel, mix is not None, len(cast), g_in, g_out),
        grid=(steps,),
        in_specs=specs + cast_specs,
        out_specs=[row(D_MODEL)] + cast_specs,
        out_shape=[jax.ShapeDtypeStruct((m, D_MODEL), F32)]
        + [jax.ShapeDtypeStruct(w.shape, BF16) for w in cast],
        compiler_params=_params(1),
        name="ffn_mix" if mix is not None else "ffn",
    )(*args, *cast)
    return out if cast else out[0]


def _ffn_stream_kernel(g_in, g_out, x_ref, ng_ref, wg_ref, wu_ref, wd_ref,
                       o_ref, wg_o, wu_o, wd_o, h_scr, acc_scr):
    c = pl.program_id(0)

    @pl.when(c == 0)
    def _():
        h_scr[...] = _rms(x_ref[...], ng_ref[g_in:g_in + 1, :]).astype(BF16)
        acc_scr[...] = jnp.zeros_like(acc_scr)

    wg, wu, wd = wg_ref[...].astype(BF16), wu_ref[...].astype(BF16), wd_ref[...].astype(BF16)
    wg_o[...] = wg
    wu_o[...] = wu
    wd_o[...] = wd
    h = h_scr[...]
    gate = _mm(h, wg)
    act = (gate * _sigmoid(gate) * _mm(h, wu)).astype(BF16)
    acc_scr[...] += _mm(act, wd)

    @pl.when(c == pl.num_programs(0) - 1)
    def _():
        o_ref[...] = x_ref[...] + 0.5 * _rms(acc_scr[...], ng_ref[g_out:g_out + 1, :])


def _ffn_stream(x, ng, wg, wu, wd, g_in, g_out):
    m = x.shape[0]
    slab = MXU_DIM
    whole = lambda a: pl.BlockSpec(a.shape, lambda c: (0,) * a.ndim)
    cols = pl.BlockSpec((D_MODEL, slab), lambda c: (0, c))
    rows = pl.BlockSpec((slab, D_MODEL), lambda c: (c, 0))
    return pl.pallas_call(
        functools.partial(_ffn_stream_kernel, g_in, g_out),
        grid=(D_FF // slab,),
        in_specs=[whole(x), whole(ng), cols, cols, rows],
        out_specs=[whole(x), cols, cols, rows],
        out_shape=[jax.ShapeDtypeStruct((m, D_MODEL), F32)]
        + [jax.ShapeDtypeStruct(w.shape, BF16) for w in (wg, wu, wd)],
        scratch_shapes=[pltpu.VMEM((m, D_MODEL), BF16), pltpu.VMEM((m, D_MODEL), F32)],
        compiler_params=_params(1),
        name="ffn_stream",
    )(x, ng, wg, wu, wd)


def _lag_rows(cur, first, lag):
    if lag % SUBLANES == 0:
        return jnp.concatenate([first, cur[:-lag]], axis=0)
    assert lag == 1
    rolled = pltpu.roll(cur, 1, 0)
    rid = lax.broadcasted_iota(jnp.int32, (SUBLANES, cur.shape[1]), 0)
    head = jnp.where(rid == 0, first, rolled[0:SUBLANES])
    return jnp.concatenate([head, rolled[SUBLANES:]], axis=0)


def _proj_kernel(n_t, lag, tiles_per_seq, has_prev, fuse_ret, *refs):
    it = iter(refs)
    x_ref = next(it)
    prev_ref = next(it) if has_prev else None
    ng_ref, win_ref, mu_ref = next(it), next(it), next(it)
    if fuse_ret:
        dm_ref, qd_ref, kd_ref, cd_ref, gn_ref = (next(it) for _ in range(5))
        cos_ref, sin_ref, mix_o, yr_o, s_o, hl_o, carry_scr, s_scr = tuple(it)
    else:
        cos_ref, sin_ref, mix_o, q_o, kr_o, vr_o, gr_o, hl_o, carry_scr = tuple(it)

    def load(ref):
        if n_t == 1:
            return ref[...]
        return jnp.concatenate([ref[t] for t in range(n_t)], axis=0)

    def store(ref, val, cols=slice(None)):
        if n_t == 1:
            ref[:, cols] = val.astype(ref.dtype)
        else:
            rows = val.shape[0] // n_t
            for t in range(n_t):
                ref[t, :, cols] = val[t * rows:(t + 1) * rows].astype(ref.dtype)

    x = load(x_ref)
    tm = x.shape[0]
    h = _rms(x, ng_ref[2:3, :])
    hl_o[0] = h[tm - lag:, :]
    hb = h.astype(BF16)
    seq_start = (pl.program_id(0) % tiles_per_seq) == 0
    prev_b = prev_ref[0].astype(BF16) if has_prev else None

    ret = lambda c: _mm(hb, win_ref[:, N_SHIFT + c * D_R:N_SHIFT + (c + 1) * D_R])
    cos2 = load(cos_ref)
    sin2 = load(sin_ref)

    def rope(t):
        parts = []
        for hh in range(H_R):
            th = t[:, hh * HEAD_R:(hh + 1) * HEAD_R]
            parts.append(th * cos2 + pltpu.roll(th, HEAD_R // 2, 1) * sin2)
        return jnp.concatenate(parts, axis=1)

    if fuse_ret:
        rq = rope(ret(0)).astype(BF16)
        rk = (rope(ret(1)) * (HEAD_R ** -0.5)).astype(BF16)
        rv = ret(2).astype(BF16)
        rg = ret(3)
        heads = list(range(H_R))
        lanes = [slice(hh * HEAD_R, (hh + 1) * HEAD_R) for hh in heads]
        r_state = [jnp.where(seq_start, 0.0, s_scr[hh]) for hh in heads]

    for slab, c0 in enumerate(range(0, N_SHIFT, D_A)):
        cols = slice(c0, min(c0 + D_A, N_SHIFT))
        cur = _mm(hb, win_ref[:, cols])
        if has_prev:
            first = _mm(prev_b, win_ref[:, cols])
        else:
            first = jnp.zeros((lag, cur.shape[1]), F32)
        if tiles_per_seq > 1:
            first = jnp.where(seq_start, first, carry_scr[0:lag, cols])
            carry_scr[0:lag, cols] = cur[tm - lag:, :]
        prv = _lag_rows(cur, first, lag)
        store(mix_o, cur + (prv - cur) * mu_ref[:, cols], cols)
        if fuse_ret:
            rows = slice(slab * RET_CHUNK, (slab + 1) * RET_CHUNK)
            pick = lambda t: [t[rows, ln] for ln in lanes]
            y, r_state = _ret_chunk(pick(rq), pick(rk), pick(rv), pick(rg), r_state, heads,
                                    dm_ref, qd_ref, kd_ref, cd_ref, gn_ref)
            for hh in heads:
                yr_o[rows, lanes[hh]] = y[hh].astype(yr_o.dtype)

    if fuse_ret:
        for hh in heads:
            s_scr[hh] = r_state[hh]
            s_o[0, hh] = r_state[hh]
        return

    q = ret(0)
    kr = ret(1)
    store(q_o, rope(q))
    vr = ret(2)
    store(kr_o, rope(kr) * (HEAD_R ** -0.5))
    gr = ret(3)
    store(vr_o, vr)
    store(gr_o, gr)


def _proj(x, prev, ng, win, mu, cos2, sin2, *, n_t, rows_per_t, lag, tiles_per_seq, qkv_dtype,
          ret_tables=None):
    tm = n_t * rows_per_t
    fuse_ret = ret_tables is not None
    assert not fuse_ret or (n_t == 1 and tm == pl.cdiv(N_SHIFT, D_A) * RET_CHUNK)
    if n_t == 1:
        m = x.shape[0]
        n_tiles = m // tm
        row = lambda w: pl.BlockSpec((tm, w), lambda i: (i, 0))
        shp = lambda w, dt: jax.ShapeDtypeStruct((m, w), dt)
        tab = pl.BlockSpec((tm, HEAD_R), lambda i: (i % tiles_per_seq, 0))
    else:
        m = x.shape[0] * x.shape[1]
        n_tiles = x.shape[1] // rows_per_t
        row = lambda w: pl.BlockSpec((n_t, rows_per_t, w), lambda i: (0, i, 0))
        shp = lambda w, dt: jax.ShapeDtypeStruct((n_t, m // n_t, w), dt)
        tab = pl.BlockSpec((n_t, rows_per_t, HEAD_R), lambda i: (0, i, 0))
    n_seq = n_tiles // tiles_per_seq
    hl_spec = pl.BlockSpec((1, lag, D_MODEL), lambda i: (i // tiles_per_seq, 0, 0))
    args = [x]
    specs = [row(D_MODEL)]
    if prev is not None:
        args.append(prev)
        specs.append(pl.BlockSpec((1, lag, D_MODEL), lambda i: (i // tiles_per_seq, 0, 0)))
    consts = (ng, win, mu) + (tuple(ret_tables) if fuse_ret else ())
    args += list(consts) + [cos2, sin2]
    specs += [_resident(c.shape) for c in consts] + [tab, tab]
    hl_shape = jax.ShapeDtypeStruct((n_seq, lag, D_MODEL), F32)
    scratch = [pltpu.VMEM((max(lag, SUBLANES), N_SHIFT), F32)]
    if fuse_ret:
        out_shape = [shp(N_SHIFT, F32), shp(D_R, BF16),
                     jax.ShapeDtypeStruct((n_seq, H_R, HEAD_R, HEAD_R), F32), hl_shape]
        out_specs = [row(N_SHIFT), row(D_R),
                     pl.BlockSpec((1, H_R, HEAD_R, HEAD_R), lambda i: (i // tiles_per_seq, 0, 0, 0)),
                     hl_spec]
        scratch.append(pltpu.VMEM((H_R, HEAD_R, HEAD_R), F32))
    else:
        out_shape = [shp(N_SHIFT, F32)] + [shp(D_R, qkv_dtype)] * 3 + [shp(D_R, F32), hl_shape]
        out_specs = [row(N_SHIFT)] + [row(D_R)] * 4 + [hl_spec]
    return pl.pallas_call(
        functools.partial(_proj_kernel, n_t, lag, tiles_per_seq, prev is not None, fuse_ret),
        grid=(n_tiles,),
        in_specs=specs,
        out_specs=out_specs,
        out_shape=out_shape,
        scratch_shapes=scratch,
        compiler_params=_params(1),
        name="proj",
    )(*args)


def _sums_stacked(x, bd2):
    return _mm(jnp.concatenate(_split_hi_lo(x), axis=1), bd2)


def _wkv_post(y, r, k, v, g, rk, lw_g, lb_g, head_sums):
    inv_n = 1.0 / HEAD_A
    mu = head_sums(y) * inv_n
    d = y - mu
    var = head_sums(d * d) * inv_n
    yn = d * lax.rsqrt(var + GN_EPS_A) * lw_g + lb_g
    bonus = head_sums(r * k * rk) * v
    return (yn + bonus) * g


def _wkv_decay_gate(wa, gd, w0, w2p, a0, a2p, g2):
    w_pre = w0 + _mm(jnp.tanh(wa).astype(BF16), w2p)
    lw = -jnp.exp(-_softplus(-w_pre) - 0.5)
    a = _sigmoid(a0 + _mm(wa.astype(BF16), a2p))
    g = _mm(_sigmoid(gd).astype(BF16), g2)
    return lw, a, g


def _wkv_keys(k, a, k_k, k_a, head_sums):
    kk = k * k_k
    kk = kk / jnp.maximum(jnp.sqrt(head_sums(kk * kk)), 1e-12)
    return k * (1.0 + (a - 1.0) * k_a), kk, kk * a


def _wkv_prompt_kernel(nb, tt, mix_ref, w0_ref, w2_ref, a0_ref, a2_ref, g2_ref, kk_ref, ka_ref,
                       rk_ref, lnw_ref, lnb_ref, bd2_ref, tri2_ref,
                       ya_o, s_o, s_scr):
    c = WKV_CHUNK
    j = pl.program_id(1)

    @pl.when(j == 0)
    def _():
        s_scr[...] = jnp.zeros_like(s_scr)

    lane = lax.broadcasted_iota(jnp.int32, (c, PAIR), 1)
    first = lane < HEAD_A

    def stack(x):
        return jnp.concatenate([jnp.where(first, x, 0.0), jnp.where(first, 0.0, x)], axis=0)

    ri = lax.broadcasted_iota(jnp.int32, (2 * c, 2 * c), 0) % c
    ci = lax.broadcasted_iota(jnp.int32, (2 * c, 2 * c), 1) % c
    strict = ri > ci
    incl = ri >= ci
    eye = (lax.broadcasted_iota(jnp.int32, (2 * c, 2 * c), 0)
           == lax.broadcasted_iota(jnp.int32, (2 * c, 2 * c), 1)).astype(F32)
    bd2 = bd2_ref[...]
    tri2 = tri2_ref[...]
    head_sums = functools.partial(_sums_stacked, bd2=bd2)

    n_pairs = H_A // 2
    units = [(bi, p) for bi in range(nb) for p in range(n_pairs)]
    n_u = range(len(units))
    slab = [slice(p * PAIR, (p + 1) * PAIR) for p in range(n_pairs)]

    tall = lambda xs: jnp.concatenate(xs, axis=0)
    per_pair = lambda ref: tall([jnp.broadcast_to(ref[:, slab[p]], (c, PAIR)) for _, p in units])
    split = lambda x: [x[u * c:(u + 1) * c] for u in n_u]

    def prep(ch):
        rows = slice(ch * c, (ch + 1) * c)
        ld = lambda c0: [mix_ref[bi, rows, c0 + p * PAIR:c0 + (p + 1) * PAIR] for bi, p in units]
        r, k_raw, v = ld(0), ld(D_A), ld(2 * D_A)
        lora = [_wkv_decay_gate(mix_ref[bi, rows, 3 * D_A:3 * D_A + LORA_W + LORA_A],
                                mix_ref[bi, rows, 3 * D_A + LORA_W + LORA_A:N_SHIFT],
                                w0_ref[...], w2_ref[...], a0_ref[...], a2_ref[...], g2_ref[...])
                for bi in range(nb)]
        lw_all = [x[0] for x in lora]
        a = tall([lora[bi][1][:, slab[p]] for bi, p in units])
        gate = tall([lora[bi][2][:, slab[p]] for bi, p in units])
        k, kk, kka = map(split, _wkv_keys(tall(k_raw), a, per_pair(kk_ref), per_pair(ka_ref),
                                          head_sums))
        return r, v, k, kk, kka, gate, lw_all

    def gram_stage(pre):
        r, v, k, kk, kka, gate, lw_all = pre
        cum_all = [_mm(tri2, jnp.concatenate(_split_hi_lo(x), axis=0)) for x in lw_all]
        lw = [lw_all[bi][:, slab[p]] for bi, p in units]
        cum = [cum_all[bi][:, slab[p]] for bi, p in units]
        cum_end = [x[c - 1:c, :] for x in cum]
        e_pos = [jnp.exp(x) for x in cum]
        e_neg = [jnp.exp(-x) for x in cum]
        e_end = [jnp.exp(x - y) for x, y in zip(cum_end, cum)]
        xs = [jnp.concatenate([stack(-kk[u] * jnp.exp(cum[u] - lw[u])), stack(r[u] * e_pos[u])],
                              axis=0).astype(BF16) for u in n_u]
        ws = [jnp.concatenate([stack(kka[u] * e_neg[u]), stack(k[u] * e_neg[u])],
                              axis=0).astype(BF16) for u in n_u]
        we = [jnp.concatenate([stack(kka[u] * e_end[u]), stack(k[u] * e_end[u])],
                              axis=0).astype(BF16) for u in n_u]
        vs = [stack(x).astype(BF16) for x in v]
        gram = [_nt(xs[u], ws[u]) for u in n_u]
        a_ab = [jnp.where(strict, g[0:2 * c, 0:2 * c], 0.0) for g in gram]
        a_ak = [jnp.where(strict, g[0:2 * c, 2 * c:], 0.0).astype(BF16) for g in gram]
        a_r = [jnp.concatenate([jnp.where(incl, g[2 * c:, 0:2 * c], 0.0),
                                jnp.where(incl, g[2 * c:, 2 * c:], 0.0)], axis=1).astype(BF16)
               for g in gram]
        return xs, we, vs, cum_end, a_ab, a_ak, a_r

    def solve_stage(ch, pre, gs):
        rows = slice(ch * c, (ch + 1) * c)
        r, v, k, kk, kka, gate, lw_all = pre
        xs, we, vs, cum_end, a_ab, a_ak, a_r = gs
        inv = [eye + a for a in a_ab]
        pw = [x.astype(BF16) for x in a_ab]
        pw = [_mm(x, x).astype(BF16) for x in pw]
        n_lvl = c.bit_length() - 2
        for lvl in range(n_lvl):
            if lvl < n_lvl - 1:
                both = [_mm(x, jnp.concatenate([x, i.astype(BF16)], axis=1)) for i, x in zip(inv, pw)]
                inv = [i + b[:, 2 * c:] for i, b in zip(inv, both)]
                pw = [b[:, 0:2 * c].astype(BF16) for b in both]
            else:
                inv = [i + _mm(x, i.astype(BF16)) for i, x in zip(inv, pw)]
        s = [s_scr[u] for u in n_u]
        z = [_nt(xs[u], s[u].astype(BF16)) for u in n_u]
        rhs = [z[u][0:2 * c] + _mm(a_ak[u], vs[u]) for u in n_u]
        uu = [_mm(inv[u].astype(BF16), rhs[u].astype(BF16)).astype(BF16) for u in n_u]
        uv = [jnp.concatenate([uu[u], vs[u]], axis=0) for u in n_u]
        y2 = [z[u][2 * c:] + _mm(a_r[u], uv[u]) for u in n_u]
        for u in n_u:
            s_scr[u] = s[u] * jnp.exp(cum_end[u]) + _tn(uv[u], we[u])
        out = _wkv_post(tall([y2[u][0:c] + y2[u][c:] for u in n_u]), tall(r), tall(k), tall(v),
                        gate, per_pair(rk_ref), per_pair(lnw_ref), per_pair(lnb_ref), head_sums)
        for u, (bi, p) in enumerate(units):
            ya_o[bi, rows, slab[p]] = out[u * c:(u + 1) * c].astype(ya_o.dtype)

    n_chunks = tt // c
    pre = prep(0)
    for ch in range(n_chunks):
        gs = gram_stage(pre)
        nxt = prep(ch + 1) if ch + 1 < n_chunks else None
        solve_stage(ch, pre, gs)
        pre = nxt

    @pl.when(j == pl.num_programs(1) - 1)
    def _():
        for u, (bi, p) in enumerate(units):
            s = s_scr[u]
            s_o[bi, 2 * p] = s[0:HEAD_A, 0:HEAD_A]
            s_o[bi, 2 * p + 1] = s[HEAD_A:, HEAD_A:]


def _wkv_prompt(mixed, consts, *, nb, tt):
    batch, seq, _ = mixed.shape
    blk = lambda w: pl.BlockSpec((nb, tt, w), lambda b, j: (b, j, 0))
    return pl.pallas_call(
        functools.partial(_wkv_prompt_kernel, nb, tt),
        grid=(batch // nb, seq // tt),
        in_specs=[blk(N_SHIFT)] + [_resident(x.shape) for x in consts],
        out_specs=[blk(D_A), pl.BlockSpec((nb, H_A, HEAD_A, HEAD_A), lambda b, j: (b, 0, 0, 0))],
        out_shape=[jax.ShapeDtypeStruct((batch, seq, D_A), BF16),
                   jax.ShapeDtypeStruct((batch, H_A, HEAD_A, HEAD_A), F32)],
        scratch_shapes=[pltpu.VMEM((nb * H_A // 2, PAIR, PAIR), F32)],
        compiler_params=_params(2),
        name="wkv_prompt",
    )(mixed, *consts)


def _wkv_sample_kernel(n_t, r_ref, k_ref, v_ref, lora_ref,
                       w0_ref, w2_ref, a0_ref, a2_ref, g2_ref, kk_ref, ka_ref,
                       rk_ref, lnw_ref, lnb_ref, bd2_ref, s_ref,
                       ya_o, s_o, yt_scr):
    n = HEAD_A
    n_b = r_ref.shape[1]
    tall = lambda xs: jnp.concatenate(xs, axis=0)
    rows_of = lambda ref: tall([ref[t] for t in range(n_t)])
    head_sums = functools.partial(_sums_stacked, bd2=bd2_ref[...])
    r, v, lora = rows_of(r_ref), rows_of(v_ref), rows_of(lora_ref)
    lw, a, gate = _wkv_decay_gate(lora[:, 0:LORA_W + LORA_A], lora[:, LORA_W + LORA_A:],
                                  w0_ref[...], w2_ref[...], a0_ref[...], a2_ref[...], g2_ref[...])
    k, kk, kka = _wkv_keys(rows_of(k_ref), a, kk_ref[...], ka_ref[...], head_sums)

    tr = lambda x: [x[t * n_b:(t + 1) * n_b].T for t in range(n_t)]
    nkk_t, kka_t, k_t, r_t, v_t, w_t = tr(-kk), tr(kka), tr(k), tr(r), tr(v), tr(jnp.exp(lw))
    rid = lax.broadcasted_iota(jnp.int32, (SUBLANES, n_b), 0)

    for hh in range(2):
        keys = slice(hh * n, (hh + 1) * n)
        for ig in range(n // SUBLANES):
            y_tiles = [jnp.zeros((SUBLANES, n_b), F32) for _ in range(n_t)]
            for ii in range(SUBLANES):
                i = ig * SUBLANES + ii
                s = s_ref[hh, i]
                for t in range(n_t):
                    sa = jnp.sum(s * nkk_t[t][keys], axis=0, keepdims=True)
                    v_row = v_t[t][hh * n + i:hh * n + i + 1]
                    s = s * w_t[t][keys] + sa * kka_t[t][keys] + v_row * k_t[t][keys]
                    y_row = jnp.sum(s * r_t[t][keys], axis=0, keepdims=True)
                    y_tiles[t] = jnp.where(rid == ii, y_row, y_tiles[t])
                s_o[hh, i] = s
            for t in range(n_t):
                yt_scr[t, hh * n + ig * SUBLANES:hh * n + (ig + 1) * SUBLANES, :] = y_tiles[t]

    out = _wkv_post(tall([yt_scr[t].T for t in range(n_t)]), r, k, v, gate,
                    rk_ref[...], lnw_ref[...], lnb_ref[...], head_sums)
    for t in range(n_t):
        ya_o[t] = out[t * n_b:(t + 1) * n_b].astype(ya_o.dtype)


def _wkv_sample(mixed, w0, w2p, a0, a2p, g2, k_k, k_a, rk, lnw, lnb, bd2, s0):
    n_t, n_b, _ = mixed.shape
    slab = lambda first: pl.BlockSpec((n_t, n_b, PAIR), lambda p: (0, 0, first + p))
    lora = pl.BlockSpec((n_t, n_b, 2 * PAIR), lambda p: (0, 0, 3 * D_A // (2 * PAIR)))
    par = pl.BlockSpec((1, PAIR), lambda p: (0, p))
    low = pl.BlockSpec((PAIR, PAIR), lambda p: (0, p))
    st = pl.BlockSpec((2, HEAD_A, HEAD_A, n_b), lambda p: (p, 0, 0, 0))
    n_slab = D_A // PAIR
    return pl.pallas_call(
        functools.partial(_wkv_sample_kernel, n_t),
        grid=(H_A // 2,),
        in_specs=[slab(0), slab(n_slab), slab(2 * n_slab), lora,
                  par, low, par, low, low, par, par, par, par, par, _resident(bd2.shape), st],
        out_specs=[slab(0), st],
        out_shape=[jax.ShapeDtypeStruct((n_t, n_b, D_A), F32),
                   jax.ShapeDtypeStruct(s0.shape, F32)],
        scratch_shapes=[pltpu.VMEM((n_t, PAIR, n_b), F32)],
        compiler_params=_params(1),
        name="wkv_sample",
    )(mixed, mixed, mixed, mixed, w0, w2p, a0, a2p, g2, k_k, k_a, rk, lnw, lnb, bd2, s0)


def _ret_chunk(q, k, v, g, s, heads, dm_ref, qd_ref, kd_ref, cd_ref, gn_ref):
    n = range(len(q))
    qb = [x.astype(BF16) for x in q]
    kb = [x.astype(BF16) for x in k]
    vb = [x.astype(BF16) for x in v]
    inner = [(_nt(qb[u], kb[u]) * dm_ref[heads[u]]).astype(BF16) for u in n]
    q_dec = [(q[u].astype(F32) * qd_ref[heads[u]]).astype(BF16) for u in n]
    k_dec = [(k[u].astype(F32) * kd_ref[heads[u]]).astype(BF16) for u in n]
    if inner[0].shape[1] % LANES == 0:
        y = [_mm(jnp.concatenate([inner[u], q_dec[u]], axis=1),
                 jnp.concatenate([vb[u], s[u].astype(BF16)], axis=0)) for u in n]
    else:
        y = [_mm(inner[u], vb[u]) + _mm(q_dec[u], s[u].astype(BF16)) for u in n]
    s_new = [s[u] * cd_ref[heads[u]] + _tn(k_dec[u], vb[u]) for u in n]
    out = []
    for u in n:
        mu = jnp.mean(y[u], axis=-1, keepdims=True)
        d = y[u] - mu
        var = jnp.mean(d * d, axis=-1, keepdims=True)
        lanes = slice(heads[u] * HEAD_R, (heads[u] + 1) * HEAD_R)
        yn = d * lax.rsqrt(var + GN_EPS_R) * gn_ref[:, lanes]
        out.append(g[u] * _sigmoid(g[u]) * yn)
    return out, s_new


def _ret_sample_kernel(n_t, bb, q_ref, k_ref, v_ref, g_ref, dm_ref, qd_ref, kd_ref, cd_ref,
                       gn_ref, s_ref, y_o, s_o):
    rid = lax.broadcasted_iota(jnp.int32, (SUBLANES, HEAD_R), 0)
    units = [(bi, hh) for bi in range(bb) for hh in range(H_R)]
    heads = [hh for _, hh in units]
    lanes = [slice(hh * HEAD_R, (hh + 1) * HEAD_R) for hh in heads]

    def seq_rows(ref):
        outs = []
        for u, (bi, _) in enumerate(units):
            out = jnp.zeros((SUBLANES, HEAD_R), F32)
            for t in range(n_t):
                out = jnp.where(rid == t, jnp.broadcast_to(ref[t, bi:bi + 1, lanes[u]], out.shape), out)
            outs.append(out)
        return outs

    y, s_new = _ret_chunk(seq_rows(q_ref), seq_rows(k_ref), seq_rows(v_ref), seq_rows(g_ref),
                          [s_ref[bi, hh] for bi, hh in units], heads,
                          dm_ref, qd_ref, kd_ref, cd_ref, gn_ref)
    for u, (bi, hh) in enumerate(units):
        s_o[bi, hh] = s_new[u]
        for t in range(n_t):
            y_o[t, bi:bi + 1, lanes[u]] = y[u][t:t + 1].astype(y_o.dtype)


def _ret_sample(q, k, v, g, dm, qd, kd, cd, gn, s0, *, bb):
    n_t, n_b, _ = q.shape
    consts = (dm, qd, kd, cd, gn)
    blk = pl.BlockSpec((n_t, bb, D_R), lambda i: (0, i, 0))
    st = pl.BlockSpec((bb, H_R, HEAD_R, HEAD_R), lambda i: (i, 0, 0, 0))
    return pl.pallas_call(
        functools.partial(_ret_sample_kernel, n_t, bb),
        grid=(n_b // bb,),
        in_specs=[blk] * 4 + [_resident(x.shape) for x in consts] + [st],
        out_specs=[blk, st],
        out_shape=[jax.ShapeDtypeStruct((n_t, n_b, D_R), F32),
                   jax.ShapeDtypeStruct(s0.shape, F32)],
        compiler_params=_params(1),
        name="ret_sample",
    )(q, k, v, g, *consts, s0)


def _rope_tables(pos):
    half = HEAD_R // 2
    inv = ROPE_BASE ** (-jnp.arange(half, dtype=F32) / half)
    ang = pos.astype(F32)[:, None] * inv[None, :]
    cos, sin = jnp.cos(ang), jnp.sin(ang)
    return jnp.concatenate([cos, cos], axis=1), jnp.concatenate([-sin, sin], axis=1)


def _ret_tables(c):
    lg = jnp.log1p(-jnp.exp2(-5.0 - jnp.arange(H_R, dtype=F32)))
    idx = jnp.arange(c, dtype=F32)
    diff = idx[:, None] - idx[None, :]
    dmask = jnp.where(diff >= 0, jnp.exp(lg[:, None, None] * jnp.maximum(diff, 0.0)), 0.0)
    ones = jnp.ones((1, 1, HEAD_R), F32)
    qdec = jnp.exp(lg[:, None] * (idx + 1.0))[:, :, None] * ones
    kdec = jnp.exp(lg[:, None] * (c - 1.0 - idx))[:, :, None] * ones
    cdec = jnp.exp(lg * c)[:, None, None] * ones
    extra = -c % SUBLANES
    dmask = jnp.pad(dmask, ((0, 0), (0, extra), (0, extra)))
    qdec = jnp.pad(qdec, ((0, 0), (0, extra), (0, 0)))
    kdec = jnp.pad(kdec, ((0, 0), (0, extra), (0, 0)))
    return dmask, qdec, kdec, cdec


def _block_ones(n, block):
    idx = jnp.arange(n) // block
    return (idx[:, None] == idx[None, :]).astype(BF16)


def kernel(x_prompt, x_sample, state_shift, state_wkv, state_ret, norm_g, ffn1_wg, ffn1_wu, ffn1_wd,
           w_in, mu_shift, w0, w2, a0, a2, g2, k_k, k_a, r_k, lnx_w, lnx_b, ret_gn_w, w_out,
           ffn2_wg, ffn2_wu, ffn2_wd):
    assert norm_g.shape[0] == 1, "single-layer configuration"
    bp, tp, _ = x_prompt.shape
    bs, ts, _ = x_sample.shape
    l = 0
    ng = norm_g[l]
    row = lambda t: t[l].reshape(1, -1)
    zpad = jnp.zeros((LORA_W, D_A), BF16)
    w2p = jnp.concatenate([w2[l].astype(BF16), zpad], axis=0)
    a2p = jnp.concatenate([zpad, a2[l].astype(BF16)], axis=0)
    rk, lnw, lnb, gn = row(r_k), row(lnx_w), row(lnx_b), row(ret_gn_w)
    wkv_params = (row(w0), w2p, row(a0), a2p, g2[l].astype(BF16), row(k_k), row(k_a), rk, lnw, lnb)
    bd_pair = _block_ones(PAIR, HEAD_A)
    bd2 = jnp.concatenate([bd_pair, bd_pair], axis=0)
    tri = (jnp.arange(WKV_CHUNK)[:, None] >= jnp.arange(WKV_CHUNK)[None, :]).astype(BF16)
    tri2 = jnp.concatenate([tri, tri], axis=1)

    m_s = bs * ts
    xs = x_sample.transpose(1, 0, 2)
    x1s, *f1 = _ffn_stream(xs.reshape(m_s, D_MODEL), ng, ffn1_wg[l], ffn1_wu[l], ffn1_wd[l], 0, 1)

    xp = x_prompt.reshape(bp * tp, D_MODEL)
    x1p, *f2, win, wo = _ffn(xp, ng, *f1, 0, 1, 1024,
                             cast=(ffn2_wg[l], ffn2_wu[l], ffn2_wd[l], w_in[l], w_out[l]))
    proj_consts = (ng, win, row(mu_shift))
    cos_p, sin_p = _rope_tables(jnp.arange(tp, dtype=jnp.int32))
    tm_p = 512
    (mixed, yr_p, ret_p, hl_p) = _proj(
        x1p, None, *proj_consts, cos_p, sin_p, n_t=1, rows_per_t=tm_p, lag=1,
        tiles_per_seq=tp // tm_p, qkv_dtype=BF16, ret_tables=(*_ret_tables(RET_CHUNK), gn))
    seq3 = lambda t: t.reshape(bp, tp, -1)
    ya_p, wkv_p = _wkv_prompt(seq3(mixed), (*wkv_params, bd2, tri2), nb=4, tt=256)
    ya_p = ya_p.reshape(bp * tp, D_A)
    yp = _ffn(x1p, ng, *f2, 4, 5, 1024, mix=(ya_p, yr_p, wo))

    cos_s, sin_s = _rope_tables(PAST_LEN + jnp.arange(ts, dtype=jnp.int32))
    rows_per_t = bs
    tab = lambda t: jnp.broadcast_to(t[:, None, :], (ts, bs, HEAD_R))
    outs = _proj(x1s.reshape(ts, bs, D_MODEL),
                 state_shift[l].reshape(bs // rows_per_t, rows_per_t, D_MODEL),
                 *proj_consts, tab(cos_s), tab(sin_s), n_t=ts, rows_per_t=rows_per_t,
                 lag=rows_per_t, tiles_per_seq=1, qkv_dtype=F32)
    (mixed, q, kr, vr, gr, hl_s) = outs
    ya_s, wkv_s = _wkv_sample(mixed, *wkv_params, bd2, state_wkv[l].transpose(1, 2, 3, 0))
    wkv_s = wkv_s.transpose(3, 0, 1, 2)
    yr_s, ret_s = _ret_sample(q, kr, vr, gr, *_ret_tables(min(RET_CHUNK, ts)), gn, state_ret[l],
                              bb=2 * SUBLANES)
    ys = _ffn(x1s, ng, *f2, 4, 5, m_s,
              mix=(ya_s.reshape(m_s, D_A), yr_s.reshape(m_s, D_R), wo))
    ys = ys.reshape(ts, bs, D_MODEL).transpose(1, 0, 2)

    return (yp.reshape(bp, tp, D_MODEL), ys,
            hl_p.reshape(1, bp, D_MODEL), wkv_p[None], ret_p[None],
            hl_s.reshape(1, bs, D_MODEL), wkv_s[None], ret_s[None])
```

```python
import functools

import jax
import jax.numpy as jnp
from jax import lax
from jax.experimental import pallas as pl
from jax.experimental.pallas import tpu as pltpu

F32 = jnp.float32
BF16 = jnp.bfloat16

D_MODEL = 1024
D_A = 512
HEAD_A = 64
H_A = D_A // HEAD_A
D_R = 512
H_R = 4
HEAD_R = D_R // H_R
LORA_W, LORA_A, LORA_G = 64, 64, 128
D_FF = 2816
RET_CHUNK = 128
ROPE_BASE = 10000.0
EPS = 1e-6
GN_EPS_A = 64e-5
GN_EPS_R = 1e-5
N_SHIFT = 3 * D_A + LORA_W + LORA_A + LORA_G
N_COLS = N_SHIFT + 4 * D_R
PAST_LEN = 16384

LANES = 128
SUBLANES = 8
VMEM_LIMIT = 52 * 1024 * 1024

MXU_DIM = 256
FF_SPLIT = 6 * MXU_DIM
WKV_CHUNK = 64
PAIR = 2 * HEAD_A


def _nt(a, b):
    return lax.dot_general(a, b, (((1,), (1,)), ((), ())), preferred_element_type=F32)


def _tn(a, b):
    return lax.dot_general(a, b, (((0,), (0,)), ((), ())), preferred_element_type=F32)


def _mm(a, b):
    return jnp.dot(a, b, preferred_element_type=F32)


def _split_hi_lo(x):
    hi = x.astype(BF16)
    lo = (x - hi.astype(F32)).astype(BF16)
    return hi, lo


def _rms(x, g):
    return x * lax.rsqrt(jnp.mean(x * x, axis=-1, keepdims=True) + EPS) * g


def _softplus(x):
    return jnp.maximum(x, 0.0) + jnp.log(1.0 + jnp.exp(-jnp.abs(x)))


def _sigmoid(x):
    return 1.0 / (1.0 + jnp.exp(-x))


def _resident(shape):
    nd = len(shape)
    return pl.BlockSpec(shape, lambda *_: (0,) * nd, pipeline_mode=pl.Buffered(1))


def _params(n_axes):
    return pltpu.CompilerParams(dimension_semantics=("arbitrary",) * n_axes,
                                vmem_limit_bytes=VMEM_LIMIT)


def _ffn_kernel(with_mix, n_cast, g_in, g_out, *refs):
    n_in = (8 if with_mix else 5) + n_cast
    ins, outs = refs[:n_in], refs[n_in:]
    if with_mix:
        x_ref, ya_ref, yr_ref, wo_ref, ng_ref, wg_ref, wu_ref, wd_ref = ins[:8]
    else:
        x_ref, ng_ref, wg_ref, wu_ref, wd_ref = ins[:5]
    o_ref = outs[0]
    for src, dst in zip(ins[n_in - n_cast:], outs[1:]):
        dst[...] = src[...].astype(dst.dtype)
    half = x_ref.shape[0] // 2
    halves = [slice(0, half), slice(half, 2 * half)]
    x = [x_ref[r, :] for r in halves]
    if with_mix:
        mix = [_mm(ya_ref[r, :].astype(BF16), wo_ref[0:D_A, :])
               + _mm(yr_ref[r, :].astype(BF16), wo_ref[D_A:, :]) for r in halves]
        x = [xi + _rms(m, ng_ref[3:4, :]) for xi, m in zip(x, mix)]
    h = [_rms(xi, ng_ref[g_in:g_in + 1, :]).astype(BF16) for xi in x]
    for i, r in enumerate(halves):
        acc = None
        for cols in (slice(0, FF_SPLIT), slice(FF_SPLIT, D_FF)):
            gate = _mm(h[i], wg_ref[:, cols])
            up = _mm(h[i], wu_ref[:, cols])
            act = (gate * _sigmoid(gate) * up).astype(BF16)
            part = _mm(act, wd_ref[cols, :])
            acc = part if acc is None else acc + part
        out = x[i] + 0.5 * _rms(acc, ng_ref[g_out:g_out + 1, :])
        if o_ref.ndim == 2:
            o_ref[r, :] = out
        else:
            n_seq = o_ref.shape[0]
            for j in range(half // n_seq):
                o_ref[:, i * (half // n_seq) + j, :] = out[j * n_seq:(j + 1) * n_seq]


BF16_ROWS = 2 * SUBLANES


def _cast_spec(rows, cols, steps):
    rep = 1
    while (rows * rep) % steps or (rows * rep // steps) % BF16_ROWS:
        rep *= 2
    return pl.BlockSpec((rows * rep // steps, cols), lambda i: (i // rep, 0))


def _ffn(x, ng, wg, wu, wd, g_in, g_out, tm, mix=None, cast=(), out_seqs=None):
    m = x.shape[0]
    steps = m // tm
    assert out_seqs is None or (steps == 1 and (m // 2) % out_seqs == 0)
    o_shape = (m, D_MODEL) if out_seqs is None else (out_seqs, m // out_seqs, D_MODEL)
    row = lambda w: pl.BlockSpec((tm, w), lambda i: (i, 0))
    if mix is None:
        args = (x, ng, wg, wu, wd)
        specs = [row(D_MODEL), _resident(ng.shape), _resident(wg.shape), _resident(wu.shape),
                 _resident(wd.shape)]
    else:
        ya, yr, wo = mix
        args = (x, ya, yr, wo, ng, wg, wu, wd)
        specs = [row(D_MODEL), row(D_A), row(D_R), _resident(wo.shape), _resident(ng.shape),
                 _resident(wg.shape), _resident(wu.shape), _resident(wd.shape)]
    cast_specs = [_cast_spec(*w.shape, steps) for w in cast]
    out = pl.pallas_call(
        functools.partial(_ffn_kernel, mix is not None, len(cast), g_in, g_out),
        grid=(steps,),
        in_specs=specs + cast_specs,
        out_specs=[row(D_MODEL) if out_seqs is None else pl.BlockSpec(o_shape, lambda i: (0, 0, 0))]
        + cast_specs,
        out_shape=[jax.ShapeDtypeStruct(o_shape, F32)]
        + [jax.ShapeDtypeStruct(w.shape, BF16) for w in cast],
        compiler_params=_params(1),
        name="ffn_mix" if mix is not None else "ffn",
    )(*args, *cast)
    return out if cast else out[0]


def _ffn_stream_kernel(g_in, g_out, x_ref, ng_ref, wg_ref, wu_ref, wd_ref,
                       o_ref, wg_o, wu_o, wd_o, h_scr, acc_scr):
    c = pl.program_id(0)
    x_rows = lambda: jnp.concatenate([x_ref[:, t, :] for t in range(x_ref.shape[1])], axis=0)

    @pl.when(c == 0)
    def _():
        h_scr[...] = _rms(x_rows(), ng_ref[g_in:g_in + 1, :]).astype(BF16)
        acc_scr[...] = jnp.zeros_like(acc_scr)

    wg, wu, wd = wg_ref[...].astype(BF16), wu_ref[...].astype(BF16), wd_ref[...].astype(BF16)
    wg_o[...] = wg
    wu_o[...] = wu
    wd_o[...] = wd
    h = h_scr[...]
    gate = _mm(h, wg)
    act = (gate * _sigmoid(gate) * _mm(h, wu)).astype(BF16)
    acc_scr[...] += _mm(act, wd)

    @pl.when(c == pl.num_programs(0) - 1)
    def _():
        o_ref[...] = x_rows() + 0.5 * _rms(acc_scr[...], ng_ref[g_out:g_out + 1, :])


def _ffn_stream(x, ng, wg, wu, wd, g_in, g_out):
    m = x.shape[0] * x.shape[1]
    slab = MXU_DIM
    whole = lambda a: pl.BlockSpec(a.shape, lambda c: (0,) * a.ndim)
    cols = pl.BlockSpec((D_MODEL, slab), lambda c: (0, c))
    rows = pl.BlockSpec((slab, D_MODEL), lambda c: (c, 0))
    return pl.pallas_call(
        functools.partial(_ffn_stream_kernel, g_in, g_out),
        grid=(D_FF // slab,),
        in_specs=[whole(x), whole(ng), cols, cols, rows],
        out_specs=[pl.BlockSpec((m, D_MODEL), lambda c: (0, 0)), cols, cols, rows],
        out_shape=[jax.ShapeDtypeStruct((m, D_MODEL), F32)]
        + [jax.ShapeDtypeStruct(w.shape, BF16) for w in (wg, wu, wd)],
        scratch_shapes=[pltpu.VMEM((m, D_MODEL), BF16), pltpu.VMEM((m, D_MODEL), F32)],
        compiler_params=_params(1),
        name="ffn_stream",
    )(x, ng, wg, wu, wd)


def _lag_rows(cur, first, lag):
    if lag % SUBLANES == 0:
        return jnp.concatenate([first, cur[:-lag]], axis=0)
    assert lag == 1
    rolled = pltpu.roll(cur, 1, 0)
    rid = lax.broadcasted_iota(jnp.int32, (SUBLANES, cur.shape[1]), 0)
    head = jnp.where(rid == 0, first, rolled[0:SUBLANES])
    return jnp.concatenate([head, rolled[SUBLANES:]], axis=0)


def _proj_kernel(n_t, lag, tiles_per_seq, has_prev, fuse_ret, *refs):
    it = iter(refs)
    x_ref = next(it)
    prev_ref = next(it) if has_prev else None
    ng_ref, win_ref, mu_ref = next(it), next(it), next(it)
    if fuse_ret:
        dm_ref, qd_ref, kd_ref, cd_ref, gn_ref = (next(it) for _ in range(5))
        cos_ref, sin_ref, mix_o, yr_o, s_o, hl_o, carry_scr, s_scr = tuple(it)
    else:
        cos_ref, sin_ref, mix_o, q_o, kr_o, vr_o, gr_o, hl_o, carry_scr = tuple(it)

    def load(ref):
        if n_t == 1:
            return ref[...]
        return jnp.concatenate([ref[t] for t in range(n_t)], axis=0)

    def store(ref, val, cols=slice(None)):
        if n_t == 1:
            ref[:, cols] = val.astype(ref.dtype)
        else:
            rows = val.shape[0] // n_t
            for t in range(n_t):
                ref[t, :, cols] = val[t * rows:(t + 1) * rows].astype(ref.dtype)

    x = load(x_ref)
    tm = x.shape[0]
    h = _rms(x, ng_ref[2:3, :])
    hl_o[0] = h[tm - lag:, :]
    hb = h.astype(BF16)
    seq_start = (pl.program_id(0) % tiles_per_seq) == 0
    prev_b = prev_ref[0].astype(BF16) if has_prev else None

    ret = lambda c: _mm(hb, win_ref[:, N_SHIFT + c * D_R:N_SHIFT + (c + 1) * D_R])
    cos2 = load(cos_ref)
    sin2 = load(sin_ref)

    def rope(t):
        parts = []
        for hh in range(H_R):
            th = t[:, hh * HEAD_R:(hh + 1) * HEAD_R]
            parts.append(th * cos2 + pltpu.roll(th, HEAD_R // 2, 1) * sin2)
        return jnp.concatenate(parts, axis=1)

    if fuse_ret:
        rq = rope(ret(0)).astype(BF16)
        rk = (rope(ret(1)) * (HEAD_R ** -0.5)).astype(BF16)
        rv = ret(2).astype(BF16)
        rg = ret(3)
        heads = list(range(H_R))
        lanes = [slice(hh * HEAD_R, (hh + 1) * HEAD_R) for hh in heads]
        r_state = [jnp.where(seq_start, 0.0, s_scr[hh]) for hh in heads]

    for slab, c0 in enumerate(range(0, N_SHIFT, D_A)):
        cols = slice(c0, min(c0 + D_A, N_SHIFT))
        cur = _mm(hb, win_ref[:, cols])
        if has_prev:
            first = _mm(prev_b, win_ref[:, cols])
        else:
            first = jnp.zeros((lag, cur.shape[1]), F32)
        if tiles_per_seq > 1:
            first = jnp.where(seq_start, first, carry_scr[0:lag, cols])
            carry_scr[0:lag, cols] = cur[tm - lag:, :]
        prv = _lag_rows(cur, first, lag)
        store(mix_o, cur + (prv - cur) * mu_ref[:, cols], cols)
        if fuse_ret:
            rows = slice(slab * RET_CHUNK, (slab + 1) * RET_CHUNK)
            pick = lambda t: [t[rows, ln] for ln in lanes]
            y, r_state = _ret_chunk(pick(rq), pick(rk), pick(rv), pick(rg), r_state, heads,
                                    dm_ref, qd_ref, kd_ref, cd_ref, gn_ref)
            for hh in heads:
                yr_o[rows, lanes[hh]] = y[hh].astype(yr_o.dtype)

    if fuse_ret:
        for hh in heads:
            s_scr[hh] = r_state[hh]
            s_o[0, hh] = r_state[hh]
        return

    q = ret(0)
    kr = ret(1)
    store(q_o, rope(q))
    vr = ret(2)
    store(kr_o, rope(kr) * (HEAD_R ** -0.5))
    gr = ret(3)
    store(vr_o, vr)
    store(gr_o, gr)


def _proj(x, prev, ng, win, mu, cos2, sin2, *, n_t, rows_per_t, lag, tiles_per_seq, qkv_dtype,
          ret_tables=None):
    tm = n_t * rows_per_t
    fuse_ret = ret_tables is not None
    assert not fuse_ret or (n_t == 1 and tm == pl.cdiv(N_SHIFT, D_A) * RET_CHUNK)
    if n_t == 1:
        m = x.shape[0]
        n_tiles = m // tm
        row = lambda w: pl.BlockSpec((tm, w), lambda i: (i, 0))
        shp = lambda w, dt: jax.ShapeDtypeStruct((m, w), dt)
        tab = pl.BlockSpec((tm, HEAD_R), lambda i: (i % tiles_per_seq, 0))
    else:
        m = x.shape[0] * x.shape[1]
        n_tiles = x.shape[1] // rows_per_t
        row = lambda w: pl.BlockSpec((n_t, rows_per_t, w), lambda i: (0, i, 0))
        shp = lambda w, dt: jax.ShapeDtypeStruct((n_t, m // n_t, w), dt)
        tab = pl.BlockSpec((n_t, rows_per_t, HEAD_R), lambda i: (0, i, 0))
    n_seq = n_tiles // tiles_per_seq
    hl_spec = pl.BlockSpec((1, lag, D_MODEL), lambda i: (i // tiles_per_seq, 0, 0))
    args = [x]
    specs = [row(D_MODEL)]
    if prev is not None:
        args.append(prev)
        specs.append(pl.BlockSpec((1, lag, D_MODEL), lambda i: (i // tiles_per_seq, 0, 0)))
    consts = (ng, win, mu) + (tuple(ret_tables) if fuse_ret else ())
    args += list(consts) + [cos2, sin2]
    specs += [_resident(c.shape) for c in consts] + [tab, tab]
    hl_shape = jax.ShapeDtypeStruct((n_seq, lag, D_MODEL), F32)
    scratch = [pltpu.VMEM((max(lag, SUBLANES), N_SHIFT), F32)]
    if fuse_ret:
        out_shape = [shp(N_SHIFT, F32), shp(D_R, BF16),
                     jax.ShapeDtypeStruct((n_seq, H_R, HEAD_R, HEAD_R), F32), hl_shape]
        out_specs = [row(N_SHIFT), row(D_R),
                     pl.BlockSpec((1, H_R, HEAD_R, HEAD_R), lambda i: (i // tiles_per_seq, 0, 0, 0)),
                     hl_spec]
        scratch.append(pltpu.VMEM((H_R, HEAD_R, HEAD_R), F32))
    else:
        out_shape = [shp(N_SHIFT, F32)] + [shp(D_R, qkv_dtype)] * 3 + [shp(D_R, F32), hl_shape]
        out_specs = [row(N_SHIFT)] + [row(D_R)] * 4 + [hl_spec]
    return pl.pallas_call(
        functools.partial(_proj_kernel, n_t, lag, tiles_per_seq, prev is not None, fuse_ret),
        grid=(n_tiles,),
        in_specs=specs,
        out_specs=out_specs,
        out_shape=out_shape,
        scratch_shapes=scratch,
        compiler_params=_params(1),
        name="proj",
    )(*args)


def _sums_stacked(x, bd2):
    return _mm(jnp.concatenate(_split_hi_lo(x), axis=1), bd2)


def _wkv_post(y, r, k, v, g, rk, lw_g, lb_g, head_sums):
    inv_n = 1.0 / HEAD_A
    mu = head_sums(y) * inv_n
    d = y - mu
    var = head_sums(d * d) * inv_n
    yn = d * lax.rsqrt(var + GN_EPS_A) * lw_g + lb_g
    bonus = head_sums(r * k * rk) * v
    return (yn + bonus) * g


def _wkv_decay_gate(wa, gd, w0, w2p, a0, a2p, g2):
    w_pre = w0 + _mm(jnp.tanh(wa).astype(BF16), w2p)
    lw = -jnp.exp(-_softplus(-w_pre) - 0.5)
    a = _sigmoid(a0 + _mm(wa.astype(BF16), a2p))
    g = _mm(_sigmoid(gd).astype(BF16), g2)
    return lw, a, g


def _wkv_keys(k, a, k_k, k_a, head_sums):
    kk = k * k_k
    kk = kk / jnp.maximum(jnp.sqrt(head_sums(kk * kk)), 1e-12)
    return k * (1.0 + (a - 1.0) * k_a), kk, kk * a


def _wkv_prompt_kernel(nb, tt, mix_ref, w0_ref, w2_ref, a0_ref, a2_ref, g2_ref, kk_ref, ka_ref,
                       rk_ref, lnw_ref, lnb_ref, bd2_ref, tri2_ref,
                       ya_o, s_o, s_scr):
    c = WKV_CHUNK
    j = pl.program_id(1)

    @pl.when(j == 0)
    def _():
        s_scr[...] = jnp.zeros_like(s_scr)

    lane = lax.broadcasted_iota(jnp.int32, (c, PAIR), 1)
    first = lane < HEAD_A

    def stack(x):
        return jnp.concatenate([jnp.where(first, x, 0.0), jnp.where(first, 0.0, x)], axis=0)

    ri = lax.broadcasted_iota(jnp.int32, (2 * c, 2 * c), 0) % c
    ci = lax.broadcasted_iota(jnp.int32, (2 * c, 2 * c), 1) % c
    strict = ri > ci
    incl = ri >= ci
    eye = (lax.broadcasted_iota(jnp.int32, (2 * c, 2 * c), 0)
           == lax.broadcasted_iota(jnp.int32, (2 * c, 2 * c), 1)).astype(F32)
    bd2 = bd2_ref[...]
    tri2 = tri2_ref[...]
    head_sums = functools.partial(_sums_stacked, bd2=bd2)

    n_pairs = H_A // 2
    units = [(bi, p) for bi in range(nb) for p in range(n_pairs)]
    n_u = range(len(units))
    slab = [slice(p * PAIR, (p + 1) * PAIR) for p in range(n_pairs)]

    tall = lambda xs: jnp.concatenate(xs, axis=0)
    per_pair = lambda ref: tall([jnp.broadcast_to(ref[:, slab[p]], (c, PAIR)) for _, p in units])
    split = lambda x: [x[u * c:(u + 1) * c] for u in n_u]

    def prep(ch):
        rows = slice(ch * c, (ch + 1) * c)
        ld = lambda c0: [mix_ref[bi, rows, c0 + p * PAIR:c0 + (p + 1) * PAIR] for bi, p in units]
        r, k_raw, v = ld(0), ld(D_A), ld(2 * D_A)
        lora = [_wkv_decay_gate(mix_ref[bi, rows, 3 * D_A:3 * D_A + LORA_W + LORA_A],
                                mix_ref[bi, rows, 3 * D_A + LORA_W + LORA_A:N_SHIFT],
                                w0_ref[...], w2_ref[...], a0_ref[...], a2_ref[...], g2_ref[...])
                for bi in range(nb)]
        lw_all = [x[0] for x in lora]
        a = tall([lora[bi][1][:, slab[p]] for bi, p in units])
        gate = tall([lora[bi][2][:, slab[p]] for bi, p in units])
        k, kk, kka = map(split, _wkv_keys(tall(k_raw), a, per_pair(kk_ref), per_pair(ka_ref),
                                          head_sums))
        return r, v, k, kk, kka, gate, lw_all

    def gram_stage(pre):
        r, v, k, kk, kka, gate, lw_all = pre
        cum_all = [_mm(tri2, jnp.concatenate(_split_hi_lo(x), axis=0)) for x in lw_all]
        lw = [lw_all[bi][:, slab[p]] for bi, p in units]
        cum = [cum_all[bi][:, slab[p]] for bi, p in units]
        cum_end = [x[c - 1:c, :] for x in cum]
        e_pos = [jnp.exp(x) for x in cum]
        e_neg = [jnp.exp(-x) for x in cum]
        e_end = [jnp.exp(x - y) for x, y in zip(cum_end, cum)]
        xs = [jnp.concatenate([stack(-kk[u] * jnp.exp(cum[u] - lw[u])), stack(r[u] * e_pos[u])],
                              axis=0).astype(BF16) for u in n_u]
        ws = [jnp.concatenate([stack(kka[u] * e_neg[u]), stack(k[u] * e_neg[u])],
                              axis=0).astype(BF16) for u in n_u]
        we = [jnp.concatenate([stack(kka[u] * e_end[u]), stack(k[u] * e_end[u])],
                              axis=0).astype(BF16) for u in n_u]
        vs = [stack(x).astype(BF16) for x in v]
        gram = [_nt(xs[u], ws[u]) for u in n_u]
        a_ab = [jnp.where(strict, g[0:2 * c, 0:2 * c], 0.0) for g in gram]
        a_ak = [jnp.where(strict, g[0:2 * c, 2 * c:], 0.0).astype(BF16) for g in gram]
        a_r = [jnp.concatenate([jnp.where(incl, g[2 * c:, 0:2 * c], 0.0),
                                jnp.where(incl, g[2 * c:, 2 * c:], 0.0)], axis=1).astype(BF16)
               for g in gram]
        return xs, we, vs, cum_end, a_ab, a_ak, a_r

    def solve_stage(ch, pre, gs):
        rows = slice(ch * c, (ch + 1) * c)
        r, v, k, kk, kka, gate, lw_all = pre
        xs, we, vs, cum_end, a_ab, a_ak, a_r = gs
        inv = [eye + a for a in a_ab]
        pw = [x.astype(BF16) for x in a_ab]
        pw = [_mm(x, x).astype(BF16) for x in pw]
        n_lvl = c.bit_length() - 2
        for lvl in range(n_lvl):
            if lvl < n_lvl - 1:
                both = [_mm(x, jnp.concatenate([x, i.astype(BF16)], axis=1)) for i, x in zip(inv, pw)]
                inv = [i + b[:, 2 * c:] for i, b in zip(inv, both)]
                pw = [b[:, 0:2 * c].astype(BF16) for b in both]
            else:
                inv = [i + _mm(x, i.astype(BF16)) for i, x in zip(inv, pw)]
        s = [s_scr[u] for u in n_u]
        z = [_nt(xs[u], s[u].astype(BF16)) for u in n_u]
        rhs = [z[u][0:2 * c] + _mm(a_ak[u], vs[u]) for u in n_u]
        uu = [_mm(inv[u].astype(BF16), rhs[u].astype(BF16)).astype(BF16) for u in n_u]
        uv = [jnp.concatenate([uu[u], vs[u]], axis=0) for u in n_u]
        y2 = [z[u][2 * c:] + _mm(a_r[u], uv[u]) for u in n_u]
        for u in n_u:
            s_scr[u] = s[u] * jnp.exp(cum_end[u]) + _tn(uv[u], we[u])
        out = _wkv_post(tall([y2[u][0:c] + y2[u][c:] for u in n_u]), tall(r), tall(k), tall(v),
                        gate, per_pair(rk_ref), per_pair(lnw_ref), per_pair(lnb_ref), head_sums)
        for u, (bi, p) in enumerate(units):
            ya_o[bi, rows, slab[p]] = out[u * c:(u + 1) * c].astype(ya_o.dtype)

    n_chunks = tt // c
    pre = prep(0)
    for ch in range(n_chunks):
        gs = gram_stage(pre)
        nxt = prep(ch + 1) if ch + 1 < n_chunks else None
        solve_stage(ch, pre, gs)
        pre = nxt

    @pl.when(j == pl.num_programs(1) - 1)
    def _():
        for u, (bi, p) in enumerate(units):
            s = s_scr[u]
            s_o[bi, 2 * p] = s[0:HEAD_A, 0:HEAD_A]
            s_o[bi, 2 * p + 1] = s[HEAD_A:, HEAD_A:]


def _wkv_prompt(mixed, consts, *, nb, tt):
    batch, seq, _ = mixed.shape
    blk = lambda w: pl.BlockSpec((nb, tt, w), lambda b, j: (b, j, 0))
    return pl.pallas_call(
        functools.partial(_wkv_prompt_kernel, nb, tt),
        grid=(batch // nb, seq // tt),
        in_specs=[blk(N_SHIFT)] + [_resident(x.shape) for x in consts],
        out_specs=[blk(D_A), pl.BlockSpec((nb, H_A, HEAD_A, HEAD_A), lambda b, j: (b, 0, 0, 0))],
        out_shape=[jax.ShapeDtypeStruct((batch, seq, D_A), BF16),
                   jax.ShapeDtypeStruct((batch, H_A, HEAD_A, HEAD_A), F32)],
        scratch_shapes=[pltpu.VMEM((nb * H_A // 2, PAIR, PAIR), F32)],
        compiler_params=_params(2),
        name="wkv_prompt",
    )(mixed, *consts)


def _wkv_sample_kernel(n_t, r_ref, k_ref, v_ref, lora_ref,
                       w0_ref, w2_ref, a0_ref, a2_ref, g2_ref, kk_ref, ka_ref,
                       rk_ref, lnw_ref, lnb_ref, bd2_ref, s_ref,
                       ya_o, s_o, yt_scr):
    n = HEAD_A
    n_b = r_ref.shape[1]
    tall = lambda xs: jnp.concatenate(xs, axis=0)
    rows_of = lambda ref: tall([ref[t] for t in range(n_t)])
    head_sums = functools.partial(_sums_stacked, bd2=bd2_ref[...])
    r, v, lora = rows_of(r_ref), rows_of(v_ref), rows_of(lora_ref)
    lw, a, gate = _wkv_decay_gate(lora[:, 0:LORA_W + LORA_A], lora[:, LORA_W + LORA_A:],
                                  w0_ref[...], w2_ref[...], a0_ref[...], a2_ref[...], g2_ref[...])
    k, kk, kka = _wkv_keys(rows_of(k_ref), a, kk_ref[...], ka_ref[...], head_sums)

    tr = lambda x: [x[t * n_b:(t + 1) * n_b].T for t in range(n_t)]
    nkk_t, kka_t, k_t, r_t, v_t, w_t = tr(-kk), tr(kka), tr(k), tr(r), tr(v), tr(jnp.exp(lw))
    rid = lax.broadcasted_iota(jnp.int32, (SUBLANES, n_b), 0)

    for hh in range(2):
        keys = slice(hh * n, (hh + 1) * n)
        for ig in range(n // SUBLANES):
            y_tiles = [jnp.zeros((SUBLANES, n_b), F32) for _ in range(n_t)]
            for ii in range(SUBLANES):
                i = ig * SUBLANES + ii
                s = s_ref[hh, i]
                for t in range(n_t):
                    sa = jnp.sum(s * nkk_t[t][keys], axis=0, keepdims=True)
                    v_row = v_t[t][hh * n + i:hh * n + i + 1]
                    s = s * w_t[t][keys] + sa * kka_t[t][keys] + v_row * k_t[t][keys]
                    y_row = jnp.sum(s * r_t[t][keys], axis=0, keepdims=True)
                    y_tiles[t] = jnp.where(rid == ii, y_row, y_tiles[t])
                s_o[hh, i] = s
            for t in range(n_t):
                yt_scr[t, hh * n + ig * SUBLANES:hh * n + (ig + 1) * SUBLANES, :] = y_tiles[t]

    out = _wkv_post(tall([yt_scr[t].T for t in range(n_t)]), r, k, v, gate,
                    rk_ref[...], lnw_ref[...], lnb_ref[...], head_sums)
    for t in range(n_t):
        ya_o[t] = out[t * n_b:(t + 1) * n_b].astype(ya_o.dtype)


def _wkv_sample(mixed, w0, w2p, a0, a2p, g2, k_k, k_a, rk, lnw, lnb, bd2, s0):
    n_t, n_b, _ = mixed.shape
    slab = lambda first: pl.BlockSpec((n_t, n_b, PAIR), lambda p: (0, 0, first + p))
    lora = pl.BlockSpec((n_t, n_b, 2 * PAIR), lambda p: (0, 0, 3 * D_A // (2 * PAIR)))
    par = pl.BlockSpec((1, PAIR), lambda p: (0, p))
    low = pl.BlockSpec((PAIR, PAIR), lambda p: (0, p))
    st = pl.BlockSpec((2, HEAD_A, HEAD_A, n_b), lambda p: (p, 0, 0, 0))
    n_slab = D_A // PAIR
    return pl.pallas_call(
        functools.partial(_wkv_sample_kernel, n_t),
        grid=(H_A // 2,),
        in_specs=[slab(0), slab(n_slab), slab(2 * n_slab), lora,
                  par, low, par, low, low, par, par, par, par, par, _resident(bd2.shape), st],
        out_specs=[slab(0), st],
        out_shape=[jax.ShapeDtypeStruct((n_t, n_b, D_A), F32),
                   jax.ShapeDtypeStruct(s0.shape, F32)],
        scratch_shapes=[pltpu.VMEM((n_t, PAIR, n_b), F32)],
        compiler_params=_params(1),
        name="wkv_sample",
    )(mixed, mixed, mixed, mixed, w0, w2p, a0, a2p, g2, k_k, k_a, rk, lnw, lnb, bd2, s0)


def _ret_chunk(q, k, v, g, s, heads, dm_ref, qd_ref, kd_ref, cd_ref, gn_ref):
    n = range(len(q))
    qb = [x.astype(BF16) for x in q]
    kb = [x.astype(BF16) for x in k]
    vb = [x.astype(BF16) for x in v]
    inner = [(_nt(qb[u], kb[u]) * dm_ref[heads[u]]).astype(BF16) for u in n]
    q_dec = [(q[u].astype(F32) * qd_ref[heads[u]]).astype(BF16) for u in n]
    k_dec = [(k[u].astype(F32) * kd_ref[heads[u]]).astype(BF16) for u in n]
    if inner[0].shape[1] % LANES == 0:
        y = [_mm(jnp.concatenate([inner[u], q_dec[u]], axis=1),
                 jnp.concatenate([vb[u], s[u].astype(BF16)], axis=0)) for u in n]
    else:
        y = [_mm(inner[u], vb[u]) + _mm(q_dec[u], s[u].astype(BF16)) for u in n]
    s_new = [s[u] * cd_ref[heads[u]] + _tn(k_dec[u], vb[u]) for u in n]
    out = []
    for u in n:
        mu = jnp.mean(y[u], axis=-1, keepdims=True)
        d = y[u] - mu
        var = jnp.mean(d * d, axis=-1, keepdims=True)
        lanes = slice(heads[u] * HEAD_R, (heads[u] + 1) * HEAD_R)
        yn = d * lax.rsqrt(var + GN_EPS_R) * gn_ref[:, lanes]
        out.append(g[u] * _sigmoid(g[u]) * yn)
    return out, s_new


def _ret_sample_kernel(n_t, bb, q_ref, k_ref, v_ref, g_ref, dm_ref, qd_ref, kd_ref, cd_ref,
                       gn_ref, s_ref, y_o, s_o):
    rid = lax.broadcasted_iota(jnp.int32, (SUBLANES, HEAD_R), 0)
    units = [(bi, hh) for bi in range(bb) for hh in range(H_R)]
    heads = [hh for _, hh in units]
    lanes = [slice(hh * HEAD_R, (hh + 1) * HEAD_R) for hh in heads]

    def seq_rows(ref):
        outs = []
        for u, (bi, _) in enumerate(units):
            out = jnp.zeros((SUBLANES, HEAD_R), F32)
            for t in range(n_t):
                out = jnp.where(rid == t, jnp.broadcast_to(ref[t, bi:bi + 1, lanes[u]], out.shape), out)
            outs.append(out)
        return outs

    y, s_new = _ret_chunk(seq_rows(q_ref), seq_rows(k_ref), seq_rows(v_ref), seq_rows(g_ref),
                          [s_ref[bi, hh] for bi, hh in units], heads,
                          dm_ref, qd_ref, kd_ref, cd_ref, gn_ref)
    for u, (bi, hh) in enumerate(units):
        s_o[bi, hh] = s_new[u]
        for t in range(n_t):
            y_o[t, bi:bi + 1, lanes[u]] = y[u][t:t + 1].astype(y_o.dtype)


def _ret_sample(q, k, v, g, dm, qd, kd, cd, gn, s0, *, bb):
    n_t, n_b, _ = q.shape
    consts = (dm, qd, kd, cd, gn)
    blk = pl.BlockSpec((n_t, bb, D_R), lambda i: (0, i, 0))
    st = pl.BlockSpec((bb, H_R, HEAD_R, HEAD_R), lambda i: (i, 0, 0, 0))
    return pl.pallas_call(
        functools.partial(_ret_sample_kernel, n_t, bb),
        grid=(n_b // bb,),
        in_specs=[blk] * 4 + [_resident(x.shape) for x in consts] + [st],
        out_specs=[blk, st],
        out_shape=[jax.ShapeDtypeStruct((n_t, n_b, D_R), F32),
                   jax.ShapeDtypeStruct(s0.shape, F32)],
        compiler_params=_params(1),
        name="ret_sample",
    )(q, k, v, g, *consts, s0)


def _rope_tables(pos):
    half = HEAD_R // 2
    inv = ROPE_BASE ** (-jnp.arange(half, dtype=F32) / half)
    ang = pos.astype(F32)[:, None] * inv[None, :]
    cos, sin = jnp.cos(ang), jnp.sin(ang)
    return jnp.concatenate([cos, cos], axis=1), jnp.concatenate([-sin, sin], axis=1)


def _ret_tables(c):
    lg = jnp.log1p(-jnp.exp2(-5.0 - jnp.arange(H_R, dtype=F32)))
    idx = jnp.arange(c, dtype=F32)
    diff = idx[:, None] - idx[None, :]
    dmask = jnp.where(diff >= 0, jnp.exp(lg[:, None, None] * jnp.maximum(diff, 0.0)), 0.0)
    ones = jnp.ones((1, 1, HEAD_R), F32)
    qdec = jnp.exp(lg[:, None] * (idx + 1.0))[:, :, None] * ones
    kdec = jnp.exp(lg[:, None] * (c - 1.0 - idx))[:, :, None] * ones
    cdec = jnp.exp(lg * c)[:, None, None] * ones
    extra = -c % SUBLANES
    dmask = jnp.pad(dmask, ((0, 0), (0, extra), (0, extra)))
    qdec = jnp.pad(qdec, ((0, 0), (0, extra), (0, 0)))
    kdec = jnp.pad(kdec, ((0, 0), (0, extra), (0, 0)))
    return dmask, qdec, kdec, cdec


def _block_ones(n, block):
    idx = jnp.arange(n) // block
    return (idx[:, None] == idx[None, :]).astype(BF16)


def kernel(x_prompt, x_sample, state_shift, state_wkv, state_ret, norm_g, ffn1_wg, ffn1_wu, ffn1_wd,
           w_in, mu_shift, w0, w2, a0, a2, g2, k_k, k_a, r_k, lnx_w, lnx_b, ret_gn_w, w_out,
           ffn2_wg, ffn2_wu, ffn2_wd):
    assert norm_g.shape[0] == 1, "single-layer configuration"
    bp, tp, _ = x_prompt.shape
    bs, ts, _ = x_sample.shape
    l = 0
    ng = norm_g[l]
    row = lambda t: t[l].reshape(1, -1)
    zpad = jnp.zeros((LORA_W, D_A), BF16)
    w2p = jnp.concatenate([w2[l].astype(BF16), zpad], axis=0)
    a2p = jnp.concatenate([zpad, a2[l].astype(BF16)], axis=0)
    rk, lnw, lnb, gn = row(r_k), row(lnx_w), row(lnx_b), row(ret_gn_w)
    wkv_params = (row(w0), w2p, row(a0), a2p, g2[l].astype(BF16), row(k_k), row(k_a), rk, lnw, lnb)
    bd_pair = _block_ones(PAIR, HEAD_A)
    bd2 = jnp.concatenate([bd_pair, bd_pair], axis=0)
    tri = (jnp.arange(WKV_CHUNK)[:, None] >= jnp.arange(WKV_CHUNK)[None, :]).astype(BF16)
    tri2 = jnp.concatenate([tri, tri], axis=1)

    m_s = bs * ts
    x1s, *f1 = _ffn_stream(x_sample, ng, ffn1_wg[l], ffn1_wu[l], ffn1_wd[l], 0, 1)

    xp = x_prompt.reshape(bp * tp, D_MODEL)
    x1p, *f2, win, wo = _ffn(xp, ng, *f1, 0, 1, 1024,
                             cast=(ffn2_wg[l], ffn2_wu[l], ffn2_wd[l], w_in[l], w_out[l]))
    proj_consts = (ng, win, row(mu_shift))
    cos_p, sin_p = _rope_tables(jnp.arange(tp, dtype=jnp.int32))
    tm_p = 512
    (mixed, yr_p, ret_p, hl_p) = _proj(
        x1p, None, *proj_consts, cos_p, sin_p, n_t=1, rows_per_t=tm_p, lag=1,
        tiles_per_seq=tp // tm_p, qkv_dtype=BF16, ret_tables=(*_ret_tables(RET_CHUNK), gn))
    seq3 = lambda t: t.reshape(bp, tp, -1)
    ya_p, wkv_p = _wkv_prompt(seq3(mixed), (*wkv_params, bd2, tri2), nb=4, tt=256)
    ya_p = ya_p.reshape(bp * tp, D_A)
    yp = _ffn(x1p, ng, *f2, 4, 5, 1024, mix=(ya_p, yr_p, wo))

    cos_s, sin_s = _rope_tables(PAST_LEN + jnp.arange(ts, dtype=jnp.int32))
    rows_per_t = bs
    tab = lambda t: jnp.broadcast_to(t[:, None, :], (ts, bs, HEAD_R))
    outs = _proj(x1s.reshape(ts, bs, D_MODEL),
                 state_shift[l].reshape(bs // rows_per_t, rows_per_t, D_MODEL),
                 *proj_consts, tab(cos_s), tab(sin_s), n_t=ts, rows_per_t=rows_per_t,
                 lag=rows_per_t, tiles_per_seq=1, qkv_dtype=F32)
    (mixed, q, kr, vr, gr, hl_s) = outs
    ya_s, wkv_s = _wkv_sample(mixed, *wkv_params, bd2, state_wkv[l].transpose(1, 2, 3, 0))
    wkv_s = wkv_s.transpose(3, 0, 1, 2)
    yr_s, ret_s = _ret_sample(q, kr, vr, gr, *_ret_tables(min(RET_CHUNK, ts)), gn, state_ret[l],
                              bb=2 * SUBLANES)
    ys = _ffn(x1s, ng, *f2, 4, 5, m_s,
              mix=(ya_s.reshape(m_s, D_A), yr_s.reshape(m_s, D_R), wo), out_seqs=bs)

    return (yp.reshape(bp, tp, D_MODEL), ys,
            hl_p.reshape(1, bp, D_MODEL), wkv_p[None], ret_p[None],
            hl_s.reshape(1, bs, D_MODEL), wkv_s[None], ret_s[None])
```

```python
import functools

import jax
import jax.numpy as jnp
import numpy as np
from jax import lax
from jax.experimental import pallas as pl
from jax.experimental.pallas import tpu as pltpu

F32 = jnp.float32
BF16 = jnp.bfloat16

D_MODEL = 1024
D_A = 512
HEAD_A = 64
H_A = D_A // HEAD_A
D_R = 512
H_R = 4
HEAD_R = D_R // H_R
LORA_W, LORA_A, LORA_G = 64, 64, 128
D_FF = 2816
RET_CHUNK = 128
ROPE_BASE = 10000.0
EPS = 1e-6
GN_EPS_A = 64e-5
GN_EPS_R = 1e-5
N_SHIFT = 3 * D_A + LORA_W + LORA_A + LORA_G
PAST_LEN = 16384

LANES = 128
SUBLANES = 8
VMEM_LIMIT = 52 * 1024 * 1024

MXU_DIM = 256
FF_SPLIT = 6 * MXU_DIM
WKV_CHUNK = 64
PAIR = 2 * HEAD_A

FFN_ROWS = 1024
PROJ_ROWS = 4 * RET_CHUNK
WKV_SEQS, WKV_ROWS = 4, 256
RET_SEQS = 2 * SUBLANES


def _nt(a, b):
    return lax.dot_general(a, b, (((1,), (1,)), ((), ())), preferred_element_type=F32)


def _tn(a, b):
    return lax.dot_general(a, b, (((0,), (0,)), ((), ())), preferred_element_type=F32)


def _mm(a, b):
    return jnp.dot(a, b, preferred_element_type=F32)


def _split_hi_lo(x):
    hi = x.astype(BF16)
    lo = (x - hi.astype(F32)).astype(BF16)
    return hi, lo


def _rms(x, g):
    return x * lax.rsqrt(jnp.mean(x * x, axis=-1, keepdims=True) + EPS) * g


def _softplus(x):
    return jnp.maximum(x, 0.0) + jnp.log(1.0 + jnp.exp(-jnp.abs(x)))


def _sigmoid(x):
    return 1.0 / (1.0 + jnp.exp(-x))


def _resident(shape):
    nd = len(shape)
    return pl.BlockSpec(shape, lambda *_: (0,) * nd, pipeline_mode=pl.Buffered(1))


def _params(n_axes):
    return pltpu.CompilerParams(dimension_semantics=("arbitrary",) * n_axes,
                                vmem_limit_bytes=VMEM_LIMIT)


def _ffn_kernel(with_mix, n_cast, g_in, g_out, *refs):
    n_in = (8 if with_mix else 5) + n_cast
    ins, outs = refs[:n_in], refs[n_in:]
    if with_mix:
        x_ref, ya_ref, yr_ref, wo_ref, ng_ref, wg_ref, wu_ref, wd_ref = ins[:8]
    else:
        x_ref, ng_ref, wg_ref, wu_ref, wd_ref = ins[:5]
    o_ref = outs[0]
    for src, dst in zip(ins[n_in - n_cast:], outs[1:]):
        dst[...] = src[...].astype(dst.dtype)
    half = x_ref.shape[0] // 2
    halves = [slice(0, half), slice(half, 2 * half)]
    x = [x_ref[r, :] for r in halves]
    if with_mix:
        mix = [_mm(ya_ref[r, :].astype(BF16), wo_ref[0:D_A, :])
               + _mm(yr_ref[r, :].astype(BF16), wo_ref[D_A:, :]) for r in halves]
        x = [xi + _rms(m, ng_ref[3:4, :]) for xi, m in zip(x, mix)]
    h = [_rms(xi, ng_ref[g_in:g_in + 1, :]).astype(BF16) for xi in x]
    for i, r in enumerate(halves):
        acc = None
        for cols in (slice(0, FF_SPLIT), slice(FF_SPLIT, D_FF)):
            gate = _mm(h[i], wg_ref[:, cols])
            up = _mm(h[i], wu_ref[:, cols])
            act = (gate * _sigmoid(gate) * up).astype(BF16)
            part = _mm(act, wd_ref[cols, :])
            acc = part if acc is None else acc + part
        out = x[i] + 0.5 * _rms(acc, ng_ref[g_out:g_out + 1, :])
        if o_ref.ndim == 2:
            o_ref[r, :] = out
        else:
            n_seq = o_ref.shape[0]
            for j in range(half // n_seq):
                o_ref[:, i * (half // n_seq) + j, :] = out[j * n_seq:(j + 1) * n_seq]


BF16_ROWS = 2 * SUBLANES


def _cast_spec(rows, cols, steps):
    rep = 1
    while (rows * rep) % steps or (rows * rep // steps) % BF16_ROWS:
        rep *= 2
    return pl.BlockSpec((rows * rep // steps, cols), lambda i: (i // rep, 0))


def _ffn(x, ng, wg, wu, wd, g_in, g_out, tm, mix=None, cast=(), out_seqs=None):
    m = x.shape[0]
    steps = m // tm
    assert out_seqs is None or (steps == 1 and (m // 2) % out_seqs == 0)
    o_shape = (m, D_MODEL) if out_seqs is None else (out_seqs, m // out_seqs, D_MODEL)
    row = lambda w: pl.BlockSpec((tm, w), lambda i: (i, 0))
    if mix is None:
        args = (x, ng, wg, wu, wd)
        specs = [row(D_MODEL), _resident(ng.shape), _resident(wg.shape), _resident(wu.shape),
                 _resident(wd.shape)]
    else:
        ya, yr, wo = mix
        args = (x, ya, yr, wo, ng, wg, wu, wd)
        specs = [row(D_MODEL), row(D_A), row(D_R), _resident(wo.shape), _resident(ng.shape),
                 _resident(wg.shape), _resident(wu.shape), _resident(wd.shape)]
    cast_specs = [_cast_spec(*w.shape, steps) for w in cast]
    out = pl.pallas_call(
        functools.partial(_ffn_kernel, mix is not None, len(cast), g_in, g_out),
        grid=(steps,),
        in_specs=specs + cast_specs,
        out_specs=[row(D_MODEL) if out_seqs is None else pl.BlockSpec(o_shape, lambda i: (0, 0, 0))]
        + cast_specs,
        out_shape=[jax.ShapeDtypeStruct(o_shape, F32)]
        + [jax.ShapeDtypeStruct(w.shape, BF16) for w in cast],
        compiler_params=_params(1),
        name="ffn_mix" if mix is not None else "ffn",
    )(*args, *cast)
    return out if cast else out[0]


def _ffn_stream_kernel(g_in, g_out, x_ref, ng_ref, wg_ref, wu_ref, wd_ref,
                       o_ref, wg_o, wu_o, wd_o, h_scr, acc_scr):
    c = pl.program_id(0)
    x_rows = lambda: jnp.concatenate([x_ref[:, t, :] for t in range(x_ref.shape[1])], axis=0)

    @pl.when(c == 0)
    def _():
        h_scr[...] = _rms(x_rows(), ng_ref[g_in:g_in + 1, :]).astype(BF16)
        acc_scr[...] = jnp.zeros_like(acc_scr)

    wg, wu, wd = wg_ref[...].astype(BF16), wu_ref[...].astype(BF16), wd_ref[...].astype(BF16)
    wg_o[...] = wg
    wu_o[...] = wu
    wd_o[...] = wd
    h = h_scr[...]
    gate = _mm(h, wg)
    act = (gate * _sigmoid(gate) * _mm(h, wu)).astype(BF16)
    acc_scr[...] += _mm(act, wd)

    @pl.when(c == pl.num_programs(0) - 1)
    def _():
        o_ref[...] = x_rows() + 0.5 * _rms(acc_scr[...], ng_ref[g_out:g_out + 1, :])


def _ffn_stream(x, ng, wg, wu, wd, g_in, g_out):
    m = x.shape[0] * x.shape[1]
    slab = MXU_DIM
    whole = lambda a: pl.BlockSpec(a.shape, lambda c: (0,) * a.ndim)
    cols = pl.BlockSpec((D_MODEL, slab), lambda c: (0, c))
    rows = pl.BlockSpec((slab, D_MODEL), lambda c: (c, 0))
    return pl.pallas_call(
        functools.partial(_ffn_stream_kernel, g_in, g_out),
        grid=(D_FF // slab,),
        in_specs=[whole(x), whole(ng), cols, cols, rows],
        out_specs=[pl.BlockSpec((m, D_MODEL), lambda c: (0, 0)), cols, cols, rows],
        out_shape=[jax.ShapeDtypeStruct((m, D_MODEL), F32)]
        + [jax.ShapeDtypeStruct(w.shape, BF16) for w in (wg, wu, wd)],
        scratch_shapes=[pltpu.VMEM((m, D_MODEL), BF16), pltpu.VMEM((m, D_MODEL), F32)],
        compiler_params=_params(1),
        name="ffn_stream",
    )(x, ng, wg, wu, wd)


def _lag_rows(cur, first, lag):
    if lag % SUBLANES == 0:
        return jnp.concatenate([first, cur[:-lag]], axis=0)
    assert lag == 1
    rolled = pltpu.roll(cur, 1, 0)
    rid = lax.broadcasted_iota(jnp.int32, (SUBLANES, cur.shape[1]), 0)
    head = jnp.where(rid == 0, first, rolled[0:SUBLANES])
    return jnp.concatenate([head, rolled[SUBLANES:]], axis=0)


def _proj_kernel(n_t, lag, tiles_per_seq, has_prev, fuse_ret, *refs):
    it = iter(refs)
    x_ref = next(it)
    prev_ref = next(it) if has_prev else None
    ng_ref, win_ref, mu_ref = next(it), next(it), next(it)
    if fuse_ret:
        dm_ref, qd_ref, kd_ref, cd_ref, gn_ref = (next(it) for _ in range(5))
        cos_ref, sin_ref, mix_o, yr_o, s_o, hl_o, carry_scr, s_scr = tuple(it)
    else:
        cos_ref, sin_ref, mix_o, q_o, kr_o, vr_o, gr_o, hl_o, carry_scr = tuple(it)

    def load(ref):
        if n_t == 1:
            return ref[...]
        return jnp.concatenate([ref[t] for t in range(n_t)], axis=0)

    def store(ref, val, cols=slice(None)):
        if n_t == 1:
            ref[:, cols] = val.astype(ref.dtype)
        else:
            rows = val.shape[0] // n_t
            for t in range(n_t):
                ref[t, :, cols] = val[t * rows:(t + 1) * rows].astype(ref.dtype)

    x = load(x_ref)
    tm = x.shape[0]
    h = _rms(x, ng_ref[2:3, :])
    hl_o[0] = h[tm - lag:, :]
    hb = h.astype(BF16)
    seq_start = (pl.program_id(0) % tiles_per_seq) == 0
    prev_b = prev_ref[0].astype(BF16) if has_prev else None

    ret = lambda c: _mm(hb, win_ref[:, N_SHIFT + c * D_R:N_SHIFT + (c + 1) * D_R])
    cos2 = load(cos_ref)
    sin2 = load(sin_ref)

    def rope(t):
        parts = []
        for hh in range(H_R):
            th = t[:, hh * HEAD_R:(hh + 1) * HEAD_R]
            parts.append(th * cos2 + pltpu.roll(th, HEAD_R // 2, 1) * sin2)
        return jnp.concatenate(parts, axis=1)

    if fuse_ret:
        rq = rope(ret(0)).astype(BF16)
        rk = (rope(ret(1)) * (HEAD_R ** -0.5)).astype(BF16)
        rv = ret(2).astype(BF16)
        rg = ret(3)
        heads = list(range(H_R))
        lanes = [slice(hh * HEAD_R, (hh + 1) * HEAD_R) for hh in heads]
        r_state = [jnp.where(seq_start, 0.0, s_scr[hh]) for hh in heads]

    for slab, c0 in enumerate(range(0, N_SHIFT, D_A)):
        cols = slice(c0, min(c0 + D_A, N_SHIFT))
        cur = _mm(hb, win_ref[:, cols])
        if has_prev:
            first = _mm(prev_b, win_ref[:, cols])
        else:
            first = jnp.zeros((lag, cur.shape[1]), F32)
        if tiles_per_seq > 1:
            first = jnp.where(seq_start, first, carry_scr[0:lag, cols])
            carry_scr[0:lag, cols] = cur[tm - lag:, :]
        prv = _lag_rows(cur, first, lag)
        store(mix_o, cur + (prv - cur) * mu_ref[:, cols], cols)
        if fuse_ret:
            rows = slice(slab * RET_CHUNK, (slab + 1) * RET_CHUNK)
            pick = lambda t: [t[rows, ln] for ln in lanes]
            y, r_state = _ret_chunk(pick(rq), pick(rk), pick(rv), pick(rg), r_state, heads,
                                    dm_ref, qd_ref, kd_ref, cd_ref, gn_ref)
            for hh in heads:
                yr_o[rows, lanes[hh]] = y[hh].astype(yr_o.dtype)

    if fuse_ret:
        for hh in heads:
            s_scr[hh] = r_state[hh]
            s_o[0, hh] = r_state[hh]
        return

    q = ret(0)
    kr = ret(1)
    store(q_o, rope(q))
    vr = ret(2)
    store(kr_o, rope(kr) * (HEAD_R ** -0.5))
    gr = ret(3)
    store(vr_o, vr)
    store(gr_o, gr)


def _proj(x, prev, ng, win, mu, cos2, sin2, *, n_t, rows_per_t, lag, tiles_per_seq,
          ret_tables=None):
    tm = n_t * rows_per_t
    fuse_ret = ret_tables is not None
    assert not fuse_ret or (n_t == 1 and tm == pl.cdiv(N_SHIFT, D_A) * RET_CHUNK)
    if n_t == 1:
        m = x.shape[0]
        n_tiles = m // tm
        row = lambda w: pl.BlockSpec((tm, w), lambda i: (i, 0))
        shp = lambda w, dt: jax.ShapeDtypeStruct((m, w), dt)
        tab = pl.BlockSpec((tm, HEAD_R), lambda i: (i % tiles_per_seq, 0))
    else:
        m = x.shape[0] * x.shape[1]
        n_tiles = x.shape[1] // rows_per_t
        row = lambda w: pl.BlockSpec((n_t, rows_per_t, w), lambda i: (0, i, 0))
        shp = lambda w, dt: jax.ShapeDtypeStruct((n_t, m // n_t, w), dt)
        tab = pl.BlockSpec((n_t, rows_per_t, HEAD_R), lambda i: (0, i, 0))
    n_seq = n_tiles // tiles_per_seq
    hl_spec = pl.BlockSpec((1, lag, D_MODEL), lambda i: (i // tiles_per_seq, 0, 0))
    args = [x]
    specs = [row(D_MODEL)]
    if prev is not None:
        args.append(prev)
        specs.append(pl.BlockSpec((1, lag, D_MODEL), lambda i: (i // tiles_per_seq, 0, 0)))
    consts = (ng, win, mu) + (tuple(ret_tables) if fuse_ret else ())
    args += list(consts) + [cos2, sin2]
    specs += [_resident(c.shape) for c in consts] + [tab, tab]
    hl_shape = jax.ShapeDtypeStruct((n_seq, lag, D_MODEL), F32)
    scratch = [pltpu.VMEM((max(lag, SUBLANES), N_SHIFT), F32)]
    if fuse_ret:
        out_shape = [shp(N_SHIFT, F32), shp(D_R, BF16),
                     jax.ShapeDtypeStruct((n_seq, H_R, HEAD_R, HEAD_R), F32), hl_shape]
        out_specs = [row(N_SHIFT), row(D_R),
                     pl.BlockSpec((1, H_R, HEAD_R, HEAD_R), lambda i: (i // tiles_per_seq, 0, 0, 0)),
                     hl_spec]
        scratch.append(pltpu.VMEM((H_R, HEAD_R, HEAD_R), F32))
    else:
        out_shape = [shp(N_SHIFT, F32)] + [shp(D_R, F32)] * 4 + [hl_shape]
        out_specs = [row(N_SHIFT)] + [row(D_R)] * 4 + [hl_spec]
    return pl.pallas_call(
        functools.partial(_proj_kernel, n_t, lag, tiles_per_seq, prev is not None, fuse_ret),
        grid=(n_tiles,),
        in_specs=specs,
        out_specs=out_specs,
        out_shape=out_shape,
        scratch_shapes=scratch,
        compiler_params=_params(1),
        name="proj",
    )(*args)


def _sums_stacked(x, bd2):
    return _mm(jnp.concatenate(_split_hi_lo(x), axis=1), bd2)


def _wkv_post(y, r, k, v, g, rk, lw_g, lb_g, head_sums):
    inv_n = 1.0 / HEAD_A
    mu = head_sums(y) * inv_n
    d = y - mu
    var = head_sums(d * d) * inv_n
    yn = d * lax.rsqrt(var + GN_EPS_A) * lw_g + lb_g
    bonus = head_sums(r * k * rk) * v
    return (yn + bonus) * g


def _wkv_decay_gate(wa, gd, w0, w2p, a0, a2p, g2):
    w_pre = w0 + _mm(jnp.tanh(wa).astype(BF16), w2p)
    lw = -jnp.exp(-_softplus(-w_pre) - 0.5)
    a = _sigmoid(a0 + _mm(wa.astype(BF16), a2p))
    g = _mm(_sigmoid(gd).astype(BF16), g2)
    return lw, a, g


def _wkv_keys(k, a, k_k, k_a, head_sums):
    kk = k * k_k
    kk = kk / jnp.maximum(jnp.sqrt(head_sums(kk * kk)), 1e-12)
    return k * (1.0 + (a - 1.0) * k_a), kk, kk * a


def _wkv_prompt_kernel(nb, tt, mix_ref, w0_ref, w2_ref, a0_ref, a2_ref, g2_ref, kk_ref, ka_ref,
                       rk_ref, lnw_ref, lnb_ref, bd2_ref, tri2_ref,
                       ya_o, s_o, s_scr):
    c = WKV_CHUNK
    j = pl.program_id(1)

    @pl.when(j == 0)
    def _():
        s_scr[...] = jnp.zeros_like(s_scr)

    lane = lax.broadcasted_iota(jnp.int32, (c, PAIR), 1)
    first = lane < HEAD_A

    def stack(x):
        return jnp.concatenate([jnp.where(first, x, 0.0), jnp.where(first, 0.0, x)], axis=0)

    ri = lax.broadcasted_iota(jnp.int32, (2 * c, 2 * c), 0) % c
    ci = lax.broadcasted_iota(jnp.int32, (2 * c, 2 * c), 1) % c
    strict = ri > ci
    incl = ri >= ci
    eye = (lax.broadcasted_iota(jnp.int32, (2 * c, 2 * c), 0)
           == lax.broadcasted_iota(jnp.int32, (2 * c, 2 * c), 1)).astype(F32)
    bd2 = bd2_ref[...]
    tri2 = tri2_ref[...]
    head_sums = functools.partial(_sums_stacked, bd2=bd2)

    n_pairs = H_A // 2
    units = [(bi, p) for bi in range(nb) for p in range(n_pairs)]
    n_u = range(len(units))
    slab = [slice(p * PAIR, (p + 1) * PAIR) for p in range(n_pairs)]

    tall = lambda xs: jnp.concatenate(xs, axis=0)
    per_pair = lambda ref: tall([jnp.broadcast_to(ref[:, slab[p]], (c, PAIR)) for _, p in units])
    split = lambda x: [x[u * c:(u + 1) * c] for u in n_u]

    def prep(ch):
        rows = slice(ch * c, (ch + 1) * c)
        ld = lambda c0: [mix_ref[bi, rows, c0 + p * PAIR:c0 + (p + 1) * PAIR] for bi, p in units]
        r, k_raw, v = ld(0), ld(D_A), ld(2 * D_A)
        lora = [_wkv_decay_gate(mix_ref[bi, rows, 3 * D_A:3 * D_A + LORA_W + LORA_A],
                                mix_ref[bi, rows, 3 * D_A + LORA_W + LORA_A:N_SHIFT],
                                w0_ref[...], w2_ref[...], a0_ref[...], a2_ref[...], g2_ref[...])
                for bi in range(nb)]
        lw_all = [x[0] for x in lora]
        a = tall([lora[bi][1][:, slab[p]] for bi, p in units])
        gate = tall([lora[bi][2][:, slab[p]] for bi, p in units])
        k, kk, kka = map(split, _wkv_keys(tall(k_raw), a, per_pair(kk_ref), per_pair(ka_ref),
                                          head_sums))
        return r, v, k, kk, kka, gate, lw_all

    def gram_stage(pre):
        r, v, k, kk, kka, gate, lw_all = pre
        cum_all = [_mm(tri2, jnp.concatenate(_split_hi_lo(x), axis=0)) for x in lw_all]
        lw = [lw_all[bi][:, slab[p]] for bi, p in units]
        cum = [cum_all[bi][:, slab[p]] for bi, p in units]
        cum_end = [x[c - 1:c, :] for x in cum]
        e_pos = [jnp.exp(x) for x in cum]
        e_neg = [jnp.exp(-x) for x in cum]
        e_end = [jnp.exp(x - y) for x, y in zip(cum_end, cum)]
        xs = [jnp.concatenate([stack(-kk[u] * jnp.exp(cum[u] - lw[u])), stack(r[u] * e_pos[u])],
                              axis=0).astype(BF16) for u in n_u]
        ws = [jnp.concatenate([stack(kka[u] * e_neg[u]), stack(k[u] * e_neg[u])],
                              axis=0).astype(BF16) for u in n_u]
        we = [jnp.concatenate([stack(kka[u] * e_end[u]), stack(k[u] * e_end[u])],
                              axis=0).astype(BF16) for u in n_u]
        vs = [stack(x).astype(BF16) for x in v]
        gram = [_nt(xs[u], ws[u]) for u in n_u]
        a_ab = [jnp.where(strict, g[0:2 * c, 0:2 * c], 0.0) for g in gram]
        a_ak = [jnp.where(strict, g[0:2 * c, 2 * c:], 0.0).astype(BF16) for g in gram]
        a_r = [jnp.concatenate([jnp.where(incl, g[2 * c:, 0:2 * c], 0.0),
                                jnp.where(incl, g[2 * c:, 2 * c:], 0.0)], axis=1).astype(BF16)
               for g in gram]
        return xs, we, vs, cum_end, a_ab, a_ak, a_r

    def solve_stage(ch, pre, gs):
        rows = slice(ch * c, (ch + 1) * c)
        r, v, k, kk, kka, gate, lw_all = pre
        xs, we, vs, cum_end, a_ab, a_ak, a_r = gs
        inv = [eye + a for a in a_ab]
        pw = [x.astype(BF16) for x in a_ab]
        pw = [_mm(x, x).astype(BF16) for x in pw]
        n_lvl = c.bit_length() - 2
        for lvl in range(n_lvl):
            if lvl < n_lvl - 1:
                both = [_mm(x, jnp.concatenate([x, i.astype(BF16)], axis=1)) for i, x in zip(inv, pw)]
                inv = [i + b[:, 2 * c:] for i, b in zip(inv, both)]
                pw = [b[:, 0:2 * c].astype(BF16) for b in both]
            else:
                inv = [i + _mm(x, i.astype(BF16)) for i, x in zip(inv, pw)]
        s = [s_scr[u] for u in n_u]
        z = [_nt(xs[u], s[u].astype(BF16)) for u in n_u]
        rhs = [z[u][0:2 * c] + _mm(a_ak[u], vs[u]) for u in n_u]
        uu = [_mm(inv[u].astype(BF16), rhs[u].astype(BF16)).astype(BF16) for u in n_u]
        uv = [jnp.concatenate([uu[u], vs[u]], axis=0) for u in n_u]
        y2 = [z[u][2 * c:] + _mm(a_r[u], uv[u]) for u in n_u]
        for u in n_u:
            s_scr[u] = s[u] * jnp.exp(cum_end[u]) + _tn(uv[u], we[u])
        out = _wkv_post(tall([y2[u][0:c] + y2[u][c:] for u in n_u]), tall(r), tall(k), tall(v),
                        gate, per_pair(rk_ref), per_pair(lnw_ref), per_pair(lnb_ref), head_sums)
        for u, (bi, p) in enumerate(units):
            ya_o[bi, rows, slab[p]] = out[u * c:(u + 1) * c].astype(ya_o.dtype)

    n_chunks = tt // c
    pre = prep(0)
    for ch in range(n_chunks):
        gs = gram_stage(pre)
        nxt = prep(ch + 1) if ch + 1 < n_chunks else None
        solve_stage(ch, pre, gs)
        pre = nxt

    @pl.when(j == pl.num_programs(1) - 1)
    def _():
        for u, (bi, p) in enumerate(units):
            s = s_scr[u]
            s_o[bi, 2 * p] = s[0:HEAD_A, 0:HEAD_A]
            s_o[bi, 2 * p + 1] = s[HEAD_A:, HEAD_A:]


def _wkv_prompt(mixed, consts, *, nb, tt):
    batch, seq, _ = mixed.shape
    blk = lambda w: pl.BlockSpec((nb, tt, w), lambda b, j: (b, j, 0))
    return pl.pallas_call(
        functools.partial(_wkv_prompt_kernel, nb, tt),
        grid=(batch // nb, seq // tt),
        in_specs=[blk(N_SHIFT)] + [_resident(x.shape) for x in consts],
        out_specs=[blk(D_A), pl.BlockSpec((nb, H_A, HEAD_A, HEAD_A), lambda b, j: (b, 0, 0, 0))],
        out_shape=[jax.ShapeDtypeStruct((batch, seq, D_A), BF16),
                   jax.ShapeDtypeStruct((batch, H_A, HEAD_A, HEAD_A), F32)],
        scratch_shapes=[pltpu.VMEM((nb * H_A // 2, PAIR, PAIR), F32)],
        compiler_params=_params(2),
        name="wkv_prompt",
    )(mixed, *consts)


def _wkv_sample_kernel(n_t, r_ref, k_ref, v_ref, lora_ref,
                       w0_ref, w2_ref, a0_ref, a2_ref, g2_ref, kk_ref, ka_ref,
                       rk_ref, lnw_ref, lnb_ref, bd2_ref, s_ref,
                       ya_o, s_o, yt_scr):
    n = HEAD_A
    n_b = r_ref.shape[1]
    tall = lambda xs: jnp.concatenate(xs, axis=0)
    rows_of = lambda ref: tall([ref[t] for t in range(n_t)])
    head_sums = functools.partial(_sums_stacked, bd2=bd2_ref[...])
    r, v, lora = rows_of(r_ref), rows_of(v_ref), rows_of(lora_ref)
    lw, a, gate = _wkv_decay_gate(lora[:, 0:LORA_W + LORA_A], lora[:, LORA_W + LORA_A:],
                                  w0_ref[...], w2_ref[...], a0_ref[...], a2_ref[...], g2_ref[...])
    k, kk, kka = _wkv_keys(rows_of(k_ref), a, kk_ref[...], ka_ref[...], head_sums)

    tr = lambda x: [x[t * n_b:(t + 1) * n_b].T for t in range(n_t)]
    nkk_t, kka_t, k_t, r_t, v_t, w_t = tr(-kk), tr(kka), tr(k), tr(r), tr(v), tr(jnp.exp(lw))
    rid = lax.broadcasted_iota(jnp.int32, (SUBLANES, n_b), 0)

    for hh in range(2):
        keys = slice(hh * n, (hh + 1) * n)
        for ig in range(n // SUBLANES):
            y_tiles = [jnp.zeros((SUBLANES, n_b), F32) for _ in range(n_t)]
            for ii in range(SUBLANES):
                i = ig * SUBLANES + ii
                s = s_ref[hh, i]
                for t in range(n_t):
                    sa = jnp.sum(s * nkk_t[t][keys], axis=0, keepdims=True)
                    v_row = v_t[t][hh * n + i:hh * n + i + 1]
                    s = s * w_t[t][keys] + sa * kka_t[t][keys] + v_row * k_t[t][keys]
                    y_row = jnp.sum(s * r_t[t][keys], axis=0, keepdims=True)
                    y_tiles[t] = jnp.where(rid == ii, y_row, y_tiles[t])
                s_o[hh, i] = s
            for t in range(n_t):
                yt_scr[t, hh * n + ig * SUBLANES:hh * n + (ig + 1) * SUBLANES, :] = y_tiles[t]

    out = _wkv_post(tall([yt_scr[t].T for t in range(n_t)]), r, k, v, gate,
                    rk_ref[...], lnw_ref[...], lnb_ref[...], head_sums)
    for t in range(n_t):
        ya_o[t] = out[t * n_b:(t + 1) * n_b].astype(ya_o.dtype)


def _wkv_sample(mixed, w0, w2p, a0, a2p, g2, k_k, k_a, rk, lnw, lnb, bd2, s0):
    n_t, n_b, _ = mixed.shape
    slab = lambda first: pl.BlockSpec((n_t, n_b, PAIR), lambda p: (0, 0, first + p))
    lora = pl.BlockSpec((n_t, n_b, 2 * PAIR), lambda p: (0, 0, 3 * D_A // (2 * PAIR)))
    par = pl.BlockSpec((1, PAIR), lambda p: (0, p))
    low = pl.BlockSpec((PAIR, PAIR), lambda p: (0, p))
    st = pl.BlockSpec((2, HEAD_A, HEAD_A, n_b), lambda p: (p, 0, 0, 0))
    n_slab = D_A // PAIR
    return pl.pallas_call(
        functools.partial(_wkv_sample_kernel, n_t),
        grid=(H_A // 2,),
        in_specs=[slab(0), slab(n_slab), slab(2 * n_slab), lora,
                  par, low, par, low, low, par, par, par, par, par, _resident(bd2.shape), st],
        out_specs=[slab(0), st],
        out_shape=[jax.ShapeDtypeStruct((n_t, n_b, D_A), F32),
                   jax.ShapeDtypeStruct(s0.shape, F32)],
        scratch_shapes=[pltpu.VMEM((n_t, PAIR, n_b), F32)],
        compiler_params=_params(1),
        name="wkv_sample",
    )(mixed, mixed, mixed, mixed, w0, w2p, a0, a2p, g2, k_k, k_a, rk, lnw, lnb, bd2, s0)


def _ret_chunk(q, k, v, g, s, heads, dm_ref, qd_ref, kd_ref, cd_ref, gn_ref):
    n = range(len(q))
    qb = [x.astype(BF16) for x in q]
    kb = [x.astype(BF16) for x in k]
    vb = [x.astype(BF16) for x in v]
    inner = [(_nt(qb[u], kb[u]) * dm_ref[heads[u]]).astype(BF16) for u in n]
    q_dec = [(q[u].astype(F32) * qd_ref[heads[u]]).astype(BF16) for u in n]
    k_dec = [(k[u].astype(F32) * kd_ref[heads[u]]).astype(BF16) for u in n]
    if inner[0].shape[1] % LANES == 0:
        y = [_mm(jnp.concatenate([inner[u], q_dec[u]], axis=1),
                 jnp.concatenate([vb[u], s[u].astype(BF16)], axis=0)) for u in n]
    else:
        y = [_mm(inner[u], vb[u]) + _mm(q_dec[u], s[u].astype(BF16)) for u in n]
    s_new = [s[u] * cd_ref[heads[u]] + _tn(k_dec[u], vb[u]) for u in n]
    out = []
    for u in n:
        mu = jnp.mean(y[u], axis=-1, keepdims=True)
        d = y[u] - mu
        var = jnp.mean(d * d, axis=-1, keepdims=True)
        lanes = slice(heads[u] * HEAD_R, (heads[u] + 1) * HEAD_R)
        yn = d * lax.rsqrt(var + GN_EPS_R) * gn_ref[:, lanes]
        out.append(g[u] * _sigmoid(g[u]) * yn)
    return out, s_new


def _ret_sample_kernel(n_t, bb, q_ref, k_ref, v_ref, g_ref, dm_ref, qd_ref, kd_ref, cd_ref,
                       gn_ref, s_ref, y_o, s_o):
    rid = lax.broadcasted_iota(jnp.int32, (SUBLANES, HEAD_R), 0)
    units = [(bi, hh) for bi in range(bb) for hh in range(H_R)]
    heads = [hh for _, hh in units]
    lanes = [slice(hh * HEAD_R, (hh + 1) * HEAD_R) for hh in heads]

    def seq_rows(ref):
        outs = []
        for u, (bi, _) in enumerate(units):
            out = jnp.zeros((SUBLANES, HEAD_R), F32)
            for t in range(n_t):
                out = jnp.where(rid == t, jnp.broadcast_to(ref[t, bi:bi + 1, lanes[u]], out.shape), out)
            outs.append(out)
        return outs

    y, s_new = _ret_chunk(seq_rows(q_ref), seq_rows(k_ref), seq_rows(v_ref), seq_rows(g_ref),
                          [s_ref[bi, hh] for bi, hh in units], heads,
                          dm_ref, qd_ref, kd_ref, cd_ref, gn_ref)
    for u, (bi, hh) in enumerate(units):
        s_o[bi, hh] = s_new[u]
        for t in range(n_t):
            y_o[t, bi:bi + 1, lanes[u]] = y[u][t:t + 1].astype(y_o.dtype)


def _ret_sample(q, k, v, g, dm, qd, kd, cd, gn, s0, *, bb):
    n_t, n_b, _ = q.shape
    consts = (dm, qd, kd, cd, gn)
    blk = pl.BlockSpec((n_t, bb, D_R), lambda i: (0, i, 0))
    st = pl.BlockSpec((bb, H_R, HEAD_R, HEAD_R), lambda i: (i, 0, 0, 0))
    return pl.pallas_call(
        functools.partial(_ret_sample_kernel, n_t, bb),
        grid=(n_b // bb,),
        in_specs=[blk] * 4 + [_resident(x.shape) for x in consts] + [st],
        out_specs=[blk, st],
        out_shape=[jax.ShapeDtypeStruct((n_t, n_b, D_R), F32),
                   jax.ShapeDtypeStruct(s0.shape, F32)],
        compiler_params=_params(1),
        name="ret_sample",
    )(q, k, v, g, *consts, s0)


def _rope_tables(pos):
    half = HEAD_R // 2
    inv = ROPE_BASE ** (-np.arange(half, dtype=np.float64) / half)
    ang = np.asarray(pos, np.float64)[:, None] * inv[None, :]
    cos, sin = np.cos(ang), np.sin(ang)
    return (np.concatenate([cos, cos], axis=1).astype(np.float32),
            np.concatenate([-sin, sin], axis=1).astype(np.float32))


def _ret_tables(c):
    lg = np.log1p(-np.exp2(-5.0 - np.arange(H_R, dtype=np.float64)))
    idx = np.arange(c, dtype=np.float64)
    diff = idx[:, None] - idx[None, :]
    dmask = np.where(diff >= 0, np.exp(lg[:, None, None] * np.maximum(diff, 0.0)), 0.0)
    ones = np.ones((1, 1, HEAD_R))
    qdec = np.exp(lg[:, None] * (idx + 1.0))[:, :, None] * ones
    kdec = np.exp(lg[:, None] * (c - 1.0 - idx))[:, :, None] * ones
    cdec = np.exp(lg * c)[:, None, None] * ones
    extra = -c % SUBLANES
    dmask = np.pad(dmask, ((0, 0), (0, extra), (0, extra)))
    qdec = np.pad(qdec, ((0, 0), (0, extra), (0, 0)))
    kdec = np.pad(kdec, ((0, 0), (0, extra), (0, 0)))
    return tuple(t.astype(np.float32) for t in (dmask, qdec, kdec, cdec))


def _block_ones(n, block):
    idx = jnp.arange(n) // block
    return (idx[:, None] == idx[None, :]).astype(BF16)


def kernel(x_prompt, x_sample, state_shift, state_wkv, state_ret, norm_g, ffn1_wg, ffn1_wu, ffn1_wd,
           w_in, mu_shift, w0, w2, a0, a2, g2, k_k, k_a, r_k, lnx_w, lnx_b, ret_gn_w, w_out,
           ffn2_wg, ffn2_wu, ffn2_wd):
    assert norm_g.shape[0] == 1, "single-layer configuration"
    bp, tp, _ = x_prompt.shape
    bs, ts, _ = x_sample.shape
    l = 0
    ng = norm_g[l]
    row = lambda t: t[l].reshape(1, -1)
    zpad = jnp.zeros((LORA_W, D_A), BF16)
    w2p = jnp.concatenate([w2[l].astype(BF16), zpad], axis=0)
    a2p = jnp.concatenate([zpad, a2[l].astype(BF16)], axis=0)
    rk, lnw, lnb, gn = row(r_k), row(lnx_w), row(lnx_b), row(ret_gn_w)
    wkv_params = (row(w0), w2p, row(a0), a2p, g2[l].astype(BF16), row(k_k), row(k_a), rk, lnw, lnb)
    bd_pair = _block_ones(PAIR, HEAD_A)
    bd2 = jnp.concatenate([bd_pair, bd_pair], axis=0)
    tri = (jnp.arange(WKV_CHUNK)[:, None] >= jnp.arange(WKV_CHUNK)[None, :]).astype(BF16)
    tri2 = jnp.concatenate([tri, tri], axis=1)

    m_s = bs * ts
    x1s, *f1 = _ffn_stream(x_sample, ng, ffn1_wg[l], ffn1_wu[l], ffn1_wd[l], 0, 1)

    xp = x_prompt.reshape(bp * tp, D_MODEL)
    x1p, *f2, win, wo = _ffn(xp, ng, *f1, 0, 1, FFN_ROWS,
                             cast=(ffn2_wg[l], ffn2_wu[l], ffn2_wd[l], w_in[l], w_out[l]))
    proj_consts = (ng, win, row(mu_shift))
    cos_p, sin_p = _rope_tables(np.arange(tp))
    (mixed, yr_p, ret_p, hl_p) = _proj(
        x1p, None, *proj_consts, cos_p, sin_p, n_t=1, rows_per_t=PROJ_ROWS, lag=1,
        tiles_per_seq=tp // PROJ_ROWS, ret_tables=(*_ret_tables(RET_CHUNK), gn))
    ya_p, wkv_p = _wkv_prompt(mixed.reshape(bp, tp, N_SHIFT), (*wkv_params, bd2, tri2),
                              nb=WKV_SEQS, tt=WKV_ROWS)
    ya_p = ya_p.reshape(bp * tp, D_A)
    yp = _ffn(x1p, ng, *f2, 4, 5, FFN_ROWS, mix=(ya_p, yr_p, wo))

    cos_s, sin_s = _rope_tables(PAST_LEN + np.arange(ts))
    rows_per_t = bs
    tab = lambda t: np.ascontiguousarray(np.broadcast_to(t[:, None, :], (ts, bs, HEAD_R)))
    outs = _proj(x1s.reshape(ts, bs, D_MODEL),
                 state_shift[l].reshape(bs // rows_per_t, rows_per_t, D_MODEL),
                 *proj_consts, tab(cos_s), tab(sin_s), n_t=ts, rows_per_t=rows_per_t,
                 lag=rows_per_t, tiles_per_seq=1)
    (mixed, q, kr, vr, gr, hl_s) = outs
    ya_s, wkv_s = _wkv_sample(mixed, *wkv_params, bd2, state_wkv[l].transpose(1, 2, 3, 0))
    wkv_s = wkv_s.transpose(3, 0, 1, 2)
    yr_s, ret_s = _ret_sample(q, kr, vr, gr, *_ret_tables(min(RET_CHUNK, ts)), gn, state_ret[l],
                              bb=RET_SEQS)
    ys = _ffn(x1s, ng, *f2, 4, 5, m_s,
              mix=(ya_s.reshape(m_s, D_A), yr_s.reshape(m_s, D_R), wo), out_seqs=bs)

    return (yp.reshape(bp, tp, D_MODEL), ys,
            hl_p.reshape(1, bp, D_MODEL), wkv_p[None], ret_p[None],
            hl_s.reshape(1, bs, D_MODEL), wkv_s[None], ret_s[None])
```

```python
import functools

import jax
import jax.numpy as jnp
import numpy as np
from jax import lax
from jax.experimental import pallas as pl
from jax.experimental.pallas import tpu as pltpu

F32 = jnp.float32
BF16 = jnp.bfloat16

D_MODEL = 1024
D_A = 512
HEAD_A = 64
H_A = D_A // HEAD_A
D_R = 512
H_R = 4
HEAD_R = D_R // H_R
LORA_W, LORA_A, LORA_G = 64, 64, 128
D_FF = 2816
RET_CHUNK = 128
ROPE_BASE = 10000.0
EPS = 1e-6
GN_EPS_A = 64e-5
GN_EPS_R = 1e-5
N_SHIFT = 3 * D_A + LORA_W + LORA_A + LORA_G
PAST_LEN = 16384

LANES = 128
SUBLANES = 8
VMEM_LIMIT = 52 * 1024 * 1024

MXU_DIM = 256
FF_SPLIT = 6 * MXU_DIM
WKV_CHUNK = 64
PAIR = 2 * HEAD_A

FFN_ROWS = 1024
PROJ_ROWS = 4 * RET_CHUNK
WKV_SEQS, WKV_ROWS = 4, 512
RET_SEQS = 2 * SUBLANES


def _nt(a, b):
    return lax.dot_general(a, b, (((1,), (1,)), ((), ())), preferred_element_type=F32)


def _tn(a, b):
    return lax.dot_general(a, b, (((0,), (0,)), ((), ())), preferred_element_type=F32)


def _mm(a, b):
    return jnp.dot(a, b, preferred_element_type=F32)


def _split_hi_lo(x):
    hi = x.astype(BF16)
    lo = (x - hi.astype(F32)).astype(BF16)
    return hi, lo


def _rms(x, g):
    return x * lax.rsqrt(jnp.mean(x * x, axis=-1, keepdims=True) + EPS) * g


def _softplus(x):
    return jnp.maximum(x, 0.0) + jnp.log(1.0 + jnp.exp(-jnp.abs(x)))


def _sigmoid(x):
    return 1.0 / (1.0 + jnp.exp(-x))


def _resident(shape):
    nd = len(shape)
    return pl.BlockSpec(shape, lambda *_: (0,) * nd, pipeline_mode=pl.Buffered(1))


def _params(n_axes):
    return pltpu.CompilerParams(dimension_semantics=("arbitrary",) * n_axes,
                                vmem_limit_bytes=VMEM_LIMIT)


def _ffn_kernel(with_mix, n_cast, g_in, g_out, *refs):
    n_in = (8 if with_mix else 5) + n_cast
    ins, outs = refs[:n_in], refs[n_in:]
    if with_mix:
        x_ref, ya_ref, yr_ref, wo_ref, ng_ref, wg_ref, wu_ref, wd_ref = ins[:8]
    else:
        x_ref, ng_ref, wg_ref, wu_ref, wd_ref = ins[:5]
    o_ref = outs[0]
    for src, dst in zip(ins[n_in - n_cast:], outs[1:]):
        dst[...] = src[...].astype(dst.dtype)
    half = x_ref.shape[0] // 2
    halves = [slice(0, half), slice(half, 2 * half)]
    x = [x_ref[r, :] for r in halves]
    if with_mix:
        mix = [_mm(ya_ref[r, :].astype(BF16), wo_ref[0:D_A, :])
               + _mm(yr_ref[r, :].astype(BF16), wo_ref[D_A:, :]) for r in halves]
        x = [xi + _rms(m, ng_ref[3:4, :]) for xi, m in zip(x, mix)]
    h = [_rms(xi, ng_ref[g_in:g_in + 1, :]).astype(BF16) for xi in x]
    for i, r in enumerate(halves):
        acc = None
        for cols in (slice(0, FF_SPLIT), slice(FF_SPLIT, D_FF)):
            gate = _mm(h[i], wg_ref[:, cols])
            up = _mm(h[i], wu_ref[:, cols])
            act = (gate * _sigmoid(gate) * up).astype(BF16)
            part = _mm(act, wd_ref[cols, :])
            acc = part if acc is None else acc + part
        out = x[i] + 0.5 * _rms(acc, ng_ref[g_out:g_out + 1, :])
        if o_ref.ndim == 2:
            o_ref[r, :] = out
        else:
            n_seq = o_ref.shape[0]
            for j in range(half // n_seq):
                o_ref[:, i * (half // n_seq) + j, :] = out[j * n_seq:(j + 1) * n_seq]


BF16_ROWS = 2 * SUBLANES


def _cast_spec(rows, cols, steps):
    rep = 1
    while (rows * rep) % steps or (rows * rep // steps) % BF16_ROWS:
        rep *= 2
    return pl.BlockSpec((rows * rep // steps, cols), lambda i: (i // rep, 0))


def _ffn(x, ng, wg, wu, wd, g_in, g_out, tm, mix=None, cast=(), out_seqs=None):
    m = x.shape[0]
    steps = m // tm
    assert out_seqs is None or (steps == 1 and (m // 2) % out_seqs == 0)
    o_shape = (m, D_MODEL) if out_seqs is None else (out_seqs, m // out_seqs, D_MODEL)
    row = lambda w: pl.BlockSpec((tm, w), lambda i: (i, 0))
    if mix is None:
        args = (x, ng, wg, wu, wd)
        specs = [row(D_MODEL), _resident(ng.shape), _resident(wg.shape), _resident(wu.shape),
                 _resident(wd.shape)]
    else:
        ya, yr, wo = mix
        args = (x, ya, yr, wo, ng, wg, wu, wd)
        specs = [row(D_MODEL), row(D_A), row(D_R), _resident(wo.shape), _resident(ng.shape),
                 _resident(wg.shape), _resident(wu.shape), _resident(wd.shape)]
    cast_specs = [_cast_spec(*w.shape, steps) for w in cast]
    out = pl.pallas_call(
        functools.partial(_ffn_kernel, mix is not None, len(cast), g_in, g_out),
        grid=(steps,),
        in_specs=specs + cast_specs,
        out_specs=[row(D_MODEL) if out_seqs is None else pl.BlockSpec(o_shape, lambda i: (0, 0, 0))]
        + cast_specs,
        out_shape=[jax.ShapeDtypeStruct(o_shape, F32)]
        + [jax.ShapeDtypeStruct(w.shape, BF16) for w in cast],
        compiler_params=_params(1),
        name="ffn_mix" if mix is not None else "ffn",
    )(*args, *cast)
    return out if cast else out[0]


def _ffn_stream_kernel(g_in, g_out, x_ref, ng_ref, wg_ref, wu_ref, wd_ref,
                       o_ref, wg_o, wu_o, wd_o, h_scr, acc_scr):
    c = pl.program_id(0)
    x_rows = lambda: jnp.concatenate([x_ref[:, t, :] for t in range(x_ref.shape[1])], axis=0)

    @pl.when(c == 0)
    def _():
        h_scr[...] = _rms(x_rows(), ng_ref[g_in:g_in + 1, :]).astype(BF16)
        acc_scr[...] = jnp.zeros_like(acc_scr)

    wg, wu, wd = wg_ref[...].astype(BF16), wu_ref[...].astype(BF16), wd_ref[...].astype(BF16)
    wg_o[...] = wg
    wu_o[...] = wu
    wd_o[...] = wd
    h = h_scr[...]
    gate = _mm(h, wg)
    act = (gate * _sigmoid(gate) * _mm(h, wu)).astype(BF16)
    acc_scr[...] += _mm(act, wd)

    @pl.when(c == pl.num_programs(0) - 1)
    def _():
        o_ref[...] = x_rows() + 0.5 * _rms(acc_scr[...], ng_ref[g_out:g_out + 1, :])


def _ffn_stream(x, ng, wg, wu, wd, g_in, g_out):
    m = x.shape[0] * x.shape[1]
    slab = MXU_DIM
    whole = lambda a: pl.BlockSpec(a.shape, lambda c: (0,) * a.ndim)
    cols = pl.BlockSpec((D_MODEL, slab), lambda c: (0, c))
    rows = pl.BlockSpec((slab, D_MODEL), lambda c: (c, 0))
    return pl.pallas_call(
        functools.partial(_ffn_stream_kernel, g_in, g_out),
        grid=(D_FF // slab,),
        in_specs=[whole(x), whole(ng), cols, cols, rows],
        out_specs=[pl.BlockSpec((m, D_MODEL), lambda c: (0, 0)), cols, cols, rows],
        out_shape=[jax.ShapeDtypeStruct((m, D_MODEL), F32)]
        + [jax.ShapeDtypeStruct(w.shape, BF16) for w in (wg, wu, wd)],
        scratch_shapes=[pltpu.VMEM((m, D_MODEL), BF16), pltpu.VMEM((m, D_MODEL), F32)],
        compiler_params=_params(1),
        name="ffn_stream",
    )(x, ng, wg, wu, wd)


def _lag_rows(cur, first, lag):
    if lag % SUBLANES == 0:
        return jnp.concatenate([first, cur[:-lag]], axis=0)
    assert lag == 1
    rolled = pltpu.roll(cur, 1, 0)
    rid = lax.broadcasted_iota(jnp.int32, (SUBLANES, cur.shape[1]), 0)
    head = jnp.where(rid == 0, first, rolled[0:SUBLANES])
    return jnp.concatenate([head, rolled[SUBLANES:]], axis=0)


def _proj_kernel(n_t, lag, tiles_per_seq, has_prev, fuse_ret, *refs):
    it = iter(refs)
    x_ref = next(it)
    prev_ref = next(it) if has_prev else None
    ng_ref, win_ref, mu_ref = next(it), next(it), next(it)
    if fuse_ret:
        dm_ref, qd_ref, kd_ref, cd_ref, gn_ref = (next(it) for _ in range(5))
        cos_ref, sin_ref, mix_o, yr_o, s_o, hl_o, carry_scr, s_scr = tuple(it)
    else:
        cos_ref, sin_ref, mix_o, q_o, kr_o, vr_o, gr_o, hl_o, carry_scr = tuple(it)

    def load(ref):
        if n_t == 1:
            return ref[...]
        return jnp.concatenate([ref[t] for t in range(n_t)], axis=0)

    def store(ref, val, cols=slice(None)):
        if n_t == 1:
            ref[:, cols] = val.astype(ref.dtype)
        else:
            rows = val.shape[0] // n_t
            for t in range(n_t):
                ref[t, :, cols] = val[t * rows:(t + 1) * rows].astype(ref.dtype)

    x = load(x_ref)
    tm = x.shape[0]
    h = _rms(x, ng_ref[2:3, :])
    hl_o[0] = h[tm - lag:, :]
    hb = h.astype(BF16)
    seq_start = (pl.program_id(0) % tiles_per_seq) == 0
    prev_b = prev_ref[0].astype(BF16) if has_prev else None

    ret = lambda c: _mm(hb, win_ref[:, N_SHIFT + c * D_R:N_SHIFT + (c + 1) * D_R])
    cos2 = load(cos_ref)
    sin2 = load(sin_ref)

    def rope(t):
        parts = []
        for hh in range(H_R):
            th = t[:, hh * HEAD_R:(hh + 1) * HEAD_R]
            parts.append(th * cos2 + pltpu.roll(th, HEAD_R // 2, 1) * sin2)
        return jnp.concatenate(parts, axis=1)

    if fuse_ret:
        rq = rope(ret(0)).astype(BF16)
        rk = (rope(ret(1)) * (HEAD_R ** -0.5)).astype(BF16)
        rv = ret(2).astype(BF16)
        rg = ret(3)
        heads = list(range(H_R))
        lanes = [slice(hh * HEAD_R, (hh + 1) * HEAD_R) for hh in heads]
        r_state = [jnp.where(seq_start, 0.0, s_scr[hh]) for hh in heads]

    for slab, c0 in enumerate(range(0, N_SHIFT, D_A)):
        cols = slice(c0, min(c0 + D_A, N_SHIFT))
        cur = _mm(hb, win_ref[:, cols])
        if has_prev:
            first = _mm(prev_b, win_ref[:, cols])
        else:
            first = jnp.zeros((lag, cur.shape[1]), F32)
        if tiles_per_seq > 1:
            first = jnp.where(seq_start, first, carry_scr[0:lag, cols])
            carry_scr[0:lag, cols] = cur[tm - lag:, :]
        prv = _lag_rows(cur, first, lag)
        store(mix_o, cur + (prv - cur) * mu_ref[:, cols], cols)
        if fuse_ret:
            rows = slice(slab * RET_CHUNK, (slab + 1) * RET_CHUNK)
            pick = lambda t: [t[rows, ln] for ln in lanes]
            y, r_state = _ret_chunk(pick(rq), pick(rk), pick(rv), pick(rg), r_state, heads,
                                    dm_ref, qd_ref, kd_ref, cd_ref, gn_ref)
            for hh in heads:
                yr_o[rows, lanes[hh]] = y[hh].astype(yr_o.dtype)

    if fuse_ret:
        for hh in heads:
            s_scr[hh] = r_state[hh]
            s_o[0, hh] = r_state[hh]
        return

    q = ret(0)
    kr = ret(1)
    store(q_o, rope(q))
    vr = ret(2)
    store(kr_o, rope(kr) * (HEAD_R ** -0.5))
    gr = ret(3)
    store(vr_o, vr)
    store(gr_o, gr)


def _proj(x, prev, ng, win, mu, cos2, sin2, *, n_t, rows_per_t, lag, tiles_per_seq,
          ret_tables=None):
    tm = n_t * rows_per_t
    fuse_ret = ret_tables is not None
    assert not fuse_ret or (n_t == 1 and tm == pl.cdiv(N_SHIFT, D_A) * RET_CHUNK)
    if n_t == 1:
        m = x.shape[0]
        n_tiles = m // tm
        row = lambda w: pl.BlockSpec((tm, w), lambda i: (i, 0))
        shp = lambda w, dt: jax.ShapeDtypeStruct((m, w), dt)
        tab = pl.BlockSpec((tm, HEAD_R), lambda i: (i % tiles_per_seq, 0))
    else:
        m = x.shape[0] * x.shape[1]
        n_tiles = x.shape[1] // rows_per_t
        row = lambda w: pl.BlockSpec((n_t, rows_per_t, w), lambda i: (0, i, 0))
        shp = lambda w, dt: jax.ShapeDtypeStruct((n_t, m // n_t, w), dt)
        tab = pl.BlockSpec((n_t, rows_per_t, HEAD_R), lambda i: (0, i, 0))
    n_seq = n_tiles // tiles_per_seq
    hl_spec = pl.BlockSpec((1, lag, D_MODEL), lambda i: (i // tiles_per_seq, 0, 0))
    args = [x]
    specs = [row(D_MODEL)]
    if prev is not None:
        args.append(prev)
        specs.append(pl.BlockSpec((1, lag, D_MODEL), lambda i: (i // tiles_per_seq, 0, 0)))
    consts = (ng, win, mu) + (tuple(ret_tables) if fuse_ret else ())
    args += list(consts) + [cos2, sin2]
    specs += [_resident(c.shape) for c in consts] + [tab, tab]
    hl_shape = jax.ShapeDtypeStruct((n_seq, lag, D_MODEL), F32)
    scratch = [pltpu.VMEM((max(lag, SUBLANES), N_SHIFT), F32)]
    if fuse_ret:
        out_shape = [shp(N_SHIFT, F32), shp(D_R, BF16),
                     jax.ShapeDtypeStruct((n_seq, H_R, HEAD_R, HEAD_R), F32), hl_shape]
        out_specs = [row(N_SHIFT), row(D_R),
                     pl.BlockSpec((1, H_R, HEAD_R, HEAD_R), lambda i: (i // tiles_per_seq, 0, 0, 0)),
                     hl_spec]
        scratch.append(pltpu.VMEM((H_R, HEAD_R, HEAD_R), F32))
    else:
        out_shape = [shp(N_SHIFT, F32)] + [shp(D_R, F32)] * 4 + [hl_shape]
        out_specs = [row(N_SHIFT)] + [row(D_R)] * 4 + [hl_spec]
    return pl.pallas_call(
        functools.partial(_proj_kernel, n_t, lag, tiles_per_seq, prev is not None, fuse_ret),
        grid=(n_tiles,),
        in_specs=specs,
        out_specs=out_specs,
        out_shape=out_shape,
        scratch_shapes=scratch,
        compiler_params=_params(1),
        name="proj",
    )(*args)


def _sums_stacked(x, bd2):
    return _mm(jnp.concatenate(_split_hi_lo(x), axis=1), bd2)


def _wkv_post(y, r, k, v, g, rk, lw_g, lb_g, head_sums):
    inv_n = 1.0 / HEAD_A
    mu = head_sums(y) * inv_n
    d = y - mu
    var = head_sums(d * d) * inv_n
    yn = d * lax.rsqrt(var + GN_EPS_A) * lw_g + lb_g
    bonus = head_sums(r * k * rk) * v
    return (yn + bonus) * g


def _wkv_decay_gate(wa, gd, w0, w2p, a0, a2p, g2):
    w_pre = w0 + _mm(jnp.tanh(wa).astype(BF16), w2p)
    lw = -jnp.exp(-_softplus(-w_pre) - 0.5)
    a = _sigmoid(a0 + _mm(wa.astype(BF16), a2p))
    g = _mm(_sigmoid(gd).astype(BF16), g2)
    return lw, a, g


def _wkv_keys(k, a, k_k, k_a, head_sums):
    kk = k * k_k
    kk = kk / jnp.maximum(jnp.sqrt(head_sums(kk * kk)), 1e-12)
    return k * (1.0 + (a - 1.0) * k_a), kk, kk * a


def _wkv_prompt_kernel(nb, tt, mix_ref, w0_ref, w2_ref, a0_ref, a2_ref, g2_ref, kk_ref, ka_ref,
                       rk_ref, lnw_ref, lnb_ref, bd2_ref, tri2_ref,
                       ya_o, s_o, s_scr):
    c = WKV_CHUNK
    j = pl.program_id(1)

    @pl.when(j == 0)
    def _():
        s_scr[...] = jnp.zeros_like(s_scr)

    lane = lax.broadcasted_iota(jnp.int32, (c, PAIR), 1)
    first = lane < HEAD_A

    def stack(x):
        return jnp.concatenate([jnp.where(first, x, 0.0), jnp.where(first, 0.0, x)], axis=0)

    ri = lax.broadcasted_iota(jnp.int32, (2 * c, 2 * c), 0) % c
    ci = lax.broadcasted_iota(jnp.int32, (2 * c, 2 * c), 1) % c
    strict = ri > ci
    incl = ri >= ci
    eye = (lax.broadcasted_iota(jnp.int32, (2 * c, 2 * c), 0)
           == lax.broadcasted_iota(jnp.int32, (2 * c, 2 * c), 1)).astype(F32)
    bd2 = bd2_ref[...]
    tri2 = tri2_ref[...]
    head_sums = functools.partial(_sums_stacked, bd2=bd2)

    n_pairs = H_A // 2
    units = [(bi, p) for bi in range(nb) for p in range(n_pairs)]
    n_u = range(len(units))
    slab = [slice(p * PAIR, (p + 1) * PAIR) for p in range(n_pairs)]

    tall = lambda xs: jnp.concatenate(xs, axis=0)
    per_pair = lambda ref: tall([jnp.broadcast_to(ref[:, slab[p]], (c, PAIR)) for _, p in units])
    split = lambda x: [x[u * c:(u + 1) * c] for u in n_u]

    def prep(ch):
        rows = slice(ch * c, (ch + 1) * c)
        ld = lambda c0: [mix_ref[bi, rows, c0 + p * PAIR:c0 + (p + 1) * PAIR] for bi, p in units]
        r, k_raw, v = ld(0), ld(D_A), ld(2 * D_A)
        lora = [_wkv_decay_gate(mix_ref[bi, rows, 3 * D_A:3 * D_A + LORA_W + LORA_A],
                                mix_ref[bi, rows, 3 * D_A + LORA_W + LORA_A:N_SHIFT],
                                w0_ref[...], w2_ref[...], a0_ref[...], a2_ref[...], g2_ref[...])
                for bi in range(nb)]
        lw_all = [x[0] for x in lora]
        a = tall([lora[bi][1][:, slab[p]] for bi, p in units])
        gate = tall([lora[bi][2][:, slab[p]] for bi, p in units])
        k, kk, kka = map(split, _wkv_keys(tall(k_raw), a, per_pair(kk_ref), per_pair(ka_ref),
                                          head_sums))
        return r, v, k, kk, kka, gate, lw_all

    def gram_stage(pre):
        r, v, k, kk, kka, gate, lw_all = pre
        cum_all = [_mm(tri2, jnp.concatenate(_split_hi_lo(x), axis=0)) for x in lw_all]
        lw = [lw_all[bi][:, slab[p]] for bi, p in units]
        cum = [cum_all[bi][:, slab[p]] for bi, p in units]
        cum_end = [x[c - 1:c, :] for x in cum]
        e_pos = [jnp.exp(x) for x in cum]
        e_neg = [jnp.exp(-x) for x in cum]
        e_end = [jnp.exp(x - y) for x, y in zip(cum_end, cum)]
        xs = [jnp.concatenate([stack(-kk[u] * jnp.exp(cum[u] - lw[u])), stack(r[u] * e_pos[u])],
                              axis=0).astype(BF16) for u in n_u]
        ws = [jnp.concatenate([stack(kka[u] * e_neg[u]), stack(k[u] * e_neg[u])],
                              axis=0).astype(BF16) for u in n_u]
        we = [jnp.concatenate([stack(kka[u] * e_end[u]), stack(k[u] * e_end[u])],
                              axis=0).astype(BF16) for u in n_u]
        vs = [stack(x).astype(BF16) for x in v]
        gram = [_nt(xs[u], ws[u]) for u in n_u]
        a_ab = [jnp.where(strict, g[0:2 * c, 0:2 * c], 0.0) for g in gram]
        a_ak = [jnp.where(strict, g[0:2 * c, 2 * c:], 0.0).astype(BF16) for g in gram]
        a_r = [jnp.concatenate([jnp.where(incl, g[2 * c:, 0:2 * c], 0.0),
                                jnp.where(incl, g[2 * c:, 2 * c:], 0.0)], axis=1).astype(BF16)
               for g in gram]
        return xs, we, vs, cum_end, a_ab, a_ak, a_r

    def solve_stage(ch, pre, gs):
        rows = slice(ch * c, (ch + 1) * c)
        r, v, k, kk, kka, gate, lw_all = pre
        xs, we, vs, cum_end, a_ab, a_ak, a_r = gs
        inv = [eye + a for a in a_ab]
        pw = [x.astype(BF16) for x in a_ab]
        pw = [_mm(x, x).astype(BF16) for x in pw]
        n_lvl = c.bit_length() - 2
        for lvl in range(n_lvl):
            if lvl < n_lvl - 1:
                both = [_mm(x, jnp.concatenate([x, i.astype(BF16)], axis=1)) for i, x in zip(inv, pw)]
                inv = [i + b[:, 2 * c:] for i, b in zip(inv, both)]
                pw = [b[:, 0:2 * c].astype(BF16) for b in both]
            else:
                inv = [i + _mm(x, i.astype(BF16)) for i, x in zip(inv, pw)]
        s = [s_scr[u] for u in n_u]
        z = [_nt(xs[u], s[u].astype(BF16)) for u in n_u]
        rhs = [z[u][0:2 * c] + _mm(a_ak[u], vs[u]) for u in n_u]
        uu = [_mm(inv[u].astype(BF16), rhs[u].astype(BF16)).astype(BF16) for u in n_u]
        uv = [jnp.concatenate([uu[u], vs[u]], axis=0) for u in n_u]
        y2 = [z[u][2 * c:] + _mm(a_r[u], uv[u]) for u in n_u]
        for u in n_u:
            s_scr[u] = s[u] * jnp.exp(cum_end[u]) + _tn(uv[u], we[u])
        out = _wkv_post(tall([y2[u][0:c] + y2[u][c:] for u in n_u]), tall(r), tall(k), tall(v),
                        gate, per_pair(rk_ref), per_pair(lnw_ref), per_pair(lnb_ref), head_sums)
        for u, (bi, p) in enumerate(units):
            ya_o[bi, rows, slab[p]] = out[u * c:(u + 1) * c].astype(ya_o.dtype)

    n_chunks = tt // c
    pre = prep(0)
    for ch in range(n_chunks):
        gs = gram_stage(pre)
        nxt = prep(ch + 1) if ch + 1 < n_chunks else None
        solve_stage(ch, pre, gs)
        pre = nxt

    @pl.when(j == pl.num_programs(1) - 1)
    def _():
        for u, (bi, p) in enumerate(units):
            s = s_scr[u]
            s_o[bi, 2 * p] = s[0:HEAD_A, 0:HEAD_A]
            s_o[bi, 2 * p + 1] = s[HEAD_A:, HEAD_A:]


def _wkv_prompt(mixed, consts, *, nb, tt):
    batch, seq, _ = mixed.shape
    blk = lambda w: pl.BlockSpec((nb, tt, w), lambda b, j: (b, j, 0))
    return pl.pallas_call(
        functools.partial(_wkv_prompt_kernel, nb, tt),
        grid=(batch // nb, seq // tt),
        in_specs=[blk(N_SHIFT)] + [_resident(x.shape) for x in consts],
        out_specs=[blk(D_A), pl.BlockSpec((nb, H_A, HEAD_A, HEAD_A), lambda b, j: (b, 0, 0, 0))],
        out_shape=[jax.ShapeDtypeStruct((batch, seq, D_A), BF16),
                   jax.ShapeDtypeStruct((batch, H_A, HEAD_A, HEAD_A), F32)],
        scratch_shapes=[pltpu.VMEM((nb * H_A // 2, PAIR, PAIR), F32)],
        compiler_params=_params(2),
        name="wkv_prompt",
    )(mixed, *consts)


def _wkv_sample_kernel(n_t, r_ref, k_ref, v_ref, lora_ref,
                       w0_ref, w2_ref, a0_ref, a2_ref, g2_ref, kk_ref, ka_ref,
                       rk_ref, lnw_ref, lnb_ref, bd2_ref, s_ref,
                       ya_o, s_o, yt_scr):
    n = HEAD_A
    n_b = r_ref.shape[1]
    tall = lambda xs: jnp.concatenate(xs, axis=0)
    rows_of = lambda ref: tall([ref[t] for t in range(n_t)])
    head_sums = functools.partial(_sums_stacked, bd2=bd2_ref[...])
    r, v, lora = rows_of(r_ref), rows_of(v_ref), rows_of(lora_ref)
    lw, a, gate = _wkv_decay_gate(lora[:, 0:LORA_W + LORA_A], lora[:, LORA_W + LORA_A:],
                                  w0_ref[...], w2_ref[...], a0_ref[...], a2_ref[...], g2_ref[...])
    k, kk, kka = _wkv_keys(rows_of(k_ref), a, kk_ref[...], ka_ref[...], head_sums)

    tr = lambda x: [x[t * n_b:(t + 1) * n_b].T for t in range(n_t)]
    nkk_t, kka_t, k_t, r_t, v_t, w_t = tr(-kk), tr(kka), tr(k), tr(r), tr(v), tr(jnp.exp(lw))
    rid = lax.broadcasted_iota(jnp.int32, (SUBLANES, n_b), 0)

    for hh in range(2):
        keys = slice(hh * n, (hh + 1) * n)
        for ig in range(n // SUBLANES):
            y_tiles = [jnp.zeros((SUBLANES, n_b), F32) for _ in range(n_t)]
            for ii in range(SUBLANES):
                i = ig * SUBLANES + ii
                s = s_ref[hh, i]
                for t in range(n_t):
                    sa = jnp.sum(s * nkk_t[t][keys], axis=0, keepdims=True)
                    v_row = v_t[t][hh * n + i:hh * n + i + 1]
                    s = s * w_t[t][keys] + sa * kka_t[t][keys] + v_row * k_t[t][keys]
                    y_row = jnp.sum(s * r_t[t][keys], axis=0, keepdims=True)
                    y_tiles[t] = jnp.where(rid == ii, y_row, y_tiles[t])
                s_o[hh, i] = s
            for t in range(n_t):
                yt_scr[t, hh * n + ig * SUBLANES:hh * n + (ig + 1) * SUBLANES, :] = y_tiles[t]

    out = _wkv_post(tall([yt_scr[t].T for t in range(n_t)]), r, k, v, gate,
                    rk_ref[...], lnw_ref[...], lnb_ref[...], head_sums)
    for t in range(n_t):
        ya_o[t] = out[t * n_b:(t + 1) * n_b].astype(ya_o.dtype)


def _wkv_sample(mixed, w0, w2p, a0, a2p, g2, k_k, k_a, rk, lnw, lnb, bd2, s0):
    n_t, n_b, _ = mixed.shape
    slab = lambda first: pl.BlockSpec((n_t, n_b, PAIR), lambda p: (0, 0, first + p))
    lora = pl.BlockSpec((n_t, n_b, 2 * PAIR), lambda p: (0, 0, 3 * D_A // (2 * PAIR)))
    par = pl.BlockSpec((1, PAIR), lambda p: (0, p))
    low = pl.BlockSpec((PAIR, PAIR), lambda p: (0, p))
    st = pl.BlockSpec((2, HEAD_A, HEAD_A, n_b), lambda p: (p, 0, 0, 0))
    n_slab = D_A // PAIR
    return pl.pallas_call(
        functools.partial(_wkv_sample_kernel, n_t),
        grid=(H_A // 2,),
        in_specs=[slab(0), slab(n_slab), slab(2 * n_slab), lora,
                  par, low, par, low, low, par, par, par, par, par, _resident(bd2.shape), st],
        out_specs=[slab(0), st],
        out_shape=[jax.ShapeDtypeStruct((n_t, n_b, D_A), F32),
                   jax.ShapeDtypeStruct(s0.shape, F32)],
        scratch_shapes=[pltpu.VMEM((n_t, PAIR, n_b), F32)],
        compiler_params=_params(1),
        name="wkv_sample",
    )(mixed, mixed, mixed, mixed, w0, w2p, a0, a2p, g2, k_k, k_a, rk, lnw, lnb, bd2, s0)


def _ret_chunk(q, k, v, g, s, heads, dm_ref, qd_ref, kd_ref, cd_ref, gn_ref):
    n = range(len(q))
    qb = [x.astype(BF16) for x in q]
    kb = [x.astype(BF16) for x in k]
    vb = [x.astype(BF16) for x in v]
    inner = [(_nt(qb[u], kb[u]) * dm_ref[heads[u]]).astype(BF16) for u in n]
    q_dec = [(q[u].astype(F32) * qd_ref[heads[u]]).astype(BF16) for u in n]
    k_dec = [(k[u].astype(F32) * kd_ref[heads[u]]).astype(BF16) for u in n]
    if inner[0].shape[1] % LANES == 0:
        y = [_mm(jnp.concatenate([inner[u], q_dec[u]], axis=1),
                 jnp.concatenate([vb[u], s[u].astype(BF16)], axis=0)) for u in n]
    else:
        y = [_mm(inner[u], vb[u]) + _mm(q_dec[u], s[u].astype(BF16)) for u in n]
    s_new = [s[u] * cd_ref[heads[u]] + _tn(k_dec[u], vb[u]) for u in n]
    out = []
    for u in n:
        mu = jnp.mean(y[u], axis=-1, keepdims=True)
        d = y[u] - mu
        var = jnp.mean(d * d, axis=-1, keepdims=True)
        lanes = slice(heads[u] * HEAD_R, (heads[u] + 1) * HEAD_R)
        yn = d * lax.rsqrt(var + GN_EPS_R) * gn_ref[:, lanes]
        out.append(g[u] * _sigmoid(g[u]) * yn)
    return out, s_new


def _ret_sample_kernel(n_t, bb, q_ref, k_ref, v_ref, g_ref, dm_ref, qd_ref, kd_ref, cd_ref,
                       gn_ref, s_ref, y_o, s_o):
    rid = lax.broadcasted_iota(jnp.int32, (SUBLANES, HEAD_R), 0)
    units = [(bi, hh) for bi in range(bb) for hh in range(H_R)]
    heads = [hh for _, hh in units]
    lanes = [slice(hh * HEAD_R, (hh + 1) * HEAD_R) for hh in heads]

    def seq_rows(ref):
        outs = []
        for u, (bi, _) in enumerate(units):
            out = jnp.zeros((SUBLANES, HEAD_R), F32)
            for t in range(n_t):
                out = jnp.where(rid == t, jnp.broadcast_to(ref[t, bi:bi + 1, lanes[u]], out.shape), out)
            outs.append(out)
        return outs

    y, s_new = _ret_chunk(seq_rows(q_ref), seq_rows(k_ref), seq_rows(v_ref), seq_rows(g_ref),
                          [s_ref[bi, hh] for bi, hh in units], heads,
                          dm_ref, qd_ref, kd_ref, cd_ref, gn_ref)
    for u, (bi, hh) in enumerate(units):
        s_o[bi, hh] = s_new[u]
        for t in range(n_t):
            y_o[t, bi:bi + 1, lanes[u]] = y[u][t:t + 1].astype(y_o.dtype)


def _ret_sample(q, k, v, g, dm, qd, kd, cd, gn, s0, *, bb):
    n_t, n_b, _ = q.shape
    consts = (dm, qd, kd, cd, gn)
    blk = pl.BlockSpec((n_t, bb, D_R), lambda i: (0, i, 0))
    st = pl.BlockSpec((bb, H_R, HEAD_R, HEAD_R), lambda i: (i, 0, 0, 0))
    return pl.pallas_call(
        functools.partial(_ret_sample_kernel, n_t, bb),
        grid=(n_b // bb,),
        in_specs=[blk] * 4 + [_resident(x.shape) for x in consts] + [st],
        out_specs=[blk, st],
        out_shape=[jax.ShapeDtypeStruct((n_t, n_b, D_R), F32),
                   jax.ShapeDtypeStruct(s0.shape, F32)],
        compiler_params=_params(1),
        name="ret_sample",
    )(q, k, v, g, *consts, s0)


def _rope_tables(pos):
    half = HEAD_R // 2
    inv = ROPE_BASE ** (-np.arange(half, dtype=np.float64) / half)
    ang = np.asarray(pos, np.float64)[:, None] * inv[None, :]
    cos, sin = np.cos(ang), np.sin(ang)
    return (np.concatenate([cos, cos], axis=1).astype(np.float32),
            np.concatenate([-sin, sin], axis=1).astype(np.float32))


def _ret_tables(c):
    lg = np.log1p(-np.exp2(-5.0 - np.arange(H_R, dtype=np.float64)))
    idx = np.arange(c, dtype=np.float64)
    diff = idx[:, None] - idx[None, :]
    dmask = np.where(diff >= 0, np.exp(lg[:, None, None] * np.maximum(diff, 0.0)), 0.0)
    ones = np.ones((1, 1, HEAD_R))
    qdec = np.exp(lg[:, None] * (idx + 1.0))[:, :, None] * ones
    kdec = np.exp(lg[:, None] * (c - 1.0 - idx))[:, :, None] * ones
    cdec = np.exp(lg * c)[:, None, None] * ones
    extra = -c % SUBLANES
    dmask = np.pad(dmask, ((0, 0), (0, extra), (0, extra)))
    qdec = np.pad(qdec, ((0, 0), (0, extra), (0, 0)))
    kdec = np.pad(kdec, ((0, 0), (0, extra), (0, 0)))
    return tuple(t.astype(np.float32) for t in (dmask, qdec, kdec, cdec))


def _block_ones(n, block):
    idx = jnp.arange(n) // block
    return (idx[:, None] == idx[None, :]).astype(BF16)


def kernel(x_prompt, x_sample, state_shift, state_wkv, state_ret, norm_g, ffn1_wg, ffn1_wu, ffn1_wd,
           w_in, mu_shift, w0, w2, a0, a2, g2, k_k, k_a, r_k, lnx_w, lnx_b, ret_gn_w, w_out,
           ffn2_wg, ffn2_wu, ffn2_wd):
    assert norm_g.shape[0] == 1, "single-layer configuration"
    bp, tp, _ = x_prompt.shape
    bs, ts, _ = x_sample.shape
    l = 0
    ng = norm_g[l]
    row = lambda t: t[l].reshape(1, -1)
    zpad = jnp.zeros((LORA_W, D_A), BF16)
    w2p = jnp.concatenate([w2[l].astype(BF16), zpad], axis=0)
    a2p = jnp.concatenate([zpad, a2[l].astype(BF16)], axis=0)
    rk, lnw, lnb, gn = row(r_k), row(lnx_w), row(lnx_b), row(ret_gn_w)
    wkv_params = (row(w0), w2p, row(a0), a2p, g2[l].astype(BF16), row(k_k), row(k_a), rk, lnw, lnb)
    bd_pair = _block_ones(PAIR, HEAD_A)
    bd2 = jnp.concatenate([bd_pair, bd_pair], axis=0)
    tri = (jnp.arange(WKV_CHUNK)[:, None] >= jnp.arange(WKV_CHUNK)[None, :]).astype(BF16)
    tri2 = jnp.concatenate([tri, tri], axis=1)

    m_s = bs * ts
    x1s, *f1 = _ffn_stream(x_sample, ng, ffn1_wg[l], ffn1_wu[l], ffn1_wd[l], 0, 1)

    xp = x_prompt.reshape(bp * tp, D_MODEL)
    x1p, *f2, win, wo = _ffn(xp, ng, *f1, 0, 1, FFN_ROWS,
                             cast=(ffn2_wg[l], ffn2_wu[l], ffn2_wd[l], w_in[l], w_out[l]))
    proj_consts = (ng, win, row(mu_shift))
    cos_p, sin_p = _rope_tables(np.arange(tp))
    (mixed, yr_p, ret_p, hl_p) = _proj(
        x1p, None, *proj_consts, cos_p, sin_p, n_t=1, rows_per_t=PROJ_ROWS, lag=1,
        tiles_per_seq=tp // PROJ_ROWS, ret_tables=(*_ret_tables(RET_CHUNK), gn))
    ya_p, wkv_p = _wkv_prompt(mixed.reshape(bp, tp, N_SHIFT), (*wkv_params, bd2, tri2),
                              nb=WKV_SEQS, tt=WKV_ROWS)
    ya_p = ya_p.reshape(bp * tp, D_A)
    yp = _ffn(x1p, ng, *f2, 4, 5, FFN_ROWS, mix=(ya_p, yr_p, wo))

    cos_s, sin_s = _rope_tables(PAST_LEN + np.arange(ts))
    rows_per_t = bs
    tab = lambda t: np.ascontiguousarray(np.broadcast_to(t[:, None, :], (ts, bs, HEAD_R)))
    outs = _proj(x1s.reshape(ts, bs, D_MODEL),
                 state_shift[l].reshape(bs // rows_per_t, rows_per_t, D_MODEL),
                 *proj_consts, tab(cos_s), tab(sin_s), n_t=ts, rows_per_t=rows_per_t,
                 lag=rows_per_t, tiles_per_seq=1)
    (mixed, q, kr, vr, gr, hl_s) = outs
    ya_s, wkv_s = _wkv_sample(mixed, *wkv_params, bd2, state_wkv[l].transpose(1, 2, 3, 0))
    wkv_s = wkv_s.transpose(3, 0, 1, 2)
    yr_s, ret_s = _ret_sample(q, kr, vr, gr, *_ret_tables(min(RET_CHUNK, ts)), gn, state_ret[l],
                              bb=RET_SEQS)
    ys = _ffn(x1s, ng, *f2, 4, 5, m_s,
              mix=(ya_s.reshape(m_s, D_A), yr_s.reshape(m_s, D_R), wo), out_seqs=bs)

    return (yp.reshape(bp, tp, D_MODEL), ys,
            hl_p.reshape(1, bp, D_MODEL), wkv_p[None], ret_p[None],
            hl_s.reshape(1, bs, D_MODEL), wkv_s[None], ret_s[None])
```

```python
import functools

import jax
import jax.numpy as jnp
import numpy as np
from jax import lax
from jax.experimental import pallas as pl
from jax.experimental.pallas import tpu as pltpu

F32 = jnp.float32
BF16 = jnp.bfloat16

D_MODEL = 1024
D_A = 512
HEAD_A = 64
H_A = D_A // HEAD_A
D_R = 512
H_R = 4
HEAD_R = D_R // H_R
LORA_W, LORA_A, LORA_G = 64, 64, 128
D_FF = 2816
RET_CHUNK = 128
ROPE_BASE = 10000.0
EPS = 1e-6
GN_EPS_A = 64e-5
GN_EPS_R = 1e-5
N_SHIFT = 3 * D_A + LORA_W + LORA_A + LORA_G
PAST_LEN = 16384

LANES = 128
SUBLANES = 8
VMEM_LIMIT = 52 * 1024 * 1024

MXU_DIM = 256
FF_SPLIT = 6 * MXU_DIM
WKV_CHUNK = 64
PAIR = 2 * HEAD_A

FFN_ROWS = 1024
PROJ_ROWS = 4 * RET_CHUNK
WKV_SEQS, WKV_ROWS = 4, 256
RET_SEQS = 2 * SUBLANES


def _nt(a, b):
    return lax.dot_general(a, b, (((1,), (1,)), ((), ())), preferred_element_type=F32)


def _tn(a, b):
    return lax.dot_general(a, b, (((0,), (0,)), ((), ())), preferred_element_type=F32)


def _mm(a, b):
    return jnp.dot(a, b, preferred_element_type=F32)


def _split_hi_lo(x):
    hi = x.astype(BF16)
    lo = (x - hi.astype(F32)).astype(BF16)
    return hi, lo


def _rms(x, g):
    return x * lax.rsqrt(jnp.mean(x * x, axis=-1, keepdims=True) + EPS) * g


def _softplus(x):
    return jnp.maximum(x, 0.0) + jnp.log(1.0 + jnp.exp(-jnp.abs(x)))


def _sigmoid(x):
    return 1.0 / (1.0 + jnp.exp(-x))


def _resident(shape):
    nd = len(shape)
    return pl.BlockSpec(shape, lambda *_: (0,) * nd, pipeline_mode=pl.Buffered(1))


def _params(n_axes):
    return pltpu.CompilerParams(dimension_semantics=("arbitrary",) * n_axes,
                                vmem_limit_bytes=VMEM_LIMIT)


def _ffn_kernel(with_mix, n_cast, g_in, g_out, *refs):
    n_in = (8 if with_mix else 5) + n_cast
    ins, outs = refs[:n_in], refs[n_in:]
    if with_mix:
        x_ref, ya_ref, yr_ref, wo_ref, ng_ref, wg_ref, wu_ref, wd_ref = ins[:8]
    else:
        x_ref, ng_ref, wg_ref, wu_ref, wd_ref = ins[:5]
    o_ref = outs[0]
    for src, dst in zip(ins[n_in - n_cast:], outs[1:]):
        dst[...] = src[...].astype(dst.dtype)
    half = x_ref.shape[0] // 2
    halves = [slice(0, half), slice(half, 2 * half)]
    x = [x_ref[r, :] for r in halves]
    if with_mix:
        mix = [_mm(ya_ref[r, :].astype(BF16), wo_ref[0:D_A, :])
               + _mm(yr_ref[r, :].astype(BF16), wo_ref[D_A:, :]) for r in halves]
        x = [xi + _rms(m, ng_ref[3:4, :]) for xi, m in zip(x, mix)]
    h = [_rms(xi, ng_ref[g_in:g_in + 1, :]).astype(BF16) for xi in x]
    for i, r in enumerate(halves):
        acc = None
        for cols in (slice(0, FF_SPLIT), slice(FF_SPLIT, D_FF)):
            gate = _mm(h[i], wg_ref[:, cols])
            up = _mm(h[i], wu_ref[:, cols])
            act = (gate * _sigmoid(gate) * up).astype(BF16)
            part = _mm(act, wd_ref[cols, :])
            acc = part if acc is None else acc + part
        out = x[i] + 0.5 * _rms(acc, ng_ref[g_out:g_out + 1, :])
        if o_ref.ndim == 2:
            o_ref[r, :] = out
        else:
            n_seq = o_ref.shape[0]
            for j in range(half // n_seq):
                o_ref[:, i * (half // n_seq) + j, :] = out[j * n_seq:(j + 1) * n_seq]


BF16_ROWS = 2 * SUBLANES


def _cast_spec(rows, cols, steps):
    rep = 1
    while (rows * rep) % steps or (rows * rep // steps) % BF16_ROWS:
        rep *= 2
    return pl.BlockSpec((rows * rep // steps, cols), lambda i: (i // rep, 0))


def _ffn(x, ng, wg, wu, wd, g_in, g_out, tm, mix=None, cast=(), out_seqs=None):
    m = x.shape[0]
    steps = m // tm
    assert out_seqs is None or (steps == 1 and (m // 2) % out_seqs == 0)
    o_shape = (m, D_MODEL) if out_seqs is None else (out_seqs, m // out_seqs, D_MODEL)
    row = lambda w: pl.BlockSpec((tm, w), lambda i: (i, 0))
    if mix is None:
        args = (x, ng, wg, wu, wd)
        specs = [row(D_MODEL), _resident(ng.shape), _resident(wg.shape), _resident(wu.shape),
                 _resident(wd.shape)]
    else:
        ya, yr, wo = mix
        args = (x, ya, yr, wo, ng, wg, wu, wd)
        specs = [row(D_MODEL), row(D_A), row(D_R), _resident(wo.shape), _resident(ng.shape),
                 _resident(wg.shape), _resident(wu.shape), _resident(wd.shape)]
    cast_specs = [_cast_spec(*w.shape, steps) for w in cast]
    out = pl.pallas_call(
        functools.partial(_ffn_kernel, mix is not None, len(cast), g_in, g_out),
        grid=(steps,),
        in_specs=specs + cast_specs,
        out_specs=[row(D_MODEL) if out_seqs is None else pl.BlockSpec(o_shape, lambda i: (0, 0, 0))]
        + cast_specs,
        out_shape=[jax.ShapeDtypeStruct(o_shape, F32)]
        + [jax.ShapeDtypeStruct(w.shape, BF16) for w in cast],
        compiler_params=_params(1),
        name="ffn_mix" if mix is not None else "ffn",
    )(*args, *cast)
    return out if cast else out[0]


def _ffn_stream_kernel(g_in, g_out, x_ref, ng_ref, wg_ref, wu_ref, wd_ref,
                       o_ref, wg_o, wu_o, wd_o, h_scr, acc_scr):
    c = pl.program_id(0)
    x_rows = lambda: jnp.concatenate([x_ref[:, t, :] for t in range(x_ref.shape[1])], axis=0)

    @pl.when(c == 0)
    def _():
        h_scr[...] = _rms(x_rows(), ng_ref[g_in:g_in + 1, :]).astype(BF16)
        acc_scr[...] = jnp.zeros_like(acc_scr)

    wg, wu, wd = wg_ref[...].astype(BF16), wu_ref[...].astype(BF16), wd_ref[...].astype(BF16)
    wg_o[...] = wg
    wu_o[...] = wu
    wd_o[...] = wd
    h = h_scr[...]
    gate = _mm(h, wg)
    act = (gate * _sigmoid(gate) * _mm(h, wu)).astype(BF16)
    acc_scr[...] += _mm(act, wd)

    @pl.when(c == pl.num_programs(0) - 1)
    def _():
        o_ref[...] = x_rows() + 0.5 * _rms(acc_scr[...], ng_ref[g_out:g_out + 1, :])


def _ffn_stream(x, ng, wg, wu, wd, g_in, g_out):
    m = x.shape[0] * x.shape[1]
    slab = MXU_DIM
    whole = lambda a: pl.BlockSpec(a.shape, lambda c: (0,) * a.ndim)
    cols = pl.BlockSpec((D_MODEL, slab), lambda c: (0, c))
    rows = pl.BlockSpec((slab, D_MODEL), lambda c: (c, 0))
    return pl.pallas_call(
        functools.partial(_ffn_stream_kernel, g_in, g_out),
        grid=(D_FF // slab,),
        in_specs=[whole(x), whole(ng), cols, cols, rows],
        out_specs=[pl.BlockSpec((m, D_MODEL), lambda c: (0, 0)), cols, cols, rows],
        out_shape=[jax.ShapeDtypeStruct((m, D_MODEL), F32)]
        + [jax.ShapeDtypeStruct(w.shape, BF16) for w in (wg, wu, wd)],
        scratch_shapes=[pltpu.VMEM((m, D_MODEL), BF16), pltpu.VMEM((m, D_MODEL), F32)],
        compiler_params=_params(1),
        name="ffn_stream",
    )(x, ng, wg, wu, wd)


def _lag_rows(cur, first, lag):
    if lag % SUBLANES == 0:
        return jnp.concatenate([first, cur[:-lag]], axis=0)
    assert lag == 1
    rolled = pltpu.roll(cur, 1, 0)
    rid = lax.broadcasted_iota(jnp.int32, (SUBLANES, cur.shape[1]), 0)
    head = jnp.where(rid == 0, first, rolled[0:SUBLANES])
    return jnp.concatenate([head, rolled[SUBLANES:]], axis=0)


def _proj_kernel(n_t, lag, tiles_per_seq, has_prev, fuse_ret, *refs):
    it = iter(refs)
    x_ref = next(it)
    prev_ref = next(it) if has_prev else None
    ng_ref, win_ref, mu_ref = next(it), next(it), next(it)
    if fuse_ret:
        dm_ref, qd_ref, kd_ref, cd_ref, gn_ref = (next(it) for _ in range(5))
        cos_ref, sin_ref, mix_o, yr_o, s_o, hl_o, carry_scr, s_scr = tuple(it)
    else:
        cos_ref, sin_ref, mix_o, q_o, kr_o, vr_o, gr_o, hl_o, carry_scr = tuple(it)

    def load(ref):
        if n_t == 1:
            return ref[...]
        return jnp.concatenate([ref[t] for t in range(n_t)], axis=0)

    def store(ref, val, cols=slice(None)):
        if n_t == 1:
            ref[:, cols] = val.astype(ref.dtype)
        else:
            rows = val.shape[0] // n_t
            for t in range(n_t):
                ref[t, :, cols] = val[t * rows:(t + 1) * rows].astype(ref.dtype)

    x = load(x_ref)
    tm = x.shape[0]
    h = _rms(x, ng_ref[2:3, :])
    hl_o[0] = h[tm - lag:, :]
    hb = h.astype(BF16)
    seq_start = (pl.program_id(0) % tiles_per_seq) == 0
    prev_b = prev_ref[0].astype(BF16) if has_prev else None

    ret = lambda c: _mm(hb, win_ref[:, N_SHIFT + c * D_R:N_SHIFT + (c + 1) * D_R])
    cos2 = load(cos_ref)
    sin2 = load(sin_ref)

    def rope(t):
        parts = []
        for hh in range(H_R):
            th = t[:, hh * HEAD_R:(hh + 1) * HEAD_R]
            parts.append(th * cos2 + pltpu.roll(th, HEAD_R // 2, 1) * sin2)
        return jnp.concatenate(parts, axis=1)

    if fuse_ret:
        rq = rope(ret(0)).astype(BF16)
        rk = (rope(ret(1)) * (HEAD_R ** -0.5)).astype(BF16)
        rv = ret(2).astype(BF16)
        rg = ret(3)
        heads = list(range(H_R))
        lanes = [slice(hh * HEAD_R, (hh + 1) * HEAD_R) for hh in heads]
        r_state = [jnp.where(seq_start, 0.0, s_scr[hh]) for hh in heads]

    for slab, c0 in enumerate(range(0, N_SHIFT, D_A)):
        cols = slice(c0, min(c0 + D_A, N_SHIFT))
        cur = _mm(hb, win_ref[:, cols])
        if has_prev:
            first = _mm(prev_b, win_ref[:, cols])
        else:
            first = jnp.zeros((lag, cur.shape[1]), F32)
        if tiles_per_seq > 1:
            first = jnp.where(seq_start, first, carry_scr[0:lag, cols])
            carry_scr[0:lag, cols] = cur[tm - lag:, :]
        prv = _lag_rows(cur, first, lag)
        store(mix_o, cur + (prv - cur) * mu_ref[:, cols], cols)
        if fuse_ret:
            rows = slice(slab * RET_CHUNK, (slab + 1) * RET_CHUNK)
            pick = lambda t: [t[rows, ln] for ln in lanes]
            y, r_state = _ret_chunk(pick(rq), pick(rk), pick(rv), pick(rg), r_state, heads,
                                    dm_ref, qd_ref, kd_ref, cd_ref, gn_ref)
            for hh in heads:
                yr_o[rows, lanes[hh]] = y[hh].astype(yr_o.dtype)

    if fuse_ret:
        for hh in heads:
            s_scr[hh] = r_state[hh]
            s_o[0, hh] = r_state[hh]
        return

    q = ret(0)
    kr = ret(1)
    store(q_o, rope(q))
    vr = ret(2)
    store(kr_o, rope(kr) * (HEAD_R ** -0.5))
    gr = ret(3)
    store(vr_o, vr)
    store(gr_o, gr)


def _proj(x, prev, ng, win, mu, cos2, sin2, *, n_t, rows_per_t, lag, tiles_per_seq,
          ret_tables=None):
    tm = n_t * rows_per_t
    fuse_ret = ret_tables is not None
    assert not fuse_ret or (n_t == 1 and tm == pl.cdiv(N_SHIFT, D_A) * RET_CHUNK)
    if n_t == 1:
        m = x.shape[0]
        n_tiles = m // tm
        row = lambda w: pl.BlockSpec((tm, w), lambda i: (i, 0))
        shp = lambda w, dt: jax.ShapeDtypeStruct((m, w), dt)
        tab = pl.BlockSpec((tm, HEAD_R), lambda i: (i % tiles_per_seq, 0))
    else:
        m = x.shape[0] * x.shape[1]
        n_tiles = x.shape[1] // rows_per_t
        row = lambda w: pl.BlockSpec((n_t, rows_per_t, w), lambda i: (0, i, 0))
        shp = lambda w, dt: jax.ShapeDtypeStruct((n_t, m // n_t, w), dt)
        tab = pl.BlockSpec((n_t, rows_per_t, HEAD_R), lambda i: (0, i, 0))
    n_seq = n_tiles // tiles_per_seq
    hl_spec = pl.BlockSpec((1, lag, D_MODEL), lambda i: (i // tiles_per_seq, 0, 0))
    args = [x]
    specs = [row(D_MODEL)]
    if prev is not None:
        args.append(prev)
        specs.append(pl.BlockSpec((1, lag, D_MODEL), lambda i: (i // tiles_per_seq, 0, 0)))
    consts = (ng, win, mu) + (tuple(ret_tables) if fuse_ret else ())
    args += list(consts) + [cos2, sin2]
    specs += [_resident(c.shape) for c in consts] + [tab, tab]
    hl_shape = jax.ShapeDtypeStruct((n_seq, lag, D_MODEL), F32)
    scratch = [pltpu.VMEM((max(lag, SUBLANES), N_SHIFT), F32)]
    if fuse_ret:
        out_shape = [shp(N_SHIFT, F32), shp(D_R, BF16),
                     jax.ShapeDtypeStruct((n_seq, H_R, HEAD_R, HEAD_R), F32), hl_shape]
        out_specs = [row(N_SHIFT), row(D_R),
                     pl.BlockSpec((1, H_R, HEAD_R, HEAD_R), lambda i: (i // tiles_per_seq, 0, 0, 0)),
                     hl_spec]
        scratch.append(pltpu.VMEM((H_R, HEAD_R, HEAD_R), F32))
    else:
        out_shape = [shp(N_SHIFT, F32)] + [shp(D_R, F32)] * 4 + [hl_shape]
        out_specs = [row(N_SHIFT)] + [row(D_R)] * 4 + [hl_spec]
    return pl.pallas_call(
        functools.partial(_proj_kernel, n_t, lag, tiles_per_seq, prev is not None, fuse_ret),
        grid=(n_tiles,),
        in_specs=specs,
        out_specs=out_specs,
        out_shape=out_shape,
        scratch_shapes=scratch,
        compiler_params=_params(1),
        name="proj",
    )(*args)


def _sums_stacked(x, bd2):
    return _mm(jnp.concatenate(_split_hi_lo(x), axis=1), bd2)


def _wkv_post(y, r, k, v, g, rk, lw_g, lb_g, head_sums):
    inv_n = 1.0 / HEAD_A
    mu = head_sums(y) * inv_n
    d = y - mu
    var = head_sums(d * d) * inv_n
    yn = d * lax.rsqrt(var + GN_EPS_A) * lw_g + lb_g
    bonus = head_sums(r * k * rk) * v
    return (yn + bonus) * g


def _wkv_decay_gate(wa, gd, w0, w2p, a0, a2p, g2):
    w_pre = w0 + _mm(jnp.tanh(wa).astype(BF16), w2p)
    lw = -jnp.exp(-_softplus(-w_pre) - 0.5)
    a = _sigmoid(a0 + _mm(wa.astype(BF16), a2p))
    g = _mm(_sigmoid(gd).astype(BF16), g2)
    return lw, a, g


def _wkv_keys(k, a, k_k, k_a, head_sums):
    kk = k * k_k
    kk = kk / jnp.maximum(jnp.sqrt(head_sums(kk * kk)), 1e-12)
    return k * (1.0 + (a - 1.0) * k_a), kk, kk * a


def _wkv_prompt_kernel(nb, tt, mix_ref, w0_ref, w2_ref, a0_ref, a2_ref, g2_ref, kk_ref, ka_ref,
                       rk_ref, lnw_ref, lnb_ref, bd2_ref, tri2_ref,
                       ya_o, s_o, s_scr):
    c = WKV_CHUNK
    j = pl.program_id(1)

    @pl.when(j == 0)
    def _():
        s_scr[...] = jnp.zeros_like(s_scr)

    lane = lax.broadcasted_iota(jnp.int32, (c, PAIR), 1)
    first = lane < HEAD_A

    def stack(x):
        return jnp.concatenate([jnp.where(first, x, 0.0), jnp.where(first, 0.0, x)], axis=0)

    ri = lax.broadcasted_iota(jnp.int32, (2 * c, 2 * c), 0) % c
    ci = lax.broadcasted_iota(jnp.int32, (2 * c, 2 * c), 1) % c
    strict = ri > ci
    incl = ri >= ci
    row_i = lax.broadcasted_iota(jnp.int32, (2 * c, 2 * c), 0)
    col_i = lax.broadcasted_iota(jnp.int32, (2 * c, 2 * c), 1)
    eye = (row_i == col_i).astype(F32)
    blk8 = (row_i // SUBLANES) == (col_i // SUBLANES)
    off_mask = {sz: ((row_i // (2 * sz)) == (col_i // (2 * sz))) & ((row_i % (2 * sz)) >= sz)
                & ((col_i % (2 * sz)) < sz) for sz in (8, 16, 32)}
    bd2 = bd2_ref[...]
    tri2 = tri2_ref[...]
    head_sums = functools.partial(_sums_stacked, bd2=bd2)

    n_pairs = H_A // 2
    units = [(bi, p) for bi in range(nb) for p in range(n_pairs)]
    n_u = range(len(units))
    slab = [slice(p * PAIR, (p + 1) * PAIR) for p in range(n_pairs)]

    tall = lambda xs: jnp.concatenate(xs, axis=0)
    per_pair = lambda ref: tall([jnp.broadcast_to(ref[:, slab[p]], (c, PAIR)) for _, p in units])
    split = lambda x: [x[u * c:(u + 1) * c] for u in n_u]

    def prep(ch):
        rows = slice(ch * c, (ch + 1) * c)
        ld = lambda c0: [mix_ref[bi, rows, c0 + p * PAIR:c0 + (p + 1) * PAIR] for bi, p in units]
        r, k_raw, v = ld(0), ld(D_A), ld(2 * D_A)
        lora = [_wkv_decay_gate(mix_ref[bi, rows, 3 * D_A:3 * D_A + LORA_W + LORA_A],
                                mix_ref[bi, rows, 3 * D_A + LORA_W + LORA_A:N_SHIFT],
                                w0_ref[...], w2_ref[...], a0_ref[...], a2_ref[...], g2_ref[...])
                for bi in range(nb)]
        lw_all = [x[0] for x in lora]
        a = tall([lora[bi][1][:, slab[p]] for bi, p in units])
        gate = tall([lora[bi][2][:, slab[p]] for bi, p in units])
        k, kk, kka = map(split, _wkv_keys(tall(k_raw), a, per_pair(kk_ref), per_pair(ka_ref),
                                          head_sums))
        return r, v, k, kk, kka, gate, lw_all

    def gram_stage(pre):
        r, v, k, kk, kka, gate, lw_all = pre
        cum_all = [_mm(tri2, jnp.concatenate(_split_hi_lo(x), axis=0)) for x in lw_all]
        lw = [lw_all[bi][:, slab[p]] for bi, p in units]
        cum = [cum_all[bi][:, slab[p]] for bi, p in units]
        cum_end = [x[c - 1:c, :] for x in cum]
        e_pos = [jnp.exp(x) for x in cum]
        e_neg = [jnp.exp(-x) for x in cum]
        e_end = [jnp.exp(x - y) for x, y in zip(cum_end, cum)]
        xs = [jnp.concatenate([stack(-kk[u] * jnp.exp(cum[u] - lw[u])), stack(r[u] * e_pos[u])],
                              axis=0).astype(BF16) for u in n_u]
        ws = [jnp.concatenate([stack(kka[u] * e_neg[u]), stack(k[u] * e_neg[u])],
                              axis=0).astype(BF16) for u in n_u]
        we = [jnp.concatenate([stack(kka[u] * e_end[u]), stack(k[u] * e_end[u])],
                              axis=0).astype(BF16) for u in n_u]
        vs = [stack(x).astype(BF16) for x in v]
        gram = [_nt(xs[u], ws[u]) for u in n_u]
        a_ab = [jnp.where(strict, g[0:2 * c, 0:2 * c], 0.0) for g in gram]
        a_ak = [jnp.where(strict, g[0:2 * c, 2 * c:], 0.0).astype(BF16) for g in gram]
        a_r = [jnp.concatenate([jnp.where(incl, g[2 * c:, 0:2 * c], 0.0),
                                jnp.where(incl, g[2 * c:, 2 * c:], 0.0)], axis=1).astype(BF16)
               for g in gram]
        return xs, we, vs, cum_end, a_ab, a_ak, a_r

    def solve_stage(ch, pre, gs):
        rows = slice(ch * c, (ch + 1) * c)
        r, v, k, kk, kka, gate, lw_all = pre
        xs, we, vs, cum_end, a_ab, a_ak, a_r = gs
        a8 = [jnp.where(blk8, a, 0.0) for a in a_ab]
        inv = [eye + a for a in a8]
        pw = [x.astype(BF16) for x in a8]
        pw = [_mm(x, x).astype(BF16) for x in pw]
        both = [_mm(x, jnp.concatenate([x, i.astype(BF16)], axis=1)) for i, x in zip(inv, pw)]
        inv = [i + b[:, 2 * c:] for i, b in zip(inv, both)]
        inv = [i + _mm(b[:, 0:2 * c].astype(BF16), i.astype(BF16)) for i, b in zip(inv, both)]
        size = SUBLANES
        while size < c:
            lows = [slice(k + size, k + 2 * size) for k in range(0, 2 * c, 2 * size)]
            ups = [slice(k, k + size) for k in range(0, 2 * c, 2 * size)]
            off = [jnp.where(off_mask[size], a, 0.0).astype(BF16) for a in a_ab]
            d_low = [jnp.concatenate([i[r] for r in lows], axis=0) for i in inv]
            x_low = [_mm(d.astype(BF16), o).astype(BF16) for d, o in zip(d_low, off)]
            new_low = [d + _mm(x, i.astype(BF16)) for d, x, i in zip(d_low, x_low, inv)]
            inv = [jnp.concatenate([piece for k, r in enumerate(ups)
                                    for piece in (i[r], nl[k * size:(k + 1) * size])], axis=0)
                   for i, nl in zip(inv, new_low)]
            size *= 2
        s = [s_scr[u] for u in n_u]
        z = [_nt(xs[u], s[u].astype(BF16)) for u in n_u]
        rhs = [z[u][0:2 * c] + _mm(a_ak[u], vs[u]) for u in n_u]
        uu = [_mm(inv[u].astype(BF16), rhs[u].astype(BF16)).astype(BF16) for u in n_u]
        uv = [jnp.concatenate([uu[u], vs[u]], axis=0) for u in n_u]
        y2 = [z[u][2 * c:] + _mm(a_r[u], uv[u]) for u in n_u]
        for u in n_u:
            s_scr[u] = s[u] * jnp.exp(cum_end[u]) + _tn(uv[u], we[u])
        out = _wkv_post(tall([y2[u][0:c] + y2[u][c:] for u in n_u]), tall(r), tall(k), tall(v),
                        gate, per_pair(rk_ref), per_pair(lnw_ref), per_pair(lnb_ref), head_sums)
        for u, (bi, p) in enumerate(units):
            ya_o[bi, rows, slab[p]] = out[u * c:(u + 1) * c].astype(ya_o.dtype)

    n_chunks = tt // c
    pre = prep(0)
    for ch in range(n_chunks):
        gs = gram_stage(pre)
        nxt = prep(ch + 1) if ch + 1 < n_chunks else None
        solve_stage(ch, pre, gs)
        pre = nxt

    @pl.when(j == pl.num_programs(1) - 1)
    def _():
        for u, (bi, p) in enumerate(units):
            s = s_scr[u]
            s_o[bi, 2 * p] = s[0:HEAD_A, 0:HEAD_A]
            s_o[bi, 2 * p + 1] = s[HEAD_A:, HEAD_A:]


def _wkv_prompt(mixed, consts, *, nb, tt):
    batch, seq, _ = mixed.shape
    blk = lambda w: pl.BlockSpec((nb, tt, w), lambda b, j: (b, j, 0))
    return pl.pallas_call(
        functools.partial(_wkv_prompt_kernel, nb, tt),
        grid=(batch // nb, seq // tt),
        in_specs=[blk(N_SHIFT)] + [_resident(x.shape) for x in consts],
        out_specs=[blk(D_A), pl.BlockSpec((nb, H_A, HEAD_A, HEAD_A), lambda b, j: (b, 0, 0, 0))],
        out_shape=[jax.ShapeDtypeStruct((batch, seq, D_A), BF16),
                   jax.ShapeDtypeStruct((batch, H_A, HEAD_A, HEAD_A), F32)],
        scratch_shapes=[pltpu.VMEM((nb * H_A // 2, PAIR, PAIR), F32)],
        compiler_params=_params(2),
        name="wkv_prompt",
    )(mixed, *consts)


def _wkv_sample_kernel(n_t, r_ref, k_ref, v_ref, lora_ref,
                       w0_ref, w2_ref, a0_ref, a2_ref, g2_ref, kk_ref, ka_ref,
                       rk_ref, lnw_ref, lnb_ref, bd2_ref, s_ref,
                       ya_o, s_o, yt_scr):
    n = HEAD_A
    n_b = r_ref.shape[1]
    tall = lambda xs: jnp.concatenate(xs, axis=0)
    rows_of = lambda ref: tall([ref[t] for t in range(n_t)])
    head_sums = functools.partial(_sums_stacked, bd2=bd2_ref[...])
    r, v, lora = rows_of(r_ref), rows_of(v_ref), rows_of(lora_ref)
    lw, a, gate = _wkv_decay_gate(lora[:, 0:LORA_W + LORA_A], lora[:, LORA_W + LORA_A:],
                                  w0_ref[...], w2_ref[...], a0_ref[...], a2_ref[...], g2_ref[...])
    k, kk, kka = _wkv_keys(rows_of(k_ref), a, kk_ref[...], ka_ref[...], head_sums)

    tr = lambda x: [x[t * n_b:(t + 1) * n_b].T for t in range(n_t)]
    nkk_t, kka_t, k_t, r_t, v_t, w_t = tr(-kk), tr(kka), tr(k), tr(r), tr(v), tr(jnp.exp(lw))
    rid = lax.broadcasted_iota(jnp.int32, (SUBLANES, n_b), 0)

    for hh in range(2):
        keys = slice(hh * n, (hh + 1) * n)
        for ig in range(n // SUBLANES):
            y_tiles = [jnp.zeros((SUBLANES, n_b), F32) for _ in range(n_t)]
            for ii in range(SUBLANES):
                i = ig * SUBLANES + ii
                s = s_ref[hh, i]
                for t in range(n_t):
                    sa = jnp.sum(s * nkk_t[t][keys], axis=0, keepdims=True)
                    v_row = v_t[t][hh * n + i:hh * n + i + 1]
                    s = s * w_t[t][keys] + sa * kka_t[t][keys] + v_row * k_t[t][keys]
                    y_row = jnp.sum(s * r_t[t][keys], axis=0, keepdims=True)
                    y_tiles[t] = jnp.where(rid == ii, y_row, y_tiles[t])
                s_o[hh, i] = s
            for t in range(n_t):
                yt_scr[t, hh * n + ig * SUBLANES:hh * n + (ig + 1) * SUBLANES, :] = y_tiles[t]

    out = _wkv_post(tall([yt_scr[t].T for t in range(n_t)]), r, k, v, gate,
                    rk_ref[...], lnw_ref[...], lnb_ref[...], head_sums)
    for t in range(n_t):
        ya_o[t] = out[t * n_b:(t + 1) * n_b].astype(ya_o.dtype)


def _wkv_sample(mixed, w0, w2p, a0, a2p, g2, k_k, k_a, rk, lnw, lnb, bd2, s0):
    n_t, n_b, _ = mixed.shape
    slab = lambda first: pl.BlockSpec((n_t, n_b, PAIR), lambda p: (0, 0, first + p))
    lora = pl.BlockSpec((n_t, n_b, 2 * PAIR), lambda p: (0, 0, 3 * D_A // (2 * PAIR)))
    par = pl.BlockSpec((1, PAIR), lambda p: (0, p))
    low = pl.BlockSpec((PAIR, PAIR), lambda p: (0, p))
    st = pl.BlockSpec((2, HEAD_A, HEAD_A, n_b), lambda p: (p, 0, 0, 0))
    n_slab = D_A // PAIR
    return pl.pallas_call(
        functools.partial(_wkv_sample_kernel, n_t),
        grid=(H_A // 2,),
        in_specs=[slab(0), slab(n_slab), slab(2 * n_slab), lora,
                  par, low, par, low, low, par, par, par, par, par, _resident(bd2.shape), st],
        out_specs=[slab(0), st],
        out_shape=[jax.ShapeDtypeStruct((n_t, n_b, D_A), F32),
                   jax.ShapeDtypeStruct(s0.shape, F32)],
        scratch_shapes=[pltpu.VMEM((n_t, PAIR, n_b), F32)],
        compiler_params=_params(1),
        name="wkv_sample",
    )(mixed, mixed, mixed, mixed, w0, w2p, a0, a2p, g2, k_k, k_a, rk, lnw, lnb, bd2, s0)


def _ret_chunk(q, k, v, g, s, heads, dm_ref, qd_ref, kd_ref, cd_ref, gn_ref):
    n = range(len(q))
    qb = [x.astype(BF16) for x in q]
    kb = [x.astype(BF16) for x in k]
    vb = [x.astype(BF16) for x in v]
    inner = [(_nt(qb[u], kb[u]) * dm_ref[heads[u]]).astype(BF16) for u in n]
    q_dec = [(q[u].astype(F32) * qd_ref[heads[u]]).astype(BF16) for u in n]
    k_dec = [(k[u].astype(F32) * kd_ref[heads[u]]).astype(BF16) for u in n]
    if inner[0].shape[1] % LANES == 0:
        y = [_mm(jnp.concatenate([inner[u], q_dec[u]], axis=1),
                 jnp.concatenate([vb[u], s[u].astype(BF16)], axis=0)) for u in n]
    else:
        y = [_mm(inner[u], vb[u]) + _mm(q_dec[u], s[u].astype(BF16)) for u in n]
    s_new = [s[u] * cd_ref[heads[u]] + _tn(k_dec[u], vb[u]) for u in n]
    out = []
    for u in n:
        mu = jnp.mean(y[u], axis=-1, keepdims=True)
        d = y[u] - mu
        var = jnp.mean(d * d, axis=-1, keepdims=True)
        lanes = slice(heads[u] * HEAD_R, (heads[u] + 1) * HEAD_R)
        yn = d * lax.rsqrt(var + GN_EPS_R) * gn_ref[:, lanes]
        out.append(g[u] * _sigmoid(g[u]) * yn)
    return out, s_new


def _ret_sample_kernel(n_t, bb, q_ref, k_ref, v_ref, g_ref, dm_ref, qd_ref, kd_ref, cd_ref,
                       gn_ref, s_ref, y_o, s_o):
    rid = lax.broadcasted_iota(jnp.int32, (SUBLANES, HEAD_R), 0)
    units = [(bi, hh) for bi in range(bb) for hh in range(H_R)]
    heads = [hh for _, hh in units]
    lanes = [slice(hh * HEAD_R, (hh + 1) * HEAD_R) for hh in heads]

    def seq_rows(ref):
        outs = []
        for u, (bi, _) in enumerate(units):
            out = jnp.zeros((SUBLANES, HEAD_R), F32)
            for t in range(n_t):
                out = jnp.where(rid == t, jnp.broadcast_to(ref[t, bi:bi + 1, lanes[u]], out.shape), out)
            outs.append(out)
        return outs

    y, s_new = _ret_chunk(seq_rows(q_ref), seq_rows(k_ref), seq_rows(v_ref), seq_rows(g_ref),
                          [s_ref[bi, hh] for bi, hh in units], heads,
                          dm_ref, qd_ref, kd_ref, cd_ref, gn_ref)
    for u, (bi, hh) in enumerate(units):
        s_o[bi, hh] = s_new[u]
        for t in range(n_t):
            y_o[t, bi:bi + 1, lanes[u]] = y[u][t:t + 1].astype(y_o.dtype)


def _ret_sample(q, k, v, g, dm, qd, kd, cd, gn, s0, *, bb):
    n_t, n_b, _ = q.shape
    consts = (dm, qd, kd, cd, gn)
    blk = pl.BlockSpec((n_t, bb, D_R), lambda i: (0, i, 0))
    st = pl.BlockSpec((bb, H_R, HEAD_R, HEAD_R), lambda i: (i, 0, 0, 0))
    return pl.pallas_call(
        functools.partial(_ret_sample_kernel, n_t, bb),
        grid=(n_b // bb,),
        in_specs=[blk] * 4 + [_resident(x.shape) for x in consts] + [st],
        out_specs=[blk, st],
        out_shape=[jax.ShapeDtypeStruct((n_t, n_b, D_R), F32),
                   jax.ShapeDtypeStruct(s0.shape, F32)],
        compiler_params=_params(1),
        name="ret_sample",
    )(q, k, v, g, *consts, s0)


def _rope_tables(pos):
    half = HEAD_R // 2
    inv = ROPE_BASE ** (-np.arange(half, dtype=np.float64) / half)
    ang = np.asarray(pos, np.float64)[:, None] * inv[None, :]
    cos, sin = np.cos(ang), np.sin(ang)
    return (np.concatenate([cos, cos], axis=1).astype(np.float32),
            np.concatenate([-sin, sin], axis=1).astype(np.float32))


def _ret_tables(c):
    lg = np.log1p(-np.exp2(-5.0 - np.arange(H_R, dtype=np.float64)))
    idx = np.arange(c, dtype=np.float64)
    diff = idx[:, None] - idx[None, :]
    dmask = np.where(diff >= 0, np.exp(lg[:, None, None] * np.maximum(diff, 0.0)), 0.0)
    ones = np.ones((1, 1, HEAD_R))
    qdec = np.exp(lg[:, None] * (idx + 1.0))[:, :, None] * ones
    kdec = np.exp(lg[:, None] * (c - 1.0 - idx))[:, :, None] * ones
    cdec = np.exp(lg * c)[:, None, None] * ones
    extra = -c % SUBLANES
    dmask = np.pad(dmask, ((0, 0), (0, extra), (0, extra)))
    qdec = np.pad(qdec, ((0, 0), (0, extra), (0, 0)))
    kdec = np.pad(kdec, ((0, 0), (0, extra), (0, 0)))
    return tuple(t.astype(np.float32) for t in (dmask, qdec, kdec, cdec))


def _block_ones(n, block):
    idx = jnp.arange(n) // block
    return (idx[:, None] == idx[None, :]).astype(BF16)


def kernel(x_prompt, x_sample, state_shift, state_wkv, state_ret, norm_g, ffn1_wg, ffn1_wu, ffn1_wd,
           w_in, mu_shift, w0, w2, a0, a2, g2, k_k, k_a, r_k, lnx_w, lnx_b, ret_gn_w, w_out,
           ffn2_wg, ffn2_wu, ffn2_wd):
    assert norm_g.shape[0] == 1, "single-layer configuration"
    bp, tp, _ = x_prompt.shape
    bs, ts, _ = x_sample.shape
    l = 0
    ng = norm_g[l]
    row = lambda t: t[l].reshape(1, -1)
    zpad = jnp.zeros((LORA_W, D_A), BF16)
    w2p = jnp.concatenate([w2[l].astype(BF16), zpad], axis=0)
    a2p = jnp.concatenate([zpad, a2[l].astype(BF16)], axis=0)
    rk, lnw, lnb, gn = row(r_k), row(lnx_w), row(lnx_b), row(ret_gn_w)
    wkv_params = (row(w0), w2p, row(a0), a2p, g2[l].astype(BF16), row(k_k), row(k_a), rk, lnw, lnb)
    bd_pair = _block_ones(PAIR, HEAD_A)
    bd2 = jnp.concatenate([bd_pair, bd_pair], axis=0)
    tri = (jnp.arange(WKV_CHUNK)[:, None] >= jnp.arange(WKV_CHUNK)[None, :]).astype(BF16)
    tri2 = jnp.concatenate([tri, tri], axis=1)

    m_s = bs * ts
    x1s, *f1 = _ffn_stream(x_sample, ng, ffn1_wg[l], ffn1_wu[l], ffn1_wd[l], 0, 1)

    xp = x_prompt.reshape(bp * tp, D_MODEL)
    x1p, *f2, win, wo = _ffn(xp, ng, *f1, 0, 1, FFN_ROWS,
                             cast=(ffn2_wg[l], ffn2_wu[l], ffn2_wd[l], w_in[l], w_out[l]))
    proj_consts = (ng, win, row(mu_shift))
    cos_p, sin_p = _rope_tables(np.arange(tp))
    (mixed, yr_p, ret_p, hl_p) = _proj(
        x1p, None, *proj_consts, cos_p, sin_p, n_t=1, rows_per_t=PROJ_ROWS, lag=1,
        tiles_per_seq=tp // PROJ_ROWS, ret_tables=(*_ret_tables(RET_CHUNK), gn))
    ya_p, wkv_p = _wkv_prompt(mixed.reshape(bp, tp, N_SHIFT), (*wkv_params, bd2, tri2),
                              nb=WKV_SEQS, tt=WKV_ROWS)
    ya_p = ya_p.reshape(bp * tp, D_A)
    yp = _ffn(x1p, ng, *f2, 4, 5, FFN_ROWS, mix=(ya_p, yr_p, wo))

    cos_s, sin_s = _rope_tables(PAST_LEN + np.arange(ts))
    rows_per_t = bs
    tab = lambda t: np.ascontiguousarray(np.broadcast_to(t[:, None, :], (ts, bs, HEAD_R)))
    outs = _proj(x1s.reshape(ts, bs, D_MODEL),
                 state_shift[l].reshape(bs // rows_per_t, rows_per_t, D_MODEL),
                 *proj_consts, tab(cos_s), tab(sin_s), n_t=ts, rows_per_t=rows_per_t,
                 lag=rows_per_t, tiles_per_seq=1)
    (mixed, q, kr, vr, gr, hl_s) = outs
    ya_s, wkv_s = _wkv_sample(mixed, *wkv_params, bd2, state_wkv[l].transpose(1, 2, 3, 0))
    wkv_s = wkv_s.transpose(3, 0, 1, 2)
    yr_s, ret_s = _ret_sample(q, kr, vr, gr, *_ret_tables(min(RET_CHUNK, ts)), gn, state_ret[l],
                              bb=RET_SEQS)
    ys = _ffn(x1s, ng, *f2, 4, 5, m_s,
              mix=(ya_s.reshape(m_s, D_A), yr_s.reshape(m_s, D_R), wo), out_seqs=bs)

    return (yp.reshape(bp, tp, D_MODEL), ys,
            hl_p.reshape(1, bp, D_MODEL), wkv_p[None], ret_p[None],
            hl_s.reshape(1, bs, D_MODEL), wkv_s[None], ret_s[None])
```

```python
import functools

import jax
import jax.numpy as jnp
import numpy as np
from jax import lax
from jax.experimental import pallas as pl
from jax.experimental.pallas import tpu as pltpu

F32 = jnp.float32
BF16 = jnp.bfloat16

D_MODEL = 1024
D_A = 512
HEAD_A = 64
H_A = D_A // HEAD_A
D_R = 512
H_R = 4
HEAD_R = D_R // H_R
LORA_W, LORA_A, LORA_G = 64, 64, 128
D_FF = 2816
RET_CHUNK = 128
ROPE_BASE = 10000.0
EPS = 1e-6
GN_EPS_A = 64e-5
GN_EPS_R = 1e-5
N_SHIFT = 3 * D_A + LORA_W + LORA_A + LORA_G
PAST_LEN = 16384

LANES = 128
SUBLANES = 8
VMEM_LIMIT = 52 * 1024 * 1024

MXU_DIM = 256
FF_SPLIT = 6 * MXU_DIM
WKV_CHUNK = 64
PAIR = 2 * HEAD_A

FFN_ROWS = 1024
PROJ_ROWS = 4 * RET_CHUNK
WKV_SEQS, WKV_ROWS = 4, 256
RET_SEQS = 2 * SUBLANES


def _nt(a, b):
    return lax.dot_general(a, b, (((1,), (1,)), ((), ())), preferred_element_type=F32)


def _tn(a, b):
    return lax.dot_general(a, b, (((0,), (0,)), ((), ())), preferred_element_type=F32)


def _mm(a, b):
    return jnp.dot(a, b, preferred_element_type=F32)


def _split_hi_lo(x):
    hi = x.astype(BF16)
    lo = (x - hi.astype(F32)).astype(BF16)
    return hi, lo


def _rms(x, g):
    return x * lax.rsqrt(jnp.mean(x * x, axis=-1, keepdims=True) + EPS) * g


def _softplus(x):
    return jnp.maximum(x, 0.0) + jnp.log(1.0 + jnp.exp(-jnp.abs(x)))


def _sigmoid(x):
    return 1.0 / (1.0 + jnp.exp(-x))


def _resident(shape):
    nd = len(shape)
    return pl.BlockSpec(shape, lambda *_: (0,) * nd, pipeline_mode=pl.Buffered(1))


def _params(n_axes):
    return pltpu.CompilerParams(dimension_semantics=("arbitrary",) * n_axes,
                                vmem_limit_bytes=VMEM_LIMIT)


def _ffn_kernel(with_mix, n_cast, g_in, g_out, *refs):
    n_in = (8 if with_mix else 5) + n_cast
    ins, outs = refs[:n_in], refs[n_in:]
    if with_mix:
        x_ref, ya_ref, yr_ref, wo_ref, ng_ref, wg_ref, wu_ref, wd_ref = ins[:8]
    else:
        x_ref, ng_ref, wg_ref, wu_ref, wd_ref = ins[:5]
    o_ref = outs[0]
    for src, dst in zip(ins[n_in - n_cast:], outs[1:]):
        dst[...] = src[...].astype(dst.dtype)
    half = x_ref.shape[0] // 2
    halves = [slice(0, half), slice(half, 2 * half)]
    x = [x_ref[r, :] for r in halves]
    if with_mix:
        mix = [_mm(ya_ref[r, :].astype(BF16), wo_ref[0:D_A, :])
               + _mm(yr_ref[r, :].astype(BF16), wo_ref[D_A:, :]) for r in halves]
        x = [xi + _rms(m, ng_ref[3:4, :]) for xi, m in zip(x, mix)]
    h = [_rms(xi, ng_ref[g_in:g_in + 1, :]).astype(BF16) for xi in x]
    for i, r in enumerate(halves):
        acc = None
        for cols in (slice(0, FF_SPLIT), slice(FF_SPLIT, D_FF)):
            gate = _mm(h[i], wg_ref[:, cols])
            up = _mm(h[i], wu_ref[:, cols])
            act = (gate * _sigmoid(gate) * up).astype(BF16)
            part = _mm(act, wd_ref[cols, :])
            acc = part if acc is None else acc + part
        out = x[i] + 0.5 * _rms(acc, ng_ref[g_out:g_out + 1, :])
        if o_ref.ndim == 2:
            o_ref[r, :] = out
        else:
            n_seq = o_ref.shape[0]
            for j in range(half // n_seq):
                o_ref[:, i * (half // n_seq) + j, :] = out[j * n_seq:(j + 1) * n_seq]


BF16_ROWS = 2 * SUBLANES


def _cast_spec(rows, cols, steps):
    rep = 1
    while (rows * rep) % steps or (rows * rep // steps) % BF16_ROWS:
        rep *= 2
    return pl.BlockSpec((rows * rep // steps, cols), lambda i: (i // rep, 0))


def _ffn(x, ng, wg, wu, wd, g_in, g_out, tm, mix=None, cast=(), out_seqs=None):
    m = x.shape[0]
    steps = m // tm
    assert out_seqs is None or (steps == 1 and (m // 2) % out_seqs == 0)
    o_shape = (m, D_MODEL) if out_seqs is None else (out_seqs, m // out_seqs, D_MODEL)
    row = lambda w: pl.BlockSpec((tm, w), lambda i: (i, 0))
    if mix is None:
        args = (x, ng, wg, wu, wd)
        specs = [row(D_MODEL), _resident(ng.shape), _resident(wg.shape), _resident(wu.shape),
                 _resident(wd.shape)]
    else:
        ya, yr, wo = mix
        args = (x, ya, yr, wo, ng, wg, wu, wd)
        specs = [row(D_MODEL), row(D_A), row(D_R), _resident(wo.shape), _resident(ng.shape),
                 _resident(wg.shape), _resident(wu.shape), _resident(wd.shape)]
    cast_specs = [_cast_spec(*w.shape, steps) for w in cast]
    out = pl.pallas_call(
        functools.partial(_ffn_kernel, mix is not None, len(cast), g_in, g_out),
        grid=(steps,),
        in_specs=specs + cast_specs,
        out_specs=[row(D_MODEL) if out_seqs is None else pl.BlockSpec(o_shape, lambda i: (0, 0, 0))]
        + cast_specs,
        out_shape=[jax.ShapeDtypeStruct(o_shape, F32)]
        + [jax.ShapeDtypeStruct(w.shape, BF16) for w in cast],
        compiler_params=_params(1),
        name="ffn_mix" if mix is not None else "ffn",
    )(*args, *cast)
    return out if cast else out[0]


def _ffn_stream_kernel(g_in, g_out, x_ref, ng_ref, wg_ref, wu_ref, wd_ref,
                       o_ref, wg_o, wu_o, wd_o, h_scr, acc_scr):
    c = pl.program_id(0)
    x_rows = lambda: jnp.concatenate([x_ref[:, t, :] for t in range(x_ref.shape[1])], axis=0)

    @pl.when(c == 0)
    def _():
        h_scr[...] = _rms(x_rows(), ng_ref[g_in:g_in + 1, :]).astype(BF16)
        acc_scr[...] = jnp.zeros_like(acc_scr)

    wg, wu, wd = wg_ref[...].astype(BF16), wu_ref[...].astype(BF16), wd_ref[...].astype(BF16)
    wg_o[...] = wg
    wu_o[...] = wu
    wd_o[...] = wd
    h = h_scr[...]
    gate = _mm(h, wg)
    act = (gate * _sigmoid(gate) * _mm(h, wu)).astype(BF16)
    acc_scr[...] += _mm(act, wd)

    @pl.when(c == pl.num_programs(0) - 1)
    def _():
        o_ref[...] = x_rows() + 0.5 * _rms(acc_scr[...], ng_ref[g_out:g_out + 1, :])


def _ffn_stream(x, ng, wg, wu, wd, g_in, g_out):
    m = x.shape[0] * x.shape[1]
    slab = MXU_DIM
    whole = lambda a: pl.BlockSpec(a.shape, lambda c: (0,) * a.ndim)
    cols = pl.BlockSpec((D_MODEL, slab), lambda c: (0, c))
    rows = pl.BlockSpec((slab, D_MODEL), lambda c: (c, 0))
    return pl.pallas_call(
        functools.partial(_ffn_stream_kernel, g_in, g_out),
        grid=(D_FF // slab,),
        in_specs=[whole(x), whole(ng), cols, cols, rows],
        out_specs=[pl.BlockSpec((m, D_MODEL), lambda c: (0, 0)), cols, cols, rows],
        out_shape=[jax.ShapeDtypeStruct((m, D_MODEL), F32)]
        + [jax.ShapeDtypeStruct(w.shape, BF16) for w in (wg, wu, wd)],
        scratch_shapes=[pltpu.VMEM((m, D_MODEL), BF16), pltpu.VMEM((m, D_MODEL), F32)],
        compiler_params=_params(1),
        name="ffn_stream",
    )(x, ng, wg, wu, wd)


def _lag_rows(cur, first, lag):
    if lag % SUBLANES == 0:
        return jnp.concatenate([first, cur[:-lag]], axis=0)
    assert lag == 1
    rolled = pltpu.roll(cur, 1, 0)
    rid = lax.broadcasted_iota(jnp.int32, (SUBLANES, cur.shape[1]), 0)
    head = jnp.where(rid == 0, first, rolled[0:SUBLANES])
    return jnp.concatenate([head, rolled[SUBLANES:]], axis=0)


def _proj_kernel(n_t, lag, tiles_per_seq, has_prev, fuse_ret, *refs):
    it = iter(refs)
    x_ref = next(it)
    prev_ref = next(it) if has_prev else None
    ng_ref, win_ref, mu_ref = next(it), next(it), next(it)
    if fuse_ret:
        dm_ref, qd_ref, kd_ref, cd_ref, gn_ref = (next(it) for _ in range(5))
        cos_ref, sin_ref, mix_o, yr_o, s_o, hl_o, carry_scr, s_scr = tuple(it)
    else:
        cos_ref, sin_ref, mix_o, q_o, kr_o, vr_o, gr_o, hl_o, carry_scr = tuple(it)

    def load(ref):
        if n_t == 1:
            return ref[...]
        return jnp.concatenate([ref[t] for t in range(n_t)], axis=0)

    def store(ref, val, cols=slice(None)):
        if n_t == 1:
            ref[:, cols] = val.astype(ref.dtype)
        else:
            rows = val.shape[0] // n_t
            for t in range(n_t):
                ref[t, :, cols] = val[t * rows:(t + 1) * rows].astype(ref.dtype)

    x = load(x_ref)
    tm = x.shape[0]
    h = _rms(x, ng_ref[2:3, :])
    hl_o[0] = h[tm - lag:, :]
    hb = h.astype(BF16)
    seq_start = (pl.program_id(0) % tiles_per_seq) == 0
    prev_b = prev_ref[0].astype(BF16) if has_prev else None

    ret = lambda c: _mm(hb, win_ref[:, N_SHIFT + c * D_R:N_SHIFT + (c + 1) * D_R])
    cos2 = load(cos_ref)
    sin2 = load(sin_ref)

    def rope(t):
        parts = []
        for hh in range(H_R):
            th = t[:, hh * HEAD_R:(hh + 1) * HEAD_R]
            parts.append(th * cos2 + pltpu.roll(th, HEAD_R // 2, 1) * sin2)
        return jnp.concatenate(parts, axis=1)

    if fuse_ret:
        rq = rope(ret(0)).astype(BF16)
        rk = (rope(ret(1)) * (HEAD_R ** -0.5)).astype(BF16)
        rv = ret(2).astype(BF16)
        rg = ret(3)
        heads = list(range(H_R))
        lanes = [slice(hh * HEAD_R, (hh + 1) * HEAD_R) for hh in heads]
        r_state = [jnp.where(seq_start, 0.0, s_scr[hh]) for hh in heads]

    for slab, c0 in enumerate(range(0, N_SHIFT, D_A)):
        cols = slice(c0, min(c0 + D_A, N_SHIFT))
        cur = _mm(hb, win_ref[:, cols])
        if has_prev:
            first = _mm(prev_b, win_ref[:, cols])
        else:
            first = jnp.zeros((lag, cur.shape[1]), F32)
        if tiles_per_seq > 1:
            first = jnp.where(seq_start, first, carry_scr[0:lag, cols])
            carry_scr[0:lag, cols] = cur[tm - lag:, :]
        prv = _lag_rows(cur, first, lag)
        store(mix_o, cur + (prv - cur) * mu_ref[:, cols], cols)
        if fuse_ret:
            rows = slice(slab * RET_CHUNK, (slab + 1) * RET_CHUNK)
            pick = lambda t: [t[rows, ln] for ln in lanes]
            y, r_state = _ret_chunk(pick(rq), pick(rk), pick(rv), pick(rg), r_state, heads,
                                    dm_ref, qd_ref, kd_ref, cd_ref, gn_ref)
            for hh in heads:
                yr_o[rows, lanes[hh]] = y[hh].astype(yr_o.dtype)

    if fuse_ret:
        for hh in heads:
            s_scr[hh] = r_state[hh]
            s_o[0, hh] = r_state[hh]
        return

    q = ret(0)
    kr = ret(1)
    store(q_o, rope(q))
    vr = ret(2)
    store(kr_o, rope(kr) * (HEAD_R ** -0.5))
    gr = ret(3)
    store(vr_o, vr)
    store(gr_o, gr)


def _proj(x, prev, ng, win, mu, cos2, sin2, *, n_t, rows_per_t, lag, tiles_per_seq,
          ret_tables=None):
    tm = n_t * rows_per_t
    fuse_ret = ret_tables is not None
    assert not fuse_ret or (n_t == 1 and tm == pl.cdiv(N_SHIFT, D_A) * RET_CHUNK)
    if n_t == 1:
        m = x.shape[0]
        n_tiles = m // tm
        row = lambda w: pl.BlockSpec((tm, w), lambda i: (i, 0))
        shp = lambda w, dt: jax.ShapeDtypeStruct((m, w), dt)
        tab = pl.BlockSpec((tm, HEAD_R), lambda i: (i % tiles_per_seq, 0))
    else:
        m = x.shape[0] * x.shape[1]
        n_tiles = x.shape[1] // rows_per_t
        row = lambda w: pl.BlockSpec((n_t, rows_per_t, w), lambda i: (0, i, 0))
        shp = lambda w, dt: jax.ShapeDtypeStruct((n_t, m // n_t, w), dt)
        tab = pl.BlockSpec((n_t, rows_per_t, HEAD_R), lambda i: (0, i, 0))
    n_seq = n_tiles // tiles_per_seq
    hl_spec = pl.BlockSpec((1, lag, D_MODEL), lambda i: (i // tiles_per_seq, 0, 0))
    args = [x]
    specs = [row(D_MODEL)]
    if prev is not None:
        args.append(prev)
        specs.append(pl.BlockSpec((1, lag, D_MODEL), lambda i: (i // tiles_per_seq, 0, 0)))
    consts = (ng, win, mu) + (tuple(ret_tables) if fuse_ret else ())
    args += list(consts) + [cos2, sin2]
    specs += [_resident(c.shape) for c in consts] + [tab, tab]
    hl_shape = jax.ShapeDtypeStruct((n_seq, lag, D_MODEL), F32)
    scratch = [pltpu.VMEM((max(lag, SUBLANES), N_SHIFT), F32)]
    if fuse_ret:
        out_shape = [shp(N_SHIFT, F32), shp(D_R, BF16),
                     jax.ShapeDtypeStruct((n_seq, H_R, HEAD_R, HEAD_R), F32), hl_shape]
        out_specs = [row(N_SHIFT), row(D_R),
                     pl.BlockSpec((1, H_R, HEAD_R, HEAD_R), lambda i: (i // tiles_per_seq, 0, 0, 0)),
                     hl_spec]
        scratch.append(pltpu.VMEM((H_R, HEAD_R, HEAD_R), F32))
    else:
        out_shape = [shp(N_SHIFT, F32)] + [shp(D_R, F32)] * 4 + [hl_shape]
        out_specs = [row(N_SHIFT)] + [row(D_R)] * 4 + [hl_spec]
    return pl.pallas_call(
        functools.partial(_proj_kernel, n_t, lag, tiles_per_seq, prev is not None, fuse_ret),
        grid=(n_tiles,),
        in_specs=specs,
        out_specs=out_specs,
        out_shape=out_shape,
        scratch_shapes=scratch,
        compiler_params=_params(1),
        name="proj",
    )(*args)


def _sums_stacked(x, bd2):
    return _mm(jnp.concatenate(_split_hi_lo(x), axis=1), bd2)


def _sums_lanes(x):
    first = lax.broadcasted_iota(jnp.int32, x.shape, 1) < HEAD_A
    s0 = jnp.sum(jnp.where(first, x, 0.0), axis=1, keepdims=True)
    s1 = jnp.sum(jnp.where(first, 0.0, x), axis=1, keepdims=True)
    return jnp.where(first, s0, s1)


def _wkv_post(y, r, k, v, g, rk, lw_g, lb_g, head_sums):
    inv_n = 1.0 / HEAD_A
    mu = head_sums(y) * inv_n
    d = y - mu
    var = head_sums(d * d) * inv_n
    yn = d * lax.rsqrt(var + GN_EPS_A) * lw_g + lb_g
    bonus = head_sums(r * k * rk) * v
    return (yn + bonus) * g


def _wkv_decay_gate(wa, gd, w0, w2p, a0, a2p, g2):
    w_pre = w0 + _mm(jnp.tanh(wa).astype(BF16), w2p)
    lw = -jnp.exp(-_softplus(-w_pre) - 0.5)
    a = _sigmoid(a0 + _mm(wa.astype(BF16), a2p))
    g = _mm(_sigmoid(gd).astype(BF16), g2)
    return lw, a, g


def _wkv_keys(k, a, k_k, k_a, head_sums):
    kk = k * k_k
    kk = kk / jnp.maximum(jnp.sqrt(head_sums(kk * kk)), 1e-12)
    return k * (1.0 + (a - 1.0) * k_a), kk, kk * a


def _wkv_prompt_kernel(nb, tt, mix_ref, w0_ref, w2_ref, a0_ref, a2_ref, g2_ref, kk_ref, ka_ref,
                       rk_ref, lnw_ref, lnb_ref, bd2_ref, tri2_ref,
                       ya_o, s_o, s_scr):
    c = WKV_CHUNK
    j = pl.program_id(1)

    @pl.when(j == 0)
    def _():
        s_scr[...] = jnp.zeros_like(s_scr)

    lane = lax.broadcasted_iota(jnp.int32, (c, PAIR), 1)
    first = lane < HEAD_A

    def stack(x):
        return jnp.concatenate([jnp.where(first, x, 0.0), jnp.where(first, 0.0, x)], axis=0)

    ri = lax.broadcasted_iota(jnp.int32, (2 * c, 2 * c), 0) % c
    ci = lax.broadcasted_iota(jnp.int32, (2 * c, 2 * c), 1) % c
    strict = ri > ci
    incl = ri >= ci
    row_i = lax.broadcasted_iota(jnp.int32, (2 * c, 2 * c), 0)
    col_i = lax.broadcasted_iota(jnp.int32, (2 * c, 2 * c), 1)
    eye = (row_i == col_i).astype(F32)
    blk8 = (row_i // SUBLANES) == (col_i // SUBLANES)
    off_mask = {sz: ((row_i // (2 * sz)) == (col_i // (2 * sz))) & ((row_i % (2 * sz)) >= sz)
                & ((col_i % (2 * sz)) < sz) for sz in (8, 16, 32)}
    bd2 = bd2_ref[...]
    tri2 = tri2_ref[...]
    head_sums = functools.partial(_sums_stacked, bd2=bd2)

    n_pairs = H_A // 2
    units = [(bi, p) for bi in range(nb) for p in range(n_pairs)]
    n_u = range(len(units))
    slab = [slice(p * PAIR, (p + 1) * PAIR) for p in range(n_pairs)]

    tall = lambda xs: jnp.concatenate(xs, axis=0)
    per_pair = lambda ref: tall([jnp.broadcast_to(ref[:, slab[p]], (c, PAIR)) for _, p in units])
    split = lambda x: [x[u * c:(u + 1) * c] for u in n_u]

    def prep(ch):
        rows = slice(ch * c, (ch + 1) * c)
        ld = lambda c0: [mix_ref[bi, rows, c0 + p * PAIR:c0 + (p + 1) * PAIR] for bi, p in units]
        r, k_raw, v = ld(0), ld(D_A), ld(2 * D_A)
        lora = [_wkv_decay_gate(mix_ref[bi, rows, 3 * D_A:3 * D_A + LORA_W + LORA_A],
                                mix_ref[bi, rows, 3 * D_A + LORA_W + LORA_A:N_SHIFT],
                                w0_ref[...], w2_ref[...], a0_ref[...], a2_ref[...], g2_ref[...])
                for bi in range(nb)]
        lw_all = [x[0] for x in lora]
        a = tall([lora[bi][1][:, slab[p]] for bi, p in units])
        gate = tall([lora[bi][2][:, slab[p]] for bi, p in units])
        k, kk, kka = map(split, _wkv_keys(tall(k_raw), a, per_pair(kk_ref), per_pair(ka_ref),
                                          _sums_lanes))
        return r, v, k, kk, kka, gate, lw_all

    def gram_stage(pre):
        r, v, k, kk, kka, gate, lw_all = pre
        cum_all = [_mm(tri2, jnp.concatenate(_split_hi_lo(x), axis=0)) for x in lw_all]
        lw = [lw_all[bi][:, slab[p]] for bi, p in units]
        cum = [cum_all[bi][:, slab[p]] for bi, p in units]
        cum_end = [x[c - 1:c, :] for x in cum]
        e_pos = [jnp.exp(x) for x in cum]
        e_neg = [jnp.exp(-x) for x in cum]
        e_end = [jnp.exp(x - y) for x, y in zip(cum_end, cum)]
        xs = [jnp.concatenate([stack(-kk[u] * jnp.exp(cum[u] - lw[u])), stack(r[u] * e_pos[u])],
                              axis=0).astype(BF16) for u in n_u]
        ws = [jnp.concatenate([stack(kka[u] * e_neg[u]), stack(k[u] * e_neg[u])],
                              axis=0).astype(BF16) for u in n_u]
        we = [jnp.concatenate([stack(kka[u] * e_end[u]), stack(k[u] * e_end[u])],
                              axis=0).astype(BF16) for u in n_u]
        vs = [stack(x).astype(BF16) for x in v]
        gram = [_nt(xs[u], ws[u]) for u in n_u]
        a_ab = [jnp.where(strict, g[0:2 * c, 0:2 * c], 0.0) for g in gram]
        a_ak = [jnp.where(strict, g[0:2 * c, 2 * c:], 0.0).astype(BF16) for g in gram]
        a_r = [jnp.concatenate([jnp.where(incl, g[2 * c:, 0:2 * c], 0.0),
                                jnp.where(incl, g[2 * c:, 2 * c:], 0.0)], axis=1).astype(BF16)
               for g in gram]
        return xs, we, vs, cum_end, a_ab, a_ak, a_r

    def solve_stage(ch, pre, gs):
        rows = slice(ch * c, (ch + 1) * c)
        r, v, k, kk, kka, gate, lw_all = pre
        xs, we, vs, cum_end, a_ab, a_ak, a_r = gs
        a8 = [jnp.where(blk8, a, 0.0) for a in a_ab]
        inv = [eye + a for a in a8]
        pw = [x.astype(BF16) for x in a8]
        pw = [_mm(x, x).astype(BF16) for x in pw]
        both = [_mm(x, jnp.concatenate([x, i.astype(BF16)], axis=1)) for i, x in zip(inv, pw)]
        inv = [i + b[:, 2 * c:] for i, b in zip(inv, both)]
        inv = [i + _mm(b[:, 0:2 * c].astype(BF16), i.astype(BF16)) for i, b in zip(inv, both)]
        size = SUBLANES
        while size < c:
            lows = [slice(k + size, k + 2 * size) for k in range(0, 2 * c, 2 * size)]
            ups = [slice(k, k + size) for k in range(0, 2 * c, 2 * size)]
            off = [jnp.where(off_mask[size], a, 0.0).astype(BF16) for a in a_ab]
            d_low = [jnp.concatenate([i[r] for r in lows], axis=0) for i in inv]
            x_low = [_mm(d.astype(BF16), o).astype(BF16) for d, o in zip(d_low, off)]
            new_low = [d + _mm(x, i.astype(BF16)) for d, x, i in zip(d_low, x_low, inv)]
            inv = [jnp.concatenate([piece for k, r in enumerate(ups)
                                    for piece in (i[r], nl[k * size:(k + 1) * size])], axis=0)
                   for i, nl in zip(inv, new_low)]
            size *= 2
        s = [s_scr[u] for u in n_u]
        z = [_nt(xs[u], s[u].astype(BF16)) for u in n_u]
        rhs = [z[u][0:2 * c] + _mm(a_ak[u], vs[u]) for u in n_u]
        uu = [_mm(inv[u].astype(BF16), rhs[u].astype(BF16)).astype(BF16) for u in n_u]
        uv = [jnp.concatenate([uu[u], vs[u]], axis=0) for u in n_u]
        y2 = [z[u][2 * c:] + _mm(a_r[u], uv[u]) for u in n_u]
        for u in n_u:
            s_scr[u] = s[u] * jnp.exp(cum_end[u]) + _tn(uv[u], we[u])
        out = _wkv_post(tall([y2[u][0:c] + y2[u][c:] for u in n_u]), tall(r), tall(k), tall(v),
                        gate, per_pair(rk_ref), per_pair(lnw_ref), per_pair(lnb_ref), head_sums)
        for u, (bi, p) in enumerate(units):
            ya_o[bi, rows, slab[p]] = out[u * c:(u + 1) * c].astype(ya_o.dtype)

    n_chunks = tt // c
    pre = prep(0)
    for ch in range(n_chunks):
        gs = gram_stage(pre)
        nxt = prep(ch + 1) if ch + 1 < n_chunks else None
        solve_stage(ch, pre, gs)
        pre = nxt

    @pl.when(j == pl.num_programs(1) - 1)
    def _():
        for u, (bi, p) in enumerate(units):
            s = s_scr[u]
            s_o[bi, 2 * p] = s[0:HEAD_A, 0:HEAD_A]
            s_o[bi, 2 * p + 1] = s[HEAD_A:, HEAD_A:]


def _wkv_prompt(mixed, consts, *, nb, tt):
    batch, seq, _ = mixed.shape
    blk = lambda w: pl.BlockSpec((nb, tt, w), lambda b, j: (b, j, 0))
    return pl.pallas_call(
        functools.partial(_wkv_prompt_kernel, nb, tt),
        grid=(batch // nb, seq // tt),
        in_specs=[blk(N_SHIFT)] + [_resident(x.shape) for x in consts],
        out_specs=[blk(D_A), pl.BlockSpec((nb, H_A, HEAD_A, HEAD_A), lambda b, j: (b, 0, 0, 0))],
        out_shape=[jax.ShapeDtypeStruct((batch, seq, D_A), BF16),
                   jax.ShapeDtypeStruct((batch, H_A, HEAD_A, HEAD_A), F32)],
        scratch_shapes=[pltpu.VMEM((nb * H_A // 2, PAIR, PAIR), F32)],
        compiler_params=_params(2),
        name="wkv_prompt",
    )(mixed, *consts)


def _wkv_sample_kernel(n_t, r_ref, k_ref, v_ref, lora_ref,
                       w0_ref, w2_ref, a0_ref, a2_ref, g2_ref, kk_ref, ka_ref,
                       rk_ref, lnw_ref, lnb_ref, bd2_ref, s_ref,
                       ya_o, s_o, yt_scr):
    n = HEAD_A
    n_b = r_ref.shape[1]
    tall = lambda xs: jnp.concatenate(xs, axis=0)
    rows_of = lambda ref: tall([ref[t] for t in range(n_t)])
    head_sums = functools.partial(_sums_stacked, bd2=bd2_ref[...])
    r, v, lora = rows_of(r_ref), rows_of(v_ref), rows_of(lora_ref)
    lw, a, gate = _wkv_decay_gate(lora[:, 0:LORA_W + LORA_A], lora[:, LORA_W + LORA_A:],
                                  w0_ref[...], w2_ref[...], a0_ref[...], a2_ref[...], g2_ref[...])
    k, kk, kka = _wkv_keys(rows_of(k_ref), a, kk_ref[...], ka_ref[...], head_sums)

    tr = lambda x: [x[t * n_b:(t + 1) * n_b].T for t in range(n_t)]
    nkk_t, kka_t, k_t, r_t, v_t, w_t = tr(-kk), tr(kka), tr(k), tr(r), tr(v), tr(jnp.exp(lw))
    rid = lax.broadcasted_iota(jnp.int32, (SUBLANES, n_b), 0)

    for hh in range(2):
        keys = slice(hh * n, (hh + 1) * n)
        for ig in range(n // SUBLANES):
            y_tiles = [jnp.zeros((SUBLANES, n_b), F32) for _ in range(n_t)]
            for ii in range(SUBLANES):
                i = ig * SUBLANES + ii
                s = s_ref[hh, i]
                for t in range(n_t):
                    sa = jnp.sum(s * nkk_t[t][keys], axis=0, keepdims=True)
                    v_row = v_t[t][hh * n + i:hh * n + i + 1]
                    s = s * w_t[t][keys] + sa * kka_t[t][keys] + v_row * k_t[t][keys]
                    y_row = jnp.sum(s * r_t[t][keys], axis=0, keepdims=True)
                    y_tiles[t] = jnp.where(rid == ii, y_row, y_tiles[t])
                s_o[hh, i] = s
            for t in range(n_t):
                yt_scr[t, hh * n + ig * SUBLANES:hh * n + (ig + 1) * SUBLANES, :] = y_tiles[t]

    out = _wkv_post(tall([yt_scr[t].T for t in range(n_t)]), r, k, v, gate,
                    rk_ref[...], lnw_ref[...], lnb_ref[...], head_sums)
    for t in range(n_t):
        ya_o[t] = out[t * n_b:(t + 1) * n_b].astype(ya_o.dtype)


def _wkv_sample(mixed, w0, w2p, a0, a2p, g2, k_k, k_a, rk, lnw, lnb, bd2, s0):
    n_t, n_b, _ = mixed.shape
    slab = lambda first: pl.BlockSpec((n_t, n_b, PAIR), lambda p: (0, 0, first + p))
    lora = pl.BlockSpec((n_t, n_b, 2 * PAIR), lambda p: (0, 0, 3 * D_A // (2 * PAIR)))
    par = pl.BlockSpec((1, PAIR), lambda p: (0, p))
    low = pl.BlockSpec((PAIR, PAIR), lambda p: (0, p))
    st = pl.BlockSpec((2, HEAD_A, HEAD_A, n_b), lambda p: (p, 0, 0, 0))
    n_slab = D_A // PAIR
    return pl.pallas_call(
        functools.partial(_wkv_sample_kernel, n_t),
        grid=(H_A // 2,),
        in_specs=[slab(0), slab(n_slab), slab(2 * n_slab), lora,
                  par, low, par, low, low, par, par, par, par, par, _resident(bd2.shape), st],
        out_specs=[slab(0), st],
        out_shape=[jax.ShapeDtypeStruct((n_t, n_b, D_A), F32),
                   jax.ShapeDtypeStruct(s0.shape, F32)],
        scratch_shapes=[pltpu.VMEM((n_t, PAIR, n_b), F32)],
        compiler_params=_params(1),
        name="wkv_sample",
    )(mixed, mixed, mixed, mixed, w0, w2p, a0, a2p, g2, k_k, k_a, rk, lnw, lnb, bd2, s0)


def _ret_chunk(q, k, v, g, s, heads, dm_ref, qd_ref, kd_ref, cd_ref, gn_ref):
    n = range(len(q))
    qb = [x.astype(BF16) for x in q]
    kb = [x.astype(BF16) for x in k]
    vb = [x.astype(BF16) for x in v]
    inner = [(_nt(qb[u], kb[u]) * dm_ref[heads[u]]).astype(BF16) for u in n]
    q_dec = [(q[u].astype(F32) * qd_ref[heads[u]]).astype(BF16) for u in n]
    k_dec = [(k[u].astype(F32) * kd_ref[heads[u]]).astype(BF16) for u in n]
    if inner[0].shape[1] % LANES == 0:
        y = [_mm(jnp.concatenate([inner[u], q_dec[u]], axis=1),
                 jnp.concatenate([vb[u], s[u].astype(BF16)], axis=0)) for u in n]
    else:
        y = [_mm(inner[u], vb[u]) + _mm(q_dec[u], s[u].astype(BF16)) for u in n]
    s_new = [s[u] * cd_ref[heads[u]] + _tn(k_dec[u], vb[u]) for u in n]
    out = []
    for u in n:
        mu = jnp.mean(y[u], axis=-1, keepdims=True)
        d = y[u] - mu
        var = jnp.mean(d * d, axis=-1, keepdims=True)
        lanes = slice(heads[u] * HEAD_R, (heads[u] + 1) * HEAD_R)
        yn = d * lax.rsqrt(var + GN_EPS_R) * gn_ref[:, lanes]
        out.append(g[u] * _sigmoid(g[u]) * yn)
    return out, s_new


def _ret_sample_kernel(n_t, bb, q_ref, k_ref, v_ref, g_ref, dm_ref, qd_ref, kd_ref, cd_ref,
                       gn_ref, s_ref, y_o, s_o):
    rid = lax.broadcasted_iota(jnp.int32, (SUBLANES, HEAD_R), 0)
    units = [(bi, hh) for bi in range(bb) for hh in range(H_R)]
    heads = [hh for _, hh in units]
    lanes = [slice(hh * HEAD_R, (hh + 1) * HEAD_R) for hh in heads]

    def seq_rows(ref):
        outs = []
        for u, (bi, _) in enumerate(units):
            out = jnp.zeros((SUBLANES, HEAD_R), F32)
            for t in range(n_t):
                out = jnp.where(rid == t, jnp.broadcast_to(ref[t, bi:bi + 1, lanes[u]], out.shape), out)
            outs.append(out)
        return outs

    y, s_new = _ret_chunk(seq_rows(q_ref), seq_rows(k_ref), seq_rows(v_ref), seq_rows(g_ref),
                          [s_ref[bi, hh] for bi, hh in units], heads,
                          dm_ref, qd_ref, kd_ref, cd_ref, gn_ref)
    for u, (bi, hh) in enumerate(units):
        s_o[bi, hh] = s_new[u]
        for t in range(n_t):
            y_o[t, bi:bi + 1, lanes[u]] = y[u][t:t + 1].astype(y_o.dtype)


def _ret_sample(q, k, v, g, dm, qd, kd, cd, gn, s0, *, bb):
    n_t, n_b, _ = q.shape
    consts = (dm, qd, kd, cd, gn)
    blk = pl.BlockSpec((n_t, bb, D_R), lambda i: (0, i, 0))
    st = pl.BlockSpec((bb, H_R, HEAD_R, HEAD_R), lambda i: (i, 0, 0, 0))
    return pl.pallas_call(
        functools.partial(_ret_sample_kernel, n_t, bb),
        grid=(n_b // bb,),
        in_specs=[blk] * 4 + [_resident(x.shape) for x in consts] + [st],
        out_specs=[blk, st],
        out_shape=[jax.ShapeDtypeStruct((n_t, n_b, D_R), F32),
                   jax.ShapeDtypeStruct(s0.shape, F32)],
        compiler_params=_params(1),
        name="ret_sample",
    )(q, k, v, g, *consts, s0)


def _rope_tables(pos):
    half = HEAD_R // 2
    inv = ROPE_BASE ** (-np.arange(half, dtype=np.float64) / half)
    ang = np.asarray(pos, np.float64)[:, None] * inv[None, :]
    cos, sin = np.cos(ang), np.sin(ang)
    return (np.concatenate([cos, cos], axis=1).astype(np.float32),
            np.concatenate([-sin, sin], axis=1).astype(np.float32))


def _ret_tables(c):
    lg = np.log1p(-np.exp2(-5.0 - np.arange(H_R, dtype=np.float64)))
    idx = np.arange(c, dtype=np.float64)
    diff = idx[:, None] - idx[None, :]
    dmask = np.where(diff >= 0, np.exp(lg[:, None, None] * np.maximum(diff, 0.0)), 0.0)
    ones = np.ones((1, 1, HEAD_R))
    qdec = np.exp(lg[:, None] * (idx + 1.0))[:, :, None] * ones
    kdec = np.exp(lg[:, None] * (c - 1.0 - idx))[:, :, None] * ones
    cdec = np.exp(lg * c)[:, None, None] * ones
    extra = -c % SUBLANES
    dmask = np.pad(dmask, ((0, 0), (0, extra), (0, extra)))
    qdec = np.pad(qdec, ((0, 0), (0, extra), (0, 0)))
    kdec = np.pad(kdec, ((0, 0), (0, extra), (0, 0)))
    return tuple(t.astype(np.float32) for t in (dmask, qdec, kdec, cdec))


def _block_ones(n, block):
    idx = jnp.arange(n) // block
    return (idx[:, None] == idx[None, :]).astype(BF16)


def kernel(x_prompt, x_sample, state_shift, state_wkv, state_ret, norm_g, ffn1_wg, ffn1_wu, ffn1_wd,
           w_in, mu_shift, w0, w2, a0, a2, g2, k_k, k_a, r_k, lnx_w, lnx_b, ret_gn_w, w_out,
           ffn2_wg, ffn2_wu, ffn2_wd):
    assert norm_g.shape[0] == 1, "single-layer configuration"
    bp, tp, _ = x_prompt.shape
    bs, ts, _ = x_sample.shape
    l = 0
    ng = norm_g[l]
    row = lambda t: t[l].reshape(1, -1)
    zpad = jnp.zeros((LORA_W, D_A), BF16)
    w2p = jnp.concatenate([w2[l].astype(BF16), zpad], axis=0)
    a2p = jnp.concatenate([zpad, a2[l].astype(BF16)], axis=0)
    rk, lnw, lnb, gn = row(r_k), row(lnx_w), row(lnx_b), row(ret_gn_w)
    wkv_params = (row(w0), w2p, row(a0), a2p, g2[l].astype(BF16), row(k_k), row(k_a), rk, lnw, lnb)
    bd_pair = _block_ones(PAIR, HEAD_A)
    bd2 = jnp.concatenate([bd_pair, bd_pair], axis=0)
    tri = (jnp.arange(WKV_CHUNK)[:, None] >= jnp.arange(WKV_CHUNK)[None, :]).astype(BF16)
    tri2 = jnp.concatenate([tri, tri], axis=1)

    m_s = bs * ts
    x1s, *f1 = _ffn_stream(x_sample, ng, ffn1_wg[l], ffn1_wu[l], ffn1_wd[l], 0, 1)

    xp = x_prompt.reshape(bp * tp, D_MODEL)
    x1p, *f2, win, wo = _ffn(xp, ng, *f1, 0, 1, FFN_ROWS,
                             cast=(ffn2_wg[l], ffn2_wu[l], ffn2_wd[l], w_in[l], w_out[l]))
    proj_consts = (ng, win, row(mu_shift))
    cos_p, sin_p = _rope_tables(np.arange(tp))
    (mixed, yr_p, ret_p, hl_p) = _proj(
        x1p, None, *proj_consts, cos_p, sin_p, n_t=1, rows_per_t=PROJ_ROWS, lag=1,
        tiles_per_seq=tp // PROJ_ROWS, ret_tables=(*_ret_tables(RET_CHUNK), gn))
    ya_p, wkv_p = _wkv_prompt(mixed.reshape(bp, tp, N_SHIFT), (*wkv_params, bd2, tri2),
                              nb=WKV_SEQS, tt=WKV_ROWS)
    ya_p = ya_p.reshape(bp * tp, D_A)
    yp = _ffn(x1p, ng, *f2, 4, 5, FFN_ROWS, mix=(ya_p, yr_p, wo))

    cos_s, sin_s = _rope_tables(PAST_LEN + np.arange(ts))
    rows_per_t = bs
    tab = lambda t: np.ascontiguousarray(np.broadcast_to(t[:, None, :], (ts, bs, HEAD_R)))
    outs = _proj(x1s.reshape(ts, bs, D_MODEL),
                 state_shift[l].reshape(bs // rows_per_t, rows_per_t, D_MODEL),
                 *proj_consts, tab(cos_s), tab(sin_s), n_t=ts, rows_per_t=rows_per_t,
                 lag=rows_per_t, tiles_per_seq=1)
    (mixed, q, kr, vr, gr, hl_s) = outs
    ya_s, wkv_s = _wkv_sample(mixed, *wkv_params, bd2, state_wkv[l].transpose(1, 2, 3, 0))
    wkv_s = wkv_s.transpose(3, 0, 1, 2)
    yr_s, ret_s = _ret_sample(q, kr, vr, gr, *_ret_tables(min(RET_CHUNK, ts)), gn, state_ret[l],
                              bb=RET_SEQS)
    ys = _ffn(x1s, ng, *f2, 4, 5, m_s,
              mix=(ya_s.reshape(m_s, D_A), yr_s.reshape(m_s, D_R), wo), out_seqs=bs)

    return (yp.reshape(bp, tp, D_MODEL), ys,
            hl_p.reshape(1, bp, D_MODEL), wkv_p[None], ret_p[None],
            hl_s.reshape(1, bs, D_MODEL), wkv_s[None], ret_s[None])
```

```python
import functools

import jax
import jax.numpy as jnp
import numpy as np
from jax import lax
from jax.experimental import pallas as pl
from jax.experimental.pallas import tpu as pltpu

F32 = jnp.float32
BF16 = jnp.bfloat16

D_MODEL = 1024
D_A = 512
HEAD_A = 64
H_A = D_A // HEAD_A
D_R = 512
H_R = 4
HEAD_R = D_R // H_R
LORA_W, LORA_A, LORA_G = 64, 64, 128
D_FF = 2816
RET_CHUNK = 128
ROPE_BASE = 10000.0
EPS = 1e-6
GN_EPS_A = 64e-5
GN_EPS_R = 1e-5
N_SHIFT = 3 * D_A + LORA_W + LORA_A + LORA_G
PAST_LEN = 16384

LANES = 128
SUBLANES = 8
VMEM_LIMIT = 52 * 1024 * 1024

MXU_DIM = 256
FF_SPLIT = 6 * MXU_DIM
WKV_CHUNK = 64
PAIR = 2 * HEAD_A

FFN_ROWS = 1024
PROJ_ROWS = 4 * RET_CHUNK
WKV_SEQS, WKV_ROWS = 4, 256
RET_SEQS = 2 * SUBLANES
RET_RING = 3


def _nt(a, b):
    return lax.dot_general(a, b, (((1,), (1,)), ((), ())), preferred_element_type=F32)


def _tn(a, b):
    return lax.dot_general(a, b, (((0,), (0,)), ((), ())), preferred_element_type=F32)


def _mm(a, b):
    return jnp.dot(a, b, preferred_element_type=F32)


def _split_hi_lo(x):
    hi = x.astype(BF16)
    lo = (x - hi.astype(F32)).astype(BF16)
    return hi, lo


def _rms(x, g):
    return x * lax.rsqrt(jnp.mean(x * x, axis=-1, keepdims=True) + EPS) * g


def _softplus(x):
    return jnp.maximum(x, 0.0) + jnp.log(1.0 + jnp.exp(-jnp.abs(x)))


def _sigmoid(x):
    return 1.0 / (1.0 + jnp.exp(-x))


def _resident(shape):
    nd = len(shape)
    return pl.BlockSpec(shape, lambda *_: (0,) * nd, pipeline_mode=pl.Buffered(1))


def _params(n_axes):
    return pltpu.CompilerParams(dimension_semantics=("arbitrary",) * n_axes,
                                vmem_limit_bytes=VMEM_LIMIT)


def _ffn_kernel(with_mix, n_cast, g_in, g_out, *refs):
    n_in = (8 if with_mix else 5) + n_cast
    ins, outs = refs[:n_in], refs[n_in:]
    if with_mix:
        x_ref, ya_ref, yr_ref, wo_ref, ng_ref, wg_ref, wu_ref, wd_ref = ins[:8]
    else:
        x_ref, ng_ref, wg_ref, wu_ref, wd_ref = ins[:5]
    o_ref = outs[0]
    for src, dst in zip(ins[n_in - n_cast:], outs[1:]):
        dst[...] = src[...].astype(dst.dtype)
    half = x_ref.shape[0] // 2
    halves = [slice(0, half), slice(half, 2 * half)]
    x = [x_ref[r, :] for r in halves]
    if with_mix:
        mix = [_mm(ya_ref[r, :].astype(BF16), wo_ref[0:D_A, :])
               + _mm(yr_ref[r, :].astype(BF16), wo_ref[D_A:, :]) for r in halves]
        x = [xi + _rms(m, ng_ref[3:4, :]) for xi, m in zip(x, mix)]
    h = [_rms(xi, ng_ref[g_in:g_in + 1, :]).astype(BF16) for xi in x]
    for i, r in enumerate(halves):
        acc = None
        for cols in (slice(0, FF_SPLIT), slice(FF_SPLIT, D_FF)):
            gate = _mm(h[i], wg_ref[:, cols])
            up = _mm(h[i], wu_ref[:, cols])
            act = (gate * _sigmoid(gate) * up).astype(BF16)
            part = _mm(act, wd_ref[cols, :])
            acc = part if acc is None else acc + part
        out = x[i] + 0.5 * _rms(acc, ng_ref[g_out:g_out + 1, :])
        if o_ref.ndim == 2:
            o_ref[r, :] = out
        else:
            n_seq = o_ref.shape[0]
            for j in range(half // n_seq):
                o_ref[:, i * (half // n_seq) + j, :] = out[j * n_seq:(j + 1) * n_seq]


BF16_ROWS = 2 * SUBLANES


def _cast_spec(rows, cols, steps):
    rep = 1
    while (rows * rep) % steps or (rows * rep // steps) % BF16_ROWS:
        rep *= 2
    return pl.BlockSpec((rows * rep // steps, cols), lambda i: (i // rep, 0))


def _ffn(x, ng, wg, wu, wd, g_in, g_out, tm, mix=None, cast=(), out_seqs=None):
    m = x.shape[0]
    steps = m // tm
    assert out_seqs is None or (steps == 1 and (m // 2) % out_seqs == 0)
    o_shape = (m, D_MODEL) if out_seqs is None else (out_seqs, m // out_seqs, D_MODEL)
    row = lambda w: pl.BlockSpec((tm, w), lambda i: (i, 0))
    if mix is None:
        args = (x, ng, wg, wu, wd)
        specs = [row(D_MODEL), _resident(ng.shape), _resident(wg.shape), _resident(wu.shape),
                 _resident(wd.shape)]
    else:
        ya, yr, wo = mix
        args = (x, ya, yr, wo, ng, wg, wu, wd)
        specs = [row(D_MODEL), row(D_A), row(D_R), _resident(wo.shape), _resident(ng.shape),
                 _resident(wg.shape), _resident(wu.shape), _resident(wd.shape)]
    cast_specs = [_cast_spec(*w.shape, steps) for w in cast]
    out = pl.pallas_call(
        functools.partial(_ffn_kernel, mix is not None, len(cast), g_in, g_out),
        grid=(steps,),
        in_specs=specs + cast_specs,
        out_specs=[row(D_MODEL) if out_seqs is None else pl.BlockSpec(o_shape, lambda i: (0, 0, 0))]
        + cast_specs,
        out_shape=[jax.ShapeDtypeStruct(o_shape, F32)]
        + [jax.ShapeDtypeStruct(w.shape, BF16) for w in cast],
        compiler_params=_params(1),
        name="ffn_mix" if mix is not None else "ffn",
    )(*args, *cast)
    return out if cast else out[0]


def _ffn_stream_kernel(g_in, g_out, x_ref, ng_ref, wg_ref, wu_ref, wd_ref,
                       o_ref, wg_o, wu_o, wd_o, h_scr, acc_scr):
    c = pl.program_id(0)
    x_rows = lambda: jnp.concatenate([x_ref[:, t, :] for t in range(x_ref.shape[1])], axis=0)

    @pl.when(c == 0)
    def _():
        h_scr[...] = _rms(x_rows(), ng_ref[g_in:g_in + 1, :]).astype(BF16)
        acc_scr[...] = jnp.zeros_like(acc_scr)

    wg, wu, wd = wg_ref[...].astype(BF16), wu_ref[...].astype(BF16), wd_ref[...].astype(BF16)
    wg_o[...] = wg
    wu_o[...] = wu
    wd_o[...] = wd
    h = h_scr[...]
    gate = _mm(h, wg)
    act = (gate * _sigmoid(gate) * _mm(h, wu)).astype(BF16)
    acc_scr[...] += _mm(act, wd)

    @pl.when(c == pl.num_programs(0) - 1)
    def _():
        o_ref[...] = x_rows() + 0.5 * _rms(acc_scr[...], ng_ref[g_out:g_out + 1, :])


def _ffn_stream(x, ng, wg, wu, wd, g_in, g_out):
    m = x.shape[0] * x.shape[1]
    slab = MXU_DIM
    whole = lambda a: pl.BlockSpec(a.shape, lambda c: (0,) * a.ndim)
    cols = pl.BlockSpec((D_MODEL, slab), lambda c: (0, c))
    rows = pl.BlockSpec((slab, D_MODEL), lambda c: (c, 0))
    return pl.pallas_call(
        functools.partial(_ffn_stream_kernel, g_in, g_out),
        grid=(D_FF // slab,),
        in_specs=[whole(x), whole(ng), cols, cols, rows],
        out_specs=[pl.BlockSpec((m, D_MODEL), lambda c: (0, 0)), cols, cols, rows],
        out_shape=[jax.ShapeDtypeStruct((m, D_MODEL), F32)]
        + [jax.ShapeDtypeStruct(w.shape, BF16) for w in (wg, wu, wd)],
        scratch_shapes=[pltpu.VMEM((m, D_MODEL), BF16), pltpu.VMEM((m, D_MODEL), F32)],
        compiler_params=_params(1),
        name="ffn_stream",
    )(x, ng, wg, wu, wd)


def _lag_rows(cur, first, lag):
    if lag % SUBLANES == 0:
        return jnp.concatenate([first, cur[:-lag]], axis=0)
    assert lag == 1
    rolled = pltpu.roll(cur, 1, 0)
    rid = lax.broadcasted_iota(jnp.int32, (SUBLANES, cur.shape[1]), 0)
    head = jnp.where(rid == 0, first, rolled[0:SUBLANES])
    return jnp.concatenate([head, rolled[SUBLANES:]], axis=0)


def _proj_kernel(n_t, lag, tiles_per_seq, has_prev, fuse_ret, *refs):
    it = iter(refs)
    x_ref = next(it)
    prev_ref = next(it) if has_prev else None
    ng_ref, win_ref, mu_ref = next(it), next(it), next(it)
    if fuse_ret:
        dm_ref, qd_ref, kd_ref, cd_ref, gn_ref = (next(it) for _ in range(5))
        cos_ref, sin_ref, mix_o, yr_o, s_o, hl_o, carry_scr, s_scr = tuple(it)
    else:
        cos_ref, sin_ref, mix_o, q_o, kr_o, vr_o, gr_o, hl_o, carry_scr = tuple(it)

    def load(ref):
        if n_t == 1:
            return ref[...]
        return jnp.concatenate([ref[t] for t in range(n_t)], axis=0)

    def store(ref, val, cols=slice(None)):
        if n_t == 1:
            ref[:, cols] = val.astype(ref.dtype)
        else:
            rows = val.shape[0] // n_t
            for t in range(n_t):
                ref[t, :, cols] = val[t * rows:(t + 1) * rows].astype(ref.dtype)

    x = load(x_ref)
    tm = x.shape[0]
    h = _rms(x, ng_ref[2:3, :])
    hl_o[0] = h[tm - lag:, :]
    hb = h.astype(BF16)
    seq_start = (pl.program_id(0) % tiles_per_seq) == 0
    prev_b = prev_ref[0].astype(BF16) if has_prev else None

    ret = lambda c: _mm(hb, win_ref[:, N_SHIFT + c * D_R:N_SHIFT + (c + 1) * D_R])
    cos2 = load(cos_ref)
    sin2 = load(sin_ref)

    def rope(t):
        parts = []
        for hh in range(H_R):
            th = t[:, hh * HEAD_R:(hh + 1) * HEAD_R]
            parts.append(th * cos2 + pltpu.roll(th, HEAD_R // 2, 1) * sin2)
        return jnp.concatenate(parts, axis=1)

    if fuse_ret:
        rq = rope(ret(0)).astype(BF16)
        rk = (rope(ret(1)) * (HEAD_R ** -0.5)).astype(BF16)
        rv = ret(2).astype(BF16)
        rg = ret(3)
        heads = list(range(H_R))
        lanes = [slice(hh * HEAD_R, (hh + 1) * HEAD_R) for hh in heads]
        r_state = [jnp.where(seq_start, 0.0, s_scr[hh]) for hh in heads]

    for slab, c0 in enumerate(range(0, N_SHIFT, D_A)):
        cols = slice(c0, min(c0 + D_A, N_SHIFT))
        cur = _mm(hb, win_ref[:, cols])
        if has_prev:
            first = _mm(prev_b, win_ref[:, cols])
        else:
            first = jnp.zeros((lag, cur.shape[1]), F32)
        if tiles_per_seq > 1:
            first = jnp.where(seq_start, first, carry_scr[0:lag, cols])
            carry_scr[0:lag, cols] = cur[tm - lag:, :]
        prv = _lag_rows(cur, first, lag)
        store(mix_o, cur + (prv - cur) * mu_ref[:, cols], cols)
        if fuse_ret:
            rows = slice(slab * RET_CHUNK, (slab + 1) * RET_CHUNK)
            pick = lambda t: [t[rows, ln] for ln in lanes]
            y, r_state = _ret_chunk(pick(rq), pick(rk), pick(rv), pick(rg), r_state, heads,
                                    dm_ref, qd_ref, kd_ref, cd_ref, gn_ref)
            for hh in heads:
                yr_o[rows, lanes[hh]] = y[hh].astype(yr_o.dtype)

    if fuse_ret:
        for hh in heads:
            s_scr[hh] = r_state[hh]
            s_o[0, hh] = r_state[hh]
        return

    q = ret(0)
    kr = ret(1)
    store(q_o, rope(q))
    vr = ret(2)
    store(kr_o, rope(kr) * (HEAD_R ** -0.5))
    gr = ret(3)
    store(vr_o, vr)
    store(gr_o, gr)


def _proj(x, prev, ng, win, mu, cos2, sin2, *, n_t, rows_per_t, lag, tiles_per_seq,
          ret_tables=None):
    tm = n_t * rows_per_t
    fuse_ret = ret_tables is not None
    assert not fuse_ret or (n_t == 1 and tm == pl.cdiv(N_SHIFT, D_A) * RET_CHUNK)
    if n_t == 1:
        m = x.shape[0]
        n_tiles = m // tm
        row = lambda w: pl.BlockSpec((tm, w), lambda i: (i, 0))
        shp = lambda w, dt: jax.ShapeDtypeStruct((m, w), dt)
        tab = pl.BlockSpec((tm, HEAD_R), lambda i: (i % tiles_per_seq, 0))
    else:
        m = x.shape[0] * x.shape[1]
        n_tiles = x.shape[1] // rows_per_t
        row = lambda w: pl.BlockSpec((n_t, rows_per_t, w), lambda i: (0, i, 0))
        shp = lambda w, dt: jax.ShapeDtypeStruct((n_t, m // n_t, w), dt)
        tab = pl.BlockSpec((n_t, rows_per_t, HEAD_R), lambda i: (0, i, 0))
    n_seq = n_tiles // tiles_per_seq
    hl_spec = pl.BlockSpec((1, lag, D_MODEL), lambda i: (i // tiles_per_seq, 0, 0))
    args = [x]
    specs = [row(D_MODEL)]
    if prev is not None:
        args.append(prev)
        specs.append(pl.BlockSpec((1, lag, D_MODEL), lambda i: (i // tiles_per_seq, 0, 0)))
    consts = (ng, win, mu) + (tuple(ret_tables) if fuse_ret else ())
    args += list(consts) + [cos2, sin2]
    specs += [_resident(c.shape) for c in consts] + [tab, tab]
    hl_shape = jax.ShapeDtypeStruct((n_seq, lag, D_MODEL), F32)
    scratch = [pltpu.VMEM((max(lag, SUBLANES), N_SHIFT), F32)]
    if fuse_ret:
        out_shape = [shp(N_SHIFT, F32), shp(D_R, BF16),
                     jax.ShapeDtypeStruct((n_seq, H_R, HEAD_R, HEAD_R), F32), hl_shape]
        out_specs = [row(N_SHIFT), row(D_R),
                     pl.BlockSpec((1, H_R, HEAD_R, HEAD_R), lambda i: (i // tiles_per_seq, 0, 0, 0)),
                     hl_spec]
        scratch.append(pltpu.VMEM((H_R, HEAD_R, HEAD_R), F32))
    else:
        out_shape = [shp(N_SHIFT, F32)] + [shp(D_R, F32)] * 4 + [hl_shape]
        out_specs = [row(N_SHIFT)] + [row(D_R)] * 4 + [hl_spec]
    return pl.pallas_call(
        functools.partial(_proj_kernel, n_t, lag, tiles_per_seq, prev is not None, fuse_ret),
        grid=(n_tiles,),
        in_specs=specs,
        out_specs=out_specs,
        out_shape=out_shape,
        scratch_shapes=scratch,
        compiler_params=_params(1),
        name="proj",
    )(*args)


def _sums_stacked(x, bd2):
    return _mm(jnp.concatenate(_split_hi_lo(x), axis=1), bd2)


def _wkv_post(y, r, k, v, g, rk, lw_g, lb_g, head_sums):
    inv_n = 1.0 / HEAD_A
    mu = head_sums(y) * inv_n
    d = y - mu
    var = head_sums(d * d) * inv_n
    yn = d * lax.rsqrt(var + GN_EPS_A) * lw_g + lb_g
    bonus = head_sums(r * k * rk) * v
    return (yn + bonus) * g


def _wkv_decay_gate(wa, gd, w0, w2p, a0, a2p, g2):
    w_pre = w0 + _mm(jnp.tanh(wa).astype(BF16), w2p)
    lw = -jnp.exp(-_softplus(-w_pre) - 0.5)
    a = _sigmoid(a0 + _mm(wa.astype(BF16), a2p))
    g = _mm(_sigmoid(gd).astype(BF16), g2)
    return lw, a, g


def _wkv_keys(k, a, k_k, k_a, head_sums):
    kk = k * k_k
    kk = kk / jnp.maximum(jnp.sqrt(head_sums(kk * kk)), 1e-12)
    return k * (1.0 + (a - 1.0) * k_a), kk, kk * a


def _wkv_prompt_kernel(nb, tt, mix_ref, w0_ref, w2_ref, a0_ref, a2_ref, g2_ref, kk_ref, ka_ref,
                       rk_ref, lnw_ref, lnb_ref, bd2_ref, tri2_ref,
                       ya_o, s_o, s_scr):
    c = WKV_CHUNK
    j = pl.program_id(1)

    @pl.when(j == 0)
    def _():
        s_scr[...] = jnp.zeros_like(s_scr)

    lane = lax.broadcasted_iota(jnp.int32, (c, PAIR), 1)
    first = lane < HEAD_A

    def stack(x):
        return jnp.concatenate([jnp.where(first, x, 0.0), jnp.where(first, 0.0, x)], axis=0)

    ri = lax.broadcasted_iota(jnp.int32, (2 * c, 2 * c), 0) % c
    ci = lax.broadcasted_iota(jnp.int32, (2 * c, 2 * c), 1) % c
    strict = ri > ci
    incl = ri >= ci
    row_i = lax.broadcasted_iota(jnp.int32, (2 * c, 2 * c), 0)
    col_i = lax.broadcasted_iota(jnp.int32, (2 * c, 2 * c), 1)
    eye = (row_i == col_i).astype(F32)
    blk8 = (row_i // SUBLANES) == (col_i // SUBLANES)
    off_mask = {sz: ((row_i // (2 * sz)) == (col_i // (2 * sz))) & ((row_i % (2 * sz)) >= sz)
                & ((col_i % (2 * sz)) < sz) for sz in (8, 16, 32)}
    bd2 = bd2_ref[...]
    tri2 = tri2_ref[...]
    head_sums = functools.partial(_sums_stacked, bd2=bd2)

    n_pairs = H_A // 2
    units = [(bi, p) for bi in range(nb) for p in range(n_pairs)]
    n_u = range(len(units))
    slab = [slice(p * PAIR, (p + 1) * PAIR) for p in range(n_pairs)]

    tall = lambda xs: jnp.concatenate(xs, axis=0)
    per_pair = lambda ref: tall([jnp.broadcast_to(ref[:, slab[p]], (c, PAIR)) for _, p in units])
    split = lambda x: [x[u * c:(u + 1) * c] for u in n_u]

    def prep(ch):
        rows = slice(ch * c, (ch + 1) * c)
        ld = lambda c0: [mix_ref[bi, rows, c0 + p * PAIR:c0 + (p + 1) * PAIR] for bi, p in units]
        r, k_raw, v = ld(0), ld(D_A), ld(2 * D_A)
        lora = [_wkv_decay_gate(mix_ref[bi, rows, 3 * D_A:3 * D_A + LORA_W + LORA_A],
                                mix_ref[bi, rows, 3 * D_A + LORA_W + LORA_A:N_SHIFT],
                                w0_ref[...], w2_ref[...], a0_ref[...], a2_ref[...], g2_ref[...])
                for bi in range(nb)]
        lw_all = [x[0] for x in lora]
        a = tall([lora[bi][1][:, slab[p]] for bi, p in units])
        gate = tall([lora[bi][2][:, slab[p]] for bi, p in units])
        k, kk, kka = map(split, _wkv_keys(tall(k_raw), a, per_pair(kk_ref), per_pair(ka_ref),
                                          head_sums))
        return r, v, k, kk, kka, gate, lw_all

    def gram_stage(pre):
        r, v, k, kk, kka, gate, lw_all = pre
        cum_all = [_mm(tri2, jnp.concatenate(_split_hi_lo(x), axis=0)) for x in lw_all]
        lw = [lw_all[bi][:, slab[p]] for bi, p in units]
        cum = [cum_all[bi][:, slab[p]] for bi, p in units]
        cum_end = [x[c - 1:c, :] for x in cum]
        e_pos = [jnp.exp(x) for x in cum]
        e_neg = [jnp.exp(-x) for x in cum]
        e_end = [jnp.exp(x - y) for x, y in zip(cum_end, cum)]
        xs = [jnp.concatenate([stack(-kk[u] * jnp.exp(cum[u] - lw[u])), stack(r[u] * e_pos[u])],
                              axis=0).astype(BF16) for u in n_u]
        ws = [jnp.concatenate([stack(kka[u] * e_neg[u]), stack(k[u] * e_neg[u])],
                              axis=0).astype(BF16) for u in n_u]
        we = [jnp.concatenate([stack(kka[u] * e_end[u]), stack(k[u] * e_end[u])],
                              axis=0).astype(BF16) for u in n_u]
        vs = [stack(x).astype(BF16) for x in v]
        gram = [_nt(xs[u], ws[u]) for u in n_u]
        a_ab = [jnp.where(strict, g[0:2 * c, 0:2 * c], 0.0) for g in gram]
        a_ak = [jnp.where(strict, g[0:2 * c, 2 * c:], 0.0).astype(BF16) for g in gram]
        a_r = [jnp.concatenate([jnp.where(incl, g[2 * c:, 0:2 * c], 0.0),
                                jnp.where(incl, g[2 * c:, 2 * c:], 0.0)], axis=1).astype(BF16)
               for g in gram]
        return xs, we, vs, cum_end, a_ab, a_ak, a_r

    def solve_stage(ch, pre, gs):
        rows = slice(ch * c, (ch + 1) * c)
        r, v, k, kk, kka, gate, lw_all = pre
        xs, we, vs, cum_end, a_ab, a_ak, a_r = gs
        a8 = [jnp.where(blk8, a, 0.0) for a in a_ab]
        inv = [eye + a for a in a8]
        pw = [x.astype(BF16) for x in a8]
        pw = [_mm(x, x).astype(BF16) for x in pw]
        both = [_mm(x, jnp.concatenate([x, i.astype(BF16)], axis=1)) for i, x in zip(inv, pw)]
        inv = [i + b[:, 2 * c:] for i, b in zip(inv, both)]
        inv = [i + _mm(b[:, 0:2 * c].astype(BF16), i.astype(BF16)) for i, b in zip(inv, both)]
        size = SUBLANES
        while size < c:
            lows = [slice(k + size, k + 2 * size) for k in range(0, 2 * c, 2 * size)]
            ups = [slice(k, k + size) for k in range(0, 2 * c, 2 * size)]
            off = [jnp.where(off_mask[size], a, 0.0).astype(BF16) for a in a_ab]
            d_low = [jnp.concatenate([i[r] for r in lows], axis=0) for i in inv]
            x_low = [_mm(d.astype(BF16), o).astype(BF16) for d, o in zip(d_low, off)]
            new_low = [d + _mm(x, i.astype(BF16)) for d, x, i in zip(d_low, x_low, inv)]
            inv = [jnp.concatenate([piece for k, r in enumerate(ups)
                                    for piece in (i[r], nl[k * size:(k + 1) * size])], axis=0)
                   for i, nl in zip(inv, new_low)]
            size *= 2
        s = [s_scr[u] for u in n_u]
        z = [_nt(xs[u], s[u].astype(BF16)) for u in n_u]
        rhs = [z[u][0:2 * c] + _mm(a_ak[u], vs[u]) for u in n_u]
        uu = [_mm(inv[u].astype(BF16), rhs[u].astype(BF16)).astype(BF16) for u in n_u]
        uv = [jnp.concatenate([uu[u], vs[u]], axis=0) for u in n_u]
        y2 = [z[u][2 * c:] + _mm(a_r[u], uv[u]) for u in n_u]
        for u in n_u:
            s_scr[u] = s[u] * jnp.exp(cum_end[u]) + _tn(uv[u], we[u])
        out = _wkv_post(tall([y2[u][0:c] + y2[u][c:] for u in n_u]), tall(r), tall(k), tall(v),
                        gate, per_pair(rk_ref), per_pair(lnw_ref), per_pair(lnb_ref), head_sums)
        for u, (bi, p) in enumerate(units):
            ya_o[bi, rows, slab[p]] = out[u * c:(u + 1) * c].astype(ya_o.dtype)

    n_chunks = tt // c
    pre = prep(0)
    for ch in range(n_chunks):
        gs = gram_stage(pre)
        nxt = prep(ch + 1) if ch + 1 < n_chunks else None
        solve_stage(ch, pre, gs)
        pre = nxt

    @pl.when(j == pl.num_programs(1) - 1)
    def _():
        for u, (bi, p) in enumerate(units):
            s = s_scr[u]
            s_o[bi, 2 * p] = s[0:HEAD_A, 0:HEAD_A]
            s_o[bi, 2 * p + 1] = s[HEAD_A:, HEAD_A:]


def _wkv_prompt(mixed, consts, *, nb, tt):
    batch, seq, _ = mixed.shape
    blk = lambda w: pl.BlockSpec((nb, tt, w), lambda b, j: (b, j, 0))
    return pl.pallas_call(
        functools.partial(_wkv_prompt_kernel, nb, tt),
        grid=(batch // nb, seq // tt),
        in_specs=[blk(N_SHIFT)] + [_resident(x.shape) for x in consts],
        out_specs=[blk(D_A), pl.BlockSpec((nb, H_A, HEAD_A, HEAD_A), lambda b, j: (b, 0, 0, 0))],
        out_shape=[jax.ShapeDtypeStruct((batch, seq, D_A), BF16),
                   jax.ShapeDtypeStruct((batch, H_A, HEAD_A, HEAD_A), F32)],
        scratch_shapes=[pltpu.VMEM((nb * H_A // 2, PAIR, PAIR), F32)],
        compiler_params=_params(2),
        name="wkv_prompt",
    )(mixed, *consts)


def _wkv_sample_kernel(n_t, r_ref, k_ref, v_ref, lora_ref,
                       w0_ref, w2_ref, a0_ref, a2_ref, g2_ref, kk_ref, ka_ref,
                       rk_ref, lnw_ref, lnb_ref, bd2_ref, s_hbm,
                       ya_o, s_o, yt_scr, s_buf, s_sem):
    step = pl.program_id(0)
    ahead = RET_RING - 1

    def block_copy(j):
        slot = j % RET_RING
        return pltpu.make_async_copy(s_hbm.at[pl.ds(2 * j, 2)], s_buf.at[slot], s_sem.at[slot])

    @pl.when(step == 0)
    def _():
        for j in range(ahead):
            block_copy(j).start()

    @pl.when(step + ahead < pl.num_programs(0))
    def _():
        block_copy(step + ahead).start()

    block_copy(step).wait()
    s_ref = s_buf.at[step % RET_RING]
    n = HEAD_A
    n_b = r_ref.shape[1]
    tall = lambda xs: jnp.concatenate(xs, axis=0)
    rows_of = lambda ref: tall([ref[t] for t in range(n_t)])
    head_sums = functools.partial(_sums_stacked, bd2=bd2_ref[...])
    r, v, lora = rows_of(r_ref), rows_of(v_ref), rows_of(lora_ref)
    lw, a, gate = _wkv_decay_gate(lora[:, 0:LORA_W + LORA_A], lora[:, LORA_W + LORA_A:],
                                  w0_ref[...], w2_ref[...], a0_ref[...], a2_ref[...], g2_ref[...])
    k, kk, kka = _wkv_keys(rows_of(k_ref), a, kk_ref[...], ka_ref[...], head_sums)

    tr = lambda x: [x[t * n_b:(t + 1) * n_b].T for t in range(n_t)]
    nkk_t, kka_t, k_t, r_t, v_t, w_t = tr(-kk), tr(kka), tr(k), tr(r), tr(v), tr(jnp.exp(lw))
    rid = lax.broadcasted_iota(jnp.int32, (SUBLANES, n_b), 0)

    for hh in range(2):
        keys = slice(hh * n, (hh + 1) * n)
        for ig in range(n // SUBLANES):
            y_tiles = [jnp.zeros((SUBLANES, n_b), F32) for _ in range(n_t)]
            for ii in range(SUBLANES):
                i = ig * SUBLANES + ii
                s = s_ref[hh, i]
                for t in range(n_t):
                    sa = jnp.sum(s * nkk_t[t][keys], axis=0, keepdims=True)
                    v_row = v_t[t][hh * n + i:hh * n + i + 1]
                    s = s * w_t[t][keys] + sa * kka_t[t][keys] + v_row * k_t[t][keys]
                    y_row = jnp.sum(s * r_t[t][keys], axis=0, keepdims=True)
                    y_tiles[t] = jnp.where(rid == ii, y_row, y_tiles[t])
                s_o[hh, i] = s
            for t in range(n_t):
                yt_scr[t, hh * n + ig * SUBLANES:hh * n + (ig + 1) * SUBLANES, :] = y_tiles[t]

    out = _wkv_post(tall([yt_scr[t].T for t in range(n_t)]), r, k, v, gate,
                    rk_ref[...], lnw_ref[...], lnb_ref[...], head_sums)
    for t in range(n_t):
        ya_o[t] = out[t * n_b:(t + 1) * n_b].astype(ya_o.dtype)


def _wkv_sample(mixed, w0, w2p, a0, a2p, g2, k_k, k_a, rk, lnw, lnb, bd2, s0):
    n_t, n_b, _ = mixed.shape
    slab = lambda first: pl.BlockSpec((n_t, n_b, PAIR), lambda p: (0, 0, first + p))
    lora = pl.BlockSpec((n_t, n_b, 2 * PAIR), lambda p: (0, 0, 3 * D_A // (2 * PAIR)))
    par = pl.BlockSpec((1, PAIR), lambda p: (0, p))
    low = pl.BlockSpec((PAIR, PAIR), lambda p: (0, p))
    st = pl.BlockSpec((2, HEAD_A, HEAD_A, n_b), lambda p: (p, 0, 0, 0))
    n_slab = D_A // PAIR
    return pl.pallas_call(
        functools.partial(_wkv_sample_kernel, n_t),
        grid=(H_A // 2,),
        in_specs=[slab(0), slab(n_slab), slab(2 * n_slab), lora,
                  par, low, par, low, low, par, par, par, par, par, _resident(bd2.shape),
                  pl.BlockSpec(memory_space=pl.ANY)],
        out_specs=[slab(0), st],
        out_shape=[jax.ShapeDtypeStruct((n_t, n_b, D_A), F32),
                   jax.ShapeDtypeStruct(s0.shape, F32)],
        scratch_shapes=[pltpu.VMEM((n_t, PAIR, n_b), F32),
                        pltpu.VMEM((RET_RING, 2, HEAD_A, HEAD_A, n_b), F32),
                        pltpu.SemaphoreType.DMA((RET_RING,))],
        compiler_params=_params(1),
        name="wkv_sample",
    )(mixed, mixed, mixed, mixed, w0, w2p, a0, a2p, g2, k_k, k_a, rk, lnw, lnb, bd2, s0)


def _ret_chunk(q, k, v, g, s, heads, dm_ref, qd_ref, kd_ref, cd_ref, gn_ref):
    n = range(len(q))
    qb = [x.astype(BF16) for x in q]
    kb = [x.astype(BF16) for x in k]
    vb = [x.astype(BF16) for x in v]
    inner = [(_nt(qb[u], kb[u]) * dm_ref[heads[u]]).astype(BF16) for u in n]
    q_dec = [(q[u].astype(F32) * qd_ref[heads[u]]).astype(BF16) for u in n]
    k_dec = [(k[u].astype(F32) * kd_ref[heads[u]]).astype(BF16) for u in n]
    if inner[0].shape[1] % LANES == 0:
        y = [_mm(jnp.concatenate([inner[u], q_dec[u]], axis=1),
                 jnp.concatenate([vb[u], s[u].astype(BF16)], axis=0)) for u in n]
    else:
        y = [_mm(inner[u], vb[u]) + _mm(q_dec[u], s[u].astype(BF16)) for u in n]
    s_new = [s[u] * cd_ref[heads[u]] + _tn(k_dec[u], vb[u]) for u in n]
    out = []
    for u in n:
        mu = jnp.mean(y[u], axis=-1, keepdims=True)
        d = y[u] - mu
        var = jnp.mean(d * d, axis=-1, keepdims=True)
        lanes = slice(heads[u] * HEAD_R, (heads[u] + 1) * HEAD_R)
        yn = d * lax.rsqrt(var + GN_EPS_R) * gn_ref[:, lanes]
        out.append(g[u] * _sigmoid(g[u]) * yn)
    return out, s_new


def _ret_sample_kernel(n_t, bb, q_ref, k_ref, v_ref, g_ref, dm_ref, qd_ref, kd_ref, cd_ref,
                       gn_ref, s_hbm, y_o, s_o, s_buf, s_sem):
    i = pl.program_id(0)
    n = pl.num_programs(0)
    ahead = RET_RING - 1

    def block_copy(j):
        slot = j % RET_RING
        return pltpu.make_async_copy(s_hbm.at[pl.ds(j * bb, bb)], s_buf.at[slot], s_sem.at[slot])

    @pl.when(i == 0)
    def _():
        for j in range(ahead):
            block_copy(j).start()

    @pl.when(i + ahead < n)
    def _():
        block_copy(i + ahead).start()

    block_copy(i).wait()
    s_ref = s_buf.at[i % RET_RING]
    rid = lax.broadcasted_iota(jnp.int32, (SUBLANES, HEAD_R), 0)
    units = [(bi, hh) for bi in range(bb) for hh in range(H_R)]
    heads = [hh for _, hh in units]
    lanes = [slice(hh * HEAD_R, (hh + 1) * HEAD_R) for hh in heads]

    def seq_rows(ref):
        outs = []
        for u, (bi, _) in enumerate(units):
            out = jnp.zeros((SUBLANES, HEAD_R), F32)
            for t in range(n_t):
                out = jnp.where(rid == t, jnp.broadcast_to(ref[t, bi:bi + 1, lanes[u]], out.shape), out)
            outs.append(out)
        return outs

    y, s_new = _ret_chunk(seq_rows(q_ref), seq_rows(k_ref), seq_rows(v_ref), seq_rows(g_ref),
                          [s_ref[bi, hh] for bi, hh in units], heads,
                          dm_ref, qd_ref, kd_ref, cd_ref, gn_ref)
    for u, (bi, hh) in enumerate(units):
        s_o[bi, hh] = s_new[u]
        for t in range(n_t):
            y_o[t, bi:bi + 1, lanes[u]] = y[u][t:t + 1].astype(y_o.dtype)


def _ret_sample(q, k, v, g, dm, qd, kd, cd, gn, s0, *, bb):
    n_t, n_b, _ = q.shape
    consts = (dm, qd, kd, cd, gn)
    blk = pl.BlockSpec((n_t, bb, D_R), lambda i: (0, i, 0))
    st = pl.BlockSpec((bb, H_R, HEAD_R, HEAD_R), lambda i: (i, 0, 0, 0))
    assert n_b // bb >= RET_RING - 1
    return pl.pallas_call(
        functools.partial(_ret_sample_kernel, n_t, bb),
        grid=(n_b // bb,),
        in_specs=[blk] * 4 + [_resident(x.shape) for x in consts]
        + [pl.BlockSpec(memory_space=pl.ANY)],
        out_specs=[blk, st],
        out_shape=[jax.ShapeDtypeStruct((n_t, n_b, D_R), F32),
                   jax.ShapeDtypeStruct(s0.shape, F32)],
        scratch_shapes=[pltpu.VMEM((RET_RING, bb, H_R, HEAD_R, HEAD_R), F32),
                        pltpu.SemaphoreType.DMA((RET_RING,))],
        compiler_params=_params(1),
        name="ret_sample",
    )(q, k, v, g, *consts, s0)


def _rope_tables(pos):
    half = HEAD_R // 2
    inv = ROPE_BASE ** (-np.arange(half, dtype=np.float64) / half)
    ang = np.asarray(pos, np.float64)[:, None] * inv[None, :]
    cos, sin = np.cos(ang), np.sin(ang)
    return (np.concatenate([cos, cos], axis=1).astype(np.float32),
            np.concatenate([-sin, sin], axis=1).astype(np.float32))


def _ret_tables(c):
    lg = np.log1p(-np.exp2(-5.0 - np.arange(H_R, dtype=np.float64)))
    idx = np.arange(c, dtype=np.float64)
    diff = idx[:, None] - idx[None, :]
    dmask = np.where(diff >= 0, np.exp(lg[:, None, None] * np.maximum(diff, 0.0)), 0.0)
    ones = np.ones((1, 1, HEAD_R))
    qdec = np.exp(lg[:, None] * (idx + 1.0))[:, :, None] * ones
    kdec = np.exp(lg[:, None] * (c - 1.0 - idx))[:, :, None] * ones
    cdec = np.exp(lg * c)[:, None, None] * ones
    extra = -c % SUBLANES
    dmask = np.pad(dmask, ((0, 0), (0, extra), (0, extra)))
    qdec = np.pad(qdec, ((0, 0), (0, extra), (0, 0)))
    kdec = np.pad(kdec, ((0, 0), (0, extra), (0, 0)))
    return tuple(t.astype(np.float32) for t in (dmask, qdec, kdec, cdec))


def _block_ones(n, block):
    idx = jnp.arange(n) // block
    return (idx[:, None] == idx[None, :]).astype(BF16)


def kernel(x_prompt, x_sample, state_shift, state_wkv, state_ret, norm_g, ffn1_wg, ffn1_wu, ffn1_wd,
           w_in, mu_shift, w0, w2, a0, a2, g2, k_k, k_a, r_k, lnx_w, lnx_b, ret_gn_w, w_out,
           ffn2_wg, ffn2_wu, ffn2_wd):
    assert norm_g.shape[0] == 1, "single-layer configuration"
    bp, tp, _ = x_prompt.shape
    bs, ts, _ = x_sample.shape
    l = 0
    ng = norm_g[l]
    row = lambda t: t[l].reshape(1, -1)
    zpad = jnp.zeros((LORA_W, D_A), BF16)
    w2p = jnp.concatenate([w2[l].astype(BF16), zpad], axis=0)
    a2p = jnp.concatenate([zpad, a2[l].astype(BF16)], axis=0)
    rk, lnw, lnb, gn = row(r_k), row(lnx_w), row(lnx_b), row(ret_gn_w)
    wkv_params = (row(w0), w2p, row(a0), a2p, g2[l].astype(BF16), row(k_k), row(k_a), rk, lnw, lnb)
    bd_pair = _block_ones(PAIR, HEAD_A)
    bd2 = jnp.concatenate([bd_pair, bd_pair], axis=0)
    tri = (jnp.arange(WKV_CHUNK)[:, None] >= jnp.arange(WKV_CHUNK)[None, :]).astype(BF16)
    tri2 = jnp.concatenate([tri, tri], axis=1)

    m_s = bs * ts
    x1s, *f1 = _ffn_stream(x_sample, ng, ffn1_wg[l], ffn1_wu[l], ffn1_wd[l], 0, 1)

    xp = x_prompt.reshape(bp * tp, D_MODEL)
    x1p, *f2, win, wo = _ffn(xp, ng, *f1, 0, 1, FFN_ROWS,
                             cast=(ffn2_wg[l], ffn2_wu[l], ffn2_wd[l], w_in[l], w_out[l]))
    proj_consts = (ng, win, row(mu_shift))
    cos_p, sin_p = _rope_tables(np.arange(tp))
    (mixed, yr_p, ret_p, hl_p) = _proj(
        x1p, None, *proj_consts, cos_p, sin_p, n_t=1, rows_per_t=PROJ_ROWS, lag=1,
        tiles_per_seq=tp // PROJ_ROWS, ret_tables=(*_ret_tables(RET_CHUNK), gn))
    ya_p, wkv_p = _wkv_prompt(mixed.reshape(bp, tp, N_SHIFT), (*wkv_params, bd2, tri2),
                              nb=WKV_SEQS, tt=WKV_ROWS)
    ya_p = ya_p.reshape(bp * tp, D_A)
    yp = _ffn(x1p, ng, *f2, 4, 5, FFN_ROWS, mix=(ya_p, yr_p, wo))

    cos_s, sin_s = _rope_tables(PAST_LEN + np.arange(ts))
    rows_per_t = bs
    tab = lambda t: np.ascontiguousarray(np.broadcast_to(t[:, None, :], (ts, bs, HEAD_R)))
    outs = _proj(x1s.reshape(ts, bs, D_MODEL),
                 state_shift[l].reshape(bs // rows_per_t, rows_per_t, D_MODEL),
                 *proj_consts, tab(cos_s), tab(sin_s), n_t=ts, rows_per_t=rows_per_t,
                 lag=rows_per_t, tiles_per_seq=1)
    (mixed, q, kr, vr, gr, hl_s) = outs
    ya_s, wkv_s = _wkv_sample(mixed, *wkv_params, bd2, state_wkv[l].transpose(1, 2, 3, 0))
    wkv_s = wkv_s.transpose(3, 0, 1, 2)
    yr_s, ret_s = _ret_sample(q, kr, vr, gr, *_ret_tables(min(RET_CHUNK, ts)), gn, state_ret[l],
                              bb=RET_SEQS)
    ys = _ffn(x1s, ng, *f2, 4, 5, m_s,
              mix=(ya_s.reshape(m_s, D_A), yr_s.reshape(m_s, D_R), wo), out_seqs=bs)

    return (yp.reshape(bp, tp, D_MODEL), ys,
            hl_p.reshape(1, bp, D_MODEL), wkv_p[None], ret_p[None],
            hl_s.reshape(1, bs, D_MODEL), wkv_s[None], ret_s[None])
```
